```python
import math
import jax
import jax.numpy as jnp
from jax import lax
import numpy as np

D_MODEL = 1024
BATCH = 4
SEQ = 4096
DEPTH = 2

GRID_W = 64
CTX_LEN = 256
EPS = 1e-6

S5_WIDTH = 256
S5_GROUP = 16
S5_NGROUPS = S5_WIDTH // S5_GROUP
S5_STATE = 64

NA_HEADS = 6
NA_HEAD_DIM = 64
NA_WIDTH = NA_HEADS * NA_HEAD_DIM
NA_KH = 8
NA_KW = 16
RPB_H = 2 * NA_KH - 1
RPB_W = 2 * NA_KW - 1

SSD_HEADS = 6
SSD_HEAD_DIM = 64
SSD_WIDTH = SSD_HEADS * SSD_HEAD_DIM
SSD_NGROUPS = 2
SSD_HEADS_PER_GROUP = SSD_HEADS // SSD_NGROUPS
SSD_STATE = 128
SSD_CONV = 5
SSD_CHUNK = 128
SSD_BC = SSD_NGROUPS * SSD_STATE
SSD_XBC = SSD_WIDTH + 2 * SSD_BC
SSD_PROJ = SSD_WIDTH + SSD_XBC + 2 * SSD_HEADS

D_MIX = S5_WIDTH + NA_WIDTH + SSD_WIDTH
IN_COLS = S5_WIDTH + 3 * NA_WIDTH + SSD_PROJ

N_EXPERTS = 16
D_EXPERT = 1024
EC_CAPACITY_FACTOR = 2

ROPE_BASE = 10000.0

kernel_name = 'hybrid_s5_natten_ssd_ec_moe_trunk'


def rmsnorm(x, w):
    xf = x.astype(jnp.float32)
    y = xf * lax.rsqrt(jnp.mean(xf * xf, axis=-1, keepdims=True) + EPS)
    return (y * w.astype(jnp.float32)).astype(x.dtype)


def modulate(x, norm_w, shift, scale):
    return rmsnorm(x, norm_w) * (1.0 + scale) + shift


def axial_rope(t):
    L, n = t.shape[1], t.shape[-1]
    half = n // 2
    nf = half // 2
    pos = jnp.arange(L)
    inv_freq = ROPE_BASE ** (-jnp.arange(nf, dtype=jnp.float32) / nf)

    def rot(u, p):
        ang = p.astype(jnp.float32)[:, None] * inv_freq
        cos = jnp.cos(ang)[None, :, None, :]
        sin = jnp.sin(ang)[None, :, None, :]
        u1, u2 = u[..., :nf], u[..., nf:]
        return jnp.concatenate([u1 * cos - u2 * sin, u1 * sin + u2 * cos], axis=-1)

    tf = t.astype(jnp.float32)
    return jnp.concatenate([rot(tf[..., :half], pos // GRID_W), rot(tf[..., half:], pos % GRID_W)], axis=-1)


def dwconv_centred(x, w, bias):
    k = w.shape[0]
    y = lax.conv_general_dilated(
        x, w[:, None, :].astype(x.dtype), window_strides=(1,), padding=[(k // 2, k // 2)],
        dimension_numbers=('NWC', 'WIO', 'NWC'), feature_group_count=x.shape[-1])
    return y + bias


def diag_linear_scan(lam_bar, bu, h0):
    a = jnp.broadcast_to(lam_bar, bu.shape)

    def combine(e1, e2):
        a1, b1 = e1
        a2, b2 = e2
        return a1 * a2, a2 * b1 + b2

    a_cum, states = lax.associative_scan(combine, (a, bu), axis=1)
    return states + a_cum * h0[:, None]


def s5_mixer(u_ctx, u_lat, lam_re, lam_im, log_dt, b_re, b_im, c_re, c_im, d_skip, glu_w, glu_b, with_ctx_out):
    f32 = jnp.float32

    def to_groups(u):
        return u.astype(f32).reshape(u.shape[0], u.shape[1], S5_NGROUPS, S5_GROUP)

    uc, ul = to_groups(u_ctx), to_groups(u_lat)
    dsk = d_skip.astype(f32)
    y_ctx, y_lat = uc * dsk, ul * dsk
    for d in range(2):
        lam = lax.complex(lam_re[d].astype(f32), lam_im[d].astype(f32))
        lam_bar = jnp.exp(lam * jnp.exp(log_dt[d].astype(f32))[:, None])
        b_bar = ((lam_bar - 1.0) / lam)[..., None] * lax.complex(b_re[d].astype(f32), b_im[d].astype(f32))
        c_mat = lax.complex(c_re[d].astype(f32), c_im[d].astype(f32))
        orient = (lambda t: t) if d == 0 else (lambda t: jnp.flip(t, axis=1))
        bu_ctx = jnp.einsum('blgc,gpc->blgp', orient(uc).astype(jnp.complex64), b_bar)
        h_zero = jnp.zeros((bu_ctx.shape[0], S5_NGROUPS, S5_STATE), jnp.complex64)
        st_ctx = diag_linear_scan(lam_bar, bu_ctx, h_zero)
        bu_lat = jnp.einsum('blgc,gpc->blgp', orient(ul).astype(jnp.complex64), b_bar)
        st_lat = diag_linear_scan(lam_bar, bu_lat, st_ctx[:, -1])
        y_lat = y_lat + orient(jnp.einsum('gcp,blgp->blgc', c_mat, st_lat).real)
        if with_ctx_out:
            y_ctx = y_ctx + orient(jnp.einsum('gcp,blgp->blgc', c_mat, st_ctx).real)

    def glu(y, ref):
        g = jax.nn.gelu(y.reshape(y.shape[0], y.shape[1], S5_WIDTH))
        return (g * jax.nn.sigmoid(g @ glu_w.astype(f32) + glu_b.astype(f32))).astype(ref.dtype)

    out_ctx = glu(y_ctx, u_ctx) if with_ctx_out else None
    return out_ctx, glu(y_lat, u_lat)


def natten_mixer(qkv_ctx, qkv_lat, rpb, with_ctx_out):
    f32 = jnp.float32
    dtype = qkv_lat.dtype
    b, L, _ = qkv_lat.shape
    rows = L // GRID_W
    kh = min(NA_KH, rows)
    scale = NA_HEAD_DIM ** -0.5

    def heads(t):
        return t.reshape(t.shape[0], t.shape[1], NA_HEADS, NA_HEAD_DIM)

    q_l, k_l, v_l = jnp.split(qkv_lat, 3, axis=-1)
    q_c, k_c, v_c = jnp.split(qkv_ctx, 3, axis=-1)
    q_c, k_c, v_c = heads(q_c), heads(k_c), heads(v_c)
    grid = (b, rows, GRID_W, NA_HEADS, NA_HEAD_DIM)
    q, k, v = q_l.reshape(grid), k_l.reshape(grid), v_l.reshape(grid)

    r = jnp.arange(rows)
    row_idx = jnp.clip(r - kh // 2, 0, rows - kh)[:, None] + jnp.arange(kh)[None, :]
    k_band = k[:, row_idx]
    v_band = v[:, row_idx]
    s_lat = jnp.einsum('brchd,brkwhd->bhrckw', q, k_band).astype(f32) * scale

    col = jnp.arange(GRID_W)
    c0 = jnp.clip(col - NA_KW // 2, 0, GRID_W - NA_KW)
    in_win = (col[None, :] >= c0[:, None]) & (col[None, :] < c0[:, None] + NA_KW)
    rel_r = row_idx - r[:, None] + (NA_KH - 1)
    rel_c = jnp.clip(col[None, :] - col[:, None] + (NA_KW - 1), 0, RPB_W - 1)
    bias = rpb.astype(f32)[:, rel_r[:, None, :, None], rel_c[None, :, None, :]]
    bias = jnp.where(in_win[None, None, :, None, :], bias, -jnp.inf)
    s_lat = s_lat + bias[None]

    s_ctx = jnp.einsum('brchd,bshd->bhrcs', q, k_c).astype(f32) * scale
    n_loc = kh * GRID_W
    probs = jax.nn.softmax(
        jnp.concatenate([s_lat.reshape(b, NA_HEADS, rows, GRID_W, n_loc), s_ctx], axis=-1), axis=-1).astype(dtype)
    p_loc = probs[..., :n_loc].reshape(b, NA_HEADS, rows, GRID_W, kh, GRID_W)
    p_ctx = probs[..., n_loc:]
    o = jnp.einsum('bhrckw,brkwhd->brchd', p_loc, v_band) + jnp.einsum('bhrcs,bshd->brchd', p_ctx, v_c)
    out_lat = o.reshape(b, L, NA_WIDTH)

    out_ctx = None
    if with_ctx_out:
        s = jnp.einsum('bqhd,bkhd->bhqk', q_c, k_c).astype(f32) * scale
        p = jax.nn.softmax(s, axis=-1).astype(dtype)
        out_ctx = jnp.einsum('bhqk,bkhd->bqhd', p, v_c).reshape(b, qkv_ctx.shape[1], NA_WIDTH)
    return out_ctx, out_lat


def segsum_exp(a):
    q = a.shape[-1]
    cs = jnp.cumsum(a, axis=-1)
    diff = cs[..., :, None] - cs[..., None, :]
    lower = jnp.tril(jnp.ones((q, q), dtype=bool))
    return jnp.where(lower, jnp.exp(jnp.where(lower, diff, 0.0)), 0.0)


def ssd_scan(xdt, a, bm, cm, h0, with_y):
    b, L, H, P = xdt.shape
    N = bm.shape[-1]
    nc = L // SSD_CHUNK
    xc = xdt.reshape(b, nc, SSD_CHUNK, H, P)
    bc = bm.reshape(b, nc, SSD_CHUNK, H, N)
    cc = cm.reshape(b, nc, SSD_CHUNK, H, N)
    ac = a.reshape(b, nc, SSD_CHUNK, H).transpose(0, 3, 1, 2)
    a_cs = jnp.cumsum(ac, axis=-1)
    decay_to_end = jnp.exp(a_cs[..., -1:] - a_cs).transpose(0, 2, 3, 1)[..., None]
    chunk_states = jnp.einsum('bclhn,bclhp->bchpn', bc * decay_to_end, xc)
    chunk_decay = jnp.exp(a_cs[..., -1])

    def step(h, inp):
        dec, st = inp
        return dec[..., None, None] * h + st, h

    h_final, h_in = lax.scan(step, h0, (jnp.moveaxis(chunk_decay, 2, 0), jnp.moveaxis(chunk_states, 1, 0)))
    if not with_y:
        return None, h_final
    h_in = jnp.moveaxis(h_in, 0, 1)
    scores = jnp.einsum('bclhn,bcshn->bhcls', cc, bc) * segsum_exp(ac)
    y = jnp.einsum('bhcls,bcshp->bclhp', scores, xc)
    y = y + jnp.einsum('bclhn,bchpn->bclhp', cc, h_in) * jnp.exp(a_cs).transpose(0, 2, 3, 1)[..., None]
    return y.reshape(b, L, H, P), h_final


def ssd_mixer(p_ctx, p_lat, conv_w, conv_b, dt_bias, a_log, d_skip, norm_w, with_ctx_out):
    f32 = jnp.float32
    a_rate = -jnp.exp(a_log.astype(f32))

    def prep(p, positional):
        b, L, _ = p.shape
        z = p[..., :SSD_WIDTH]
        xbc = jax.nn.silu(dwconv_centred(p[..., SSD_WIDTH:SSD_WIDTH + SSD_XBC], conv_w, conv_b))
        xs = xbc[..., :SSD_WIDTH].reshape(b, L, SSD_HEADS, SSD_HEAD_DIM).astype(f32)
        bm = xbc[..., SSD_WIDTH:SSD_WIDTH + SSD_BC].reshape(b, L, SSD_NGROUPS, SSD_STATE)
        cm = xbc[..., SSD_WIDTH + SSD_BC:].reshape(b, L, SSD_NGROUPS, SSD_STATE)
        if positional:
            bm, cm = axial_rope(bm), axial_rope(cm)
        bm = jnp.repeat(bm.astype(f32), SSD_HEADS_PER_GROUP, axis=2)
        cm = jnp.repeat(cm.astype(f32), SSD_HEADS_PER_GROUP, axis=2)
        dt_raw = p[..., SSD_WIDTH + SSD_XBC:].reshape(b, L, 2, SSD_HEADS).astype(f32)
        dt = jax.nn.softplus(dt_raw + dt_bias.astype(f32))
        return z, xs, bm, cm, dt

    zc, xc, bc, cc, dtc = prep(p_ctx, False)
    zl, xl, bl, cl, dtl = prep(p_lat, True)
    dsk = d_skip.astype(f32)[:, None]
    y_ctx, y_lat = xc * dsk, xl * dsk
    for d in range(2):
        orient = (lambda t: t) if d == 0 else (lambda t: jnp.flip(t, axis=1))
        b = xc.shape[0]
        h_zero = jnp.zeros((b, SSD_HEADS, SSD_HEAD_DIM, SSD_STATE), f32)
        y_c, h_c = ssd_scan(orient(xc * dtc[:, :, d, :, None]), orient(dtc[:, :, d] * a_rate[d]),
                            orient(bc), orient(cc), h_zero, with_ctx_out)
        y_l, _ = ssd_scan(orient(xl * dtl[:, :, d, :, None]), orient(dtl[:, :, d] * a_rate[d]),
                          orient(bl), orient(cl), h_c, True)
        y_lat = y_lat + orient(y_l)
        if with_ctx_out:
            y_ctx = y_ctx + orient(y_c)

    def gated_out(y, z):
        y = y.reshape(y.shape[0], y.shape[1], SSD_WIDTH) * jax.nn.silu(z.astype(f32))
        return rmsnorm(y, norm_w).astype(z.dtype)

    out_ctx = gated_out(y_ctx, zc) if with_ctx_out else None
    return out_ctx, gated_out(y_lat, zl)


def expert_choice_ffn(h, router, w_gate, w_up, w_down):
    b, L, _ = h.shape
    cap = EC_CAPACITY_FACTOR * L // N_EXPERTS
    aff = jax.nn.softmax((h @ router).astype(jnp.float32), axis=-1)
    gate, idx = lax.top_k(jnp.swapaxes(aff, 1, 2), cap)
    bidx = jnp.arange(b)[:, None, None]
    xs = h[bidx, idx]
    hid = jax.nn.silu(jnp.einsum('becd,edf->becf', xs, w_gate)) * jnp.einsum('becd,edf->becf', xs, w_up)
    ye = jnp.einsum('becf,efd->becd', hid, w_down) * gate[..., None].astype(h.dtype)
    return jnp.zeros_like(h).at[bidx, idx].add(ye)


def trunk_layer(x_ctx, x_lat, c, c_ctx, w_ada, b_ada, norm1_w, norm2_w, w_in, w_out,
                s5_lam_re, s5_lam_im, s5_log_dt, s5_b_re, s5_b_im, s5_c_re, s5_c_im, s5_d, s5_glu_w, s5_glu_b,
                na_rpb, ssd_conv_w, ssd_conv_b, ssd_dt_bias, ssd_a_log, ssd_d, ssd_norm_w,
                moe_router, moe_w_gate, moe_w_up, moe_w_down, with_ctx_out):
    mod_lat = jax.nn.silu(c) @ w_ada + b_ada
    mod_ctx = jax.nn.silu(c_ctx)[None, :] @ w_ada + b_ada
    sh1_l, sc1_l, g1_l, sh2_l, sc2_l, g2_l = jnp.split(mod_lat[:, None, :], 6, axis=-1)
    sh1_c, sc1_c, g1_c, sh2_c, sc2_c, g2_c = jnp.split(mod_ctx[:, None, :], 6, axis=-1)

    p_lat = modulate(x_lat, norm1_w, sh1_l, sc1_l) @ w_in
    p_ctx = modulate(x_ctx, norm1_w, sh1_c, sc1_c) @ w_in
    e1 = S5_WIDTH
    e2 = e1 + 3 * NA_WIDTH
    s5_c, s5_l = s5_mixer(p_ctx[..., :e1], p_lat[..., :e1], s5_lam_re, s5_lam_im, s5_log_dt,
                          s5_b_re, s5_b_im, s5_c_re, s5_c_im, s5_d, s5_glu_w, s5_glu_b, with_ctx_out)
    na_c, na_l = natten_mixer(p_ctx[..., e1:e2], p_lat[..., e1:e2], na_rpb, with_ctx_out)
    ssd_c, ssd_l = ssd_mixer(p_ctx[..., e2:], p_lat[..., e2:], ssd_conv_w, ssd_conv_b, ssd_dt_bias,
                             ssd_a_log, ssd_d, ssd_norm_w, with_ctx_out)

    x_lat = x_lat + g1_l * (jnp.concatenate([s5_l, na_l, ssd_l], axis=-1) @ w_out)
    x_lat = x_lat + g2_l * expert_choice_ffn(modulate(x_lat, norm2_w, sh2_l, sc2_l),
                                             moe_router, moe_w_gate, moe_w_up, moe_w_down)
    if with_ctx_out:
        x_ctx = x_ctx + g1_c * (jnp.concatenate([s5_c, na_c, ssd_c], axis=-1) @ w_out)
        x_ctx = x_ctx + g2_c * expert_choice_ffn(modulate(x_ctx, norm2_w, sh2_c, sc2_c),
                                                 moe_router, moe_w_gate, moe_w_up, moe_w_down)
    return x_ctx, x_lat


def setup_inputs(seed: int = 0) -> dict:
    key = jax.random.key(seed)
    keys = iter(jax.random.split(key, 40))
    f32 = jnp.float32

    def nrm(shape, s):
        return jax.random.normal(next(keys), shape, f32) * s

    def unif(shape, lo, hi):
        return jax.random.uniform(next(keys), shape, f32, lo, hi)

    Ld = DEPTH
    G, P, C = S5_NGROUPS, S5_STATE, S5_GROUP
    x = nrm((BATCH, SEQ, D_MODEL), 1.0)
    c = nrm((BATCH, D_MODEL), 1.0)
    ctx = nrm((BATCH, CTX_LEN, D_MODEL), 1.0)
    c_ctx = nrm((D_MODEL,), 1.0)
    w_ada = nrm((Ld, D_MODEL, 6 * D_MODEL), D_MODEL ** -0.5)
    b_ada = nrm((Ld, 6 * D_MODEL), 0.02)
    norm1_w = 1.0 + nrm((Ld, D_MODEL), 0.02)
    norm2_w = 1.0 + nrm((Ld, D_MODEL), 0.02)
    w_in = nrm((Ld, D_MODEL, IN_COLS), D_MODEL ** -0.5)
    w_out = nrm((Ld, D_MIX, D_MODEL), D_MIX ** -0.5)
    s5_lam_re = -0.5 + nrm((Ld, 2, G, P), 0.01)
    s5_lam_im = math.pi * jnp.arange(P, dtype=f32) + nrm((Ld, 2, G, P), 0.01)
    s5_log_dt = unif((Ld, 2, G), math.log(1e-3), math.log(1e-1))
    s5_b_re = nrm((Ld, 2, G, P, C), (2 * C) ** -0.5)
    s5_b_im = nrm((Ld, 2, G, P, C), (2 * C) ** -0.5)
    s5_c_re = nrm((Ld, 2, G, C, P), (2 * P) ** -0.5)
    s5_c_im = nrm((Ld, 2, G, C, P), (2 * P) ** -0.5)
    s5_d = nrm((Ld, G, C), 0.5)
    s5_glu_w = nrm((Ld, S5_WIDTH, S5_WIDTH), S5_WIDTH ** -0.5)
    s5_glu_b = nrm((Ld, S5_WIDTH), 0.02)
    na_rpb = nrm((Ld, NA_HEADS, RPB_H, RPB_W), 0.1)
    ssd_conv_w = nrm((Ld, SSD_CONV, SSD_XBC), SSD_CONV ** -0.5)
    ssd_conv_b = nrm((Ld, SSD_XBC), 0.02)
    dt0 = jnp.exp(unif((Ld, 2, SSD_HEADS), math.log(1e-3), math.log(1e-1)))
    ssd_dt_bias = dt0 + jnp.log(-jnp.expm1(-dt0))
    ssd_a_log = jnp.log(unif((Ld, 2, SSD_HEADS), 1.0, 16.0))
    ssd_d = 1.0 + nrm((Ld, SSD_HEADS), 0.1)
    ssd_norm_w = 1.0 + nrm((Ld, SSD_WIDTH), 0.02)
    moe_router = nrm((Ld, D_MODEL, N_EXPERTS), D_MODEL ** -0.5)
    moe_w_gate = nrm((Ld, N_EXPERTS, D_MODEL, D_EXPERT), D_MODEL ** -0.5)
    moe_w_up = nrm((Ld, N_EXPERTS, D_MODEL, D_EXPERT), D_MODEL ** -0.5)
    moe_w_down = nrm((Ld, N_EXPERTS, D_EXPERT, D_MODEL), D_EXPERT ** -0.5)
    final_norm_w = 1.0 + nrm((D_MODEL,), 0.02)
    return {
        'x': x, 'c': c, 'ctx': ctx, 'c_ctx': c_ctx,
        'w_ada': w_ada, 'b_ada': b_ada, 'norm1_w': norm1_w, 'norm2_w': norm2_w,
        'w_in': w_in, 'w_out': w_out,
        's5_lam_re': s5_lam_re, 's5_lam_im': s5_lam_im, 's5_log_dt': s5_log_dt,
        's5_b_re': s5_b_re, 's5_b_im': s5_b_im, 's5_c_re': s5_c_re, 's5_c_im': s5_c_im,
        's5_d': s5_d, 's5_glu_w': s5_glu_w, 's5_glu_b': s5_glu_b,
        'na_rpb': na_rpb,
        'ssd_conv_w': ssd_conv_w, 'ssd_conv_b': ssd_conv_b, 'ssd_dt_bias': ssd_dt_bias,
        'ssd_a_log': ssd_a_log, 'ssd_d': ssd_d, 'ssd_norm_w': ssd_norm_w,
        'moe_router': moe_router, 'moe_w_gate': moe_w_gate, 'moe_w_up': moe_w_up, 'moe_w_down': moe_w_down,
        'final_norm_w': final_norm_w,
    }


def reference(x, c, ctx, c_ctx, w_ada, b_ada, norm1_w, norm2_w, w_in, w_out,
              s5_lam_re, s5_lam_im, s5_log_dt, s5_b_re, s5_b_im, s5_c_re, s5_c_im, s5_d, s5_glu_w, s5_glu_b,
              na_rpb, ssd_conv_w, ssd_conv_b, ssd_dt_bias, ssd_a_log, ssd_d, ssd_norm_w,
              moe_router, moe_w_gate, moe_w_up, moe_w_down, final_norm_w):
    x_ctx, x_lat = ctx, x
    for l in range(DEPTH):
        with_ctx_out = l < DEPTH - 1
        x_ctx, x_lat = trunk_layer(
            x_ctx, x_lat, c, c_ctx, w_ada[l], b_ada[l], norm1_w[l], norm2_w[l], w_in[l], w_out[l],
            s5_lam_re[l], s5_lam_im[l], s5_log_dt[l], s5_b_re[l], s5_b_im[l], s5_c_re[l], s5_c_im[l],
            s5_d[l], s5_glu_w[l], s5_glu_b[l],
            na_rpb[l], ssd_conv_w[l], ssd_conv_b[l], ssd_dt_bias[l], ssd_a_log[l], ssd_d[l], ssd_norm_w[l],
            moe_router[l], moe_w_gate[l], moe_w_up[l], moe_w_down[l], with_ctx_out)
    return rmsnorm(x_lat, final_norm_w)
```

```python
import functools
import math

import jax
import jax.numpy as jnp
from jax import lax
from jax.experimental import pallas as pl
from jax.experimental.pallas import tpu as pltpu

F32 = jnp.float32
BF16 = jnp.bfloat16
I32 = jnp.int32

D_MODEL = 1024
BATCH = 4
SEQ = 4096
DEPTH = 2
GRID_W = 64
CTX_LEN = 256
EPS = 1e-6

S5_WIDTH = 256
S5_GROUP = 16
S5_NGROUPS = 16
S5_STATE = 64
S5_NSTATE = S5_NGROUPS * S5_STATE

NA_HEADS = 6
NA_HEAD_DIM = 64
NA_WIDTH = 384
NA_KH = 8
NA_KW = 16
RPB_W = 2 * NA_KW - 1

SSD_HEADS = 6
SSD_HEAD_DIM = 64
SSD_WIDTH = 384
SSD_NGROUPS = 2
SSD_STATE = 128
SSD_CONV = 5
SSD_BC = 256
SSD_XBC = 896

N_EXPERTS = 16
D_EXPERT = 1024
ROPE_BASE = 10000.0

T = CTX_LEN + SEQ
TILE = 256
NT = T // TILE
LANE = 128
SUB = 8
ROWS = SEQ // GRID_W
CAP_LAT = 2 * SEQ // N_EXPERTS
CAP_CTX = 2 * CTX_LEN // N_EXPERTS
NEG = -1e30

C_U = 0
C_Q = 256
C_K = 640
C_V = 1024
C_Z = 1408
C_XBC = 1792
C_DT = 2688
IN_COLS = 2700
IN_COLS_PAD = 2816

VMEM_LIMIT = 56 * 1024 * 1024


def _cp(*sem):
    return pltpu.CompilerParams(dimension_semantics=sem, vmem_limit_bytes=VMEM_LIMIT)


def _dot(a, b):
    return jnp.dot(a, b, preferred_element_type=F32)


def _dot_nt(a, b):
    return lax.dot_general(a, b, (((1,), (1,)), ((), ())), preferred_element_type=F32)


def _split3(x):
    hi = x.astype(BF16)
    r = x - hi.astype(F32)
    mid = r.astype(BF16)
    lo = (r - mid.astype(F32)).astype(BF16)
    return hi, mid, lo


def _dot_exact_rhs(a_bf16, b_f32):
    hi, mid, lo = _split3(b_f32)
    return _dot(a_bf16, hi) + _dot(a_bf16, mid) + _dot(a_bf16, lo)


def _dot_exact_lhs(a_f32, b_bf16):
    hi, mid, lo = _split3(a_f32)
    return _dot(hi, b_bf16) + _dot(mid, b_bf16) + _dot(lo, b_bf16)


def _dot_x3(a, b):
    ah = a.astype(BF16)
    al = (a - ah.astype(F32)).astype(BF16)
    bh = b.astype(BF16)
    bl = (b - bh.astype(F32)).astype(BF16)
    return _dot(ah, bh) + _dot(ah, bl) + _dot(al, bh)


def _silu(x):
    return x * jax.nn.sigmoid(x)


def _seg(t):
    return jnp.where(t >= CTX_LEN // TILE, 1, 0)


def _bwd_tile(i):
    return jnp.where(i == 0, 0, NT - i)


def _ada_kernel(c_ref, w_ref, b_ref, o_ref):
    s = _silu(c_ref[...])
    o_ref[0] = _dot_x3(s, w_ref[0]) + b_ref[0]


def ada_mod(cvec, w_ada, b_ada):
    nb = 1024
    return pl.pallas_call(
        _ada_kernel,
        grid=(DEPTH, 6 * D_MODEL // nb),
        in_specs=[pl.BlockSpec((SUB, D_MODEL), lambda l, j: (0, 0)),
                  pl.BlockSpec((1, D_MODEL, nb), lambda l, j: (l, 0, j)),
                  pl.BlockSpec((1, 1, nb), lambda l, j: (l, 0, j))],
        out_specs=pl.BlockSpec((1, SUB, nb), lambda l, j: (l, 0, j)),
        out_shape=jax.ShapeDtypeStruct((DEPTH, SUB, 6 * D_MODEL), F32),
        compiler_params=_cp("arbitrary", "arbitrary"),
        name="ada_mod",
    )(cvec, w_ada, b_ada.reshape(DEPTH, 1, 6 * D_MODEL))


def _modulated_norm(x, nw, shift, scale):
    y = x * lax.rsqrt(jnp.mean(x * x, axis=-1, keepdims=True) + EPS) * nw
    return y * (1.0 + scale) + shift


def _inproj_kernel(x_ref, mod_ref, nw_ref, w_ref, u_ref, q_ref, k_ref, v_ref, z_ref, xbc_ref, dt_ref):
    h = _modulated_norm(x_ref[0], nw_ref[...], mod_ref[0, 0, 0:1, :], mod_ref[0, 0, 1:2, :]).astype(BF16)

    def proj(lo, hi):
        return _dot(h, w_ref[:, lo:hi])

    u_ref[0] = proj(C_U, C_Q).astype(BF16)
    q_ref[0] = (proj(C_Q, C_K) * (NA_HEAD_DIM ** -0.5)).astype(BF16)
    k_ref[0] = proj(C_K, C_V).astype(BF16)
    v_ref[0] = proj(C_V, C_Z).astype(BF16)
    z_ref[0] = proj(C_Z, C_XBC).astype(BF16)
    xbc_ref[0] = proj(C_XBC, C_DT).astype(BF16)
    dt_ref[0] = proj(C_DT, IN_COLS_PAD)


def in_proj(x, mod, norm_w, w_in_p):
    tok = lambda w, dt: jax.ShapeDtypeStruct((BATCH, T, w), dt)
    tspec = lambda w: pl.BlockSpec((1, TILE, w), lambda b, t: (b, t, 0))
    return pl.pallas_call(
        _inproj_kernel,
        grid=(BATCH, NT),
        in_specs=[tspec(D_MODEL),
                  pl.BlockSpec((1, 1, 6, D_MODEL), lambda b, t: (b, _seg(t), 0, 0)),
                  pl.BlockSpec((1, D_MODEL), lambda b, t: (0, 0)),
                  pl.BlockSpec((D_MODEL, IN_COLS_PAD), lambda b, t: (0, 0))],
        out_specs=[tspec(S5_WIDTH), tspec(NA_WIDTH), tspec(NA_WIDTH), tspec(NA_WIDTH),
                   tspec(SSD_WIDTH), tspec(SSD_XBC), tspec(LANE)],
        out_shape=[tok(S5_WIDTH, BF16), tok(NA_WIDTH, BF16), tok(NA_WIDTH, BF16), tok(NA_WIDTH, BF16),
                   tok(SSD_WIDTH, BF16), tok(SSD_XBC, BF16), tok(LANE, F32)],
        compiler_params=_cp("arbitrary", "arbitrary"),
        name="in_proj",
    )(x, mod, norm_w.reshape(1, D_MODEL), w_in_p)


def _s5_kernel(uf_ref, ub_ref, bblk_ref, cblk_ref, mul_ref, yf_ref, yb_ref, st_ref, carry_ref):
    i = pl.program_id(1)
    n = S5_NSTATE

    @pl.when(i == 0)
    def _():
        carry_ref[...] = jnp.zeros_like(carry_ref)

    for d, (u_ref, y_ref) in enumerate(((uf_ref, yf_ref), (ub_ref, yb_ref))):
        st_ref[...] = _dot(u_ref[0], bblk_ref[d])
        nblk = TILE // SUB

        def body(j, carry, d=d):
            cr, ci = carry
            r = j if d == 0 else nblk - 1 - j
            row = pl.multiple_of(r * SUB, SUB)
            re = st_ref[pl.ds(row, SUB), 0:n]
            im = st_ref[pl.ds(row, SUB), n:2 * n]
            for kk, sh in enumerate((1, 2, 4)):
                mr = mul_ref[d, kk * SUB:(kk + 1) * SUB, 0:n]
                mi = mul_ref[d, kk * SUB:(kk + 1) * SUB, n:2 * n]
                s = sh if d == 0 else SUB - sh
                sr = pltpu.roll(re, s, 0)
                si = pltpu.roll(im, s, 0)
                re, im = re + (mr * sr - mi * si), im + (mr * si + mi * sr)
            pr = mul_ref[d, 3 * SUB:4 * SUB, 0:n]
            pi = mul_ref[d, 3 * SUB:4 * SUB, n:2 * n]
            re, im = re + (pr * cr - pi * ci), im + (pr * ci + pi * cr)
            st_ref[pl.ds(row, SUB), 0:n] = re
            st_ref[pl.ds(row, SUB), n:2 * n] = im
            last = SUB - 1 if d == 0 else 0
            return re[last:last + 1, :], im[last:last + 1, :]

        cr, ci = lax.fori_loop(0, nblk, body, (carry_ref[d, 0:1, 0:n], carry_ref[d, 0:1, n:2 * n]), unroll=2)
        carry_ref[d, 0:1, 0:n] = cr
        carry_ref[d, 0:1, n:2 * n] = ci
        y_ref[0] = _dot(st_ref[...].astype(BF16), cblk_ref[d])


def s5_scan(u, bblk, cblk, mul):
    uspec = lambda f: pl.BlockSpec((1, TILE, S5_WIDTH), f)
    fwd = lambda b, i: (b, i, 0)
    bwd = lambda b, i: (b, _bwd_tile(i), 0)
    whole = lambda shp: pl.BlockSpec(shp, lambda b, i: (0,) * len(shp))
    return pl.pallas_call(
        _s5_kernel,
        grid=(BATCH, NT),
        in_specs=[uspec(fwd), uspec(bwd), whole((2, S5_WIDTH, 2 * S5_NSTATE)),
                  whole((2, 2 * S5_NSTATE, S5_WIDTH)), whole((2, 4 * SUB, 2 * S5_NSTATE))],
        out_specs=[uspec(fwd), uspec(bwd)],
        out_shape=[jax.ShapeDtypeStruct((BATCH, T, S5_WIDTH), F32)] * 2,
        scratch_shapes=[pltpu.VMEM((TILE, 2 * S5_NSTATE), F32), pltpu.VMEM((2, SUB, 2 * S5_NSTATE), F32)],
        compiler_params=_cp("arbitrary", "arbitrary"),
        name="s5_scan",
    )(u, u, bblk, cblk, mul)


def s5_params(lam_re, lam_im, log_dt, b_re, b_im, c_re, c_im):
    G, P, C = S5_NGROUPS, S5_STATE, S5_GROUP
    lam = lax.complex(lam_re.astype(F32), lam_im.astype(F32))
    step = jnp.exp(log_dt.astype(F32))[..., None]
    lam_bar = jnp.exp(lam * step)
    b_bar = ((lam_bar - 1.0) / lam)[..., None] * lax.complex(b_re.astype(F32), b_im.astype(F32))
    eye = jnp.eye(G, dtype=F32)
    b_t = jnp.transpose(b_bar, (0, 1, 3, 2))

    def blockdiag_in(m):
        return jnp.einsum('dgcp,gh->dgchp', m, eye).reshape(2, G * C, G * P)

    bblk = jnp.concatenate([blockdiag_in(jnp.real(b_t)), blockdiag_in(jnp.imag(b_t))], axis=-1)
    c_t_re = jnp.transpose(c_re.astype(F32), (0, 1, 3, 2))
    c_t_im = jnp.transpose(c_im.astype(F32), (0, 1, 3, 2))

    def blockdiag_out(m):
        return jnp.einsum('dgpc,gh->dgphc', m, eye).reshape(2, G * P, G * C)

    cblk = jnp.concatenate([blockdiag_out(c_t_re), blockdiag_out(-c_t_im)], axis=1)
    rows = jnp.arange(SUB)
    pieces = []
    for d in range(2):
        log_lb = (lam[d] * step[d]).reshape(1, G * P)
        per_d = []
        for sh in (1, 2, 4):
            valid = (rows >= sh) if d == 0 else (rows < SUB - sh)
            per_d.append(jnp.where(valid[:, None], jnp.exp(log_lb * float(sh)), 0.0))
        expo = (rows + 1) if d == 0 else (SUB - rows)
        per_d.append(jnp.exp(log_lb * expo[:, None].astype(F32)))
        m = jnp.concatenate(per_d, axis=0)
        pieces.append(jnp.concatenate([jnp.real(m), jnp.imag(m)], axis=-1))
    mul = jnp.stack(pieces, axis=0).astype(F32)
    return bblk.astype(BF16), cblk.astype(BF16), mul


def _softmax_pv(parts):
    m = parts[0][0].max(axis=-1, keepdims=True)
    for s, _ in parts[1:]:
        m = jnp.maximum(m, s.max(axis=-1, keepdims=True))
    den = 0.0
    acc = 0.0
    for s, v in parts:
        p = jnp.exp(s - m)
        den = den + p.sum(axis=-1, keepdims=True)
        acc = acc + _dot(p.astype(BF16), v)
    return acc / den


def _na_kernel(q_ref, k_ref, v_ref, bias_ref, o_ref):
    t = pl.program_id(1)
    first = lax.broadcasted_iota(I32, (1, LANE), 1) < NA_HEAD_DIM
    rows_per_tile = TILE // GRID_W
    nband = NA_KH * GRID_W

    @pl.when(t == 0)
    def _():
        for pp in range(NA_HEADS // 2):
            ls = slice(pp * LANE, (pp + 1) * LANE)
            qp = q_ref[0, :, ls]
            kc = k_ref[0, 0:CTX_LEN, ls]
            vc = v_ref[0, 0:CTX_LEN, ls]
            outs = []
            for hh in range(2):
                qm = jnp.where(first if hh == 0 else ~first, qp, jnp.zeros_like(qp))
                outs.append(_softmax_pv([(_dot_nt(qm, kc), vc)]))
            o_ref[0, :, ls] = jnp.where(first, outs[0], outs[1]).astype(BF16)

    @pl.when(t > 0)
    def _():
        def row_body(rr, carry):
            r = (t - 1) * rows_per_tile + rr
            r0 = jnp.clip(r - NA_KH // 2, 0, ROWS - NA_KH)
            cfg = r0 - r + (NA_KH - 1)
            start = pl.multiple_of(CTX_LEN + r0 * GRID_W, GRID_W)
            qrow = pl.multiple_of(rr * GRID_W, GRID_W)
            for pp in range(NA_HEADS // 2):
                ls = slice(pp * LANE, (pp + 1) * LANE)
                qp = q_ref[0, pl.ds(qrow, GRID_W), ls]
                kb = k_ref[0, pl.ds(start, nband), ls]
                vb = v_ref[0, pl.ds(start, nband), ls]
                kc = k_ref[0, 0:CTX_LEN, ls]
                vc = v_ref[0, 0:CTX_LEN, ls]
                outs = []
                for hh in range(2):
                    qm = jnp.where(first if hh == 0 else ~first, qp, jnp.zeros_like(qp))
                    s_loc = _dot_nt(qm, kb) + bias_ref[2 * pp + hh, cfg]
                    outs.append(_softmax_pv([(s_loc, vb), (_dot_nt(qm, kc), vc)]))
                o_ref[0, pl.ds(qrow, GRID_W), ls] = jnp.where(first, outs[0], outs[1]).astype(BF16)
            return carry

        lax.fori_loop(0, rows_per_tile, row_body, 0)


def natten(q, k, v, bias):
    whole = pl.BlockSpec((1, T, NA_WIDTH), lambda b, t: (b, 0, 0))
    tile = pl.BlockSpec((1, TILE, NA_WIDTH), lambda b, t: (b, t, 0))
    return pl.pallas_call(
        _na_kernel,
        grid=(BATCH, NT),
        in_specs=[tile, whole, whole,
                  pl.BlockSpec((NA_HEADS, NA_KH, GRID_W, NA_KH * GRID_W), lambda b, t: (0, 0, 0, 0))],
        out_specs=tile,
        out_shape=jax.ShapeDtypeStruct((BATCH, T, NA_WIDTH), BF16),
        compiler_params=_cp("arbitrary", "arbitrary"),
        name="natten",
    )(q, k, v, bias)


def natten_bias(rpb):
    col = jnp.arange(GRID_W)
    c0 = jnp.clip(col - NA_KW // 2, 0, GRID_W - NA_KW)
    in_win = (col[None, :] >= c0[:, None]) & (col[None, :] < c0[:, None] + NA_KW)
    rel_c = jnp.clip(col[None, :] - col[:, None] + (NA_KW - 1), 0, RPB_W - 1)
    rel_r = jnp.arange(NA_KH)[:, None] + jnp.arange(NA_KH)[None, :]
    b = rpb.astype(F32)[:, rel_r[:, None, :, None], rel_c[None, :, None, :]]
    b = jnp.where(in_win[None, None, :, None, :], b, NEG)
    return b.reshape(NA_HEADS, NA_KH, GRID_W, NA_KH * GRID_W)


def _softplus(x):
    return jnp.maximum(x, 0.0) + jnp.log(1.0 + jnp.exp(-jnp.abs(x)))


def _ssd_prep_kernel(prev_ref, cur_ref, next_ref, dtr_ref, cw_ref, cb_ref, dtb_ref, ar_ref, cos_ref, sin_ref,
                     xs_ref, bm_ref, cm_ref, dt_ref, a_ref):
    t = pl.program_id(1)
    halo = prev_ref.shape[1]
    has_prev = t >= 2
    has_next = (t >= 1) & (t <= NT - 2)
    prev = jnp.where(has_prev, prev_ref[0].astype(F32), 0.0)
    nxt = jnp.where(has_next, next_ref[0].astype(F32), 0.0)
    ext = jnp.concatenate([prev, cur_ref[0].astype(F32), nxt], axis=0)
    n = ext.shape[0]
    acc = cb_ref[...] + cw_ref[SSD_CONV // 2:SSD_CONV // 2 + 1, :] * ext
    for kk in range(SSD_CONV):
        off = kk - SSD_CONV // 2
        if off != 0:
            acc = acc + cw_ref[kk:kk + 1, :] * pltpu.roll(ext, (-off) % n, 0)
    y = _silu(acc[halo:halo + TILE, :])
    xs_ref[0] = y[:, 0:SSD_WIDTH].astype(BF16)

    lane = lax.broadcasted_iota(I32, (1, LANE), 1)
    low = (lane & (NA_HEAD_DIM // 2)) == 0
    cos = cos_ref[...]
    sin = sin_ref[...]
    for g in range(2 * SSD_NGROUPS):
        v = y[:, SSD_WIDTH + g * LANE:SSD_WIDTH + (g + 1) * LANE]
        sw = jnp.where(low, pltpu.roll(v, LANE - 32, 1), pltpu.roll(v, 32, 1))
        rot = (v * cos + sw * sin).astype(BF16)
        if g < SSD_NGROUPS:
            bm_ref[0, :, g * LANE:(g + 1) * LANE] = rot
        else:
            cm_ref[0, :, (g - SSD_NGROUPS) * LANE:(g - SSD_NGROUPS + 1) * LANE] = rot

    dt = _softplus(dtr_ref[0] + dtb_ref[...])
    dt_ref[0] = dt
    a_ref[0] = dt * ar_ref[...]


def ssd_prep(xbc, dt_raw, conv_w, conv_b, dt_bias, a_log, cos_t, sin_t):
    halo = 16
    per = TILE // halo
    nhalo = T // halo
    tok = lambda w, dt: jax.ShapeDtypeStruct((BATCH, T, w), dt)
    tspec = lambda w: pl.BlockSpec((1, TILE, w), lambda b, t: (b, t, 0))
    row = lambda w: pl.BlockSpec((1, w), lambda b, t: (0, 0))
    cw = jnp.zeros((SUB, SSD_XBC), F32).at[:SSD_CONV].set(conv_w.astype(F32))
    pad12 = lambda v: jnp.zeros((1, LANE), F32).at[0, :2 * SSD_HEADS].set(v.astype(F32).reshape(-1))
    return pl.pallas_call(
        _ssd_prep_kernel,
        grid=(BATCH, NT),
        in_specs=[pl.BlockSpec((1, halo, SSD_XBC), lambda b, t: (b, jnp.maximum(t * per - 1, 0), 0)),
                  tspec(SSD_XBC),
                  pl.BlockSpec((1, halo, SSD_XBC), lambda b, t: (b, jnp.minimum((t + 1) * per, nhalo - 1), 0)),
                  tspec(LANE),
                  pl.BlockSpec((SUB, SSD_XBC), lambda b, t: (0, 0)), row(SSD_XBC), row(LANE), row(LANE),
                  pl.BlockSpec((TILE, LANE), lambda b, t: (t, 0)), pl.BlockSpec((TILE, LANE), lambda b, t: (t, 0))],
        out_specs=[tspec(SSD_WIDTH), tspec(SSD_BC), tspec(SSD_BC), tspec(LANE), tspec(LANE)],
        out_shape=[tok(SSD_WIDTH, BF16), tok(SSD_BC, BF16), tok(SSD_BC, BF16), tok(LANE, F32), tok(LANE, F32)],
        compiler_params=_cp("arbitrary", "arbitrary"),
        name="ssd_prep",
    )(xbc, xbc, xbc, dt_raw, cw, conv_b.astype(F32).reshape(1, SSD_XBC), pad12(dt_bias),
      pad12(-jnp.exp(a_log.astype(F32))), cos_t, sin_t)


def rope_tables():
    half = SSD_STATE // 2
    nf = half // 2
    pos = jnp.arange(SEQ)
    inv_freq = ROPE_BASE ** (-jnp.arange(nf, dtype=F32) / nf)
    lane = jnp.arange(LANE)
    p = jnp.where(lane[None, :] < half, (pos // GRID_W)[:, None], (pos % GRID_W)[:, None]).astype(F32)
    ang = p * inv_freq[lane % nf][None, :]
    sign = jnp.where((lane & nf) == 0, -1.0, 1.0)[None, :]
    cos_t = jnp.concatenate([jnp.ones((CTX_LEN, LANE), F32), jnp.cos(ang)], axis=0)
    sin_t = jnp.concatenate([jnp.zeros((CTX_LEN, LANE), F32), jnp.sin(ang) * sign], axis=0)
    return cos_t, sin_t


def _ssd_dir(d, xs_ref, bm_ref, cm_ref, bt_ref, dt_ref, a_ref, at_ref, tri_ref, y_ref, st_ref):
    q = TILE
    lane = lax.broadcasted_iota(I32, (1, LANE), 1)
    first = lane < SSD_HEAD_DIM
    ri = lax.broadcasted_iota(I32, (q, q), 0)
    ci = lax.broadcasted_iota(I32, (q, q), 1)
    keep = (ci <= ri) if d == 0 else (ci >= ri)
    end = q - 1 if d == 0 else 0
    tri_col = tri_ref[d]
    tri_row = tri_ref[1 - d]
    a = a_ref[0]
    dt = dt_ref[0]
    cs_col = _dot_exact_rhs(tri_col, a)
    cs_row = _dot_exact_lhs(at_ref[0], tri_row)
    g_mats = [_dot_nt(cm_ref[0, :, g * SSD_STATE:(g + 1) * SSD_STATE],
                      bm_ref[0, :, g * SSD_STATE:(g + 1) * SSD_STATE]) for g in range(SSD_NGROUPS)]

    def head_col(m, h):
        c = d * SSD_HEADS + h
        return m[:, c:c + 1]

    for pp in range(SSD_HEADS // 2):
        ls = slice(pp * LANE, (pp + 1) * LANE)
        h0, h1 = 2 * pp, 2 * pp + 1
        x = xs_ref[0, :, ls].astype(F32)
        dt_l = jnp.where(first, head_col(dt, h0), head_col(dt, h1))
        cs_l = jnp.where(first, head_col(cs_col, h0), head_col(cs_col, h1))
        cs_end = cs_l[end:end + 1, :]
        xdt = x * dt_l
        xdt_b = xdt.astype(BF16)
        xw = (xdt * jnp.exp(cs_end - cs_l)).astype(BF16)
        st = st_ref[d, pp]
        st_b = st.astype(BF16)
        ys, ups = [], []
        for h in (h0, h1):
            g = h // (SSD_HEADS // SSD_NGROUPS)
            c = d * SSD_HEADS + h
            diff = head_col(cs_col, h) - cs_row[c:c + 1, :]
            decay = jnp.where(keep, jnp.exp(jnp.where(keep, diff, 0.0)), 0.0)
            m = (g_mats[g] * decay).astype(BF16)
            y_h = _dot(m, xdt_b) + _dot(cm_ref[0, :, g * SSD_STATE:(g + 1) * SSD_STATE], st_b) * jnp.exp(cs_l)
            ys.append(y_h)
            ups.append(_dot(bt_ref[0, g * SSD_STATE:(g + 1) * SSD_STATE, :], xw))
        y_ref[0, :, ls] = jnp.where(first, ys[0], ys[1])
        st_ref[d, pp] = jnp.exp(cs_end) * st + jnp.where(first, ups[0], ups[1])


def _ssd_scan_kernel(xs_f, bm_f, cm_f, bt_f, dt_f, a_f, at_f, xs_b, bm_b, cm_b, bt_b, dt_b, a_b, at_b, tri_ref,
                     yf_ref, yb_ref, st_ref):
    @pl.when(pl.program_id(1) == 0)
    def _():
        st_ref[...] = jnp.zeros_like(st_ref)

    _ssd_dir(0, xs_f, bm_f, cm_f, bt_f, dt_f, a_f, at_f, tri_ref, yf_ref, st_ref)
    _ssd_dir(1, xs_b, bm_b, cm_b, bt_b, dt_b, a_b, at_b, tri_ref, yb_ref, st_ref)


def ssd_scan(xs, bm, cm, dt, a):
    bt = jnp.swapaxes(bm, 1, 2)
    at = jnp.swapaxes(a[:, :, :2 * SUB], 1, 2)
    idx = jnp.arange(TILE)
    tri = jnp.stack([idx[None, :] <= idx[:, None], idx[None, :] >= idx[:, None]]).astype(BF16)
    fwd = lambda b, i: (b, i, 0)
    bwd = lambda b, i: (b, _bwd_tile(i), 0)
    fwd_t = lambda b, i: (b, 0, i)
    bwd_t = lambda b, i: (b, 0, _bwd_tile(i))

    def specs(f, ft):
        return [pl.BlockSpec((1, TILE, SSD_WIDTH), f), pl.BlockSpec((1, TILE, SSD_BC), f),
                pl.BlockSpec((1, TILE, SSD_BC), f), pl.BlockSpec((1, SSD_BC, TILE), ft),
                pl.BlockSpec((1, TILE, LANE), f), pl.BlockSpec((1, TILE, LANE), f),
                pl.BlockSpec((1, 2 * SUB, TILE), ft)]

    args = (xs, bm, cm, bt, dt, a, at)
    return pl.pallas_call(
        _ssd_scan_kernel,
        grid=(BATCH, NT),
        in_specs=specs(fwd, fwd_t) + specs(bwd, bwd_t) + [pl.BlockSpec((2, TILE, TILE), lambda b, i: (0, 0, 0))],
        out_specs=[pl.BlockSpec((1, TILE, SSD_WIDTH), fwd), pl.BlockSpec((1, TILE, SSD_WIDTH), bwd)],
        out_shape=[jax.ShapeDtypeStruct((BATCH, T, SSD_WIDTH), F32)] * 2,
        scratch_shapes=[pltpu.VMEM((2, SSD_HEADS // 2, SSD_STATE, LANE), F32)],
        compiler_params=_cp("arbitrary", "arbitrary"),
        name="ssd_scan",
    )(*args, *args, tri)


def _gelu_tanh(x):
    return 0.5 * x * (1.0 + jnp.tanh(math.sqrt(2.0 / math.pi) * (x + 0.044715 * (x * x * x))))


def _post_kernel(x_ref, mod_ref, u_ref, s5f_ref, s5b_ref, na_ref, xs_ref, z_ref, sdf_ref, sdb_ref,
                 s5d_ref, gw_ref, gb_ref, sdd_ref, snw_ref, wo_ref, n2w_ref, rt_ref,
                 x1_ref, h_ref, lg_ref):
    ys5 = u_ref[0].astype(F32) * s5d_ref[...] + s5f_ref[0] + s5b_ref[0]
    g = _gelu_tanh(ys5)
    s5o = g * jax.nn.sigmoid(_dot(g.astype(BF16), gw_ref[...]) + gb_ref[...])
    yssd = (xs_ref[0].astype(F32) * sdd_ref[...] + sdf_ref[0] + sdb_ref[0]) * _silu(z_ref[0].astype(F32))
    ssdo = yssd * lax.rsqrt(jnp.mean(yssd * yssd, axis=-1, keepdims=True) + EPS) * snw_ref[...]
    mix = jnp.concatenate([s5o.astype(BF16), na_ref[0], ssdo.astype(BF16)], axis=-1)
    x1 = x_ref[0] + mod_ref[0, 0, 2:3, :] * _dot(mix, wo_ref[...])
    x1_ref[0] = x1
    h = _modulated_norm(x1, n2w_ref[...], mod_ref[0, 0, 3:4, :], mod_ref[0, 0, 4:5, :])
    h_ref[0] = h.astype(BF16)
    lg_ref[0] = _dot_x3(h, rt_ref[...])


def post_mixer(x, mod, u, s5f, s5b, na, xs, z, sdf, sdb, s5_d, glu_w, glu_b, ssd_d, ssd_norm_w, w_out, norm2_w,
               router):
    tspec = lambda w: pl.BlockSpec((1, TILE, w), lambda b, t: (b, t, 0))
    whole = lambda *shp: pl.BlockSpec(shp, lambda b, t: (0,) * len(shp))
    rt = jnp.zeros((D_MODEL, LANE), F32).at[:, :N_EXPERTS].set(router.astype(F32))
    return pl.pallas_call(
        _post_kernel,
        grid=(BATCH, NT),
        in_specs=[tspec(D_MODEL), pl.BlockSpec((1, 1, 6, D_MODEL), lambda b, t: (b, _seg(t), 0, 0)),
                  tspec(S5_WIDTH), tspec(S5_WIDTH), tspec(S5_WIDTH), tspec(NA_WIDTH),
                  tspec(SSD_WIDTH), tspec(SSD_WIDTH), tspec(SSD_WIDTH), tspec(SSD_WIDTH),
                  whole(1, S5_WIDTH), whole(S5_WIDTH, S5_WIDTH), whole(1, S5_WIDTH),
                  whole(1, SSD_WIDTH), whole(1, SSD_WIDTH), whole(D_MODEL, D_MODEL), whole(1, D_MODEL),
                  whole(D_MODEL, LANE)],
        out_specs=[tspec(D_MODEL), tspec(D_MODEL), tspec(LANE)],
        out_shape=[jax.ShapeDtypeStruct((BATCH, T, D_MODEL), F32), jax.ShapeDtypeStruct((BATCH, T, D_MODEL), BF16),
                   jax.ShapeDtypeStruct((BATCH, T, LANE), F32)],
        compiler_params=_cp("arbitrary", "arbitrary"),
        name="post_mixer",
    )(x, mod, u, s5f, s5b, na, xs, z, sdf, sdb,
      s5_d.astype(F32).reshape(1, S5_WIDTH), glu_w.astype(BF16), glu_b.astype(F32).reshape(1, S5_WIDTH),
      jnp.repeat(ssd_d.astype(F32), SSD_HEAD_DIM).reshape(1, SSD_WIDTH), ssd_norm_w.astype(F32).reshape(1, SSD_WIDTH),
      w_out.astype(BF16), norm2_w.astype(F32).reshape(1, D_MODEL), rt)


def _route_kernel(lg_ref, tri_ref, slot_ref, aff_ref, *, with_ctx):
    lg = lg_ref[0]
    m = lg.max(axis=0, keepdims=True)
    e = jnp.exp(lg - m)
    aff = e / e.sum(axis=0, keepdims=True)
    aff_ref[0] = aff
    bits = pltpu.bitcast(aff, I32)
    is_ctx = lax.broadcasted_iota(I32, (N_EXPERTS, T), 1) < CTX_LEN

    def count(mask):
        return jnp.where(mask, 1.0, 0.0).sum(axis=1, keepdims=True)

    def kth_largest(seg, k):
        def body(i, prefix):
            cand = prefix | lax.shift_left(jnp.int32(1), 30 - i)
            return jnp.where(count((bits >= cand) & seg) >= k, cand, prefix)
        return lax.fori_loop(0, 31, body, jnp.zeros((N_EXPERTS, 1), I32))

    def excl_cumsum(x01):
        carry = jnp.zeros((N_EXPERTS, 1), F32)
        pieces = []
        for j in range(T // LANE):
            blk = x01[:, j * LANE:(j + 1) * LANE]
            inc = _dot(blk.astype(BF16), tri_ref[...])
            pieces.append(inc - blk + carry)
            carry = carry + inc[:, LANE - 1:LANE]
        return jnp.concatenate(pieces, axis=1)

    thr = kth_largest(~is_ctx, float(CAP_LAT))
    k_of = jnp.full((N_EXPERTS, T), float(CAP_LAT), F32)
    if with_ctx:
        thr = jnp.where(is_ctx, kth_largest(is_ctx, float(CAP_CTX)), thr)
        k_of = jnp.where(is_ctx, float(CAP_CTX), k_of)
    gt = bits > thr
    eq = bits == thr
    if not with_ctx:
        gt = gt & ~is_ctx
        eq = eq & ~is_ctx
    n_gt = jnp.where(is_ctx, count(gt & is_ctx), count(gt & ~is_ctx))
    tie_rank = excl_cumsum(jnp.where(eq, 1.0, 0.0))
    tie_rank = tie_rank - jnp.where(is_ctx, 0.0, count(eq & is_ctx))
    sel = gt | (eq & (tie_rank < k_of - n_gt))
    pos = excl_cumsum(jnp.where(sel, 1.0, 0.0))
    slot = jnp.where(is_ctx, pos + float(CAP_LAT), pos - count(sel & is_ctx))
    slot_ref[0] = jnp.where(sel, slot, -1.0).astype(I32)


def route(logits_t, with_ctx):
    idx = jnp.arange(LANE)
    tri = (idx[:, None] <= idx[None, :]).astype(BF16)
    spec = pl.BlockSpec((1, N_EXPERTS, T), lambda b: (b, 0, 0))
    return pl.pallas_call(
        functools.partial(_route_kernel, with_ctx=with_ctx),
        grid=(BATCH,),
        in_specs=[spec, pl.BlockSpec((LANE, LANE), lambda b: (0, 0))],
        out_specs=[spec, spec],
        out_shape=[jax.ShapeDtypeStruct((BATCH, N_EXPERTS, T), I32), jax.ShapeDtypeStruct((BATCH, N_EXPERTS, T), F32)],
        compiler_params=_cp("arbitrary"),
        name="route",
    )(logits_t, tri)


def _ffn_kernel(h_ref, slot_ref, wg_ref, wu_ref, wd_ref, y_ref, xs_ref, *, nslot):
    sid = lax.broadcasted_iota(I32, (nslot, TILE), 0)
    xs_ref[...] = jnp.zeros_like(xs_ref)
    t0 = 0 if nslot > CAP_LAT else CTX_LEN // TILE
    for t in range(t0, NT):
        onehot = jnp.where(sid == slot_ref[0, 0, :, t * TILE:(t + 1) * TILE], 1.0, 0.0).astype(BF16)
        xs_ref[...] += _dot(onehot, h_ref[0, t * TILE:(t + 1) * TILE, :])
    xs = xs_ref[...].astype(BF16)
    hid = _silu(_dot(xs, wg_ref[0, 0])) * _dot(xs, wu_ref[0, 0])
    y_ref[0, 0] = _dot(hid.astype(BF16), wd_ref[0, 0]).astype(BF16)


def moe_ffn(h, slot, layer, wg, wu, wd, nslot):
    wspec = lambda: pl.BlockSpec((1, 1, D_MODEL, D_EXPERT), lambda b, e: (layer, e, 0, 0))
    return pl.pallas_call(
        functools.partial(_ffn_kernel, nslot=nslot),
        grid=(BATCH, N_EXPERTS),
        in_specs=[pl.BlockSpec((1, T, D_MODEL), lambda b, e: (b, 0, 0)),
                  pl.BlockSpec((1, 1, 1, T), lambda b, e: (b, e, 0, 0)),
                  wspec(), wspec(), wspec()],
        out_specs=pl.BlockSpec((1, 1, nslot, D_MODEL), lambda b, e: (b, e, 0, 0)),
        out_shape=jax.ShapeDtypeStruct((BATCH, N_EXPERTS, nslot, D_MODEL), BF16),
        scratch_shapes=[pltpu.VMEM((nslot, D_MODEL), F32)],
        compiler_params=_cp("arbitrary", "arbitrary"),
        name="moe_ffn",
    )(h, slot.reshape(BATCH, N_EXPERTS, 1, T), wg, wu, wd)


def _combine_kernel(x_ref, mod_ref, slot_ref, aff_ref, y_ref, o_ref, *, nslot, final_w):
    t = pl.program_id(2)

    def gathered(lo, n):
        lane = lax.broadcasted_iota(I32, (TILE, n), 1) + lo
        acc = jnp.zeros((TILE, y_ref.shape[-1]), F32)
        for e in range(N_EXPERTS):
            w = jnp.where(slot_ref[0, :, e:e + 1] == lane, aff_ref[0, :, e:e + 1], 0.0).astype(BF16)
            acc = acc + _dot(w, y_ref[0, e, lo:lo + n, :])
        return acc

    def finish(acc):
        o_ref[0] = x_ref[0] + mod_ref[0, 0, 5:6, :] * acc

    if nslot > CAP_LAT:
        @pl.when(t == 0)
        def _():
            finish(gathered(CAP_LAT, nslot - CAP_LAT))

        @pl.when(t > 0)
        def _():
            finish(gathered(0, CAP_LAT))
    else:
        @pl.when(t == 0)
        def _():
            o_ref[0] = x_ref[0]

        @pl.when(t > 0)
        def _():
            finish(gathered(0, CAP_LAT))


def moe_combine(x1, mod, slot_tok, aff_tok, y, nslot):
    nh = 2
    dh = D_MODEL // nh
    return pl.pallas_call(
        functools.partial(_combine_kernel, nslot=nslot, final_w=None),
        grid=(BATCH, nh, NT),
        in_specs=[pl.BlockSpec((1, TILE, dh), lambda b, c, t: (b, t, c)),
                  pl.BlockSpec((1, 1, 6, dh), lambda b, c, t: (b, _seg(t), 0, c)),
                  pl.BlockSpec((1, TILE, N_EXPERTS), lambda b, c, t: (b, t, 0)),
                  pl.BlockSpec((1, TILE, N_EXPERTS), lambda b, c, t: (b, t, 0)),
                  pl.BlockSpec((1, N_EXPERTS, nslot, dh), lambda b, c, t: (b, 0, 0, c))],
        out_specs=pl.BlockSpec((1, TILE, dh), lambda b, c, t: (b, t, c)),
        out_shape=jax.ShapeDtypeStruct((BATCH, T, D_MODEL), F32),
        compiler_params=_cp("arbitrary", "arbitrary", "arbitrary"),
        name="moe_combine",
    )(x1, mod, slot_tok, aff_tok, y)


def _final_kernel(x_ref, w_ref, o_ref):
    x = x_ref[0]
    o_ref[0] = x * lax.rsqrt(jnp.mean(x * x, axis=-1, keepdims=True) + EPS) * w_ref[...]


def final_norm(x, w):
    off = CTX_LEN // TILE
    return pl.pallas_call(
        _final_kernel,
        grid=(BATCH, SEQ // TILE),
        in_specs=[pl.BlockSpec((1, TILE, D_MODEL), lambda b, t: (b, t + off, 0)),
                  pl.BlockSpec((1, D_MODEL), lambda b, t: (0, 0))],
        out_specs=pl.BlockSpec((1, TILE, D_MODEL), lambda b, t: (b, t, 0)),
        out_shape=jax.ShapeDtypeStruct((BATCH, SEQ, D_MODEL), F32),
        compiler_params=_cp("arbitrary", "arbitrary"),
        name="final_norm",
    )(x, w.astype(F32).reshape(1, D_MODEL))


def trunk_layer(l, x, mod, cos_t, sin_t, norm1_w, norm2_w, w_in, w_out,
                s5_lam_re, s5_lam_im, s5_log_dt, s5_b_re, s5_b_im, s5_c_re, s5_c_im, s5_d, s5_glu_w, s5_glu_b,
                na_rpb, ssd_conv_w, ssd_conv_b, ssd_dt_bias, ssd_a_log, ssd_d, ssd_norm_w,
                moe_router, wg, wu, wd, with_ctx_out):
    w_in_p = jnp.zeros((D_MODEL, IN_COLS_PAD), BF16).at[:, :IN_COLS].set(w_in.astype(BF16))
    u, q, k, v, z, xbc, dt_raw = in_proj(x, mod, norm1_w, w_in_p)

    bblk, cblk, mul = s5_params(s5_lam_re, s5_lam_im, s5_log_dt, s5_b_re, s5_b_im, s5_c_re, s5_c_im)
    s5f, s5b = s5_scan(u, bblk, cblk, mul)
    na = natten(q, k, v, natten_bias(na_rpb))
    xs, bm, cm, dt, a = ssd_prep(xbc, dt_raw, ssd_conv_w, ssd_conv_b, ssd_dt_bias, ssd_a_log, cos_t, sin_t)
    sdf, sdb = ssd_scan(xs, bm, cm, dt, a)

    x1, h, logits = post_mixer(x, mod, u, s5f, s5b, na, xs, z, sdf, sdb, s5_d, s5_glu_w, s5_glu_b,
                               ssd_d, ssd_norm_w, w_out, norm2_w, moe_router)
    slot, aff = route(jnp.swapaxes(logits[:, :, :N_EXPERTS], 1, 2), with_ctx_out)
    nslot = CAP_LAT + CAP_CTX if with_ctx_out else CAP_LAT
    y = moe_ffn(h, slot, l, wg, wu, wd, nslot)
    return moe_combine(x1, mod, jnp.swapaxes(slot, 1, 2), jnp.swapaxes(aff, 1, 2), y, nslot)


def kernel(x, c, ctx, c_ctx, w_ada, b_ada, norm1_w, norm2_w, w_in, w_out, s5_lam_re, s5_lam_im, s5_log_dt, s5_b_re, s5_b_im, s5_c_re, s5_c_im, s5_d, s5_glu_w, s5_glu_b, na_rpb, ssd_conv_w, ssd_conv_b, ssd_dt_bias, ssd_a_log, ssd_d, ssd_norm_w, moe_router, moe_w_gate, moe_w_up, moe_w_down, final_norm_w):
    xa = jnp.concatenate([ctx, x], axis=1).astype(F32)
    cvec = jnp.zeros((SUB, D_MODEL), F32).at[0].set(c_ctx.astype(F32)).at[1:1 + BATCH].set(c.astype(F32))
    mods = ada_mod(cvec, w_ada.astype(F32), b_ada.astype(F32)).reshape(DEPTH, SUB, 6, D_MODEL)
    cos_t, sin_t = rope_tables()
    wg, wu, wd = moe_w_gate.astype(BF16), moe_w_up.astype(BF16), moe_w_down.astype(BF16)
    for l in range(DEPTH):
        mod = jnp.stack([jnp.broadcast_to(mods[l, 0], (BATCH, 6, D_MODEL)), mods[l, 1:1 + BATCH]], axis=1)
        xa = trunk_layer(
            l, xa, mod, cos_t, sin_t, norm1_w[l], norm2_w[l], w_in[l], w_out[l],
            s5_lam_re[l], s5_lam_im[l], s5_log_dt[l], s5_b_re[l], s5_b_im[l], s5_c_re[l], s5_c_im[l],
            s5_d[l], s5_glu_w[l], s5_glu_b[l],
            na_rpb[l], ssd_conv_w[l], ssd_conv_b[l], ssd_dt_bias[l], ssd_a_log[l], ssd_d[l], ssd_norm_w[l],
            moe_router[l], wg, wu, wd, l < DEPTH - 1)
    return final_norm(xa, final_norm_w)
```

```python
import functools
import math

import jax
import jax.numpy as jnp
from jax import lax
from jax.experimental import pallas as pl
from jax.experimental.pallas import tpu as pltpu

F32 = jnp.float32
BF16 = jnp.bfloat16
I32 = jnp.int32

D_MODEL = 1024
BATCH = 4
SEQ = 4096
DEPTH = 2
GRID_W = 64
CTX_LEN = 256
EPS = 1e-6

S5_WIDTH = 256
S5_GROUP = 16
S5_NGROUPS = 16
S5_STATE = 64
S5_NSTATE = S5_NGROUPS * S5_STATE

NA_HEADS = 6
NA_HEAD_DIM = 64
NA_WIDTH = 384
NA_KH = 8
NA_KW = 16
NA_BAND = 12
RPB_W = 2 * NA_KW - 1

SSD_HEADS = 6
SSD_HEAD_DIM = 64
SSD_WIDTH = 384
SSD_NGROUPS = 2
SSD_STATE = 128
SSD_CONV = 5
SSD_BC = 256
SSD_XBC = 896

N_EXPERTS = 16
D_EXPERT = 1024
ROPE_BASE = 10000.0

T = CTX_LEN + SEQ
TILE = 256
NT = T // TILE
LANE = 128
SUB = 8
ROWS = SEQ // GRID_W
CAP_LAT = 2 * SEQ // N_EXPERTS
CAP_CTX = 2 * CTX_LEN // N_EXPERTS
NEG = -1e30

C_U = 0
C_Q = 256
C_K = 640
C_V = 1024
C_Z = 1408
C_XBC = 1792
C_DT = 2688
IN_COLS = 2700
IN_COLS_PAD = 2816

VMEM_LIMIT = 56 * 1024 * 1024


def _cp(*sem):
    return pltpu.CompilerParams(dimension_semantics=sem, vmem_limit_bytes=VMEM_LIMIT)


def _dot(a, b):
    return jnp.dot(a, b, preferred_element_type=F32)


def _dot_nt(a, b):
    return lax.dot_general(a, b, (((1,), (1,)), ((), ())), preferred_element_type=F32)


def _split3(x):
    hi = x.astype(BF16)
    r = x - hi.astype(F32)
    mid = r.astype(BF16)
    lo = (r - mid.astype(F32)).astype(BF16)
    return hi, mid, lo


def _dot_exact_rhs(a_bf16, b_f32):
    hi, mid, lo = _split3(b_f32)
    return _dot(a_bf16, hi) + _dot(a_bf16, mid) + _dot(a_bf16, lo)


def _dot_exact_lhs(a_f32, b_bf16):
    hi, mid, lo = _split3(a_f32)
    return _dot(hi, b_bf16) + _dot(mid, b_bf16) + _dot(lo, b_bf16)


def _dot_x3(a, b):
    ah = a.astype(BF16)
    al = (a - ah.astype(F32)).astype(BF16)
    bh = b.astype(BF16)
    bl = (b - bh.astype(F32)).astype(BF16)
    return _dot(ah, bh) + _dot(ah, bl) + _dot(al, bh)


def _silu(x):
    return x * jax.nn.sigmoid(x)


def _seg(t):
    return jnp.where(t >= CTX_LEN // TILE, 1, 0)


def _bwd_tile(i):
    return jnp.where(i == 0, 0, NT - i)


def _ada_kernel(c_ref, w_ref, b_ref, o_ref):
    s = _silu(c_ref[...])
    o_ref[0] = _dot_x3(s, w_ref[0]) + b_ref[0]


def ada_mod(cvec, w_ada, b_ada):
    nb = 1024
    return pl.pallas_call(
        _ada_kernel,
        grid=(DEPTH, 6 * D_MODEL // nb),
        in_specs=[pl.BlockSpec((SUB, D_MODEL), lambda l, j: (0, 0)),
                  pl.BlockSpec((1, D_MODEL, nb), lambda l, j: (l, 0, j)),
                  pl.BlockSpec((1, 1, nb), lambda l, j: (l, 0, j))],
        out_specs=pl.BlockSpec((1, SUB, nb), lambda l, j: (l, 0, j)),
        out_shape=jax.ShapeDtypeStruct((DEPTH, SUB, 6 * D_MODEL), F32),
        compiler_params=_cp("arbitrary", "arbitrary"),
        name="ada_mod",
    )(cvec, w_ada, b_ada.reshape(DEPTH, 1, 6 * D_MODEL))


def _modulated_norm(x, nw, shift, scale):
    y = x * lax.rsqrt(jnp.mean(x * x, axis=-1, keepdims=True) + EPS) * nw
    return y * (1.0 + scale) + shift


def _inproj_kernel(x_ref, mod_ref, nw_ref, w_ref, u_ref, q_ref, k_ref, v_ref, z_ref, xbc_ref, dt_ref):
    h = _modulated_norm(x_ref[0], nw_ref[...], mod_ref[0, 0, 0:1, :], mod_ref[0, 0, 1:2, :]).astype(BF16)

    def proj(lo, hi):
        return _dot(h, w_ref[:, lo:hi])

    u_ref[0] = proj(C_U, C_Q).astype(BF16)
    q_ref[0] = (proj(C_Q, C_K) * (NA_HEAD_DIM ** -0.5)).astype(BF16)
    k_ref[0] = proj(C_K, C_V).astype(BF16)
    v_ref[0] = proj(C_V, C_Z).astype(BF16)
    z_ref[0] = proj(C_Z, C_XBC).astype(BF16)
    xbc_ref[0] = proj(C_XBC, C_DT).astype(BF16)
    dt_ref[0] = proj(C_DT, IN_COLS_PAD)


def in_proj(x, mod, norm_w, w_in_p):
    tok = lambda w, dt: jax.ShapeDtypeStruct((BATCH, T, w), dt)
    tspec = lambda w: pl.BlockSpec((1, TILE, w), lambda b, t: (b, t, 0))
    return pl.pallas_call(
        _inproj_kernel,
        grid=(BATCH, NT),
        in_specs=[tspec(D_MODEL),
                  pl.BlockSpec((1, 1, 6, D_MODEL), lambda b, t: (b, _seg(t), 0, 0)),
                  pl.BlockSpec((1, D_MODEL), lambda b, t: (0, 0)),
                  pl.BlockSpec((D_MODEL, IN_COLS_PAD), lambda b, t: (0, 0))],
        out_specs=[tspec(S5_WIDTH), tspec(NA_WIDTH), tspec(NA_WIDTH), tspec(NA_WIDTH),
                   tspec(SSD_WIDTH), tspec(SSD_XBC), tspec(LANE)],
        out_shape=[tok(S5_WIDTH, BF16), tok(NA_WIDTH, BF16), tok(NA_WIDTH, BF16), tok(NA_WIDTH, BF16),
                   tok(SSD_WIDTH, BF16), tok(SSD_XBC, BF16), tok(LANE, F32)],
        compiler_params=_cp("arbitrary", "arbitrary"),
        name="in_proj",
    )(x, mod, norm_w.reshape(1, D_MODEL), w_in_p)


def _s5_kernel(uf_ref, ub_ref, bblk_ref, cblk_ref, mul_ref, yf_ref, yb_ref, st_ref, carry_ref):
    i = pl.program_id(1)
    n = S5_NSTATE

    @pl.when(i == 0)
    def _():
        carry_ref[...] = jnp.zeros_like(carry_ref)

    for d, (u_ref, y_ref) in enumerate(((uf_ref, yf_ref), (ub_ref, yb_ref))):
        st_ref[...] = _dot(u_ref[0], bblk_ref[d])
        nblk = TILE // SUB

        def body(j, carry, d=d):
            cr, ci = carry
            r = j if d == 0 else nblk - 1 - j
            row = pl.multiple_of(r * SUB, SUB)
            re = st_ref[pl.ds(row, SUB), 0:n]
            im = st_ref[pl.ds(row, SUB), n:2 * n]
            for kk, sh in enumerate((1, 2, 4)):
                mr = mul_ref[d, kk * SUB:(kk + 1) * SUB, 0:n]
                mi = mul_ref[d, kk * SUB:(kk + 1) * SUB, n:2 * n]
                s = sh if d == 0 else SUB - sh
                sr = pltpu.roll(re, s, 0)
                si = pltpu.roll(im, s, 0)
                re, im = re + (mr * sr - mi * si), im + (mr * si + mi * sr)
            pr = mul_ref[d, 3 * SUB:4 * SUB, 0:n]
            pi = mul_ref[d, 3 * SUB:4 * SUB, n:2 * n]
            re, im = re + (pr * cr - pi * ci), im + (pr * ci + pi * cr)
            st_ref[pl.ds(row, SUB), 0:n] = re
            st_ref[pl.ds(row, SUB), n:2 * n] = im
            last = SUB - 1 if d == 0 else 0
            return re[last:last + 1, :], im[last:last + 1, :]

        cr, ci = lax.fori_loop(0, nblk, body, (carry_ref[d, 0:1, 0:n], carry_ref[d, 0:1, n:2 * n]), unroll=2)
        carry_ref[d, 0:1, 0:n] = cr
        carry_ref[d, 0:1, n:2 * n] = ci
        y_ref[0] = _dot(st_ref[...].astype(BF16), cblk_ref[d])


def s5_scan(u, bblk, cblk, mul):
    uspec = lambda f: pl.BlockSpec((1, TILE, S5_WIDTH), f)
    fwd = lambda b, i: (b, i, 0)
    bwd = lambda b, i: (b, _bwd_tile(i), 0)
    whole = lambda shp: pl.BlockSpec(shp, lambda b, i: (0,) * len(shp))
    return pl.pallas_call(
        _s5_kernel,
        grid=(BATCH, NT),
        in_specs=[uspec(fwd), uspec(bwd), whole((2, S5_WIDTH, 2 * S5_NSTATE)),
                  whole((2, 2 * S5_NSTATE, S5_WIDTH)), whole((2, 4 * SUB, 2 * S5_NSTATE))],
        out_specs=[uspec(fwd), uspec(bwd)],
        out_shape=[jax.ShapeDtypeStruct((BATCH, T, S5_WIDTH), F32)] * 2,
        scratch_shapes=[pltpu.VMEM((TILE, 2 * S5_NSTATE), F32), pltpu.VMEM((2, SUB, 2 * S5_NSTATE), F32)],
        compiler_params=_cp("arbitrary", "arbitrary"),
        name="s5_scan",
    )(u, u, bblk, cblk, mul)


def s5_params(lam_re, lam_im, log_dt, b_re, b_im, c_re, c_im):
    G, P, C = S5_NGROUPS, S5_STATE, S5_GROUP
    lam = lax.complex(lam_re.astype(F32), lam_im.astype(F32))
    step = jnp.exp(log_dt.astype(F32))[..., None]
    lam_bar = jnp.exp(lam * step)
    b_bar = ((lam_bar - 1.0) / lam)[..., None] * lax.complex(b_re.astype(F32), b_im.astype(F32))
    eye = jnp.eye(G, dtype=F32)
    b_t = jnp.transpose(b_bar, (0, 1, 3, 2))

    def blockdiag_in(m):
        return jnp.einsum('dgcp,gh->dgchp', m, eye).reshape(2, G * C, G * P)

    bblk = jnp.concatenate([blockdiag_in(jnp.real(b_t)), blockdiag_in(jnp.imag(b_t))], axis=-1)
    c_t_re = jnp.transpose(c_re.astype(F32), (0, 1, 3, 2))
    c_t_im = jnp.transpose(c_im.astype(F32), (0, 1, 3, 2))

    def blockdiag_out(m):
        return jnp.einsum('dgpc,gh->dgphc', m, eye).reshape(2, G * P, G * C)

    cblk = jnp.concatenate([blockdiag_out(c_t_re), blockdiag_out(-c_t_im)], axis=1)
    rows = jnp.arange(SUB)
    pieces = []
    for d in range(2):
        log_lb = (lam[d] * step[d]).reshape(1, G * P)
        per_d = []
        for sh in (1, 2, 4):
            valid = (rows >= sh) if d == 0 else (rows < SUB - sh)
            per_d.append(jnp.where(valid[:, None], jnp.exp(log_lb * float(sh)), 0.0))
        expo = (rows + 1) if d == 0 else (SUB - rows)
        per_d.append(jnp.exp(log_lb * expo[:, None].astype(F32)))
        m = jnp.concatenate(per_d, axis=0)
        pieces.append(jnp.concatenate([jnp.real(m), jnp.imag(m)], axis=-1))
    mul = jnp.stack(pieces, axis=0).astype(F32)
    return bblk.astype(BF16), cblk.astype(BF16), mul


def _softmax_pv(parts):
    m = parts[0][0].max(axis=-1, keepdims=True)
    for s, _ in parts[1:]:
        m = jnp.maximum(m, s.max(axis=-1, keepdims=True))
    den = 0.0
    acc = 0.0
    for s, v in parts:
        p = jnp.exp(s - m)
        den = den + p.sum(axis=-1, keepdims=True)
        acc = acc + _dot(p.astype(BF16), v)
    return acc / den


def _na_kernel(q_ref, k_ref, v_ref, bias_ref, o_ref):
    t = pl.program_id(1)
    first = lax.broadcasted_iota(I32, (1, LANE), 1) < NA_HEAD_DIM

    def pair_attention(pp, start):
        ls = slice(pp * LANE, (pp + 1) * LANE)
        qp = q_ref[0, :, ls]
        kc = k_ref[0, 0:CTX_LEN, ls]
        vc = v_ref[0, 0:CTX_LEN, ls]
        outs = []
        for hh in range(2):
            qm = jnp.where(first if hh == 0 else ~first, qp, jnp.zeros_like(qp))
            parts = [(_dot_nt(qm, kc), vc)]
            if start is not None:
                kb = k_ref[0, pl.ds(start, NA_BAND * GRID_W), ls]
                vb = v_ref[0, pl.ds(start, NA_BAND * GRID_W), ls]
                parts.append((_dot_nt(qm, kb) + bias_ref[2 * pp + hh, 0], vb))
            outs.append(_softmax_pv(parts))
        o_ref[0, :, ls] = jnp.where(first, outs[0], outs[1]).astype(BF16)

    @pl.when(t == 0)
    def _():
        for pp in range(NA_HEADS // 2):
            pair_attention(pp, None)

    @pl.when(t > 0)
    def _():
        first_row = (t - 1) * (TILE // GRID_W)
        u0 = jnp.clip(first_row - NA_KH // 2, 0, ROWS - NA_BAND)
        start = pl.multiple_of(CTX_LEN + u0 * GRID_W, LANE)
        for pp in range(NA_HEADS // 2):
            pair_attention(pp, start)


def _na_cfg(t):
    return jnp.where(t <= 1, 0, jnp.where(t == NT - 1, 2, 1))


def natten(q, k, v, bias):
    whole = pl.BlockSpec((1, T, NA_WIDTH), lambda b, t: (b, 0, 0))
    tile = pl.BlockSpec((1, TILE, NA_WIDTH), lambda b, t: (b, t, 0))
    return pl.pallas_call(
        _na_kernel,
        grid=(BATCH, NT),
        in_specs=[tile, whole, whole,
                  pl.BlockSpec((NA_HEADS, 1, TILE, NA_BAND * GRID_W), lambda b, t: (0, _na_cfg(t), 0, 0))],
        out_specs=tile,
        out_shape=jax.ShapeDtypeStruct((BATCH, T, NA_WIDTH), BF16),
        compiler_params=_cp("arbitrary", "arbitrary"),
        name="natten",
    )(q, k, v, bias)


def natten_bias(rpb):
    rows_per_tile = TILE // GRID_W
    col = jnp.arange(GRID_W)
    c0 = jnp.clip(col - NA_KW // 2, 0, GRID_W - NA_KW)
    in_win = (col[None, :] >= c0[:, None]) & (col[None, :] < c0[:, None] + NA_KW)
    rel_c = jnp.clip(col[None, :] - col[:, None] + (NA_KW - 1), 0, RPB_W - 1)
    first_row = jnp.array([0, 2 * rows_per_tile, ROWS - rows_per_tile])
    u0 = jnp.clip(first_row - NA_KH // 2, 0, ROWS - NA_BAND)
    r = first_row[:, None] + jnp.arange(rows_per_tile)[None, :]
    kr = u0[:, None] + jnp.arange(NA_BAND)[None, :]
    r0 = jnp.clip(r - NA_KH // 2, 0, ROWS - NA_KH)
    in_band = (kr[:, None, :] >= r0[:, :, None]) & (kr[:, None, :] < r0[:, :, None] + NA_KH)
    rel_r = jnp.clip(kr[:, None, :] - r[:, :, None] + (NA_KH - 1), 0, 2 * NA_KH - 2)
    pick_r = jax.nn.one_hot(rel_r, 2 * NA_KH - 1, dtype=F32)
    pick_c = jax.nn.one_hot(rel_c, RPB_W, dtype=F32)
    b = jnp.einsum('hyx,arjy,qkx->harqjk', rpb.astype(F32), pick_r, pick_c,
                   precision=lax.Precision.HIGHEST)
    ok = in_band[None, :, :, None, :, None] & in_win[None, None, None, :, None, :]
    return jnp.where(ok, b, NEG).reshape(NA_HEADS, 3, TILE, NA_BAND * GRID_W)


def _softplus(x):
    return jnp.maximum(x, 0.0) + jnp.log(1.0 + jnp.exp(-jnp.abs(x)))


def _ssd_prep_kernel(prev_ref, cur_ref, next_ref, dtr_ref, cw_ref, cb_ref, dtb_ref, ar_ref, cos_ref, sin_ref,
                     xs_ref, bm_ref, cm_ref, dt_ref, a_ref):
    t = pl.program_id(1)
    halo = prev_ref.shape[1]
    has_prev = t >= 2
    has_next = (t >= 1) & (t <= NT - 2)
    prev = jnp.where(has_prev, prev_ref[0].astype(F32), 0.0)
    nxt = jnp.where(has_next, next_ref[0].astype(F32), 0.0)
    ext = jnp.concatenate([prev, cur_ref[0].astype(F32), nxt], axis=0)
    n = ext.shape[0]
    acc = cb_ref[...] + cw_ref[SSD_CONV // 2:SSD_CONV // 2 + 1, :] * ext
    for kk in range(SSD_CONV):
        off = kk - SSD_CONV // 2
        if off != 0:
            acc = acc + cw_ref[kk:kk + 1, :] * pltpu.roll(ext, (-off) % n, 0)
    y = _silu(acc[halo:halo + TILE, :])
    xs_ref[0] = y[:, 0:SSD_WIDTH].astype(BF16)

    lane = lax.broadcasted_iota(I32, (1, LANE), 1)
    low = (lane & (NA_HEAD_DIM // 2)) == 0
    cos = cos_ref[...]
    sin = sin_ref[...]
    for g in range(2 * SSD_NGROUPS):
        v = y[:, SSD_WIDTH + g * LANE:SSD_WIDTH + (g + 1) * LANE]
        sw = jnp.where(low, pltpu.roll(v, LANE - 32, 1), pltpu.roll(v, 32, 1))
        rot = (v * cos + sw * sin).astype(BF16)
        if g < SSD_NGROUPS:
            bm_ref[0, :, g * LANE:(g + 1) * LANE] = rot
        else:
            cm_ref[0, :, (g - SSD_NGROUPS) * LANE:(g - SSD_NGROUPS + 1) * LANE] = rot

    dt = _softplus(dtr_ref[0] + dtb_ref[...])
    dt_ref[0] = dt
    a_ref[0] = dt * ar_ref[...]


def ssd_prep(xbc, dt_raw, conv_w, conv_b, dt_bias, a_log, cos_t, sin_t):
    halo = 16
    per = TILE // halo
    nhalo = T // halo
    tok = lambda w, dt: jax.ShapeDtypeStruct((BATCH, T, w), dt)
    tspec = lambda w: pl.BlockSpec((1, TILE, w), lambda b, t: (b, t, 0))
    row = lambda w: pl.BlockSpec((1, w), lambda b, t: (0, 0))
    cw = jnp.zeros((SUB, SSD_XBC), F32).at[:SSD_CONV].set(conv_w.astype(F32))
    pad12 = lambda v: jnp.zeros((1, LANE), F32).at[0, :2 * SSD_HEADS].set(v.astype(F32).reshape(-1))
    return pl.pallas_call(
        _ssd_prep_kernel,
        grid=(BATCH, NT),
        in_specs=[pl.BlockSpec((1, halo, SSD_XBC), lambda b, t: (b, jnp.maximum(t * per - 1, 0), 0)),
                  tspec(SSD_XBC),
                  pl.BlockSpec((1, halo, SSD_XBC), lambda b, t: (b, jnp.minimum((t + 1) * per, nhalo - 1), 0)),
                  tspec(LANE),
                  pl.BlockSpec((SUB, SSD_XBC), lambda b, t: (0, 0)), row(SSD_XBC), row(LANE), row(LANE),
                  pl.BlockSpec((TILE, LANE), lambda b, t: (t, 0)), pl.BlockSpec((TILE, LANE), lambda b, t: (t, 0))],
        out_specs=[tspec(SSD_WIDTH), tspec(SSD_BC), tspec(SSD_BC), tspec(LANE), tspec(LANE)],
        out_shape=[tok(SSD_WIDTH, BF16), tok(SSD_BC, BF16), tok(SSD_BC, BF16), tok(LANE, F32), tok(LANE, F32)],
        compiler_params=_cp("arbitrary", "arbitrary"),
        name="ssd_prep",
    )(xbc, xbc, xbc, dt_raw, cw, conv_b.astype(F32).reshape(1, SSD_XBC), pad12(dt_bias),
      pad12(-jnp.exp(a_log.astype(F32))), cos_t, sin_t)


def rope_tables():
    half = SSD_STATE // 2
    nf = half // 2
    pos = jnp.arange(SEQ)
    inv_freq = ROPE_BASE ** (-jnp.arange(nf, dtype=F32) / nf)
    lane = jnp.arange(LANE)
    p = jnp.where(lane[None, :] < half, (pos // GRID_W)[:, None], (pos % GRID_W)[:, None]).astype(F32)
    ang = p * inv_freq[lane % nf][None, :]
    sign = jnp.where((lane & nf) == 0, -1.0, 1.0)[None, :]
    cos_t = jnp.concatenate([jnp.ones((CTX_LEN, LANE), F32), jnp.cos(ang)], axis=0)
    sin_t = jnp.concatenate([jnp.zeros((CTX_LEN, LANE), F32), jnp.sin(ang) * sign], axis=0)
    return cos_t, sin_t


def _ssd_dir(d, xs_ref, bm_ref, cm_ref, bt_ref, dt_ref, a_ref, at_ref, tri_ref, y_ref, st_ref):
    q = TILE
    lane = lax.broadcasted_iota(I32, (1, LANE), 1)
    first = lane < SSD_HEAD_DIM
    ri = lax.broadcasted_iota(I32, (q, q), 0)
    ci = lax.broadcasted_iota(I32, (q, q), 1)
    keep = (ci <= ri) if d == 0 else (ci >= ri)
    end = q - 1 if d == 0 else 0
    tri_col = tri_ref[d]
    tri_row = tri_ref[1 - d]
    a = a_ref[0]
    dt = dt_ref[0]
    cs_col = _dot_exact_rhs(tri_col, a)
    cs_row = _dot_exact_lhs(at_ref[0], tri_row)
    g_mats = [_dot_nt(cm_ref[0, :, g * SSD_STATE:(g + 1) * SSD_STATE],
                      bm_ref[0, :, g * SSD_STATE:(g + 1) * SSD_STATE]) for g in range(SSD_NGROUPS)]

    def head_col(m, h):
        c = d * SSD_HEADS + h
        return m[:, c:c + 1]

    for pp in range(SSD_HEADS // 2):
        ls = slice(pp * LANE, (pp + 1) * LANE)
        h0, h1 = 2 * pp, 2 * pp + 1
        x = xs_ref[0, :, ls].astype(F32)
        dt_l = jnp.where(first, head_col(dt, h0), head_col(dt, h1))
        cs_l = jnp.where(first, head_col(cs_col, h0), head_col(cs_col, h1))
        cs_end = cs_l[end:end + 1, :]
        xdt = x * dt_l
        xdt_b = xdt.astype(BF16)
        xw = (xdt * jnp.exp(cs_end - cs_l)).astype(BF16)
        st = st_ref[d, pp]
        st_b = st.astype(BF16)
        ys, ups = [], []
        for h in (h0, h1):
            g = h // (SSD_HEADS // SSD_NGROUPS)
            c = d * SSD_HEADS + h
            diff = head_col(cs_col, h) - cs_row[c:c + 1, :]
            decay = jnp.where(keep, jnp.exp(jnp.where(keep, diff, 0.0)), 0.0)
            m = (g_mats[g] * decay).astype(BF16)
            y_h = _dot(m, xdt_b) + _dot(cm_ref[0, :, g * SSD_STATE:(g + 1) * SSD_STATE], st_b) * jnp.exp(cs_l)
            ys.append(y_h)
            ups.append(_dot(bt_ref[0, g * SSD_STATE:(g + 1) * SSD_STATE, :], xw))
        y_ref[0, :, ls] = jnp.where(first, ys[0], ys[1])
        st_ref[d, pp] = jnp.exp(cs_end) * st + jnp.where(first, ups[0], ups[1])


def _ssd_scan_kernel(xs_f, bm_f, cm_f, bt_f, dt_f, a_f, at_f, xs_b, bm_b, cm_b, bt_b, dt_b, a_b, at_b, tri_ref,
                     yf_ref, yb_ref, st_ref):
    @pl.when(pl.program_id(1) == 0)
    def _():
        st_ref[...] = jnp.zeros_like(st_ref)

    _ssd_dir(0, xs_f, bm_f, cm_f, bt_f, dt_f, a_f, at_f, tri_ref, yf_ref, st_ref)
    _ssd_dir(1, xs_b, bm_b, cm_b, bt_b, dt_b, a_b, at_b, tri_ref, yb_ref, st_ref)


def ssd_scan(xs, bm, cm, dt, a):
    bt = jnp.swapaxes(bm, 1, 2)
    at = jnp.swapaxes(a[:, :, :2 * SUB], 1, 2)
    idx = jnp.arange(TILE)
    tri = jnp.stack([idx[None, :] <= idx[:, None], idx[None, :] >= idx[:, None]]).astype(BF16)
    fwd = lambda b, i: (b, i, 0)
    bwd = lambda b, i: (b, _bwd_tile(i), 0)
    fwd_t = lambda b, i: (b, 0, i)
    bwd_t = lambda b, i: (b, 0, _bwd_tile(i))

    def specs(f, ft):
        return [pl.BlockSpec((1, TILE, SSD_WIDTH), f), pl.BlockSpec((1, TILE, SSD_BC), f),
                pl.BlockSpec((1, TILE, SSD_BC), f), pl.BlockSpec((1, SSD_BC, TILE), ft),
                pl.BlockSpec((1, TILE, LANE), f), pl.BlockSpec((1, TILE, LANE), f),
                pl.BlockSpec((1, 2 * SUB, TILE), ft)]

    args = (xs, bm, cm, bt, dt, a, at)
    return pl.pallas_call(
        _ssd_scan_kernel,
        grid=(BATCH, NT),
        in_specs=specs(fwd, fwd_t) + specs(bwd, bwd_t) + [pl.BlockSpec((2, TILE, TILE), lambda b, i: (0, 0, 0))],
        out_specs=[pl.BlockSpec((1, TILE, SSD_WIDTH), fwd), pl.BlockSpec((1, TILE, SSD_WIDTH), bwd)],
        out_shape=[jax.ShapeDtypeStruct((BATCH, T, SSD_WIDTH), F32)] * 2,
        scratch_shapes=[pltpu.VMEM((2, SSD_HEADS // 2, SSD_STATE, LANE), F32)],
        compiler_params=_cp("arbitrary", "arbitrary"),
        name="ssd_scan",
    )(*args, *args, tri)


def _gelu_tanh(x):
    return 0.5 * x * (1.0 + jnp.tanh(math.sqrt(2.0 / math.pi) * (x + 0.044715 * (x * x * x))))


def _post_kernel(x_ref, mod_ref, u_ref, s5f_ref, s5b_ref, na_ref, xs_ref, z_ref, sdf_ref, sdb_ref,
                 s5d_ref, gw_ref, gb_ref, sdd_ref, snw_ref, wo_ref, n2w_ref, rt_ref,
                 x1_ref, h_ref, lg_ref):
    ys5 = u_ref[0].astype(F32) * s5d_ref[...] + s5f_ref[0] + s5b_ref[0]
    g = _gelu_tanh(ys5)
    s5o = g * jax.nn.sigmoid(_dot(g.astype(BF16), gw_ref[...]) + gb_ref[...])
    yssd = (xs_ref[0].astype(F32) * sdd_ref[...] + sdf_ref[0] + sdb_ref[0]) * _silu(z_ref[0].astype(F32))
    ssdo = yssd * lax.rsqrt(jnp.mean(yssd * yssd, axis=-1, keepdims=True) + EPS) * snw_ref[...]
    mix = jnp.concatenate([s5o.astype(BF16), na_ref[0], ssdo.astype(BF16)], axis=-1)
    x1 = x_ref[0] + mod_ref[0, 0, 2:3, :] * _dot(mix, wo_ref[...])
    x1_ref[0] = x1
    h = _modulated_norm(x1, n2w_ref[...], mod_ref[0, 0, 3:4, :], mod_ref[0, 0, 4:5, :])
    h_ref[0] = h.astype(BF16)
    lg_ref[0] = _dot_x3(h, rt_ref[...])


def post_mixer(x, mod, u, s5f, s5b, na, xs, z, sdf, sdb, s5_d, glu_w, glu_b, ssd_d, ssd_norm_w, w_out, norm2_w,
               router):
    tspec = lambda w: pl.BlockSpec((1, TILE, w), lambda b, t: (b, t, 0))
    whole = lambda *shp: pl.BlockSpec(shp, lambda b, t: (0,) * len(shp))
    rt = jnp.zeros((D_MODEL, LANE), F32).at[:, :N_EXPERTS].set(router.astype(F32))
    return pl.pallas_call(
        _post_kernel,
        grid=(BATCH, NT),
        in_specs=[tspec(D_MODEL), pl.BlockSpec((1, 1, 6, D_MODEL), lambda b, t: (b, _seg(t), 0, 0)),
                  tspec(S5_WIDTH), tspec(S5_WIDTH), tspec(S5_WIDTH), tspec(NA_WIDTH),
                  tspec(SSD_WIDTH), tspec(SSD_WIDTH), tspec(SSD_WIDTH), tspec(SSD_WIDTH),
                  whole(1, S5_WIDTH), whole(S5_WIDTH, S5_WIDTH), whole(1, S5_WIDTH),
                  whole(1, SSD_WIDTH), whole(1, SSD_WIDTH), whole(D_MODEL, D_MODEL), whole(1, D_MODEL),
                  whole(D_MODEL, LANE)],
        out_specs=[tspec(D_MODEL), tspec(D_MODEL), tspec(LANE)],
        out_shape=[jax.ShapeDtypeStruct((BATCH, T, D_MODEL), F32), jax.ShapeDtypeStruct((BATCH, T, D_MODEL), BF16),
                   jax.ShapeDtypeStruct((BATCH, T, LANE), F32)],
        compiler_params=_cp("arbitrary", "arbitrary"),
        name="post_mixer",
    )(x, mod, u, s5f, s5b, na, xs, z, sdf, sdb,
      s5_d.astype(F32).reshape(1, S5_WIDTH), glu_w.astype(BF16), glu_b.astype(F32).reshape(1, S5_WIDTH),
      jnp.repeat(ssd_d.astype(F32), SSD_HEAD_DIM).reshape(1, SSD_WIDTH), ssd_norm_w.astype(F32).reshape(1, SSD_WIDTH),
      w_out.astype(BF16), norm2_w.astype(F32).reshape(1, D_MODEL), rt)


def _route_kernel(lg_ref, tri_ref, slot_ref, aff_ref, *, with_ctx):
    lg = lg_ref[0]
    m = lg.max(axis=0, keepdims=True)
    e = jnp.exp(lg - m)
    aff = e / e.sum(axis=0, keepdims=True)
    aff_ref[0] = aff
    bits = pltpu.bitcast(aff, I32)
    is_ctx = lax.broadcasted_iota(I32, (N_EXPERTS, T), 1) < CTX_LEN

    def count(mask):
        return jnp.where(mask, 1.0, 0.0).sum(axis=1, keepdims=True)

    def kth_largest(seg, k):
        def body(i, prefix):
            cand = prefix | lax.shift_left(jnp.int32(1), 30 - i)
            return jnp.where(count((bits >= cand) & seg) >= k, cand, prefix)
        return lax.fori_loop(0, 31, body, jnp.zeros((N_EXPERTS, 1), I32))

    def excl_cumsum(x01):
        carry = jnp.zeros((N_EXPERTS, 1), F32)
        pieces = []
        for j in range(T // LANE):
            blk = x01[:, j * LANE:(j + 1) * LANE]
            inc = _dot(blk.astype(BF16), tri_ref[...])
            pieces.append(inc - blk + carry)
            carry = carry + inc[:, LANE - 1:LANE]
        return jnp.concatenate(pieces, axis=1)

    thr = kth_largest(~is_ctx, float(CAP_LAT))
    k_of = jnp.full((N_EXPERTS, T), float(CAP_LAT), F32)
    if with_ctx:
        thr = jnp.where(is_ctx, kth_largest(is_ctx, float(CAP_CTX)), thr)
        k_of = jnp.where(is_ctx, float(CAP_CTX), k_of)
    gt = bits > thr
    eq = bits == thr
    if not with_ctx:
        gt = gt & ~is_ctx
        eq = eq & ~is_ctx
    n_gt = jnp.where(is_ctx, count(gt & is_ctx), count(gt & ~is_ctx))
    tie_rank = excl_cumsum(jnp.where(eq, 1.0, 0.0))
    tie_rank = tie_rank - jnp.where(is_ctx, 0.0, count(eq & is_ctx))
    sel = gt | (eq & (tie_rank < k_of - n_gt))
    pos = excl_cumsum(jnp.where(sel, 1.0, 0.0))
    slot = jnp.where(is_ctx, pos + float(CAP_LAT), pos - count(sel & is_ctx))
    slot_ref[0] = jnp.where(sel, slot, -1.0).astype(I32)


def route(logits_t, with_ctx):
    idx = jnp.arange(LANE)
    tri = (idx[:, None] <= idx[None, :]).astype(BF16)
    spec = pl.BlockSpec((1, N_EXPERTS, T), lambda b: (b, 0, 0))
    return pl.pallas_call(
        functools.partial(_route_kernel, with_ctx=with_ctx),
        grid=(BATCH,),
        in_specs=[spec, pl.BlockSpec((LANE, LANE), lambda b: (0, 0))],
        out_specs=[spec, spec],
        out_shape=[jax.ShapeDtypeStruct((BATCH, N_EXPERTS, T), I32), jax.ShapeDtypeStruct((BATCH, N_EXPERTS, T), F32)],
        compiler_params=_cp("arbitrary"),
        name="route",
    )(logits_t, tri)


NJ = CAP_LAT // LANE


def slot_ranges(slot):
    s = slot.reshape(BATCH, N_EXPERTS, NT, TILE)
    has = s >= 0
    smin = jnp.min(jnp.where(has, s, CAP_LAT + CAP_CTX), axis=-1)
    smax = jnp.max(jnp.where(has, s, -1), axis=-1)
    j = jnp.arange(NJ)[:, None]
    latent = jnp.arange(NT)[None, :] >= CTX_LEN // TILE
    act = (smin[:, :, None, :] <= j * LANE + LANE - 1) & (smax[:, :, None, :] >= j * LANE) & latent
    tlo = jnp.argmax(act, axis=-1)
    thi = NT - 1 - jnp.argmax(act[..., ::-1], axis=-1)
    any_sel = smax >= 0
    jlo = jnp.where(any_sel, smin // LANE, 0)
    jhi = jnp.where(any_sel, jnp.minimum(smax // LANE, NJ - 1), -1)
    flat = lambda a: a.reshape(-1).astype(I32)
    return flat(tlo), flat(thi), flat(jlo), flat(jhi)


def _ffn_kernel(tlo_ref, thi_ref, h_ref, slot_ref, wg_ref, wu_ref, wd_ref, y_ref, xs_ref, *, nslot):
    base = (pl.program_id(0) * N_EXPERTS + pl.program_id(1)) * NJ
    sid = lax.broadcasted_iota(I32, (LANE, TILE), 0)
    for j in range(NJ):
        rows = slice(j * LANE, (j + 1) * LANE)
        xs_ref[rows, :] = jnp.zeros((LANE, D_MODEL), F32)

        def body(t, carry, j=j, rows=rows):
            tok = pl.multiple_of(t * TILE, TILE)
            onehot = jnp.where(sid + j * LANE == slot_ref[0, 0, pl.ds(t, 1), :], 1.0, 0.0).astype(BF16)
            xs_ref[rows, :] += _dot(onehot, h_ref[0, pl.ds(tok, TILE), :])
            return carry

        lax.fori_loop(tlo_ref[base + j], thi_ref[base + j] + 1, body, 0)
    if nslot > CAP_LAT:
        cid = lax.broadcasted_iota(I32, (nslot - CAP_LAT, CTX_LEN), 0) + CAP_LAT
        onehot = jnp.where(cid == slot_ref[0, 0, 0:1, :], 1.0, 0.0).astype(BF16)
        xs_ref[CAP_LAT:nslot, :] = _dot(onehot, h_ref[0, 0:CTX_LEN, :])
    xs = xs_ref[...].astype(BF16)
    hid = _silu(_dot(xs, wg_ref[0, 0])) * _dot(xs, wu_ref[0, 0])
    y_ref[0, 0] = _dot(hid.astype(BF16), wd_ref[0, 0]).astype(BF16)


def moe_ffn(h, slot, tlo, thi, layer, wg, wu, wd, nslot):
    wspec = lambda: pl.BlockSpec((1, 1, D_MODEL, D_EXPERT), lambda b, e, *_: (layer, e, 0, 0))
    return pl.pallas_call(
        functools.partial(_ffn_kernel, nslot=nslot),
        grid_spec=pltpu.PrefetchScalarGridSpec(
            num_scalar_prefetch=2,
            grid=(BATCH, N_EXPERTS),
            in_specs=[pl.BlockSpec((1, T, D_MODEL), lambda b, e, *_: (b, 0, 0)),
                      pl.BlockSpec((1, 1, NT, TILE), lambda b, e, *_: (b, e, 0, 0)),
                      wspec(), wspec(), wspec()],
            out_specs=pl.BlockSpec((1, 1, nslot, D_MODEL), lambda b, e, *_: (b, e, 0, 0)),
            scratch_shapes=[pltpu.VMEM((nslot, D_MODEL), F32)]),
        out_shape=jax.ShapeDtypeStruct((BATCH, N_EXPERTS, nslot, D_MODEL), BF16),
        compiler_params=_cp("arbitrary", "arbitrary"),
        name="moe_ffn",
    )(tlo, thi, h, slot.reshape(BATCH, N_EXPERTS, NT, TILE), wg, wu, wd)


def _combine_kernel(jlo_ref, jhi_ref, x_ref, mod_ref, slot_ref, aff_ref, y_ref, o_ref, acc_ref, *, nslot):
    b = pl.program_id(0)
    t = pl.program_id(2)

    def weights(e, lane):
        return jnp.where(slot_ref[0, :, e:e + 1] == lane, aff_ref[0, :, e:e + 1], 0.0).astype(BF16)

    def finish(acc):
        o_ref[0] = x_ref[0] + mod_ref[0, 0, 5:6, :] * acc

    @pl.when(t == 0)
    def _():
        if nslot > CAP_LAT:
            lane = lax.broadcasted_iota(I32, (TILE, nslot - CAP_LAT), 1) + CAP_LAT
            acc = jnp.zeros((TILE, y_ref.shape[-1]), F32)
            for e in range(N_EXPERTS):
                acc = acc + _dot(weights(e, lane), y_ref[0, e, CAP_LAT:nslot, :])
            finish(acc)
        else:
            o_ref[0] = x_ref[0]

    @pl.when(t > 0)
    def _():
        acc_ref[...] = jnp.zeros_like(acc_ref)
        lane = lax.broadcasted_iota(I32, (TILE, LANE), 1)
        for e in range(N_EXPERTS):
            base = (b * N_EXPERTS + e) * NT + t

            def body(j, carry, e=e):
                row = pl.multiple_of(j * LANE, LANE)
                acc_ref[...] += _dot(weights(e, lane + j * LANE), y_ref[0, e, pl.ds(row, LANE), :])
                return carry

            lax.fori_loop(jlo_ref[base], jhi_ref[base] + 1, body, 0)
        finish(acc_ref[...])


def moe_combine(x1, mod, slot_tok, aff_tok, y, jlo, jhi, nslot):
    nh = 2
    dh = D_MODEL // nh
    return pl.pallas_call(
        functools.partial(_combine_kernel, nslot=nslot),
        grid_spec=pltpu.PrefetchScalarGridSpec(
            num_scalar_prefetch=2,
            grid=(BATCH, nh, NT),
            in_specs=[pl.BlockSpec((1, TILE, dh), lambda b, c, t, *_: (b, t, c)),
                      pl.BlockSpec((1, 1, 6, dh), lambda b, c, t, *_: (b, _seg(t), 0, c)),
                      pl.BlockSpec((1, TILE, N_EXPERTS), lambda b, c, t, *_: (b, t, 0)),
                      pl.BlockSpec((1, TILE, N_EXPERTS), lambda b, c, t, *_: (b, t, 0)),
                      pl.BlockSpec((1, N_EXPERTS, nslot, dh), lambda b, c, t, *_: (b, 0, 0, c))],
            out_specs=pl.BlockSpec((1, TILE, dh), lambda b, c, t, *_: (b, t, c)),
            scratch_shapes=[pltpu.VMEM((TILE, dh), F32)]),
        out_shape=jax.ShapeDtypeStruct((BATCH, T, D_MODEL), F32),
        compiler_params=_cp("arbitrary", "arbitrary", "arbitrary"),
        name="moe_combine",
    )(jlo, jhi, x1, mod, slot_tok, aff_tok, y)


def _final_kernel(x_ref, w_ref, o_ref):
    x = x_ref[0]
    o_ref[0] = x * lax.rsqrt(jnp.mean(x * x, axis=-1, keepdims=True) + EPS) * w_ref[...]


def final_norm(x, w):
    off = CTX_LEN // TILE
    return pl.pallas_call(
        _final_kernel,
        grid=(BATCH, SEQ // TILE),
        in_specs=[pl.BlockSpec((1, TILE, D_MODEL), lambda b, t: (b, t + off, 0)),
                  pl.BlockSpec((1, D_MODEL), lambda b, t: (0, 0))],
        out_specs=pl.BlockSpec((1, TILE, D_MODEL), lambda b, t: (b, t, 0)),
        out_shape=jax.ShapeDtypeStruct((BATCH, SEQ, D_MODEL), F32),
        compiler_params=_cp("arbitrary", "arbitrary"),
        name="final_norm",
    )(x, w.astype(F32).reshape(1, D_MODEL))


def trunk_layer(l, x, mod, cos_t, sin_t, norm1_w, norm2_w, w_in, w_out,
                s5_lam_re, s5_lam_im, s5_log_dt, s5_b_re, s5_b_im, s5_c_re, s5_c_im, s5_d, s5_glu_w, s5_glu_b,
                na_rpb, ssd_conv_w, ssd_conv_b, ssd_dt_bias, ssd_a_log, ssd_d, ssd_norm_w,
                moe_router, wg, wu, wd, with_ctx_out):
    w_in_p = jnp.zeros((D_MODEL, IN_COLS_PAD), BF16).at[:, :IN_COLS].set(w_in.astype(BF16))
    u, q, k, v, z, xbc, dt_raw = in_proj(x, mod, norm1_w, w_in_p)

    bblk, cblk, mul = s5_params(s5_lam_re, s5_lam_im, s5_log_dt, s5_b_re, s5_b_im, s5_c_re, s5_c_im)
    s5f, s5b = s5_scan(u, bblk, cblk, mul)
    na = natten(q, k, v, natten_bias(na_rpb))
    xs, bm, cm, dt, a = ssd_prep(xbc, dt_raw, ssd_conv_w, ssd_conv_b, ssd_dt_bias, ssd_a_log, cos_t, sin_t)
    sdf, sdb = ssd_scan(xs, bm, cm, dt, a)

    x1, h, logits = post_mixer(x, mod, u, s5f, s5b, na, xs, z, sdf, sdb, s5_d, s5_glu_w, s5_glu_b,
                               ssd_d, ssd_norm_w, w_out, norm2_w, moe_router)
    slot, aff = route(jnp.swapaxes(logits[:, :, :N_EXPERTS], 1, 2), with_ctx_out)
    nslot = CAP_LAT + CAP_CTX if with_ctx_out else CAP_LAT
    tlo, thi, jlo, jhi = slot_ranges(slot)
    y = moe_ffn(h, slot, tlo, thi, l, wg, wu, wd, nslot)
    return moe_combine(x1, mod, jnp.swapaxes(slot, 1, 2), jnp.swapaxes(aff, 1, 2), y, jlo, jhi, nslot)


def kernel(x, c, ctx, c_ctx, w_ada, b_ada, norm1_w, norm2_w, w_in, w_out, s5_lam_re, s5_lam_im, s5_log_dt, s5_b_re, s5_b_im, s5_c_re, s5_c_im, s5_d, s5_glu_w, s5_glu_b, na_rpb, ssd_conv_w, ssd_conv_b, ssd_dt_bias, ssd_a_log, ssd_d, ssd_norm_w, moe_router, moe_w_gate, moe_w_up, moe_w_down, final_norm_w):
    xa = jnp.concatenate([ctx, x], axis=1).astype(F32)
    cvec = jnp.zeros((SUB, D_MODEL), F32).at[0].set(c_ctx.astype(F32)).at[1:1 + BATCH].set(c.astype(F32))
    mods = ada_mod(cvec, w_ada.astype(F32), b_ada.astype(F32)).reshape(DEPTH, SUB, 6, D_MODEL)
    cos_t, sin_t = rope_tables()
    wg, wu, wd = moe_w_gate.astype(BF16), moe_w_up.astype(BF16), moe_w_down.astype(BF16)
    for l in range(DEPTH):
        mod = jnp.stack([jnp.broadcast_to(mods[l, 0], (BATCH, 6, D_MODEL)), mods[l, 1:1 + BATCH]], axis=1)
        xa = trunk_layer(
            l, xa, mod, cos_t, sin_t, norm1_w[l], norm2_w[l], w_in[l], w_out[l],
            s5_lam_re[l], s5_lam_im[l], s5_log_dt[l], s5_b_re[l], s5_b_im[l], s5_c_re[l], s5_c_im[l],
            s5_d[l], s5_glu_w[l], s5_glu_b[l],
            na_rpb[l], ssd_conv_w[l], ssd_conv_b[l], ssd_dt_bias[l], ssd_a_log[l], ssd_d[l], ssd_norm_w[l],
            moe_router[l], wg, wu, wd, l < DEPTH - 1)
    return final_norm(xa, final_norm_w)
```

```python
import functools
import math

import jax
import jax.numpy as jnp
from jax import lax
from jax.experimental import pallas as pl
from jax.experimental.pallas import tpu as pltpu

F32 = jnp.float32
BF16 = jnp.bfloat16
I32 = jnp.int32

D_MODEL = 1024
BATCH = 4
SEQ = 4096
DEPTH = 2
GRID_W = 64
CTX_LEN = 256
EPS = 1e-6

S5_WIDTH = 256
S5_GROUP = 16
S5_NGROUPS = 16
S5_STATE = 64
S5_NSTATE = S5_NGROUPS * S5_STATE

NA_HEADS = 6
NA_HEAD_DIM = 64
NA_WIDTH = 384
NA_KH = 8
NA_KW = 16
NA_BAND = 12
RPB_W = 2 * NA_KW - 1

SSD_HEADS = 6
SSD_HEAD_DIM = 64
SSD_WIDTH = 384
SSD_NGROUPS = 2
SSD_STATE = 128
SSD_CONV = 5
SSD_BC = 256
SSD_XBC = 896

N_EXPERTS = 16
D_EXPERT = 1024
ROPE_BASE = 10000.0

T = CTX_LEN + SEQ
TILE = 256
NT = T // TILE
LANE = 128
SUB = 8
ROWS = SEQ // GRID_W
CAP_LAT = 2 * SEQ // N_EXPERTS
CAP_CTX = 2 * CTX_LEN // N_EXPERTS
NEG = -1e30

C_U = 0
C_Q = 256
C_K = 640
C_V = 1024
C_Z = 1408
C_XBC = 1792
C_DT = 2688
IN_COLS = 2700
IN_COLS_PAD = 2816

VMEM_LIMIT = 56 * 1024 * 1024


def _cp(*sem):
    return pltpu.CompilerParams(dimension_semantics=sem, vmem_limit_bytes=VMEM_LIMIT)


def _dot(a, b):
    return jnp.dot(a, b, preferred_element_type=F32)


def _dot_nt(a, b):
    return lax.dot_general(a, b, (((1,), (1,)), ((), ())), preferred_element_type=F32)


def _split3(x):
    hi = x.astype(BF16)
    r = x - hi.astype(F32)
    mid = r.astype(BF16)
    lo = (r - mid.astype(F32)).astype(BF16)
    return hi, mid, lo


def _dot_exact_rhs(a_bf16, b_f32):
    hi, mid, lo = _split3(b_f32)
    return _dot(a_bf16, hi) + _dot(a_bf16, mid) + _dot(a_bf16, lo)


def _dot_exact_lhs(a_f32, b_bf16):
    hi, mid, lo = _split3(a_f32)
    return _dot(hi, b_bf16) + _dot(mid, b_bf16) + _dot(lo, b_bf16)


def _dot_x3(a, b):
    ah = a.astype(BF16)
    al = (a - ah.astype(F32)).astype(BF16)
    bh = b.astype(BF16)
    bl = (b - bh.astype(F32)).astype(BF16)
    return _dot(ah, bh) + _dot(ah, bl) + _dot(al, bh)


def _silu(x):
    return x * jax.nn.sigmoid(x)


def _seg(t):
    return jnp.where(t >= CTX_LEN // TILE, 1, 0)


def _bwd_tile(i):
    return jnp.where(i == 0, 0, NT - i)


def _ada_kernel(c_ref, w_ref, b_ref, o_ref):
    s = _silu(c_ref[...])
    o_ref[0] = _dot_x3(s, w_ref[0]) + b_ref[0]


def ada_mod(cvec, w_ada, b_ada):
    nb = 1024
    return pl.pallas_call(
        _ada_kernel,
        grid=(DEPTH, 6 * D_MODEL // nb),
        in_specs=[pl.BlockSpec((SUB, D_MODEL), lambda l, j: (0, 0)),
                  pl.BlockSpec((1, D_MODEL, nb), lambda l, j: (l, 0, j)),
                  pl.BlockSpec((1, 1, nb), lambda l, j: (l, 0, j))],
        out_specs=pl.BlockSpec((1, SUB, nb), lambda l, j: (l, 0, j)),
        out_shape=jax.ShapeDtypeStruct((DEPTH, SUB, 6 * D_MODEL), F32),
        compiler_params=_cp("arbitrary", "arbitrary"),
        name="ada_mod",
    )(cvec, w_ada, b_ada.reshape(DEPTH, 1, 6 * D_MODEL))


def _modulated_norm(x, nw, shift, scale):
    y = x * lax.rsqrt(jnp.mean(x * x, axis=-1, keepdims=True) + EPS) * nw
    return y * (1.0 + scale) + shift


def _inproj_kernel(x_ref, mod_ref, nw_ref, w_ref, u_ref, q_ref, k_ref, v_ref, z_ref, xbc_ref, dt_ref):
    h = _modulated_norm(x_ref[0], nw_ref[...], mod_ref[0, 0, 0:1, :], mod_ref[0, 0, 1:2, :]).astype(BF16)

    def proj(lo, hi):
        return _dot(h, w_ref[:, lo:hi])

    u_ref[0] = proj(C_U, C_Q).astype(BF16)
    q_ref[0] = (proj(C_Q, C_K) * (NA_HEAD_DIM ** -0.5)).astype(BF16)
    k_ref[0] = proj(C_K, C_V).astype(BF16)
    v_ref[0] = proj(C_V, C_Z).astype(BF16)
    z_ref[0] = proj(C_Z, C_XBC).astype(BF16)
    xbc_ref[0] = proj(C_XBC, C_DT).astype(BF16)
    dt_ref[0] = proj(C_DT, IN_COLS_PAD)


def in_proj(x, mod, norm_w, w_in_p):
    tok = lambda w, dt: jax.ShapeDtypeStruct((BATCH, T, w), dt)
    tspec = lambda w: pl.BlockSpec((1, TILE, w), lambda b, t: (b, t, 0))
    return pl.pallas_call(
        _inproj_kernel,
        grid=(BATCH, NT),
        in_specs=[tspec(D_MODEL),
                  pl.BlockSpec((1, 1, 6, D_MODEL), lambda b, t: (b, _seg(t), 0, 0)),
                  pl.BlockSpec((1, D_MODEL), lambda b, t: (0, 0)),
                  pl.BlockSpec((D_MODEL, IN_COLS_PAD), lambda b, t: (0, 0))],
        out_specs=[tspec(S5_WIDTH), tspec(NA_WIDTH), tspec(NA_WIDTH), tspec(NA_WIDTH),
                   tspec(SSD_WIDTH), tspec(SSD_XBC), tspec(LANE)],
        out_shape=[tok(S5_WIDTH, BF16), tok(NA_WIDTH, BF16), tok(NA_WIDTH, BF16), tok(NA_WIDTH, BF16),
                   tok(SSD_WIDTH, BF16), tok(SSD_XBC, BF16), tok(LANE, F32)],
        compiler_params=_cp("arbitrary", "arbitrary"),
        name="in_proj",
    )(x, mod, norm_w.reshape(1, D_MODEL), w_in_p)


S5_BLK = SUB
NB = T // S5_BLK
NB_CTX = CTX_LEN // S5_BLK
S5_NPAIR = S5_NGROUPS // 2
S5_PW = 2 * S5_BLK * S5_GROUP


def _s5_kernel(u_ref, w1_ref, w2_ref, w3_ref, mul_ref, y_ref, st_ref):
    n = S5_NSTATE
    half = LANE
    for d in range(2):
        for pp in range(S5_NPAIR):
            s = _dot(u_ref[0, :, pp * S5_PW:(pp + 1) * S5_PW], w1_ref[d, pp])
            st_ref[d, :, pp * half:(pp + 1) * half] = s[:, :half]
            st_ref[d, :, n + pp * half:n + (pp + 1) * half] = s[:, half:]

    ngrp = NB // SUB
    nctx = NB_CTX // SUB
    rowid = lax.broadcasted_iota(I32, (SUB, n), 0)
    for d in range(2):
        def body(j, carry, d=d):
            cr, ci = carry
            r = j if d == 0 else jnp.where(j < nctx, nctx - 1 - j, ngrp - 1 + nctx - j)
            row = pl.multiple_of(r * SUB, SUB)
            re = st_ref[d, pl.ds(row, SUB), 0:n]
            im = st_ref[d, pl.ds(row, SUB), n:2 * n]
            for kk, sh in enumerate((1, 2, 4)):
                mr = mul_ref[d, kk * SUB:(kk + 1) * SUB, 0:n]
                mi = mul_ref[d, kk * SUB:(kk + 1) * SUB, n:2 * n]
                s = sh if d == 0 else SUB - sh
                sr = pltpu.roll(re, s, 0)
                si = pltpu.roll(im, s, 0)
                re, im = re + (mr * sr - mi * si), im + (mr * si + mi * sr)
            pr = mul_ref[d, 3 * SUB:4 * SUB, 0:n]
            pi = mul_ref[d, 3 * SUB:4 * SUB, n:2 * n]
            re, im = re + (pr * cr - pi * ci), im + (pr * ci + pi * cr)
            edge, last, sh = (0, SUB - 1, 1) if d == 0 else (SUB - 1, 0, SUB - 1)
            st_ref[d, pl.ds(row, SUB), 0:n] = jnp.where(rowid == edge, cr, pltpu.roll(re, sh, 0))
            st_ref[d, pl.ds(row, SUB), n:2 * n] = jnp.where(rowid == edge, ci, pltpu.roll(im, sh, 0))
            return re[last:last + 1, :], im[last:last + 1, :]

        zero = jnp.zeros((1, n), F32)
        lax.fori_loop(0, ngrp, body, (zero, zero), unroll=2)

    for pp in range(S5_NPAIR):
        up = u_ref[0, :, pp * S5_PW:(pp + 1) * S5_PW]
        acc = None
        for d in range(2):
            enter = jnp.concatenate([st_ref[d, :, pp * half:(pp + 1) * half],
                                     st_ref[d, :, n + pp * half:n + (pp + 1) * half]], axis=1).astype(BF16)
            term = _dot(up, w2_ref[d, pp]) + _dot(enter, w3_ref[d, pp])
            acc = term if acc is None else acc + term
        y_ref[0, :, pp * S5_PW:(pp + 1) * S5_PW] = acc


def s5_mix(u, w1, w2, w3, mul):
    G, C, J = S5_NGROUPS, S5_GROUP, S5_BLK
    ub = u.reshape(BATCH, NB, J, G, C).transpose(0, 1, 3, 2, 4).reshape(BATCH, NB, G * J * C)
    wspec = pl.BlockSpec((2, S5_NPAIR, S5_PW, S5_PW), lambda b: (0, 0, 0, 0))
    yb = pl.pallas_call(
        _s5_kernel,
        grid=(BATCH,),
        in_specs=[pl.BlockSpec((1, NB, G * J * C), lambda b: (b, 0, 0)), wspec, wspec, wspec,
                  pl.BlockSpec((2, 4 * SUB, 2 * S5_NSTATE), lambda b: (0, 0, 0))],
        out_specs=pl.BlockSpec((1, NB, G * J * C), lambda b: (b, 0, 0)),
        out_shape=jax.ShapeDtypeStruct((BATCH, NB, G * J * C), F32),
        scratch_shapes=[pltpu.VMEM((2, NB, 2 * S5_NSTATE), F32)],
        compiler_params=_cp("arbitrary"),
        name="s5_mix",
    )(ub, w1, w2, w3, mul)
    return yb.reshape(BATCH, NB, G, J, C).transpose(0, 1, 3, 2, 4).reshape(BATCH, T, G * C)


def s5_params(lam_re, lam_im, log_dt, b_re, b_im, c_re, c_im):
    G, P, C = S5_NGROUPS, S5_STATE, S5_GROUP
    lam = lax.complex(lam_re.astype(F32), lam_im.astype(F32))
    step = jnp.exp(log_dt.astype(F32))[..., None]
    log_lb = lam * step
    lam_bar = jnp.exp(log_lb)
    b_bar = ((lam_bar - 1.0) / lam)[..., None] * lax.complex(b_re.astype(F32), b_im.astype(F32))
    c_mat = lax.complex(c_re.astype(F32), c_im.astype(F32))
    J = S5_BLK
    jj = jnp.arange(J)
    hi = lax.Precision.HIGHEST

    def powers(expo):
        e = expo.astype(F32).reshape((2, 1, 1) + expo.shape[1:])
        return jnp.exp(log_lb.reshape((2, G, P) + (1,) * (expo.ndim - 1)) * e)

    def pair_blockdiag(m):
        r, s = m.shape[2:]
        m = m.reshape(2, G // 2, 2, r, s)
        return jnp.einsum('dqgrs,gh->dqgrhs', m, jnp.eye(2, dtype=F32)).reshape(2, G // 2, 2 * r, 2 * s)

    w1c = jnp.einsum('dgpj,dgpc->dgjcp', powers(jnp.stack([J - 1 - jj, jj])), b_bar,
                     precision=hi).reshape(2, G, J * C, P)
    w1 = jnp.concatenate([pair_blockdiag(jnp.real(w1c)), pair_blockdiag(jnp.imag(w1c))], axis=-1)
    lag = jj[None, :] - jj[:, None]
    lag = jnp.stack([lag, -lag])
    lam_k = jnp.where((lag >= 0).reshape(2, 1, 1, J, J), powers(jnp.maximum(lag, 0)), 0.0)
    w2 = jnp.real(jnp.einsum('dgcp,dgpji,dgpk->dgjkic', c_mat, lam_k, b_bar, precision=hi))
    w2 = pair_blockdiag(w2.reshape(2, G, J * C, J * C))
    cl = jnp.einsum('dgcp,dgpi->dgpic', c_mat, powers(jnp.stack([jj + 1, J - jj])), precision=hi)
    cl = cl.reshape(2, G, P, J * C)
    w3 = jnp.concatenate([pair_blockdiag(jnp.real(cl)), pair_blockdiag(-jnp.imag(cl))], axis=2)
    rows = jnp.arange(SUB)
    pieces = []
    for d in range(2):
        log_blk = (log_lb[d] * float(J)).reshape(1, G * P)
        per_d = []
        for sh in (1, 2, 4):
            valid = (rows >= sh) if d == 0 else (rows < SUB - sh)
            per_d.append(jnp.where(valid[:, None], jnp.exp(log_blk * float(sh)), 0.0))
        expo = (rows + 1) if d == 0 else (SUB - rows)
        per_d.append(jnp.exp(log_blk * expo[:, None].astype(F32)))
        m = jnp.concatenate(per_d, axis=0)
        pieces.append(jnp.concatenate([jnp.real(m), jnp.imag(m)], axis=-1))
    mul = jnp.stack(pieces, axis=0).astype(F32)
    return w1.astype(BF16), w2.astype(BF16), w3.astype(BF16), mul


def _softmax_pv(parts):
    m = parts[0][0].max(axis=-1, keepdims=True)
    for s, _ in parts[1:]:
        m = jnp.maximum(m, s.max(axis=-1, keepdims=True))
    den = 0.0
    acc = 0.0
    for s, v in parts:
        p = jnp.exp(s - m)
        den = den + p.sum(axis=-1, keepdims=True)
        acc = acc + _dot(p.astype(BF16), v)
    return acc / den


def _na_kernel(q_ref, k_ref, v_ref, bias_ref, o_ref):
    t = pl.program_id(1)
    first = lax.broadcasted_iota(I32, (1, LANE), 1) < NA_HEAD_DIM

    def pair_attention(pp, start):
        ls = slice(pp * LANE, (pp + 1) * LANE)
        qp = q_ref[0, :, ls]
        kc = k_ref[0, 0:CTX_LEN, ls]
        vc = v_ref[0, 0:CTX_LEN, ls]
        outs = []
        for hh in range(2):
            qm = jnp.where(first if hh == 0 else ~first, qp, jnp.zeros_like(qp))
            parts = [(_dot_nt(qm, kc), vc)]
            if start is not None:
                kb = k_ref[0, pl.ds(start, NA_BAND * GRID_W), ls]
                vb = v_ref[0, pl.ds(start, NA_BAND * GRID_W), ls]
                parts.append((_dot_nt(qm, kb) + bias_ref[2 * pp + hh, 0], vb))
            outs.append(_softmax_pv(parts))
        o_ref[0, :, ls] = jnp.where(first, outs[0], outs[1]).astype(BF16)

    @pl.when(t == 0)
    def _():
        for pp in range(NA_HEADS // 2):
            pair_attention(pp, None)

    @pl.when(t > 0)
    def _():
        first_row = (t - 1) * (TILE // GRID_W)
        u0 = jnp.clip(first_row - NA_KH // 2, 0, ROWS - NA_BAND)
        start = pl.multiple_of(CTX_LEN + u0 * GRID_W, LANE)
        for pp in range(NA_HEADS // 2):
            pair_attention(pp, start)


def _na_cfg(t):
    return jnp.where(t <= 1, 0, jnp.where(t == NT - 1, 2, 1))


def natten(q, k, v, bias):
    whole = pl.BlockSpec((1, T, NA_WIDTH), lambda b, t: (b, 0, 0))
    tile = pl.BlockSpec((1, TILE, NA_WIDTH), lambda b, t: (b, t, 0))
    return pl.pallas_call(
        _na_kernel,
        grid=(BATCH, NT),
        in_specs=[tile, whole, whole,
                  pl.BlockSpec((NA_HEADS, 1, TILE, NA_BAND * GRID_W), lambda b, t: (0, _na_cfg(t), 0, 0))],
        out_specs=tile,
        out_shape=jax.ShapeDtypeStruct((BATCH, T, NA_WIDTH), BF16),
        compiler_params=_cp("arbitrary", "arbitrary"),
        name="natten",
    )(q, k, v, bias)


def natten_bias(rpb):
    rows_per_tile = TILE // GRID_W
    col = jnp.arange(GRID_W)
    c0 = jnp.clip(col - NA_KW // 2, 0, GRID_W - NA_KW)
    in_win = (col[None, :] >= c0[:, None]) & (col[None, :] < c0[:, None] + NA_KW)
    rel_c = jnp.clip(col[None, :] - col[:, None] + (NA_KW - 1), 0, RPB_W - 1)
    first_row = jnp.array([0, 2 * rows_per_tile, ROWS - rows_per_tile])
    u0 = jnp.clip(first_row - NA_KH // 2, 0, ROWS - NA_BAND)
    r = first_row[:, None] + jnp.arange(rows_per_tile)[None, :]
    kr = u0[:, None] + jnp.arange(NA_BAND)[None, :]
    r0 = jnp.clip(r - NA_KH // 2, 0, ROWS - NA_KH)
    in_band = (kr[:, None, :] >= r0[:, :, None]) & (kr[:, None, :] < r0[:, :, None] + NA_KH)
    rel_r = jnp.clip(kr[:, None, :] - r[:, :, None] + (NA_KH - 1), 0, 2 * NA_KH - 2)
    pick_r = jax.nn.one_hot(rel_r, 2 * NA_KH - 1, dtype=F32)
    pick_c = jax.nn.one_hot(rel_c, RPB_W, dtype=F32)
    b = jnp.einsum('hyx,arjy,qkx->harqjk', rpb.astype(F32), pick_r, pick_c,
                   precision=lax.Precision.HIGHEST)
    ok = in_band[None, :, :, None, :, None] & in_win[None, None, None, :, None, :]
    return jnp.where(ok, b, NEG).reshape(NA_HEADS, 3, TILE, NA_BAND * GRID_W)


def _softplus(x):
    return jnp.maximum(x, 0.0) + jnp.log(1.0 + jnp.exp(-jnp.abs(x)))


def _ssd_prep_kernel(prev_ref, cur_ref, next_ref, dtr_ref, cw_ref, cb_ref, dtb_ref, ar_ref, cos_ref, sin_ref,
                     xs_ref, bm_ref, cm_ref, dt_ref, a_ref):
    t = pl.program_id(1)
    halo = prev_ref.shape[1]
    has_prev = t >= 2
    has_next = (t >= 1) & (t <= NT - 2)
    prev = jnp.where(has_prev, prev_ref[0].astype(F32), 0.0)
    nxt = jnp.where(has_next, next_ref[0].astype(F32), 0.0)
    ext = jnp.concatenate([prev, cur_ref[0].astype(F32), nxt], axis=0)
    n = ext.shape[0]
    acc = cb_ref[...] + cw_ref[SSD_CONV // 2:SSD_CONV // 2 + 1, :] * ext
    for kk in range(SSD_CONV):
        off = kk - SSD_CONV // 2
        if off != 0:
            acc = acc + cw_ref[kk:kk + 1, :] * pltpu.roll(ext, (-off) % n, 0)
    y = _silu(acc[halo:halo + TILE, :])
    xs_ref[0] = y[:, 0:SSD_WIDTH].astype(BF16)

    lane = lax.broadcasted_iota(I32, (1, LANE), 1)
    low = (lane & (NA_HEAD_DIM // 2)) == 0
    cos = cos_ref[...]
    sin = sin_ref[...]
    for g in range(2 * SSD_NGROUPS):
        v = y[:, SSD_WIDTH + g * LANE:SSD_WIDTH + (g + 1) * LANE]
        sw = jnp.where(low, pltpu.roll(v, LANE - 32, 1), pltpu.roll(v, 32, 1))
        rot = (v * cos + sw * sin).astype(BF16)
        if g < SSD_NGROUPS:
            bm_ref[0, :, g * LANE:(g + 1) * LANE] = rot
        else:
            cm_ref[0, :, (g - SSD_NGROUPS) * LANE:(g - SSD_NGROUPS + 1) * LANE] = rot

    dt = _softplus(dtr_ref[0] + dtb_ref[...])
    dt_ref[0] = dt
    a_ref[0] = dt * ar_ref[...]


def ssd_prep(xbc, dt_raw, conv_w, conv_b, dt_bias, a_log, cos_t, sin_t):
    halo = 16
    per = TILE // halo
    nhalo = T // halo
    tok = lambda w, dt: jax.ShapeDtypeStruct((BATCH, T, w), dt)
    tspec = lambda w: pl.BlockSpec((1, TILE, w), lambda b, t: (b, t, 0))
    row = lambda w: pl.BlockSpec((1, w), lambda b, t: (0, 0))
    cw = jnp.zeros((SUB, SSD_XBC), F32).at[:SSD_CONV].set(conv_w.astype(F32))
    pad12 = lambda v: jnp.zeros((1, LANE), F32).at[0, :2 * SSD_HEADS].set(v.astype(F32).reshape(-1))
    return pl.pallas_call(
        _ssd_prep_kernel,
        grid=(BATCH, NT),
        in_specs=[pl.BlockSpec((1, halo, SSD_XBC), lambda b, t: (b, jnp.maximum(t * per - 1, 0), 0)),
                  tspec(SSD_XBC),
                  pl.BlockSpec((1, halo, SSD_XBC), lambda b, t: (b, jnp.minimum((t + 1) * per, nhalo - 1), 0)),
                  tspec(LANE),
                  pl.BlockSpec((SUB, SSD_XBC), lambda b, t: (0, 0)), row(SSD_XBC), row(LANE), row(LANE),
                  pl.BlockSpec((TILE, LANE), lambda b, t: (t, 0)), pl.BlockSpec((TILE, LANE), lambda b, t: (t, 0))],
        out_specs=[tspec(SSD_WIDTH), tspec(SSD_BC), tspec(SSD_BC), tspec(LANE), tspec(LANE)],
        out_shape=[tok(SSD_WIDTH, BF16), tok(SSD_BC, BF16), tok(SSD_BC, BF16), tok(LANE, F32), tok(LANE, F32)],
        compiler_params=_cp("arbitrary", "arbitrary"),
        name="ssd_prep",
    )(xbc, xbc, xbc, dt_raw, cw, conv_b.astype(F32).reshape(1, SSD_XBC), pad12(dt_bias),
      pad12(-jnp.exp(a_log.astype(F32))), cos_t, sin_t)


def rope_tables():
    half = SSD_STATE // 2
    nf = half // 2
    pos = jnp.arange(SEQ)
    inv_freq = ROPE_BASE ** (-jnp.arange(nf, dtype=F32) / nf)
    lane = jnp.arange(LANE)
    p = jnp.where(lane[None, :] < half, (pos // GRID_W)[:, None], (pos % GRID_W)[:, None]).astype(F32)
    ang = p * inv_freq[lane % nf][None, :]
    sign = jnp.where((lane & nf) == 0, -1.0, 1.0)[None, :]
    cos_t = jnp.concatenate([jnp.ones((CTX_LEN, LANE), F32), jnp.cos(ang)], axis=0)
    sin_t = jnp.concatenate([jnp.zeros((CTX_LEN, LANE), F32), jnp.sin(ang) * sign], axis=0)
    return cos_t, sin_t


def _ssd_dir(d, xs_ref, bm_ref, cm_ref, bt_ref, dt_ref, a_ref, at_ref, tri_ref, y_ref, st_ref):
    q = TILE
    lane = lax.broadcasted_iota(I32, (1, LANE), 1)
    first = lane < SSD_HEAD_DIM
    ri = lax.broadcasted_iota(I32, (q, q), 0)
    ci = lax.broadcasted_iota(I32, (q, q), 1)
    keep = (ci <= ri) if d == 0 else (ci >= ri)
    end = q - 1 if d == 0 else 0
    tri_col = tri_ref[d]
    tri_row = tri_ref[1 - d]
    a = a_ref[0]
    dt = dt_ref[0]
    cs_col = _dot_exact_rhs(tri_col, a)
    cs_row = _dot_exact_lhs(at_ref[0], tri_row)
    g_mats = [_dot_nt(cm_ref[0, :, g * SSD_STATE:(g + 1) * SSD_STATE],
                      bm_ref[0, :, g * SSD_STATE:(g + 1) * SSD_STATE]) for g in range(SSD_NGROUPS)]

    def head_col(m, h):
        c = d * SSD_HEADS + h
        return m[:, c:c + 1]

    for pp in range(SSD_HEADS // 2):
        ls = slice(pp * LANE, (pp + 1) * LANE)
        h0, h1 = 2 * pp, 2 * pp + 1
        x = xs_ref[0, :, ls].astype(F32)
        dt_l = jnp.where(first, head_col(dt, h0), head_col(dt, h1))
        cs_l = jnp.where(first, head_col(cs_col, h0), head_col(cs_col, h1))
        cs_end = cs_l[end:end + 1, :]
        xdt = x * dt_l
        xdt_b = xdt.astype(BF16)
        xw = (xdt * jnp.exp(cs_end - cs_l)).astype(BF16)
        st = st_ref[d, pp]
        st_b = st.astype(BF16)
        ys, ups = [], []
        for h in (h0, h1):
            g = h // (SSD_HEADS // SSD_NGROUPS)
            c = d * SSD_HEADS + h
            diff = head_col(cs_col, h) - cs_row[c:c + 1, :]
            decay = jnp.where(keep, jnp.exp(jnp.where(keep, diff, 0.0)), 0.0)
            m = (g_mats[g] * decay).astype(BF16)
            y_h = _dot(m, xdt_b) + _dot(cm_ref[0, :, g * SSD_STATE:(g + 1) * SSD_STATE], st_b) * jnp.exp(cs_l)
            ys.append(y_h)
            ups.append(_dot(bt_ref[0, g * SSD_STATE:(g + 1) * SSD_STATE, :], xw))
        y_ref[0, :, ls] = jnp.where(first, ys[0], ys[1])
        st_ref[d, pp] = jnp.exp(cs_end) * st + jnp.where(first, ups[0], ups[1])


def _ssd_scan_kernel(xs_f, bm_f, cm_f, bt_f, dt_f, a_f, at_f, xs_b, bm_b, cm_b, bt_b, dt_b, a_b, at_b, tri_ref,
                     yf_ref, yb_ref, st_ref):
    @pl.when(pl.program_id(1) == 0)
    def _():
        st_ref[...] = jnp.zeros_like(st_ref)

    _ssd_dir(0, xs_f, bm_f, cm_f, bt_f, dt_f, a_f, at_f, tri_ref, yf_ref, st_ref)
    _ssd_dir(1, xs_b, bm_b, cm_b, bt_b, dt_b, a_b, at_b, tri_ref, yb_ref, st_ref)


def ssd_scan(xs, bm, cm, dt, a):
    bt = jnp.swapaxes(bm, 1, 2)
    at = jnp.swapaxes(a[:, :, :2 * SUB], 1, 2)
    idx = jnp.arange(TILE)
    tri = jnp.stack([idx[None, :] <= idx[:, None], idx[None, :] >= idx[:, None]]).astype(BF16)
    fwd = lambda b, i: (b, i, 0)
    bwd = lambda b, i: (b, _bwd_tile(i), 0)
    fwd_t = lambda b, i: (b, 0, i)
    bwd_t = lambda b, i: (b, 0, _bwd_tile(i))

    def specs(f, ft):
        return [pl.BlockSpec((1, TILE, SSD_WIDTH), f), pl.BlockSpec((1, TILE, SSD_BC), f),
                pl.BlockSpec((1, TILE, SSD_BC), f), pl.BlockSpec((1, SSD_BC, TILE), ft),
                pl.BlockSpec((1, TILE, LANE), f), pl.BlockSpec((1, TILE, LANE), f),
                pl.BlockSpec((1, 2 * SUB, TILE), ft)]

    args = (xs, bm, cm, bt, dt, a, at)
    return pl.pallas_call(
        _ssd_scan_kernel,
        grid=(BATCH, NT),
        in_specs=specs(fwd, fwd_t) + specs(bwd, bwd_t) + [pl.BlockSpec((2, TILE, TILE), lambda b, i: (0, 0, 0))],
        out_specs=[pl.BlockSpec((1, TILE, SSD_WIDTH), fwd), pl.BlockSpec((1, TILE, SSD_WIDTH), bwd)],
        out_shape=[jax.ShapeDtypeStruct((BATCH, T, SSD_WIDTH), F32)] * 2,
        scratch_shapes=[pltpu.VMEM((2, SSD_HEADS // 2, SSD_STATE, LANE), F32)],
        compiler_params=_cp("arbitrary", "arbitrary"),
        name="ssd_scan",
    )(*args, *args, tri)


def _gelu_tanh(x):
    return 0.5 * x * (1.0 + jnp.tanh(math.sqrt(2.0 / math.pi) * (x + 0.044715 * (x * x * x))))


def _post_kernel(x_ref, mod_ref, u_ref, s5y_ref, na_ref, xs_ref, z_ref, sdf_ref, sdb_ref,
                 s5d_ref, gw_ref, gb_ref, sdd_ref, snw_ref, wo_ref, n2w_ref, rt_ref,
                 x1_ref, h_ref, lg_ref):
    ys5 = u_ref[0].astype(F32) * s5d_ref[...] + s5y_ref[0]
    g = _gelu_tanh(ys5)
    s5o = g * jax.nn.sigmoid(_dot(g.astype(BF16), gw_ref[...]) + gb_ref[...])
    yssd = (xs_ref[0].astype(F32) * sdd_ref[...] + sdf_ref[0] + sdb_ref[0]) * _silu(z_ref[0].astype(F32))
    ssdo = yssd * lax.rsqrt(jnp.mean(yssd * yssd, axis=-1, keepdims=True) + EPS) * snw_ref[...]
    mix = jnp.concatenate([s5o.astype(BF16), na_ref[0], ssdo.astype(BF16)], axis=-1)
    x1 = x_ref[0] + mod_ref[0, 0, 2:3, :] * _dot(mix, wo_ref[...])
    x1_ref[0] = x1
    h = _modulated_norm(x1, n2w_ref[...], mod_ref[0, 0, 3:4, :], mod_ref[0, 0, 4:5, :])
    h_ref[0] = h.astype(BF16)
    lg_ref[0] = _dot_x3(h, rt_ref[...])


def post_mixer(x, mod, u, s5y, na, xs, z, sdf, sdb, s5_d, glu_w, glu_b, ssd_d, ssd_norm_w, w_out, norm2_w,
               router):
    tspec = lambda w: pl.BlockSpec((1, TILE, w), lambda b, t: (b, t, 0))
    whole = lambda *shp: pl.BlockSpec(shp, lambda b, t: (0,) * len(shp))
    rt = jnp.zeros((D_MODEL, LANE), F32).at[:, :N_EXPERTS].set(router.astype(F32))
    return pl.pallas_call(
        _post_kernel,
        grid=(BATCH, NT),
        in_specs=[tspec(D_MODEL), pl.BlockSpec((1, 1, 6, D_MODEL), lambda b, t: (b, _seg(t), 0, 0)),
                  tspec(S5_WIDTH), tspec(S5_WIDTH), tspec(NA_WIDTH),
                  tspec(SSD_WIDTH), tspec(SSD_WIDTH), tspec(SSD_WIDTH), tspec(SSD_WIDTH),
                  whole(1, S5_WIDTH), whole(S5_WIDTH, S5_WIDTH), whole(1, S5_WIDTH),
                  whole(1, SSD_WIDTH), whole(1, SSD_WIDTH), whole(D_MODEL, D_MODEL), whole(1, D_MODEL),
                  whole(D_MODEL, LANE)],
        out_specs=[tspec(D_MODEL), tspec(D_MODEL), tspec(LANE)],
        out_shape=[jax.ShapeDtypeStruct((BATCH, T, D_MODEL), F32), jax.ShapeDtypeStruct((BATCH, T, D_MODEL), BF16),
                   jax.ShapeDtypeStruct((BATCH, T, LANE), F32)],
        compiler_params=_cp("arbitrary", "arbitrary"),
        name="post_mixer",
    )(x, mod, u, s5y, na, xs, z, sdf, sdb,
      s5_d.astype(F32).reshape(1, S5_WIDTH), glu_w.astype(BF16), glu_b.astype(F32).reshape(1, S5_WIDTH),
      jnp.repeat(ssd_d.astype(F32), SSD_HEAD_DIM).reshape(1, SSD_WIDTH), ssd_norm_w.astype(F32).reshape(1, SSD_WIDTH),
      w_out.astype(BF16), norm2_w.astype(F32).reshape(1, D_MODEL), rt)


def _route_kernel(lg_ref, tri_ref, slot_ref, aff_ref, *, with_ctx):
    lg = lg_ref[0]
    m = lg.max(axis=0, keepdims=True)
    e = jnp.exp(lg - m)
    aff = e / e.sum(axis=0, keepdims=True)
    aff_ref[0] = aff
    bits = pltpu.bitcast(aff, I32)
    is_ctx = lax.broadcasted_iota(I32, (N_EXPERTS, T), 1) < CTX_LEN

    def count(mask):
        return jnp.where(mask, 1.0, 0.0).sum(axis=1, keepdims=True)

    def kth_largest(seg, k):
        def body(i, prefix):
            cand = prefix | lax.shift_left(jnp.int32(1), 30 - i)
            return jnp.where(count((bits >= cand) & seg) >= k, cand, prefix)
        return lax.fori_loop(0, 31, body, jnp.zeros((N_EXPERTS, 1), I32))

    def excl_cumsum(x01):
        carry = jnp.zeros((N_EXPERTS, 1), F32)
        pieces = []
        for j in range(T // LANE):
            blk = x01[:, j * LANE:(j + 1) * LANE]
            inc = _dot(blk.astype(BF16), tri_ref[...])
            pieces.append(inc - blk + carry)
            carry = carry + inc[:, LANE - 1:LANE]
        return jnp.concatenate(pieces, axis=1)

    thr = kth_largest(~is_ctx, float(CAP_LAT))
    k_of = jnp.full((N_EXPERTS, T), float(CAP_LAT), F32)
    if with_ctx:
        thr = jnp.where(is_ctx, kth_largest(is_ctx, float(CAP_CTX)), thr)
        k_of = jnp.where(is_ctx, float(CAP_CTX), k_of)
    gt = bits > thr
    eq = bits == thr
    if not with_ctx:
        gt = gt & ~is_ctx
        eq = eq & ~is_ctx
    n_gt = jnp.where(is_ctx, count(gt & is_ctx), count(gt & ~is_ctx))
    tie_rank = excl_cumsum(jnp.where(eq, 1.0, 0.0))
    tie_rank = tie_rank - jnp.where(is_ctx, 0.0, count(eq & is_ctx))
    sel = gt | (eq & (tie_rank < k_of - n_gt))
    pos = excl_cumsum(jnp.where(sel, 1.0, 0.0))
    slot = jnp.where(is_ctx, pos + float(CAP_LAT), pos - count(sel & is_ctx))
    slot_ref[0] = jnp.where(sel, slot, -1.0).astype(I32)


def route(logits_t, with_ctx):
    idx = jnp.arange(LANE)
    tri = (idx[:, None] <= idx[None, :]).astype(BF16)
    spec = pl.BlockSpec((1, N_EXPERTS, T), lambda b: (b, 0, 0))
    return pl.pallas_call(
        functools.partial(_route_kernel, with_ctx=with_ctx),
        grid=(BATCH,),
        in_specs=[spec, pl.BlockSpec((LANE, LANE), lambda b: (0, 0))],
        out_specs=[spec, spec],
        out_shape=[jax.ShapeDtypeStruct((BATCH, N_EXPERTS, T), I32), jax.ShapeDtypeStruct((BATCH, N_EXPERTS, T), F32)],
        compiler_params=_cp("arbitrary"),
        name="route",
    )(logits_t, tri)


GATHER_WIN = LANE
COMBINE_WIN = 2 * LANE


def slot_ranges(slot):
    s = slot.reshape(BATCH, N_EXPERTS, NT, TILE)
    has = s >= 0
    smax = jnp.max(jnp.where(has, s, -1), axis=-1)
    smin = jnp.where(smax >= 0, jnp.min(jnp.where(has, s, CAP_LAT + CAP_CTX), axis=-1), 0)
    return smin.reshape(-1).astype(I32), smax.reshape(-1).astype(I32)


def _ffn_kernel(smin_ref, smax_ref, h_ref, slot_ref, wg_ref, wu_ref, wd_ref, y_ref, xs_ref, *, nslot):
    base = (pl.program_id(0) * N_EXPERTS + pl.program_id(1)) * NT
    sid = lax.broadcasted_iota(I32, (GATHER_WIN, TILE), 0)
    xs_ref[0:CAP_LAT, :] = jnp.zeros((CAP_LAT, D_MODEL), F32)

    def add_rows(t, ws, lo):
        srow = slot_ref[0, 0, pl.ds(t, 1), :]
        onehot = jnp.where((sid + ws == srow) & (srow >= lo), 1.0, 0.0).astype(BF16)
        tok = pl.multiple_of(t * TILE, TILE)
        rows = pl.ds(pl.multiple_of(ws, SUB), GATHER_WIN)
        xs_ref[rows, :] += _dot(onehot, h_ref[0, pl.ds(tok, TILE), :])

    for t in range(CTX_LEN // TILE, NT):
        lo = smin_ref[base + t] & ~(SUB - 1)
        add_rows(t, jnp.minimum(lo, CAP_LAT - GATHER_WIN), lo)
        extra = lax.shift_right_arithmetic(smax_ref[base + t] - lo, GATHER_WIN.bit_length() - 1)

        def more(k, carry, t=t, lo=lo):
            lo_k = lo + k * GATHER_WIN
            add_rows(t, jnp.minimum(lo_k, CAP_LAT - GATHER_WIN), lo_k)
            return carry

        lax.fori_loop(1, extra + 1, more, 0)
    if nslot > CAP_LAT:
        cid = lax.broadcasted_iota(I32, (nslot - CAP_LAT, CTX_LEN), 0) + CAP_LAT
        onehot = jnp.where(cid == slot_ref[0, 0, 0:1, :], 1.0, 0.0).astype(BF16)
        xs_ref[CAP_LAT:nslot, :] = _dot(onehot, h_ref[0, 0:CTX_LEN, :])
    xs = xs_ref[...].astype(BF16)
    hid = _silu(_dot(xs, wg_ref[0, 0])) * _dot(xs, wu_ref[0, 0])
    y_ref[0, 0] = _dot(hid.astype(BF16), wd_ref[0, 0]).astype(BF16)


def moe_ffn(h, slot, smin, smax, layer, wg, wu, wd, nslot):
    wspec = lambda: pl.BlockSpec((1, 1, D_MODEL, D_EXPERT), lambda b, e, *_: (layer, e, 0, 0))
    return pl.pallas_call(
        functools.partial(_ffn_kernel, nslot=nslot),
        grid_spec=pltpu.PrefetchScalarGridSpec(
            num_scalar_prefetch=2,
            grid=(BATCH, N_EXPERTS),
            in_specs=[pl.BlockSpec((1, T, D_MODEL), lambda b, e, *_: (b, 0, 0)),
                      pl.BlockSpec((1, 1, NT, TILE), lambda b, e, *_: (b, e, 0, 0)),
                      wspec(), wspec(), wspec()],
            out_specs=pl.BlockSpec((1, 1, nslot, D_MODEL), lambda b, e, *_: (b, e, 0, 0)),
            scratch_shapes=[pltpu.VMEM((nslot, D_MODEL), F32)]),
        out_shape=jax.ShapeDtypeStruct((BATCH, N_EXPERTS, nslot, D_MODEL), BF16),
        compiler_params=_cp("arbitrary", "arbitrary"),
        name="moe_ffn",
    )(smin, smax, h, slot.reshape(BATCH, N_EXPERTS, NT, TILE), wg, wu, wd)


def _combine_kernel(smin_ref, smax_ref, x_ref, mod_ref, slot_ref, aff_ref, y_ref, o_ref, acc_ref, *, nslot):
    b = pl.program_id(0)
    t = pl.program_id(2)
    shift = COMBINE_WIN.bit_length() - 1

    def weights(e, lane, lo=None):
        s = slot_ref[0, :, e:e + 1]
        hit = (s == lane) if lo is None else ((s == lane) & (s >= lo))
        return jnp.where(hit, aff_ref[0, :, e:e + 1], 0.0).astype(BF16)

    def finish(acc):
        o_ref[0] = x_ref[0] + mod_ref[0, 0, 5:6, :] * acc

    @pl.when(t == 0)
    def _():
        if nslot > CAP_LAT:
            lane = lax.broadcasted_iota(I32, (TILE, nslot - CAP_LAT), 1) + CAP_LAT
            acc = jnp.zeros((TILE, y_ref.shape[-1]), F32)
            for e in range(N_EXPERTS):
                acc = acc + _dot(weights(e, lane), y_ref[0, e, CAP_LAT:nslot, :])
            finish(acc)
        else:
            o_ref[0] = x_ref[0]

    @pl.when(t > 0)
    def _():
        lane = lax.broadcasted_iota(I32, (TILE, COMBINE_WIN), 1)
        acc = jnp.zeros((TILE, y_ref.shape[-1]), F32)
        los, extras = [], []
        for e in range(N_EXPERTS):
            base = (b * N_EXPERTS + e) * NT + t
            lo = smin_ref[base] & ~(LANE - 1)
            ws = pl.multiple_of(jnp.minimum(lo, CAP_LAT - COMBINE_WIN), LANE)
            acc = acc + _dot(weights(e, lane + ws), y_ref[0, e, pl.ds(ws, COMBINE_WIN), :])
            los.append(lo)
            extras.append(lax.shift_right_arithmetic(smax_ref[base] - lo, shift))
        acc_ref[...] = acc
        most = extras[0]
        for x in extras[1:]:
            most = jnp.maximum(most, x)

        @pl.when(most > 0)
        def _():
            for e in range(N_EXPERTS):
                def more(k, carry, e=e):
                    lo_k = los[e] + k * COMBINE_WIN
                    ws = pl.multiple_of(jnp.minimum(lo_k, CAP_LAT - COMBINE_WIN), LANE)
                    acc_ref[...] += _dot(weights(e, lane + ws, lo_k), y_ref[0, e, pl.ds(ws, COMBINE_WIN), :])
                    return carry

                lax.fori_loop(1, extras[e] + 1, more, 0)

        finish(acc_ref[...])


def moe_combine(x1, mod, slot_tok, aff_tok, y, smin, smax, nslot):
    nh = 2
    dh = D_MODEL // nh
    return pl.pallas_call(
        functools.partial(_combine_kernel, nslot=nslot),
        grid_spec=pltpu.PrefetchScalarGridSpec(
            num_scalar_prefetch=2,
            grid=(BATCH, nh, NT),
            in_specs=[pl.BlockSpec((1, TILE, dh), lambda b, c, t, *_: (b, t, c)),
                      pl.BlockSpec((1, 1, 6, dh), lambda b, c, t, *_: (b, _seg(t), 0, c)),
                      pl.BlockSpec((1, TILE, N_EXPERTS), lambda b, c, t, *_: (b, t, 0)),
                      pl.BlockSpec((1, TILE, N_EXPERTS), lambda b, c, t, *_: (b, t, 0)),
                      pl.BlockSpec((1, N_EXPERTS, nslot, dh), lambda b, c, t, *_: (b, 0, 0, c))],
            out_specs=pl.BlockSpec((1, TILE, dh), lambda b, c, t, *_: (b, t, c)),
            scratch_shapes=[pltpu.VMEM((TILE, dh), F32)]),
        out_shape=jax.ShapeDtypeStruct((BATCH, T, D_MODEL), F32),
        compiler_params=_cp("arbitrary", "arbitrary", "arbitrary"),
        name="moe_combine",
    )(smin, smax, x1, mod, slot_tok, aff_tok, y)


def _final_kernel(x_ref, w_ref, o_ref):
    x = x_ref[0]
    o_ref[0] = x * lax.rsqrt(jnp.mean(x * x, axis=-1, keepdims=True) + EPS) * w_ref[...]


def final_norm(x, w):
    off = CTX_LEN // TILE
    return pl.pallas_call(
        _final_kernel,
        grid=(BATCH, SEQ // TILE),
        in_specs=[pl.BlockSpec((1, TILE, D_MODEL), lambda b, t: (b, t + off, 0)),
                  pl.BlockSpec((1, D_MODEL), lambda b, t: (0, 0))],
        out_specs=pl.BlockSpec((1, TILE, D_MODEL), lambda b, t: (b, t, 0)),
        out_shape=jax.ShapeDtypeStruct((BATCH, SEQ, D_MODEL), F32),
        compiler_params=_cp("arbitrary", "arbitrary"),
        name="final_norm",
    )(x, w.astype(F32).reshape(1, D_MODEL))


def trunk_layer(l, x, mod, cos_t, sin_t, norm1_w, norm2_w, w_in, w_out,
                s5_lam_re, s5_lam_im, s5_log_dt, s5_b_re, s5_b_im, s5_c_re, s5_c_im, s5_d, s5_glu_w, s5_glu_b,
                na_rpb, ssd_conv_w, ssd_conv_b, ssd_dt_bias, ssd_a_log, ssd_d, ssd_norm_w,
                moe_router, wg, wu, wd, with_ctx_out):
    w_in_p = jnp.zeros((D_MODEL, IN_COLS_PAD), BF16).at[:, :IN_COLS].set(w_in.astype(BF16))
    u, q, k, v, z, xbc, dt_raw = in_proj(x, mod, norm1_w, w_in_p)

    s5y = s5_mix(u, *s5_params(s5_lam_re, s5_lam_im, s5_log_dt, s5_b_re, s5_b_im, s5_c_re, s5_c_im))
    na = natten(q, k, v, natten_bias(na_rpb))
    xs, bm, cm, dt, a = ssd_prep(xbc, dt_raw, ssd_conv_w, ssd_conv_b, ssd_dt_bias, ssd_a_log, cos_t, sin_t)
    sdf, sdb = ssd_scan(xs, bm, cm, dt, a)

    x1, h, logits = post_mixer(x, mod, u, s5y, na, xs, z, sdf, sdb, s5_d, s5_glu_w, s5_glu_b,
                               ssd_d, ssd_norm_w, w_out, norm2_w, moe_router)
    slot, aff = route(jnp.swapaxes(logits[:, :, :N_EXPERTS], 1, 2), with_ctx_out)
    nslot = CAP_LAT + CAP_CTX if with_ctx_out else CAP_LAT
    smin, smax = slot_ranges(slot)
    y = moe_ffn(h, slot, smin, smax, l, wg, wu, wd, nslot)
    return moe_combine(x1, mod, jnp.swapaxes(slot, 1, 2), jnp.swapaxes(aff, 1, 2), y, smin, smax, nslot)


def kernel(x, c, ctx, c_ctx, w_ada, b_ada, norm1_w, norm2_w, w_in, w_out, s5_lam_re, s5_lam_im, s5_log_dt, s5_b_re, s5_b_im, s5_c_re, s5_c_im, s5_d, s5_glu_w, s5_glu_b, na_rpb, ssd_conv_w, ssd_conv_b, ssd_dt_bias, ssd_a_log, ssd_d, ssd_norm_w, moe_router, moe_w_gate, moe_w_up, moe_w_down, final_norm_w):
    xa = jnp.concatenate([ctx, x], axis=1).astype(F32)
    cvec = jnp.zeros((SUB, D_MODEL), F32).at[0].set(c_ctx.astype(F32)).at[1:1 + BATCH].set(c.astype(F32))
    mods = ada_mod(cvec, w_ada.astype(F32), b_ada.astype(F32)).reshape(DEPTH, SUB, 6, D_MODEL)
    cos_t, sin_t = rope_tables()
    wg, wu, wd = moe_w_gate.astype(BF16), moe_w_up.astype(BF16), moe_w_down.astype(BF16)
    for l in range(DEPTH):
        mod = jnp.stack([jnp.broadcast_to(mods[l, 0], (BATCH, 6, D_MODEL)), mods[l, 1:1 + BATCH]], axis=1)
        xa = trunk_layer(
            l, xa, mod, cos_t, sin_t, norm1_w[l], norm2_w[l], w_in[l], w_out[l],
            s5_lam_re[l], s5_lam_im[l], s5_log_dt[l], s5_b_re[l], s5_b_im[l], s5_c_re[l], s5_c_im[l],
            s5_d[l], s5_glu_w[l], s5_glu_b[l],
            na_rpb[l], ssd_conv_w[l], ssd_conv_b[l], ssd_dt_bias[l], ssd_a_log[l], ssd_d[l], ssd_norm_w[l],
            moe_router[l], wg, wu, wd, l < DEPTH - 1)
    return final_norm(xa, final_norm_w)
```

```python
import functools
import math

import jax
import jax.numpy as jnp
from jax import lax
from jax.experimental import pallas as pl
from jax.experimental.pallas import tpu as pltpu

F32 = jnp.float32
BF16 = jnp.bfloat16
I32 = jnp.int32

D_MODEL = 1024
BATCH = 4
SEQ = 4096
DEPTH = 2
GRID_W = 64
CTX_LEN = 256
EPS = 1e-6

S5_WIDTH = 256
S5_GROUP = 16
S5_NGROUPS = 16
S5_STATE = 64
S5_NSTATE = S5_NGROUPS * S5_STATE

NA_HEADS = 6
NA_HEAD_DIM = 64
NA_WIDTH = 384
NA_KH = 8
NA_KW = 16
NA_BAND = 12
RPB_W = 2 * NA_KW - 1

SSD_HEADS = 6
SSD_HEAD_DIM = 64
SSD_WIDTH = 384
SSD_NGROUPS = 2
SSD_STATE = 128
SSD_CONV = 5
SSD_BC = 256
SSD_XBC = 896

N_EXPERTS = 16
D_EXPERT = 1024
ROPE_BASE = 10000.0

T = CTX_LEN + SEQ
TILE = 256
NT = T // TILE
LANE = 128
SUB = 8
ROWS = SEQ // GRID_W
CAP_LAT = 2 * SEQ // N_EXPERTS
CAP_CTX = 2 * CTX_LEN // N_EXPERTS
NEG = -1e30

C_U = 0
C_Q = 256
C_K = 640
C_V = 1024
C_Z = 1408
C_XBC = 1792
C_DT = 2688
IN_COLS = 2700
IN_COLS_PAD = 2816

VMEM_LIMIT = 56 * 1024 * 1024


def _cp(*sem):
    return pltpu.CompilerParams(dimension_semantics=sem, vmem_limit_bytes=VMEM_LIMIT)


def _dot(a, b):
    return jnp.dot(a, b, preferred_element_type=F32)


def _dot_nt(a, b):
    return lax.dot_general(a, b, (((1,), (1,)), ((), ())), preferred_element_type=F32)


def _split3(x):
    hi = x.astype(BF16)
    r = x - hi.astype(F32)
    mid = r.astype(BF16)
    lo = (r - mid.astype(F32)).astype(BF16)
    return hi, mid, lo


def _dot_exact_rhs(a_bf16, b_f32):
    hi, mid, lo = _split3(b_f32)
    return _dot(a_bf16, hi) + _dot(a_bf16, mid) + _dot(a_bf16, lo)


def _dot_exact_lhs(a_f32, b_bf16):
    hi, mid, lo = _split3(a_f32)
    return _dot(hi, b_bf16) + _dot(mid, b_bf16) + _dot(lo, b_bf16)


def _dot_x3(a, b):
    ah = a.astype(BF16)
    al = (a - ah.astype(F32)).astype(BF16)
    bh = b.astype(BF16)
    bl = (b - bh.astype(F32)).astype(BF16)
    return _dot(ah, bh) + _dot(ah, bl) + _dot(al, bh)


def _silu(x):
    return x * jax.nn.sigmoid(x)


def _seg(t):
    return jnp.where(t >= CTX_LEN // TILE, 1, 0)


def _bwd_tile(i):
    return jnp.where(i == 0, 0, NT - i)


def _ada_kernel(c_ref, w_ref, b_ref, o_ref):
    s = _silu(c_ref[...])
    o_ref[0] = _dot_x3(s, w_ref[0]) + b_ref[0]


def ada_mod(cvec, w_ada, b_ada):
    nb = 1024
    return pl.pallas_call(
        _ada_kernel,
        grid=(DEPTH, 6 * D_MODEL // nb),
        in_specs=[pl.BlockSpec((SUB, D_MODEL), lambda l, j: (0, 0)),
                  pl.BlockSpec((1, D_MODEL, nb), lambda l, j: (l, 0, j)),
                  pl.BlockSpec((1, 1, nb), lambda l, j: (l, 0, j))],
        out_specs=pl.BlockSpec((1, SUB, nb), lambda l, j: (l, 0, j)),
        out_shape=jax.ShapeDtypeStruct((DEPTH, SUB, 6 * D_MODEL), F32),
        compiler_params=_cp("arbitrary", "arbitrary"),
        name="ada_mod",
    )(cvec, w_ada, b_ada.reshape(DEPTH, 1, 6 * D_MODEL))


def _modulated_norm(x, nw, shift, scale):
    y = x * lax.rsqrt(jnp.mean(x * x, axis=-1, keepdims=True) + EPS) * nw
    return y * (1.0 + scale) + shift


def _inproj_kernel(x_ref, mod_ref, nw_ref, w_ref, u_ref, q_ref, k_ref, v_ref, z_ref, xbc_ref, dt_ref):
    h = _modulated_norm(x_ref[0], nw_ref[...], mod_ref[0, 0, 0:1, :], mod_ref[0, 0, 1:2, :]).astype(BF16)

    def proj(lo, hi):
        return _dot(h, w_ref[:, lo:hi])

    u_ref[0] = proj(C_U, C_Q).astype(BF16)
    q_ref[0] = (proj(C_Q, C_K) * (NA_HEAD_DIM ** -0.5)).astype(BF16)
    k_ref[0] = proj(C_K, C_V).astype(BF16)
    v_ref[0] = proj(C_V, C_Z).astype(BF16)
    z_ref[0] = proj(C_Z, C_XBC).astype(BF16)
    xbc_ref[0] = proj(C_XBC, C_DT).astype(BF16)
    dt_ref[0] = proj(C_DT, IN_COLS_PAD)


def in_proj(x, mod, norm_w, w_in_p):
    tok = lambda w, dt: jax.ShapeDtypeStruct((BATCH, T, w), dt)
    tspec = lambda w: pl.BlockSpec((1, TILE, w), lambda b, t: (b, t, 0))
    return pl.pallas_call(
        _inproj_kernel,
        grid=(BATCH, NT),
        in_specs=[tspec(D_MODEL),
                  pl.BlockSpec((1, 1, 6, D_MODEL), lambda b, t: (b, _seg(t), 0, 0)),
                  pl.BlockSpec((1, D_MODEL), lambda b, t: (0, 0)),
                  pl.BlockSpec((D_MODEL, IN_COLS_PAD), lambda b, t: (0, 0))],
        out_specs=[tspec(S5_WIDTH), tspec(NA_WIDTH), tspec(NA_WIDTH), tspec(NA_WIDTH),
                   tspec(SSD_WIDTH), tspec(SSD_XBC), tspec(LANE)],
        out_shape=[tok(S5_WIDTH, BF16), tok(NA_WIDTH, BF16), tok(NA_WIDTH, BF16), tok(NA_WIDTH, BF16),
                   tok(SSD_WIDTH, BF16), tok(SSD_XBC, BF16), tok(LANE, F32)],
        compiler_params=_cp("arbitrary", "arbitrary"),
        name="in_proj",
    )(x, mod, norm_w.reshape(1, D_MODEL), w_in_p)


S5_BLK = SUB
NB = T // S5_BLK
NB_CTX = CTX_LEN // S5_BLK
S5_NPAIR = S5_NGROUPS // 2
S5_PW = 2 * S5_BLK * S5_GROUP


def _s5_kernel(u_ref, w1_ref, w2_ref, w3_ref, mul_ref, y_ref, st_ref):
    n = S5_NSTATE
    half = LANE
    for d in range(2):
        for pp in range(S5_NPAIR):
            s = _dot(u_ref[0, :, pp * S5_PW:(pp + 1) * S5_PW], w1_ref[d, pp])
            st_ref[d, :, pp * half:(pp + 1) * half] = s[:, :half]
            st_ref[d, :, n + pp * half:n + (pp + 1) * half] = s[:, half:]

    ngrp = NB // SUB
    nctx = NB_CTX // SUB
    rowid = lax.broadcasted_iota(I32, (SUB, n), 0)
    for d in range(2):
        def body(j, carry, d=d):
            cr, ci = carry
            r = j if d == 0 else jnp.where(j < nctx, nctx - 1 - j, ngrp - 1 + nctx - j)
            row = pl.multiple_of(r * SUB, SUB)
            re = st_ref[d, pl.ds(row, SUB), 0:n]
            im = st_ref[d, pl.ds(row, SUB), n:2 * n]
            for kk, sh in enumerate((1, 2, 4)):
                mr = mul_ref[d, kk * SUB:(kk + 1) * SUB, 0:n]
                mi = mul_ref[d, kk * SUB:(kk + 1) * SUB, n:2 * n]
                s = sh if d == 0 else SUB - sh
                sr = pltpu.roll(re, s, 0)
                si = pltpu.roll(im, s, 0)
                re, im = re + (mr * sr - mi * si), im + (mr * si + mi * sr)
            pr = mul_ref[d, 3 * SUB:4 * SUB, 0:n]
            pi = mul_ref[d, 3 * SUB:4 * SUB, n:2 * n]
            re, im = re + (pr * cr - pi * ci), im + (pr * ci + pi * cr)
            edge, last, sh = (0, SUB - 1, 1) if d == 0 else (SUB - 1, 0, SUB - 1)
            st_ref[d, pl.ds(row, SUB), 0:n] = jnp.where(rowid == edge, cr, pltpu.roll(re, sh, 0))
            st_ref[d, pl.ds(row, SUB), n:2 * n] = jnp.where(rowid == edge, ci, pltpu.roll(im, sh, 0))
            return re[last:last + 1, :], im[last:last + 1, :]

        zero = jnp.zeros((1, n), F32)
        lax.fori_loop(0, ngrp, body, (zero, zero), unroll=2)

    for pp in range(S5_NPAIR):
        up = u_ref[0, :, pp * S5_PW:(pp + 1) * S5_PW]
        acc = None
        for d in range(2):
            enter = jnp.concatenate([st_ref[d, :, pp * half:(pp + 1) * half],
                                     st_ref[d, :, n + pp * half:n + (pp + 1) * half]], axis=1).astype(BF16)
            term = _dot(up, w2_ref[d, pp]) + _dot(enter, w3_ref[d, pp])
            acc = term if acc is None else acc + term
        y_ref[0, :, pp * S5_PW:(pp + 1) * S5_PW] = acc


def s5_mix(u, w1, w2, w3, mul):
    G, C, J = S5_NGROUPS, S5_GROUP, S5_BLK
    ub = u.reshape(BATCH, NB, J, G, C).transpose(0, 1, 3, 2, 4).reshape(BATCH, NB, G * J * C)
    wspec = pl.BlockSpec((2, S5_NPAIR, S5_PW, S5_PW), lambda b: (0, 0, 0, 0))
    yb = pl.pallas_call(
        _s5_kernel,
        grid=(BATCH,),
        in_specs=[pl.BlockSpec((1, NB, G * J * C), lambda b: (b, 0, 0)), wspec, wspec, wspec,
                  pl.BlockSpec((2, 4 * SUB, 2 * S5_NSTATE), lambda b: (0, 0, 0))],
        out_specs=pl.BlockSpec((1, NB, G * J * C), lambda b: (b, 0, 0)),
        out_shape=jax.ShapeDtypeStruct((BATCH, NB, G * J * C), F32),
        scratch_shapes=[pltpu.VMEM((2, NB, 2 * S5_NSTATE), F32)],
        compiler_params=_cp("arbitrary"),
        name="s5_mix",
    )(ub, w1, w2, w3, mul)
    return yb.reshape(BATCH, NB, G, J, C).transpose(0, 1, 3, 2, 4).reshape(BATCH, T, G * C)


def s5_params(lam_re, lam_im, log_dt, b_re, b_im, c_re, c_im):
    G, P, C = S5_NGROUPS, S5_STATE, S5_GROUP
    lam = lax.complex(lam_re.astype(F32), lam_im.astype(F32))
    step = jnp.exp(log_dt.astype(F32))[..., None]
    log_lb = lam * step
    lam_bar = jnp.exp(log_lb)
    b_bar = ((lam_bar - 1.0) / lam)[..., None] * lax.complex(b_re.astype(F32), b_im.astype(F32))
    c_mat = lax.complex(c_re.astype(F32), c_im.astype(F32))
    J = S5_BLK
    jj = jnp.arange(J)
    hi = lax.Precision.HIGHEST

    def powers(expo):
        e = expo.astype(F32).reshape((2, 1, 1) + expo.shape[1:])
        return jnp.exp(log_lb.reshape((2, G, P) + (1,) * (expo.ndim - 1)) * e)

    def pair_blockdiag(m):
        r, s = m.shape[2:]
        m = m.reshape(2, G // 2, 2, r, s)
        return jnp.einsum('dqgrs,gh->dqgrhs', m, jnp.eye(2, dtype=F32)).reshape(2, G // 2, 2 * r, 2 * s)

    w1c = jnp.einsum('dgpj,dgpc->dgjcp', powers(jnp.stack([J - 1 - jj, jj])), b_bar,
                     precision=hi).reshape(2, G, J * C, P)
    w1 = jnp.concatenate([pair_blockdiag(jnp.real(w1c)), pair_blockdiag(jnp.imag(w1c))], axis=-1)
    lag = jj[None, :] - jj[:, None]
    lag = jnp.stack([lag, -lag])
    lam_k = jnp.where((lag >= 0).reshape(2, 1, 1, J, J), powers(jnp.maximum(lag, 0)), 0.0)
    w2 = jnp.real(jnp.einsum('dgcp,dgpji,dgpk->dgjkic', c_mat, lam_k, b_bar, precision=hi))
    w2 = pair_blockdiag(w2.reshape(2, G, J * C, J * C))
    cl = jnp.einsum('dgcp,dgpi->dgpic', c_mat, powers(jnp.stack([jj + 1, J - jj])), precision=hi)
    cl = cl.reshape(2, G, P, J * C)
    w3 = jnp.concatenate([pair_blockdiag(jnp.real(cl)), pair_blockdiag(-jnp.imag(cl))], axis=2)
    rows = jnp.arange(SUB)
    pieces = []
    for d in range(2):
        log_blk = (log_lb[d] * float(J)).reshape(1, G * P)
        per_d = []
        for sh in (1, 2, 4):
            valid = (rows >= sh) if d == 0 else (rows < SUB - sh)
            per_d.append(jnp.where(valid[:, None], jnp.exp(log_blk * float(sh)), 0.0))
        expo = (rows + 1) if d == 0 else (SUB - rows)
        per_d.append(jnp.exp(log_blk * expo[:, None].astype(F32)))
        m = jnp.concatenate(per_d, axis=0)
        pieces.append(jnp.concatenate([jnp.real(m), jnp.imag(m)], axis=-1))
    mul = jnp.stack(pieces, axis=0).astype(F32)
    return w1.astype(BF16), w2.astype(BF16), w3.astype(BF16), mul


def _softmax_pv(parts):
    m = parts[0][0].max(axis=-1, keepdims=True)
    for s, _ in parts[1:]:
        m = jnp.maximum(m, s.max(axis=-1, keepdims=True))
    den = 0.0
    acc = 0.0
    for s, v in parts:
        p = jnp.exp(s - m)
        den = den + p.sum(axis=-1, keepdims=True)
        acc = acc + _dot(p.astype(BF16), v)
    return acc / den


def _na_kernel(q_ref, k_ref, v_ref, bias_ref, o_ref):
    t = pl.program_id(1)
    first = lax.broadcasted_iota(I32, (1, LANE), 1) < NA_HEAD_DIM

    def pair_attention(pp, start):
        ls = slice(pp * LANE, (pp + 1) * LANE)
        qp = q_ref[0, :, ls]
        kc = k_ref[0, 0:CTX_LEN, ls]
        vc = v_ref[0, 0:CTX_LEN, ls]
        outs = []
        for hh in range(2):
            qm = jnp.where(first if hh == 0 else ~first, qp, jnp.zeros_like(qp))
            parts = [(_dot_nt(qm, kc), vc)]
            if start is not None:
                kb = k_ref[0, pl.ds(start, NA_BAND * GRID_W), ls]
                vb = v_ref[0, pl.ds(start, NA_BAND * GRID_W), ls]
                parts.append((_dot_nt(qm, kb) + bias_ref[2 * pp + hh, 0], vb))
            outs.append(_softmax_pv(parts))
        o_ref[0, :, ls] = jnp.where(first, outs[0], outs[1]).astype(BF16)

    @pl.when(t == 0)
    def _():
        for pp in range(NA_HEADS // 2):
            pair_attention(pp, None)

    @pl.when(t > 0)
    def _():
        first_row = (t - 1) * (TILE // GRID_W)
        u0 = jnp.clip(first_row - NA_KH // 2, 0, ROWS - NA_BAND)
        start = pl.multiple_of(CTX_LEN + u0 * GRID_W, LANE)
        for pp in range(NA_HEADS // 2):
            pair_attention(pp, start)


def _na_cfg(t):
    return jnp.where(t <= 1, 0, jnp.where(t == NT - 1, 2, 1))


def natten(q, k, v, bias):
    whole = pl.BlockSpec((1, T, NA_WIDTH), lambda b, t: (b, 0, 0))
    tile = pl.BlockSpec((1, TILE, NA_WIDTH), lambda b, t: (b, t, 0))
    return pl.pallas_call(
        _na_kernel,
        grid=(BATCH, NT),
        in_specs=[tile, whole, whole,
                  pl.BlockSpec((NA_HEADS, 1, TILE, NA_BAND * GRID_W), lambda b, t: (0, _na_cfg(t), 0, 0))],
        out_specs=tile,
        out_shape=jax.ShapeDtypeStruct((BATCH, T, NA_WIDTH), BF16),
        compiler_params=_cp("arbitrary", "arbitrary"),
        name="natten",
    )(q, k, v, bias)


def natten_bias(rpb):
    col = jnp.arange(GRID_W)
    c0 = jnp.clip(col - NA_KW // 2, 0, GRID_W - NA_KW)
    in_win = (col[None, :] >= c0[:, None]) & (col[None, :] < c0[:, None] + NA_KW)
    rel_c = jnp.clip(col[None, :] - col[:, None] + (NA_KW - 1), 0, RPB_W - 1)
    pick_c = jax.nn.one_hot(rel_c, RPB_W, dtype=F32)
    blocks = jnp.einsum('hax,qkx->haqk', rpb.astype(F32), pick_c, precision=lax.Precision.HIGHEST)
    blocks = jnp.where(in_win[None, None], blocks, NEG)
    blocks = jnp.concatenate([blocks, jnp.full((NA_HEADS, 1, GRID_W, GRID_W), NEG, F32)], axis=1)
    blocks = jnp.concatenate([blocks, blocks], axis=-1)
    return pl.pallas_call(
        _bias_kernel,
        grid=(NA_HEADS, 3),
        in_specs=[pl.BlockSpec((1, 2 * NA_KH, GRID_W, LANE), lambda h, c: (h, 0, 0, 0))],
        out_specs=pl.BlockSpec((1, 1, TILE, NA_BAND * GRID_W), lambda h, c: (h, c, 0, 0)),
        out_shape=jax.ShapeDtypeStruct((NA_HEADS, 3, TILE, NA_BAND * GRID_W), F32),
        compiler_params=_cp("arbitrary", "arbitrary"),
        name="natten_bias",
    )(blocks)


def _bias_kernel(blk_ref, o_ref):
    rows_per_tile = TILE // GRID_W
    masked = 2 * NA_KH - 1
    for cfg, first_row in enumerate((0, 2 * rows_per_tile, ROWS - rows_per_tile)):
        @pl.when(pl.program_id(1) == cfg)
        def _(first_row=first_row):
            u0 = min(max(first_row - NA_KH // 2, 0), ROWS - NA_BAND)
            for rr in range(rows_per_tile):
                r = first_row + rr
                r0 = min(max(r - NA_KH // 2, 0), ROWS - NA_KH)
                for j in range(NA_BAND):
                    kr = u0 + j
                    a = kr - r + NA_KH - 1 if r0 <= kr < r0 + NA_KH else masked
                    half = (j % 2) * GRID_W
                    o_ref[0, 0, rr * GRID_W:(rr + 1) * GRID_W, j * GRID_W:(j + 1) * GRID_W] = (
                        blk_ref[0, a, :, half:half + GRID_W])


def _softplus(x):
    return jnp.maximum(x, 0.0) + jnp.log(1.0 + jnp.exp(-jnp.abs(x)))


def _ssd_prep_kernel(prev_ref, cur_ref, next_ref, dtr_ref, cw_ref, cb_ref, dtb_ref, ar_ref, cos_ref, sin_ref,
                     xs_ref, bm_ref, cm_ref, dt_ref, a_ref):
    t = pl.program_id(1)
    halo = prev_ref.shape[1]
    has_prev = t >= 2
    has_next = (t >= 1) & (t <= NT - 2)
    prev = jnp.where(has_prev, prev_ref[0].astype(F32), 0.0)
    nxt = jnp.where(has_next, next_ref[0].astype(F32), 0.0)
    ext = jnp.concatenate([prev, cur_ref[0].astype(F32), nxt], axis=0)
    n = ext.shape[0]
    acc = cb_ref[...] + cw_ref[SSD_CONV // 2:SSD_CONV // 2 + 1, :] * ext
    for kk in range(SSD_CONV):
        off = kk - SSD_CONV // 2
        if off != 0:
            acc = acc + cw_ref[kk:kk + 1, :] * pltpu.roll(ext, (-off) % n, 0)
    y = _silu(acc[halo:halo + TILE, :])
    xs_ref[0] = y[:, 0:SSD_WIDTH].astype(BF16)

    lane = lax.broadcasted_iota(I32, (1, LANE), 1)
    low = (lane & (NA_HEAD_DIM // 2)) == 0
    cos = cos_ref[...]
    sin = sin_ref[...]
    for g in range(2 * SSD_NGROUPS):
        v = y[:, SSD_WIDTH + g * LANE:SSD_WIDTH + (g + 1) * LANE]
        sw = jnp.where(low, pltpu.roll(v, LANE - 32, 1), pltpu.roll(v, 32, 1))
        rot = (v * cos + sw * sin).astype(BF16)
        if g < SSD_NGROUPS:
            bm_ref[0, :, g * LANE:(g + 1) * LANE] = rot
        else:
            cm_ref[0, :, (g - SSD_NGROUPS) * LANE:(g - SSD_NGROUPS + 1) * LANE] = rot

    dt = _softplus(dtr_ref[0] + dtb_ref[...])
    dt_ref[0] = dt
    a_ref[0] = dt * ar_ref[...]


def ssd_prep(xbc, dt_raw, conv_w, conv_b, dt_bias, a_log, cos_t, sin_t):
    halo = 16
    per = TILE // halo
    nhalo = T // halo
    tok = lambda w, dt: jax.ShapeDtypeStruct((BATCH, T, w), dt)
    tspec = lambda w: pl.BlockSpec((1, TILE, w), lambda b, t: (b, t, 0))
    row = lambda w: pl.BlockSpec((1, w), lambda b, t: (0, 0))
    cw = jnp.zeros((SUB, SSD_XBC), F32).at[:SSD_CONV].set(conv_w.astype(F32))
    pad12 = lambda v: jnp.zeros((1, LANE), F32).at[0, :2 * SSD_HEADS].set(v.astype(F32).reshape(-1))
    return pl.pallas_call(
        _ssd_prep_kernel,
        grid=(BATCH, NT),
        in_specs=[pl.BlockSpec((1, halo, SSD_XBC), lambda b, t: (b, jnp.maximum(t * per - 1, 0), 0)),
                  tspec(SSD_XBC),
                  pl.BlockSpec((1, halo, SSD_XBC), lambda b, t: (b, jnp.minimum((t + 1) * per, nhalo - 1), 0)),
                  tspec(LANE),
                  pl.BlockSpec((SUB, SSD_XBC), lambda b, t: (0, 0)), row(SSD_XBC), row(LANE), row(LANE),
                  pl.BlockSpec((TILE, LANE), lambda b, t: (t, 0)), pl.BlockSpec((TILE, LANE), lambda b, t: (t, 0))],
        out_specs=[tspec(SSD_WIDTH), tspec(SSD_BC), tspec(SSD_BC), tspec(LANE), tspec(LANE)],
        out_shape=[tok(SSD_WIDTH, BF16), tok(SSD_BC, BF16), tok(SSD_BC, BF16), tok(LANE, F32), tok(LANE, F32)],
        compiler_params=_cp("arbitrary", "arbitrary"),
        name="ssd_prep",
    )(xbc, xbc, xbc, dt_raw, cw, conv_b.astype(F32).reshape(1, SSD_XBC), pad12(dt_bias),
      pad12(-jnp.exp(a_log.astype(F32))), cos_t, sin_t)


def rope_tables():
    half = SSD_STATE // 2
    nf = half // 2
    pos = jnp.arange(SEQ)
    inv_freq = ROPE_BASE ** (-jnp.arange(nf, dtype=F32) / nf)
    lane = jnp.arange(LANE)
    p = jnp.where(lane[None, :] < half, (pos // GRID_W)[:, None], (pos % GRID_W)[:, None]).astype(F32)
    ang = p * inv_freq[lane % nf][None, :]
    sign = jnp.where((lane & nf) == 0, -1.0, 1.0)[None, :]
    cos_t = jnp.concatenate([jnp.ones((CTX_LEN, LANE), F32), jnp.cos(ang)], axis=0)
    sin_t = jnp.concatenate([jnp.zeros((CTX_LEN, LANE), F32), jnp.sin(ang) * sign], axis=0)
    return cos_t, sin_t


def _ssd_dir(d, xs_ref, bm_ref, cm_ref, bt_ref, dt_ref, a_ref, at_ref, tri_ref, y_ref, st_ref):
    q = TILE
    lane = lax.broadcasted_iota(I32, (1, LANE), 1)
    first = lane < SSD_HEAD_DIM
    ri = lax.broadcasted_iota(I32, (q, q), 0)
    ci = lax.broadcasted_iota(I32, (q, q), 1)
    keep = (ci <= ri) if d == 0 else (ci >= ri)
    end = q - 1 if d == 0 else 0
    tri_col = tri_ref[d]
    tri_row = tri_ref[1 - d]
    a = a_ref[0]
    dt = dt_ref[0]
    cs_col = _dot_exact_rhs(tri_col, a)
    cs_row = _dot_exact_lhs(at_ref[0], tri_row)
    g_mats = [_dot_nt(cm_ref[0, :, g * SSD_STATE:(g + 1) * SSD_STATE],
                      bm_ref[0, :, g * SSD_STATE:(g + 1) * SSD_STATE]) for g in range(SSD_NGROUPS)]

    def head_col(m, h):
        c = d * SSD_HEADS + h
        return m[:, c:c + 1]

    for pp in range(SSD_HEADS // 2):
        ls = slice(pp * LANE, (pp + 1) * LANE)
        h0, h1 = 2 * pp, 2 * pp + 1
        x = xs_ref[0, :, ls].astype(F32)
        dt_l = jnp.where(first, head_col(dt, h0), head_col(dt, h1))
        cs_l = jnp.where(first, head_col(cs_col, h0), head_col(cs_col, h1))
        cs_end = cs_l[end:end + 1, :]
        xdt = x * dt_l
        xdt_b = xdt.astype(BF16)
        xw = (xdt * jnp.exp(cs_end - cs_l)).astype(BF16)
        st = st_ref[d, pp]
        st_b = st.astype(BF16)
        ys, ups = [], []
        for h in (h0, h1):
            g = h // (SSD_HEADS // SSD_NGROUPS)
            c = d * SSD_HEADS + h
            diff = head_col(cs_col, h) - cs_row[c:c + 1, :]
            decay = jnp.where(keep, jnp.exp(jnp.where(keep, diff, 0.0)), 0.0)
            m = (g_mats[g] * decay).astype(BF16)
            y_h = _dot(m, xdt_b) + _dot(cm_ref[0, :, g * SSD_STATE:(g + 1) * SSD_STATE], st_b) * jnp.exp(cs_l)
            ys.append(y_h)
            ups.append(_dot(bt_ref[0, g * SSD_STATE:(g + 1) * SSD_STATE, :], xw))
        y_ref[0, :, ls] = jnp.where(first, ys[0], ys[1])
        st_ref[d, pp] = jnp.exp(cs_end) * st + jnp.where(first, ups[0], ups[1])


def _ssd_scan_kernel(xs_f, bm_f, cm_f, bt_f, dt_f, a_f, at_f, xs_b, bm_b, cm_b, bt_b, dt_b, a_b, at_b, tri_ref,
                     yf_ref, yb_ref, st_ref):
    @pl.when(pl.program_id(1) == 0)
    def _():
        st_ref[...] = jnp.zeros_like(st_ref)

    _ssd_dir(0, xs_f, bm_f, cm_f, bt_f, dt_f, a_f, at_f, tri_ref, yf_ref, st_ref)
    _ssd_dir(1, xs_b, bm_b, cm_b, bt_b, dt_b, a_b, at_b, tri_ref, yb_ref, st_ref)


def ssd_scan(xs, bm, cm, dt, a):
    bt = jnp.swapaxes(bm, 1, 2)
    at = jnp.swapaxes(a[:, :, :2 * SUB], 1, 2)
    idx = jnp.arange(TILE)
    tri = jnp.stack([idx[None, :] <= idx[:, None], idx[None, :] >= idx[:, None]]).astype(BF16)
    fwd = lambda b, i: (b, i, 0)
    bwd = lambda b, i: (b, _bwd_tile(i), 0)
    fwd_t = lambda b, i: (b, 0, i)
    bwd_t = lambda b, i: (b, 0, _bwd_tile(i))

    def specs(f, ft):
        return [pl.BlockSpec((1, TILE, SSD_WIDTH), f), pl.BlockSpec((1, TILE, SSD_BC), f),
                pl.BlockSpec((1, TILE, SSD_BC), f), pl.BlockSpec((1, SSD_BC, TILE), ft),
                pl.BlockSpec((1, TILE, LANE), f), pl.BlockSpec((1, TILE, LANE), f),
                pl.BlockSpec((1, 2 * SUB, TILE), ft)]

    args = (xs, bm, cm, bt, dt, a, at)
    return pl.pallas_call(
        _ssd_scan_kernel,
        grid=(BATCH, NT),
        in_specs=specs(fwd, fwd_t) + specs(bwd, bwd_t) + [pl.BlockSpec((2, TILE, TILE), lambda b, i: (0, 0, 0))],
        out_specs=[pl.BlockSpec((1, TILE, SSD_WIDTH), fwd), pl.BlockSpec((1, TILE, SSD_WIDTH), bwd)],
        out_shape=[jax.ShapeDtypeStruct((BATCH, T, SSD_WIDTH), F32)] * 2,
        scratch_shapes=[pltpu.VMEM((2, SSD_HEADS // 2, SSD_STATE, LANE), F32)],
        compiler_params=_cp("arbitrary", "arbitrary"),
        name="ssd_scan",
    )(*args, *args, tri)


def _gelu_tanh(x):
    return 0.5 * x * (1.0 + jnp.tanh(math.sqrt(2.0 / math.pi) * (x + 0.044715 * (x * x * x))))


def _post_kernel(x_ref, mod_ref, u_ref, s5y_ref, na_ref, xs_ref, z_ref, sdf_ref, sdb_ref,
                 s5d_ref, gw_ref, gb_ref, sdd_ref, snw_ref, wo_ref, n2w_ref, rt_ref,
                 x1_ref, h_ref, lg_ref):
    ys5 = u_ref[0].astype(F32) * s5d_ref[...] + s5y_ref[0]
    g = _gelu_tanh(ys5)
    s5o = g * jax.nn.sigmoid(_dot(g.astype(BF16), gw_ref[...]) + gb_ref[...])
    yssd = (xs_ref[0].astype(F32) * sdd_ref[...] + sdf_ref[0] + sdb_ref[0]) * _silu(z_ref[0].astype(F32))
    ssdo = yssd * lax.rsqrt(jnp.mean(yssd * yssd, axis=-1, keepdims=True) + EPS) * snw_ref[...]
    mix = jnp.concatenate([s5o.astype(BF16), na_ref[0], ssdo.astype(BF16)], axis=-1)
    x1 = x_ref[0] + mod_ref[0, 0, 2:3, :] * _dot(mix, wo_ref[...])
    x1_ref[0] = x1
    h = _modulated_norm(x1, n2w_ref[...], mod_ref[0, 0, 3:4, :], mod_ref[0, 0, 4:5, :])
    h_ref[0] = h.astype(BF16)
    lg_ref[0] = _dot_x3(h, rt_ref[...])


def post_mixer(x, mod, u, s5y, na, xs, z, sdf, sdb, s5_d, glu_w, glu_b, ssd_d, ssd_norm_w, w_out, norm2_w,
               router):
    tspec = lambda w: pl.BlockSpec((1, TILE, w), lambda b, t: (b, t, 0))
    whole = lambda *shp: pl.BlockSpec(shp, lambda b, t: (0,) * len(shp))
    rt = jnp.zeros((D_MODEL, LANE), F32).at[:, :N_EXPERTS].set(router.astype(F32))
    return pl.pallas_call(
        _post_kernel,
        grid=(BATCH, NT),
        in_specs=[tspec(D_MODEL), pl.BlockSpec((1, 1, 6, D_MODEL), lambda b, t: (b, _seg(t), 0, 0)),
                  tspec(S5_WIDTH), tspec(S5_WIDTH), tspec(NA_WIDTH),
                  tspec(SSD_WIDTH), tspec(SSD_WIDTH), tspec(SSD_WIDTH), tspec(SSD_WIDTH),
                  whole(1, S5_WIDTH), whole(S5_WIDTH, S5_WIDTH), whole(1, S5_WIDTH),
                  whole(1, SSD_WIDTH), whole(1, SSD_WIDTH), whole(D_MODEL, D_MODEL), whole(1, D_MODEL),
                  whole(D_MODEL, LANE)],
        out_specs=[tspec(D_MODEL), tspec(D_MODEL), tspec(LANE)],
        out_shape=[jax.ShapeDtypeStruct((BATCH, T, D_MODEL), F32), jax.ShapeDtypeStruct((BATCH, T, D_MODEL), BF16),
                   jax.ShapeDtypeStruct((BATCH, T, LANE), F32)],
        compiler_params=_cp("arbitrary", "arbitrary"),
        name="post_mixer",
    )(x, mod, u, s5y, na, xs, z, sdf, sdb,
      s5_d.astype(F32).reshape(1, S5_WIDTH), glu_w.astype(BF16), glu_b.astype(F32).reshape(1, S5_WIDTH),
      jnp.repeat(ssd_d.astype(F32), SSD_HEAD_DIM).reshape(1, SSD_WIDTH), ssd_norm_w.astype(F32).reshape(1, SSD_WIDTH),
      w_out.astype(BF16), norm2_w.astype(F32).reshape(1, D_MODEL), rt)


def _route_kernel(lg_ref, tri_ref, slot_ref, aff_ref, *, with_ctx):
    lg = lg_ref[0]
    m = lg.max(axis=0, keepdims=True)
    e = jnp.exp(lg - m)
    aff = e / e.sum(axis=0, keepdims=True)
    aff_ref[0] = aff
    bits = pltpu.bitcast(aff, I32)
    is_ctx = lax.broadcasted_iota(I32, (N_EXPERTS, T), 1) < CTX_LEN

    def count(mask):
        return jnp.where(mask, 1.0, 0.0).sum(axis=1, keepdims=True)

    def kth_largest(seg, k):
        def body(i, prefix):
            cand = prefix | lax.shift_left(jnp.int32(1), 30 - i)
            return jnp.where(count((bits >= cand) & seg) >= k, cand, prefix)
        return lax.fori_loop(0, 31, body, jnp.zeros((N_EXPERTS, 1), I32))

    def excl_cumsum(x01):
        carry = jnp.zeros((N_EXPERTS, 1), F32)
        pieces = []
        for j in range(T // LANE):
            blk = x01[:, j * LANE:(j + 1) * LANE]
            inc = _dot(blk.astype(BF16), tri_ref[...])
            pieces.append(inc - blk + carry)
            carry = carry + inc[:, LANE - 1:LANE]
        return jnp.concatenate(pieces, axis=1)

    thr = kth_largest(~is_ctx, float(CAP_LAT))
    k_of = jnp.full((N_EXPERTS, T), float(CAP_LAT), F32)
    if with_ctx:
        thr = jnp.where(is_ctx, kth_largest(is_ctx, float(CAP_CTX)), thr)
        k_of = jnp.where(is_ctx, float(CAP_CTX), k_of)
    gt = bits > thr
    eq = bits == thr
    if not with_ctx:
        gt = gt & ~is_ctx
        eq = eq & ~is_ctx
    n_gt = jnp.where(is_ctx, count(gt & is_ctx), count(gt & ~is_ctx))
    tie_rank = excl_cumsum(jnp.where(eq, 1.0, 0.0))
    tie_rank = tie_rank - jnp.where(is_ctx, 0.0, count(eq & is_ctx))
    sel = gt | (eq & (tie_rank < k_of - n_gt))
    pos = excl_cumsum(jnp.where(sel, 1.0, 0.0))
    slot = jnp.where(is_ctx, pos + float(CAP_LAT), pos - count(sel & is_ctx))
    slot_ref[0] = jnp.where(sel, slot, -1.0).astype(I32)


def route(logits_t, with_ctx):
    idx = jnp.arange(LANE)
    tri = (idx[:, None] <= idx[None, :]).astype(BF16)
    spec = pl.BlockSpec((1, N_EXPERTS, T), lambda b: (b, 0, 0))
    return pl.pallas_call(
        functools.partial(_route_kernel, with_ctx=with_ctx),
        grid=(BATCH,),
        in_specs=[spec, pl.BlockSpec((LANE, LANE), lambda b: (0, 0))],
        out_specs=[spec, spec],
        out_shape=[jax.ShapeDtypeStruct((BATCH, N_EXPERTS, T), I32), jax.ShapeDtypeStruct((BATCH, N_EXPERTS, T), F32)],
        compiler_params=_cp("arbitrary"),
        name="route",
    )(logits_t, tri)


GATHER_WIN = LANE
COMBINE_WIN = LANE


def slot_ranges(slot):
    s = slot.reshape(BATCH, N_EXPERTS, NT, TILE)
    has = s >= 0
    smax = jnp.max(jnp.where(has, s, -1), axis=-1)
    smin = jnp.where(smax >= 0, jnp.min(jnp.where(has, s, CAP_LAT + CAP_CTX), axis=-1), 0)
    return smin.reshape(-1).astype(I32), smax.reshape(-1).astype(I32)


def _gather_kernel(smin_ref, smax_ref, h_ref, slot_ref, xs_ref, *, nslot):
    b = pl.program_id(0)
    t = pl.program_id(1)
    group = 8
    sid = lax.broadcasted_iota(I32, (GATHER_WIN, TILE), 0)
    align = 2 * SUB

    def onehot(e, ws, lo):
        srow = slot_ref[0, e:e + 1, :]
        return jnp.where((sid + ws == srow) & (srow >= lo), 1.0, 0.0).astype(BF16)

    def add_rows(e, ws, rows):
        win = pl.ds(pl.multiple_of(ws, align), GATHER_WIN)
        xs_ref[0, e, win, :] = (xs_ref[0, e, win, :].astype(F32) + rows).astype(BF16)

    @pl.when(t == 0)
    def _():
        for e in range(N_EXPERTS):
            xs_ref[0, e, 0:CAP_LAT, :] = jnp.zeros((CAP_LAT, D_MODEL), BF16)
            if nslot > CAP_LAT:
                cid = lax.broadcasted_iota(I32, (nslot - CAP_LAT, TILE), 0) + CAP_LAT
                pick = jnp.where(cid == slot_ref[0, e:e + 1, :], 1.0, 0.0).astype(BF16)
                xs_ref[0, e, CAP_LAT:nslot, :] = _dot(pick, h_ref[0]).astype(BF16)

    @pl.when(t > 0)
    def _():
        los, wss, extras = [], [], []
        for e in range(N_EXPERTS):
            base = (b * N_EXPERTS + e) * NT + t
            lo = smin_ref[base] & ~(align - 1)
            los.append(lo)
            wss.append(jnp.minimum(lo, CAP_LAT - GATHER_WIN))
            extras.append(lax.shift_right_arithmetic(smax_ref[base] - lo, GATHER_WIN.bit_length() - 1))
        for g0 in range(0, N_EXPERTS, group):
            pick = jnp.concatenate([onehot(e, wss[e], los[e]) for e in range(g0, g0 + group)], axis=0)
            rows = _dot(pick, h_ref[0])
            for i, e in enumerate(range(g0, g0 + group)):
                add_rows(e, wss[e], rows[i * GATHER_WIN:(i + 1) * GATHER_WIN, :])
        most = extras[0]
        for x in extras[1:]:
            most = jnp.maximum(most, x)

        @pl.when(most > 0)
        def _():
            for e in range(N_EXPERTS):
                def more(k, carry, e=e):
                    lo_k = los[e] + k * GATHER_WIN
                    ws = jnp.minimum(lo_k, CAP_LAT - GATHER_WIN)
                    add_rows(e, ws, _dot(onehot(e, ws, lo_k), h_ref[0]))
                    return carry

                lax.fori_loop(1, extras[e] + 1, more, 0)


def moe_gather(h, slot, smin, smax, nslot):
    return pl.pallas_call(
        functools.partial(_gather_kernel, nslot=nslot),
        grid_spec=pltpu.PrefetchScalarGridSpec(
            num_scalar_prefetch=2,
            grid=(BATCH, NT),
            in_specs=[pl.BlockSpec((1, TILE, D_MODEL), lambda b, t, *_: (b, t, 0)),
                      pl.BlockSpec((1, N_EXPERTS, TILE), lambda b, t, *_: (b, 0, t))],
            out_specs=pl.BlockSpec((1, N_EXPERTS, nslot, D_MODEL), lambda b, t, *_: (b, 0, 0, 0))),
        out_shape=jax.ShapeDtypeStruct((BATCH, N_EXPERTS, nslot, D_MODEL), BF16),
        compiler_params=_cp("arbitrary", "arbitrary"),
        name="moe_gather",
    )(smin, smax, h, slot)


def _ffn_kernel(xs_ref, wg_ref, wu_ref, wd_ref, y_ref, w_ref):
    @pl.when(pl.program_id(1) == 0)
    def _():
        w_ref[0] = wg_ref[0, 0].astype(BF16)
        w_ref[1] = wu_ref[0, 0].astype(BF16)
        w_ref[2] = wd_ref[0, 0].astype(BF16)

    xs = xs_ref[0, 0]
    hid = _silu(_dot(xs, w_ref[0])) * _dot(xs, w_ref[1])
    y_ref[0, 0] = _dot(hid.astype(BF16), w_ref[2]).astype(BF16)


def moe_ffn(xs, layer, wg, wu, wd, nslot):
    wspec = lambda: pl.BlockSpec((1, 1, D_MODEL, D_EXPERT), lambda e, b: (layer, e, 0, 0))
    rows = pl.BlockSpec((1, 1, nslot, D_MODEL), lambda e, b: (b, e, 0, 0))
    return pl.pallas_call(
        _ffn_kernel,
        grid=(N_EXPERTS, BATCH),
        in_specs=[rows, wspec(), wspec(), wspec()],
        out_specs=rows,
        out_shape=jax.ShapeDtypeStruct((BATCH, N_EXPERTS, nslot, D_MODEL), BF16),
        scratch_shapes=[pltpu.VMEM((3, D_MODEL, D_EXPERT), BF16)],
        compiler_params=_cp("arbitrary", "arbitrary"),
        name="moe_ffn",
    )(xs, wg, wu, wd)


def _combine_kernel(smin_ref, smax_ref, x_ref, mod_ref, slot_ref, aff_ref, y_ref, fw_ref, o_ref, acc_ref, *,
                    nslot, last):
    b = pl.program_id(0)
    t = pl.program_id(1)
    shift = COMBINE_WIN.bit_length() - 1
    align = 2 * SUB

    def weights(e, lane, lo=None):
        s = slot_ref[0, :, e:e + 1]
        hit = (s == lane) if lo is None else ((s == lane) & (s >= lo))
        return jnp.where(hit, aff_ref[0, :, e:e + 1], 0.0).astype(BF16)

    def finish(acc):
        x2 = x_ref[0] + mod_ref[0, 0, 5:6, :] * acc
        if last:
            x2 = x2 * lax.rsqrt(jnp.mean(x2 * x2, axis=-1, keepdims=True) + EPS) * fw_ref[...]
        o_ref[0] = x2

    if not last:
        @pl.when(t == 0)
        def _():
            lane = lax.broadcasted_iota(I32, (TILE, nslot - CAP_LAT), 1) + CAP_LAT
            acc = jnp.zeros((TILE, D_MODEL), F32)
            for e in range(N_EXPERTS):
                acc = acc + _dot(weights(e, lane), y_ref[0, e, CAP_LAT:nslot, :])
            finish(acc)

    @pl.when(t > 0)
    def _():
        lane = lax.broadcasted_iota(I32, (TILE, COMBINE_WIN), 1)
        acc = jnp.zeros((TILE, D_MODEL), F32)
        los, extras = [], []
        for e in range(N_EXPERTS):
            base = (b * N_EXPERTS + e) * NT + t
            lo = smin_ref[base] & ~(align - 1)
            ws = pl.multiple_of(jnp.minimum(lo, CAP_LAT - COMBINE_WIN), align)
            acc = acc + _dot(weights(e, lane + ws), y_ref[0, e, pl.ds(ws, COMBINE_WIN), :])
            los.append(lo)
            extras.append(lax.shift_right_arithmetic(smax_ref[base] - lo, shift))
        acc_ref[...] = acc
        most = extras[0]
        for x in extras[1:]:
            most = jnp.maximum(most, x)

        @pl.when(most > 0)
        def _():
            for e in range(N_EXPERTS):
                def more(k, carry, e=e):
                    lo_k = los[e] + k * COMBINE_WIN
                    ws = pl.multiple_of(jnp.minimum(lo_k, CAP_LAT - COMBINE_WIN), align)
                    acc_ref[...] += _dot(weights(e, lane + ws, lo_k), y_ref[0, e, pl.ds(ws, COMBINE_WIN), :])
                    return carry

                lax.fori_loop(1, extras[e] + 1, more, 0)

        finish(acc_ref[...])


def moe_combine(x1, mod, slot_tok, aff_tok, y, smin, smax, nslot, final_w):
    last = final_w is not None
    tspec = lambda w: pl.BlockSpec((1, TILE, w), lambda b, t, *_: (b, t, 0))
    if last:
        first_lat = CTX_LEN // TILE
        out_spec = pl.BlockSpec((1, TILE, D_MODEL), lambda b, t, *_: (b, jnp.maximum(t - first_lat, 0), 0))
        out_shape = jax.ShapeDtypeStruct((BATCH, SEQ, D_MODEL), F32)
        fw = final_w.astype(F32).reshape(1, D_MODEL)
    else:
        out_spec = tspec(D_MODEL)
        out_shape = jax.ShapeDtypeStruct((BATCH, T, D_MODEL), F32)
        fw = jnp.ones((1, D_MODEL), F32)
    return pl.pallas_call(
        functools.partial(_combine_kernel, nslot=nslot, last=last),
        grid_spec=pltpu.PrefetchScalarGridSpec(
            num_scalar_prefetch=2,
            grid=(BATCH, NT),
            in_specs=[tspec(D_MODEL),
                      pl.BlockSpec((1, 1, 6, D_MODEL), lambda b, t, *_: (b, _seg(t), 0, 0)),
                      tspec(N_EXPERTS), tspec(N_EXPERTS),
                      pl.BlockSpec((1, N_EXPERTS, nslot, D_MODEL), lambda b, t, *_: (b, 0, 0, 0)),
                      pl.BlockSpec((1, D_MODEL), lambda b, t, *_: (0, 0))],
            out_specs=out_spec,
            scratch_shapes=[pltpu.VMEM((TILE, D_MODEL), F32)]),
        out_shape=out_shape,
        compiler_params=_cp("arbitrary", "arbitrary"),
        name="moe_combine",
    )(smin, smax, x1, mod, slot_tok, aff_tok, y, fw)


def trunk_layer(l, x, mod, cos_t, sin_t, norm1_w, norm2_w, w_in, w_out,
                s5_lam_re, s5_lam_im, s5_log_dt, s5_b_re, s5_b_im, s5_c_re, s5_c_im, s5_d, s5_glu_w, s5_glu_b,
                na_rpb, ssd_conv_w, ssd_conv_b, ssd_dt_bias, ssd_a_log, ssd_d, ssd_norm_w,
                moe_router, wg, wu, wd, final_w):
    with_ctx_out = final_w is None
    w_in_p = jnp.zeros((D_MODEL, IN_COLS_PAD), BF16).at[:, :IN_COLS].set(w_in.astype(BF16))
    u, q, k, v, z, xbc, dt_raw = in_proj(x, mod, norm1_w, w_in_p)

    s5y = s5_mix(u, *s5_params(s5_lam_re, s5_lam_im, s5_log_dt, s5_b_re, s5_b_im, s5_c_re, s5_c_im))
    na = natten(q, k, v, natten_bias(na_rpb))
    xs, bm, cm, dt, a = ssd_prep(xbc, dt_raw, ssd_conv_w, ssd_conv_b, ssd_dt_bias, ssd_a_log, cos_t, sin_t)
    sdf, sdb = ssd_scan(xs, bm, cm, dt, a)

    x1, h, logits = post_mixer(x, mod, u, s5y, na, xs, z, sdf, sdb, s5_d, s5_glu_w, s5_glu_b,
                               ssd_d, ssd_norm_w, w_out, norm2_w, moe_router)
    slot, aff = route(jnp.swapaxes(logits[:, :, :N_EXPERTS], 1, 2), with_ctx_out)
    nslot = CAP_LAT + CAP_CTX if with_ctx_out else CAP_LAT
    smin, smax = slot_ranges(slot)
    y = moe_ffn(moe_gather(h, slot, smin, smax, nslot), l, wg, wu, wd, nslot)
    return moe_combine(x1, mod, jnp.swapaxes(slot, 1, 2), jnp.swapaxes(aff, 1, 2), y, smin, smax, nslot, final_w)


def kernel(x, c, ctx, c_ctx, w_ada, b_ada, norm1_w, norm2_w, w_in, w_out, s5_lam_re, s5_lam_im, s5_log_dt, s5_b_re, s5_b_im, s5_c_re, s5_c_im, s5_d, s5_glu_w, s5_glu_b, na_rpb, ssd_conv_w, ssd_conv_b, ssd_dt_bias, ssd_a_log, ssd_d, ssd_norm_w, moe_router, moe_w_gate, moe_w_up, moe_w_down, final_norm_w):
    xa = jnp.concatenate([ctx, x], axis=1).astype(F32)
    cvec = jnp.zeros((SUB, D_MODEL), F32).at[0].set(c_ctx.astype(F32)).at[1:1 + BATCH].set(c.astype(F32))
    mods = ada_mod(cvec, w_ada.astype(F32), b_ada.astype(F32)).reshape(DEPTH, SUB, 6, D_MODEL)
    cos_t, sin_t = rope_tables()
    wg, wu, wd = moe_w_gate.astype(F32), moe_w_up.astype(F32), moe_w_down.astype(F32)
    for l in range(DEPTH):
        mod = jnp.stack([jnp.broadcast_to(mods[l, 0], (BATCH, 6, D_MODEL)), mods[l, 1:1 + BATCH]], axis=1)
        xa = trunk_layer(
            l, xa, mod, cos_t, sin_t, norm1_w[l], norm2_w[l], w_in[l], w_out[l],
            s5_lam_re[l], s5_lam_im[l], s5_log_dt[l], s5_b_re[l], s5_b_im[l], s5_c_re[l], s5_c_im[l],
            s5_d[l], s5_glu_w[l], s5_glu_b[l],
            na_rpb[l], ssd_conv_w[l], ssd_conv_b[l], ssd_dt_bias[l], ssd_a_log[l], ssd_d[l], ssd_norm_w[l],
            moe_router[l], wg, wu, wd, final_norm_w if l == DEPTH - 1 else None)
    return xa
```

```python
import functools
import math

import jax
import jax.numpy as jnp
import numpy as np
from jax import lax
from jax.experimental import pallas as pl
from jax.experimental.pallas import tpu as pltpu

F32 = jnp.float32
BF16 = jnp.bfloat16
I32 = jnp.int32

D_MODEL = 1024
BATCH = 4
SEQ = 4096
DEPTH = 2
GRID_W = 64
CTX_LEN = 256
EPS = 1e-6

S5_WIDTH = 256
S5_GROUP = 16
S5_NGROUPS = 16
S5_STATE = 64
S5_NSTATE = S5_NGROUPS * S5_STATE

NA_HEADS = 6
NA_HEAD_DIM = 64
NA_WIDTH = 384
NA_KH = 8
NA_KW = 16
NA_BAND = 12
RPB_W = 2 * NA_KW - 1

SSD_HEADS = 6
SSD_HEAD_DIM = 64
SSD_WIDTH = 384
SSD_NGROUPS = 2
SSD_STATE = 128
SSD_CONV = 5
SSD_BC = 256
SSD_XBC = 896

N_EXPERTS = 16
D_EXPERT = 1024
ROPE_BASE = 10000.0

T = CTX_LEN + SEQ
TILE = 256
NT = T // TILE
LANE = 128
SUB = 8
ROWS = SEQ // GRID_W
CAP_LAT = 2 * SEQ // N_EXPERTS
CAP_CTX = 2 * CTX_LEN // N_EXPERTS
NEG = -1e30

C_U = 0
C_Q = 256
C_K = 640
C_V = 1024
C_Z = 1408
C_XBC = 1792
C_DT = 2688
IN_COLS = 2700
IN_COLS_PAD = 2816

VMEM_LIMIT = 56 * 1024 * 1024


def _cp(*sem):
    return pltpu.CompilerParams(dimension_semantics=sem, vmem_limit_bytes=VMEM_LIMIT)


def _dot(a, b):
    return jnp.dot(a, b, preferred_element_type=F32)


def _dot_nt(a, b):
    return lax.dot_general(a, b, (((1,), (1,)), ((), ())), preferred_element_type=F32)


def _split3(x):
    hi = x.astype(BF16)
    r = x - hi.astype(F32)
    mid = r.astype(BF16)
    lo = (r - mid.astype(F32)).astype(BF16)
    return hi, mid, lo


def _dot_exact_rhs(a_bf16, b_f32):
    hi, mid, lo = _split3(b_f32)
    return _dot(a_bf16, hi) + _dot(a_bf16, mid) + _dot(a_bf16, lo)


def _dot_exact_lhs(a_f32, b_bf16):
    hi, mid, lo = _split3(a_f32)
    return _dot(hi, b_bf16) + _dot(mid, b_bf16) + _dot(lo, b_bf16)


def _dot_x3(a, b):
    ah = a.astype(BF16)
    al = (a - ah.astype(F32)).astype(BF16)
    bh = b.astype(BF16)
    bl = (b - bh.astype(F32)).astype(BF16)
    return _dot(ah, bh) + _dot(ah, bl) + _dot(al, bh)


def _silu(x):
    return x * jax.nn.sigmoid(x)


def _seg(t):
    return jnp.where(t >= CTX_LEN // TILE, 1, 0)


def _bwd_tile(i):
    return jnp.where(i == 0, 0, NT - i)


def _ada_kernel(c_ref, w_ref, b_ref, o_ref):
    s = _silu(c_ref[...])
    o_ref[0] = _dot_x3(s, w_ref[0]) + b_ref[0]


def ada_mod(cvec, w_ada, b_ada):
    nb = 1024
    return pl.pallas_call(
        _ada_kernel,
        grid=(DEPTH, 6 * D_MODEL // nb),
        in_specs=[pl.BlockSpec((SUB, D_MODEL), lambda l, j: (0, 0)),
                  pl.BlockSpec((1, D_MODEL, nb), lambda l, j: (l, 0, j)),
                  pl.BlockSpec((1, 1, nb), lambda l, j: (l, 0, j))],
        out_specs=pl.BlockSpec((1, SUB, nb), lambda l, j: (l, 0, j)),
        out_shape=jax.ShapeDtypeStruct((DEPTH, SUB, 6 * D_MODEL), F32),
        compiler_params=_cp("arbitrary", "arbitrary"),
        name="ada_mod",
    )(cvec, w_ada, b_ada.reshape(DEPTH, 1, 6 * D_MODEL))


def _modulated_norm(x, nw, shift, scale):
    y = x * lax.rsqrt(jnp.mean(x * x, axis=-1, keepdims=True) + EPS) * nw
    return y * (1.0 + scale) + shift


def _inproj_kernel(x_ref, mod_ref, nw_ref, w_ref, u_ref, q_ref, k_ref, v_ref, z_ref, xbc_ref, dt_ref):
    h = _modulated_norm(x_ref[0], nw_ref[...], mod_ref[0, 0, 0:1, :], mod_ref[0, 0, 1:2, :]).astype(BF16)

    def proj(lo, hi):
        return _dot(h, w_ref[:, lo:hi])

    for s in range(S5_WIDTH // LANE):
        u_ref[0, s] = proj(C_U + s * LANE, C_U + (s + 1) * LANE)
    q_ref[0] = (proj(C_Q, C_K) * (NA_HEAD_DIM ** -0.5)).astype(BF16)
    k_ref[0] = proj(C_K, C_V).astype(BF16)
    v_ref[0] = proj(C_V, C_Z).astype(BF16)
    z_ref[0] = proj(C_Z, C_XBC).astype(BF16)
    xbc_ref[0] = proj(C_XBC, C_DT).astype(BF16)
    dt_ref[0] = proj(C_DT, IN_COLS_PAD)


def in_proj(x, mod, norm_w, w_in_p):
    tok = lambda w, dt: jax.ShapeDtypeStruct((BATCH, T, w), dt)
    tspec = lambda w: pl.BlockSpec((1, TILE, w), lambda b, t: (b, t, 0))
    return pl.pallas_call(
        _inproj_kernel,
        grid=(BATCH, NT),
        in_specs=[tspec(D_MODEL),
                  pl.BlockSpec((1, 1, 6, D_MODEL), lambda b, t: (b, _seg(t), 0, 0)),
                  pl.BlockSpec((1, D_MODEL), lambda b, t: (0, 0)),
                  pl.BlockSpec((D_MODEL, IN_COLS_PAD), lambda b, t: (0, 0))],
        out_specs=[pl.BlockSpec((1, S5_WIDTH // LANE, TILE, LANE), lambda b, t: (b, 0, t, 0)),
                   tspec(NA_WIDTH), tspec(NA_WIDTH), tspec(NA_WIDTH),
                   tspec(SSD_WIDTH), tspec(SSD_XBC), tspec(LANE)],
        out_shape=[jax.ShapeDtypeStruct((BATCH, S5_WIDTH // LANE, T, LANE), F32),
                   tok(NA_WIDTH, BF16), tok(NA_WIDTH, BF16), tok(NA_WIDTH, BF16),
                   tok(SSD_WIDTH, BF16), tok(SSD_XBC, BF16), tok(LANE, F32)],
        compiler_params=_cp("arbitrary", "arbitrary"),
        name="in_proj",
    )(x, mod, norm_w.reshape(1, D_MODEL), w_in_p)


S5_BLK = SUB
NB = T // S5_BLK
NB_CTX = CTX_LEN // S5_BLK
S5_NPAIR = S5_NGROUPS // 2
S5_PW = 2 * S5_BLK * S5_GROUP


def _s5_kernel(u_ref, pin_ref, pout_ref, w1_ref, w2_ref, w3_ref, mul_ref, y_ref, ub_ref, yb_ref, st_ref):
    n = S5_NSTATE
    half = LANE
    per_half = S5_NPAIR // 2
    toks = [[u_ref[0, h, pl.ds(j, NB, stride=S5_BLK), :].astype(BF16) for j in range(S5_BLK)]
            for h in range(2)]
    for pp in range(S5_NPAIR):
        h, q = divmod(pp, per_half)
        acc = None
        for j in range(S5_BLK):
            term = _dot(toks[h][j], pin_ref[q, j])
            acc = term if acc is None else acc + term
        ub_ref[:, pp * S5_PW:(pp + 1) * S5_PW] = acc.astype(BF16)

    for d in range(2):
        for pp in range(S5_NPAIR):
            s = _dot(ub_ref[:, pp * S5_PW:(pp + 1) * S5_PW], w1_ref[d, pp])
            st_ref[d, :, pp * half:(pp + 1) * half] = s[:, :half]
            st_ref[d, :, n + pp * half:n + (pp + 1) * half] = s[:, half:]

    ngrp = NB // SUB
    nctx = NB_CTX // SUB
    rowid = lax.broadcasted_iota(I32, (SUB, n), 0)
    for d in range(2):
        def body(j, carry, d=d):
            cr, ci = carry
            r = j if d == 0 else jnp.where(j < nctx, nctx - 1 - j, ngrp - 1 + nctx - j)
            row = pl.multiple_of(r * SUB, SUB)
            re = st_ref[d, pl.ds(row, SUB), 0:n]
            im = st_ref[d, pl.ds(row, SUB), n:2 * n]
            for kk, sh in enumerate((1, 2, 4)):
                mr = mul_ref[d, kk * SUB:(kk + 1) * SUB, 0:n]
                mi = mul_ref[d, kk * SUB:(kk + 1) * SUB, n:2 * n]
                s = sh if d == 0 else SUB - sh
                sr = pltpu.roll(re, s, 0)
                si = pltpu.roll(im, s, 0)
                re, im = re + (mr * sr - mi * si), im + (mr * si + mi * sr)
            pr = mul_ref[d, 3 * SUB:4 * SUB, 0:n]
            pi = mul_ref[d, 3 * SUB:4 * SUB, n:2 * n]
            re, im = re + (pr * cr - pi * ci), im + (pr * ci + pi * cr)
            edge, last, sh = (0, SUB - 1, 1) if d == 0 else (SUB - 1, 0, SUB - 1)
            st_ref[d, pl.ds(row, SUB), 0:n] = jnp.where(rowid == edge, cr, pltpu.roll(re, sh, 0))
            st_ref[d, pl.ds(row, SUB), n:2 * n] = jnp.where(rowid == edge, ci, pltpu.roll(im, sh, 0))
            return re[last:last + 1, :], im[last:last + 1, :]

        zero = jnp.zeros((1, n), F32)
        lax.fori_loop(0, ngrp, body, (zero, zero), unroll=2)

    for pp in range(S5_NPAIR):
        up = ub_ref[:, pp * S5_PW:(pp + 1) * S5_PW]
        acc = None
        for d in range(2):
            enter = jnp.concatenate([st_ref[d, :, pp * half:(pp + 1) * half],
                                     st_ref[d, :, n + pp * half:n + (pp + 1) * half]], axis=1).astype(BF16)
            term = _dot(up, w2_ref[d, pp]) + _dot(enter, w3_ref[d, pp])
            acc = term if acc is None else acc + term
        yb_ref[:, pp * S5_PW:(pp + 1) * S5_PW] = acc.astype(BF16)

    for i in range(S5_BLK):
        for h in range(2):
            acc = None
            for q in range(per_half):
                pp = h * per_half + q
                term = _dot(yb_ref[:, pp * S5_PW:(pp + 1) * S5_PW], pout_ref[q, i])
                acc = term if acc is None else acc + term
            y_ref[0, h, pl.ds(i, NB, stride=S5_BLK), :] = acc


def _s5_regroup_matrices():
    per_half = S5_NPAIR // 2
    pin = np.zeros((per_half, S5_BLK, LANE, S5_PW), np.float32)
    for q in range(per_half):
        for j in range(S5_BLK):
            for gg in range(2):
                for c in range(S5_GROUP):
                    pin[q, j, (2 * q + gg) * S5_GROUP + c, gg * S5_BLK * S5_GROUP + j * S5_GROUP + c] = 1.0
    return jnp.asarray(pin, BF16), jnp.asarray(pin.transpose(0, 1, 3, 2), BF16)


def s5_mix(u, w1, w2, w3, mul):
    pin, pout = _s5_regroup_matrices()
    per_half = S5_NPAIR // 2
    wspec = pl.BlockSpec((2, S5_NPAIR, S5_PW, S5_PW), lambda b: (0, 0, 0, 0))
    tok = pl.BlockSpec((1, S5_WIDTH // LANE, T, LANE), lambda b: (b, 0, 0, 0))
    return pl.pallas_call(
        _s5_kernel,
        grid=(BATCH,),
        in_specs=[tok, pl.BlockSpec((per_half, S5_BLK, LANE, S5_PW), lambda b: (0, 0, 0, 0)),
                  pl.BlockSpec((per_half, S5_BLK, S5_PW, LANE), lambda b: (0, 0, 0, 0)), wspec, wspec, wspec,
                  pl.BlockSpec((2, 4 * SUB, 2 * S5_NSTATE), lambda b: (0, 0, 0))],
        out_specs=tok,
        out_shape=jax.ShapeDtypeStruct((BATCH, S5_WIDTH // LANE, T, LANE), F32),
        scratch_shapes=[pltpu.VMEM((NB, S5_NGROUPS * S5_BLK * S5_GROUP), BF16),
                        pltpu.VMEM((NB, S5_NGROUPS * S5_BLK * S5_GROUP), BF16),
                        pltpu.VMEM((2, NB, 2 * S5_NSTATE), F32)],
        compiler_params=_cp("arbitrary"),
        name="s5_mix",
    )(u, pin, pout, w1, w2, w3, mul)


def s5_params(lam_re, lam_im, log_dt, b_re, b_im, c_re, c_im):
    G, P, C = S5_NGROUPS, S5_STATE, S5_GROUP
    lam = lax.complex(lam_re.astype(F32), lam_im.astype(F32))
    step = jnp.exp(log_dt.astype(F32))[..., None]
    log_lb = lam * step
    lam_bar = jnp.exp(log_lb)
    b_bar = ((lam_bar - 1.0) / lam)[..., None] * lax.complex(b_re.astype(F32), b_im.astype(F32))
    c_mat = lax.complex(c_re.astype(F32), c_im.astype(F32))
    J = S5_BLK
    jj = jnp.arange(J)
    hi = lax.Precision.HIGHEST

    def powers(expo):
        e = expo.astype(F32).reshape((2, 1, 1) + expo.shape[1:])
        return jnp.exp(log_lb.reshape((2, G, P) + (1,) * (expo.ndim - 1)) * e)

    def pair_blockdiag(m):
        r, s = m.shape[2:]
        m = m.reshape(2, G // 2, 2, r, s)
        return jnp.einsum('dqgrs,gh->dqgrhs', m, jnp.eye(2, dtype=F32)).reshape(2, G // 2, 2 * r, 2 * s)

    w1c = jnp.einsum('dgpj,dgpc->dgjcp', powers(jnp.stack([J - 1 - jj, jj])), b_bar,
                     precision=hi).reshape(2, G, J * C, P)
    w1 = jnp.concatenate([pair_blockdiag(jnp.real(w1c)), pair_blockdiag(jnp.imag(w1c))], axis=-1)
    lam_l = powers(jnp.stack([jj, jj]))
    by_lag = jnp.real(jnp.sum(c_mat[:, :, None, :, :, None] * jnp.moveaxis(lam_l, 3, 2)[:, :, :, None, :, None]
                              * b_bar[:, :, None, None, :, :], axis=4))
    lag = np.arange(J)[None, :] - np.arange(J)[:, None]
    w2 = jnp.stack([jnp.where((sgn * lag >= 0)[None, :, :, None, None],
                              by_lag[d][:, np.abs(lag)], 0.0) for d, sgn in enumerate((1, -1))])
    w2 = pair_blockdiag(jnp.transpose(w2, (0, 1, 2, 5, 3, 4)).reshape(2, G, J * C, J * C))
    cl = jnp.einsum('dgcp,dgpi->dgpic', c_mat, powers(jnp.stack([jj + 1, J - jj])), precision=hi)
    cl = cl.reshape(2, G, P, J * C)
    w3 = jnp.concatenate([pair_blockdiag(jnp.real(cl)), pair_blockdiag(-jnp.imag(cl))], axis=2)
    rows = jnp.arange(SUB)
    pieces = []
    for d in range(2):
        log_blk = (log_lb[d] * float(J)).reshape(1, G * P)
        per_d = []
        for sh in (1, 2, 4):
            valid = (rows >= sh) if d == 0 else (rows < SUB - sh)
            per_d.append(jnp.where(valid[:, None], jnp.exp(log_blk * float(sh)), 0.0))
        expo = (rows + 1) if d == 0 else (SUB - rows)
        per_d.append(jnp.exp(log_blk * expo[:, None].astype(F32)))
        m = jnp.concatenate(per_d, axis=0)
        pieces.append(jnp.concatenate([jnp.real(m), jnp.imag(m)], axis=-1))
    mul = jnp.stack(pieces, axis=0).astype(F32)
    return w1.astype(BF16), w2.astype(BF16), w3.astype(BF16), mul


def _softmax_pv(parts):
    m = parts[0][0].max(axis=-1, keepdims=True)
    for s, _ in parts[1:]:
        m = jnp.maximum(m, s.max(axis=-1, keepdims=True))
    den = 0.0
    acc = 0.0
    for s, v in parts:
        p = jnp.exp(s - m)
        den = den + p.sum(axis=-1, keepdims=True)
        acc = acc + _dot(p.astype(BF16), v)
    return acc / den


def _na_kernel(q_ref, k_ref, v_ref, bias_ref, o_ref):
    t = pl.program_id(1)
    first = lax.broadcasted_iota(I32, (1, LANE), 1) < NA_HEAD_DIM

    def pair_attention(pp, start):
        ls = slice(pp * LANE, (pp + 1) * LANE)
        qp = q_ref[0, :, ls]
        kc = k_ref[0, 0:CTX_LEN, ls]
        vc = v_ref[0, 0:CTX_LEN, ls]
        outs = []
        for hh in range(2):
            qm = jnp.where(first if hh == 0 else ~first, qp, jnp.zeros_like(qp))
            parts = [(_dot_nt(qm, kc), vc)]
            if start is not None:
                kb = k_ref[0, pl.ds(start, NA_BAND * GRID_W), ls]
                vb = v_ref[0, pl.ds(start, NA_BAND * GRID_W), ls]
                parts.append((_dot_nt(qm, kb) + bias_ref[2 * pp + hh, 0], vb))
            outs.append(_softmax_pv(parts))
        o_ref[0, :, ls] = jnp.where(first, outs[0], outs[1]).astype(BF16)

    @pl.when(t == 0)
    def _():
        for pp in range(NA_HEADS // 2):
            pair_attention(pp, None)

    @pl.when(t > 0)
    def _():
        first_row = (t - 1) * (TILE // GRID_W)
        u0 = jnp.clip(first_row - NA_KH // 2, 0, ROWS - NA_BAND)
        start = pl.multiple_of(CTX_LEN + u0 * GRID_W, LANE)
        for pp in range(NA_HEADS // 2):
            pair_attention(pp, start)


def _na_cfg(t):
    return jnp.where(t <= 1, 0, jnp.where(t == NT - 1, 2, 1))


def natten(q, k, v, bias):
    whole = pl.BlockSpec((1, T, NA_WIDTH), lambda b, t: (b, 0, 0))
    tile = pl.BlockSpec((1, TILE, NA_WIDTH), lambda b, t: (b, t, 0))
    return pl.pallas_call(
        _na_kernel,
        grid=(BATCH, NT),
        in_specs=[tile, whole, whole,
                  pl.BlockSpec((NA_HEADS, 1, TILE, NA_BAND * GRID_W), lambda b, t: (0, _na_cfg(t), 0, 0))],
        out_specs=tile,
        out_shape=jax.ShapeDtypeStruct((BATCH, T, NA_WIDTH), BF16),
        compiler_params=_cp("arbitrary", "arbitrary"),
        name="natten",
    )(q, k, v, bias)


def natten_bias(rpb):
    col = jnp.arange(GRID_W)
    c0 = jnp.clip(col - NA_KW // 2, 0, GRID_W - NA_KW)
    in_win = (col[None, :] >= c0[:, None]) & (col[None, :] < c0[:, None] + NA_KW)
    rel_c = jnp.clip(col[None, :] - col[:, None] + (NA_KW - 1), 0, RPB_W - 1)
    pick_c = jax.nn.one_hot(rel_c, RPB_W, dtype=F32)
    blocks = jnp.einsum('hax,qkx->haqk', rpb.astype(F32), pick_c, precision=lax.Precision.HIGHEST)
    blocks = jnp.where(in_win[None, None], blocks, NEG)
    blocks = jnp.concatenate([blocks, jnp.full((NA_HEADS, 1, GRID_W, GRID_W), NEG, F32)], axis=1)
    blocks = jnp.concatenate([blocks, blocks], axis=-1)
    return pl.pallas_call(
        _bias_kernel,
        grid=(NA_HEADS, 3),
        in_specs=[pl.BlockSpec((1, 2 * NA_KH, GRID_W, LANE), lambda h, c: (h, 0, 0, 0))],
        out_specs=pl.BlockSpec((1, 1, TILE, NA_BAND * GRID_W), lambda h, c: (h, c, 0, 0)),
        out_shape=jax.ShapeDtypeStruct((NA_HEADS, 3, TILE, NA_BAND * GRID_W), F32),
        compiler_params=_cp("arbitrary", "arbitrary"),
        name="natten_bias",
    )(blocks)


def _bias_kernel(blk_ref, o_ref):
    rows_per_tile = TILE // GRID_W
    masked = 2 * NA_KH - 1
    for cfg, first_row in enumerate((0, 2 * rows_per_tile, ROWS - rows_per_tile)):
        @pl.when(pl.program_id(1) == cfg)
        def _(first_row=first_row):
            u0 = min(max(first_row - NA_KH // 2, 0), ROWS - NA_BAND)
            for rr in range(rows_per_tile):
                r = first_row + rr
                r0 = min(max(r - NA_KH // 2, 0), ROWS - NA_KH)
                for j in range(NA_BAND):
                    kr = u0 + j
                    a = kr - r + NA_KH - 1 if r0 <= kr < r0 + NA_KH else masked
                    half = (j % 2) * GRID_W
                    o_ref[0, 0, rr * GRID_W:(rr + 1) * GRID_W, j * GRID_W:(j + 1) * GRID_W] = (
                        blk_ref[0, a, :, half:half + GRID_W])


def _softplus(x):
    return jnp.maximum(x, 0.0) + jnp.log(1.0 + jnp.exp(-jnp.abs(x)))


def _ssd_prep_kernel(prev_ref, cur_ref, next_ref, dtr_ref, cw_ref, cb_ref, dtb_ref, ar_ref, cos_ref, sin_ref,
                     xs_ref, bm_ref, cm_ref, dt_ref, a_ref):
    t = pl.program_id(1)
    halo = prev_ref.shape[1]
    has_prev = t >= 2
    has_next = (t >= 1) & (t <= NT - 2)
    prev = jnp.where(has_prev, prev_ref[0].astype(F32), 0.0)
    nxt = jnp.where(has_next, next_ref[0].astype(F32), 0.0)
    ext = jnp.concatenate([prev, cur_ref[0].astype(F32), nxt], axis=0)
    n = ext.shape[0]
    acc = cb_ref[...] + cw_ref[SSD_CONV // 2:SSD_CONV // 2 + 1, :] * ext
    for kk in range(SSD_CONV):
        off = kk - SSD_CONV // 2
        if off != 0:
            acc = acc + cw_ref[kk:kk + 1, :] * pltpu.roll(ext, (-off) % n, 0)
    y = _silu(acc[halo:halo + TILE, :])
    xs_ref[0] = y[:, 0:SSD_WIDTH].astype(BF16)

    lane = lax.broadcasted_iota(I32, (1, LANE), 1)
    low = (lane & (NA_HEAD_DIM // 2)) == 0
    cos = cos_ref[...]
    sin = sin_ref[...]
    for g in range(2 * SSD_NGROUPS):
        v = y[:, SSD_WIDTH + g * LANE:SSD_WIDTH + (g + 1) * LANE]
        sw = jnp.where(low, pltpu.roll(v, LANE - 32, 1), pltpu.roll(v, 32, 1))
        rot = (v * cos + sw * sin).astype(BF16)
        if g < SSD_NGROUPS:
            bm_ref[0, :, g * LANE:(g + 1) * LANE] = rot
        else:
            cm_ref[0, :, (g - SSD_NGROUPS) * LANE:(g - SSD_NGROUPS + 1) * LANE] = rot

    dt = _softplus(dtr_ref[0] + dtb_ref[...])
    dt_ref[0] = dt
    a_ref[0] = dt * ar_ref[...]


def ssd_prep(xbc, dt_raw, conv_w, conv_b, dt_bias, a_log, cos_t, sin_t):
    halo = 16
    per = TILE // halo
    nhalo = T // halo
    tok = lambda w, dt: jax.ShapeDtypeStruct((BATCH, T, w), dt)
    tspec = lambda w: pl.BlockSpec((1, TILE, w), lambda b, t: (b, t, 0))
    row = lambda w: pl.BlockSpec((1, w), lambda b, t: (0, 0))
    cw = jnp.zeros((SUB, SSD_XBC), F32).at[:SSD_CONV].set(conv_w.astype(F32))
    pad12 = lambda v: jnp.zeros((1, LANE), F32).at[0, :2 * SSD_HEADS].set(v.astype(F32).reshape(-1))
    return pl.pallas_call(
        _ssd_prep_kernel,
        grid=(BATCH, NT),
        in_specs=[pl.BlockSpec((1, halo, SSD_XBC), lambda b, t: (b, jnp.maximum(t * per - 1, 0), 0)),
                  tspec(SSD_XBC),
                  pl.BlockSpec((1, halo, SSD_XBC), lambda b, t: (b, jnp.minimum((t + 1) * per, nhalo - 1), 0)),
                  tspec(LANE),
                  pl.BlockSpec((SUB, SSD_XBC), lambda b, t: (0, 0)), row(SSD_XBC), row(LANE), row(LANE),
                  pl.BlockSpec((TILE, LANE), lambda b, t: (t, 0)), pl.BlockSpec((TILE, LANE), lambda b, t: (t, 0))],
        out_specs=[tspec(SSD_WIDTH), tspec(SSD_BC), tspec(SSD_BC), tspec(LANE), tspec(LANE)],
        out_shape=[tok(SSD_WIDTH, BF16), tok(SSD_BC, BF16), tok(SSD_BC, BF16), tok(LANE, F32), tok(LANE, F32)],
        compiler_params=_cp("arbitrary", "arbitrary"),
        name="ssd_prep",
    )(xbc, xbc, xbc, dt_raw, cw, conv_b.astype(F32).reshape(1, SSD_XBC), pad12(dt_bias),
      pad12(-jnp.exp(a_log.astype(F32))), cos_t, sin_t)


def rope_tables():
    half = SSD_STATE // 2
    nf = half // 2
    pos = jnp.arange(SEQ)
    inv_freq = ROPE_BASE ** (-jnp.arange(nf, dtype=F32) / nf)
    lane = jnp.arange(LANE)
    p = jnp.where(lane[None, :] < half, (pos // GRID_W)[:, None], (pos % GRID_W)[:, None]).astype(F32)
    ang = p * inv_freq[lane % nf][None, :]
    sign = jnp.where((lane & nf) == 0, -1.0, 1.0)[None, :]
    cos_t = jnp.concatenate([jnp.ones((CTX_LEN, LANE), F32), jnp.cos(ang)], axis=0)
    sin_t = jnp.concatenate([jnp.zeros((CTX_LEN, LANE), F32), jnp.sin(ang) * sign], axis=0)
    return cos_t, sin_t


def _ssd_dir(d, xs_ref, bm_ref, cm_ref, bt_ref, dt_ref, a_ref, at_ref, tri_ref, y_ref, st_ref):
    q = TILE
    lane = lax.broadcasted_iota(I32, (1, LANE), 1)
    first = lane < SSD_HEAD_DIM
    ri = lax.broadcasted_iota(I32, (q, q), 0)
    ci = lax.broadcasted_iota(I32, (q, q), 1)
    keep = (ci <= ri) if d == 0 else (ci >= ri)
    end = q - 1 if d == 0 else 0
    tri_col = tri_ref[d]
    tri_row = tri_ref[1 - d]
    a = a_ref[0]
    dt = dt_ref[0]
    cs_col = _dot_exact_rhs(tri_col, a)
    cs_row = _dot_exact_lhs(at_ref[0], tri_row)
    g_mats = [_dot_nt(cm_ref[0, :, g * SSD_STATE:(g + 1) * SSD_STATE],
                      bm_ref[0, :, g * SSD_STATE:(g + 1) * SSD_STATE]) for g in range(SSD_NGROUPS)]

    def head_col(m, h):
        c = d * SSD_HEADS + h
        return m[:, c:c + 1]

    for pp in range(SSD_HEADS // 2):
        ls = slice(pp * LANE, (pp + 1) * LANE)
        h0, h1 = 2 * pp, 2 * pp + 1
        x = xs_ref[0, :, ls].astype(F32)
        dt_l = jnp.where(first, head_col(dt, h0), head_col(dt, h1))
        cs_l = jnp.where(first, head_col(cs_col, h0), head_col(cs_col, h1))
        cs_end = cs_l[end:end + 1, :]
        xdt = x * dt_l
        xdt_b = xdt.astype(BF16)
        xw = (xdt * jnp.exp(cs_end - cs_l)).astype(BF16)
        st = st_ref[d, pp]
        st_b = st.astype(BF16)
        ys, ups = [], []
        for h in (h0, h1):
            g = h // (SSD_HEADS // SSD_NGROUPS)
            c = d * SSD_HEADS + h
            diff = head_col(cs_col, h) - cs_row[c:c + 1, :]
            decay = jnp.where(keep, jnp.exp(jnp.where(keep, diff, 0.0)), 0.0)
            m = (g_mats[g] * decay).astype(BF16)
            y_h = _dot(m, xdt_b) + _dot(cm_ref[0, :, g * SSD_STATE:(g + 1) * SSD_STATE], st_b) * jnp.exp(cs_l)
            ys.append(y_h)
            ups.append(_dot(bt_ref[0, g * SSD_STATE:(g + 1) * SSD_STATE, :], xw))
        y_ref[0, :, ls] = jnp.where(first, ys[0], ys[1])
        st_ref[d, pp] = jnp.exp(cs_end) * st + jnp.where(first, ups[0], ups[1])


def _ssd_scan_kernel(xs_f, bm_f, cm_f, bt_f, dt_f, a_f, at_f, xs_b, bm_b, cm_b, bt_b, dt_b, a_b, at_b, tri_ref,
                     yf_ref, yb_ref, st_ref):
    @pl.when(pl.program_id(1) == 0)
    def _():
        st_ref[...] = jnp.zeros_like(st_ref)

    _ssd_dir(0, xs_f, bm_f, cm_f, bt_f, dt_f, a_f, at_f, tri_ref, yf_ref, st_ref)
    _ssd_dir(1, xs_b, bm_b, cm_b, bt_b, dt_b, a_b, at_b, tri_ref, yb_ref, st_ref)


def ssd_scan(xs, bm, cm, dt, a):
    bt = jnp.swapaxes(bm, 1, 2)
    at = jnp.swapaxes(a[:, :, :2 * SUB], 1, 2)
    idx = jnp.arange(TILE)
    tri = jnp.stack([idx[None, :] <= idx[:, None], idx[None, :] >= idx[:, None]]).astype(BF16)
    fwd = lambda b, i: (b, i, 0)
    bwd = lambda b, i: (b, _bwd_tile(i), 0)
    fwd_t = lambda b, i: (b, 0, i)
    bwd_t = lambda b, i: (b, 0, _bwd_tile(i))

    def specs(f, ft):
        return [pl.BlockSpec((1, TILE, SSD_WIDTH), f), pl.BlockSpec((1, TILE, SSD_BC), f),
                pl.BlockSpec((1, TILE, SSD_BC), f), pl.BlockSpec((1, SSD_BC, TILE), ft),
                pl.BlockSpec((1, TILE, LANE), f), pl.BlockSpec((1, TILE, LANE), f),
                pl.BlockSpec((1, 2 * SUB, TILE), ft)]

    args = (xs, bm, cm, bt, dt, a, at)
    return pl.pallas_call(
        _ssd_scan_kernel,
        grid=(BATCH, NT),
        in_specs=specs(fwd, fwd_t) + specs(bwd, bwd_t) + [pl.BlockSpec((2, TILE, TILE), lambda b, i: (0, 0, 0))],
        out_specs=[pl.BlockSpec((1, TILE, SSD_WIDTH), fwd), pl.BlockSpec((1, TILE, SSD_WIDTH), bwd)],
        out_shape=[jax.ShapeDtypeStruct((BATCH, T, SSD_WIDTH), F32)] * 2,
        scratch_shapes=[pltpu.VMEM((2, SSD_HEADS // 2, SSD_STATE, LANE), F32)],
        compiler_params=_cp("arbitrary", "arbitrary"),
        name="ssd_scan",
    )(*args, *args, tri)


def _gelu_tanh(x):
    return 0.5 * x * (1.0 + jnp.tanh(math.sqrt(2.0 / math.pi) * (x + 0.044715 * (x * x * x))))


def _post_kernel(x_ref, mod_ref, u_ref, s5y_ref, na_ref, xs_ref, z_ref, sdf_ref, sdb_ref,
                 s5d_ref, gw_ref, gb_ref, sdd_ref, snw_ref, wo_ref, n2w_ref, rt_ref,
                 x1_ref, h_ref, lg_ref):
    slabs = lambda r: jnp.concatenate([r[0, h] for h in range(S5_WIDTH // LANE)], axis=-1)
    ys5 = slabs(u_ref) * s5d_ref[...] + slabs(s5y_ref)
    g = _gelu_tanh(ys5)
    s5o = g * jax.nn.sigmoid(_dot(g.astype(BF16), gw_ref[...]) + gb_ref[...])
    yssd = (xs_ref[0].astype(F32) * sdd_ref[...] + sdf_ref[0] + sdb_ref[0]) * _silu(z_ref[0].astype(F32))
    ssdo = yssd * lax.rsqrt(jnp.mean(yssd * yssd, axis=-1, keepdims=True) + EPS) * snw_ref[...]
    mix = jnp.concatenate([s5o.astype(BF16), na_ref[0], ssdo.astype(BF16)], axis=-1)
    x1 = x_ref[0] + mod_ref[0, 0, 2:3, :] * _dot(mix, wo_ref[...])
    x1_ref[0] = x1
    h = _modulated_norm(x1, n2w_ref[...], mod_ref[0, 0, 3:4, :], mod_ref[0, 0, 4:5, :])
    h_ref[0] = h.astype(BF16)
    lg_ref[0] = _dot_x3(h, rt_ref[...])


def post_mixer(x, mod, u, s5y, na, xs, z, sdf, sdb, s5_d, glu_w, glu_b, ssd_d, ssd_norm_w, w_out, norm2_w,
               router):
    tspec = lambda w: pl.BlockSpec((1, TILE, w), lambda b, t: (b, t, 0))
    whole = lambda *shp: pl.BlockSpec(shp, lambda b, t: (0,) * len(shp))
    rt = jnp.zeros((D_MODEL, LANE), F32).at[:, :N_EXPERTS].set(router.astype(F32))
    slab = pl.BlockSpec((1, S5_WIDTH // LANE, TILE, LANE), lambda b, t: (b, 0, t, 0))
    return pl.pallas_call(
        _post_kernel,
        grid=(BATCH, NT),
        in_specs=[tspec(D_MODEL), pl.BlockSpec((1, 1, 6, D_MODEL), lambda b, t: (b, _seg(t), 0, 0)),
                  slab, slab, tspec(NA_WIDTH),
                  tspec(SSD_WIDTH), tspec(SSD_WIDTH), tspec(SSD_WIDTH), tspec(SSD_WIDTH),
                  whole(1, S5_WIDTH), whole(S5_WIDTH, S5_WIDTH), whole(1, S5_WIDTH),
                  whole(1, SSD_WIDTH), whole(1, SSD_WIDTH), whole(D_MODEL, D_MODEL), whole(1, D_MODEL),
                  whole(D_MODEL, LANE)],
        out_specs=[tspec(D_MODEL), tspec(D_MODEL), tspec(LANE)],
        out_shape=[jax.ShapeDtypeStruct((BATCH, T, D_MODEL), F32), jax.ShapeDtypeStruct((BATCH, T, D_MODEL), BF16),
                   jax.ShapeDtypeStruct((BATCH, T, LANE), F32)],
        compiler_params=_cp("arbitrary", "arbitrary"),
        name="post_mixer",
    )(x, mod, u, s5y, na, xs, z, sdf, sdb,
      s5_d.astype(F32).reshape(1, S5_WIDTH), glu_w.astype(BF16), glu_b.astype(F32).reshape(1, S5_WIDTH),
      jnp.repeat(ssd_d.astype(F32), SSD_HEAD_DIM).reshape(1, SSD_WIDTH), ssd_norm_w.astype(F32).reshape(1, SSD_WIDTH),
      w_out.astype(BF16), norm2_w.astype(F32).reshape(1, D_MODEL), rt)


def _route_kernel(lg_ref, tri_ref, slot_ref, aff_ref, *, with_ctx):
    lg = lg_ref[0]
    m = lg.max(axis=0, keepdims=True)
    e = jnp.exp(lg - m)
    aff = e / e.sum(axis=0, keepdims=True)
    aff_ref[0] = aff
    bits = pltpu.bitcast(aff, I32)
    is_ctx = lax.broadcasted_iota(I32, (N_EXPERTS, T), 1) < CTX_LEN

    def count(mask):
        return jnp.where(mask, 1.0, 0.0).sum(axis=1, keepdims=True)

    def kth_largest(seg, k):
        def body(i, prefix):
            cand = prefix | lax.shift_left(jnp.int32(1), 30 - i)
            return jnp.where(count((bits >= cand) & seg) >= k, cand, prefix)
        return lax.fori_loop(0, 31, body, jnp.zeros((N_EXPERTS, 1), I32))

    def excl_cumsum(x01):
        carry = jnp.zeros((N_EXPERTS, 1), F32)
        pieces = []
        for j in range(T // LANE):
            blk = x01[:, j * LANE:(j + 1) * LANE]
            inc = _dot(blk.astype(BF16), tri_ref[...])
            pieces.append(inc - blk + carry)
            carry = carry + inc[:, LANE - 1:LANE]
        return jnp.concatenate(pieces, axis=1)

    thr = kth_largest(~is_ctx, float(CAP_LAT))
    k_of = jnp.full((N_EXPERTS, T), float(CAP_LAT), F32)
    if with_ctx:
        thr = jnp.where(is_ctx, kth_largest(is_ctx, float(CAP_CTX)), thr)
        k_of = jnp.where(is_ctx, float(CAP_CTX), k_of)
    gt = bits > thr
    eq = bits == thr
    if not with_ctx:
        gt = gt & ~is_ctx
        eq = eq & ~is_ctx
    n_gt = jnp.where(is_ctx, count(gt & is_ctx), count(gt & ~is_ctx))
    tie_rank = excl_cumsum(jnp.where(eq, 1.0, 0.0))
    tie_rank = tie_rank - jnp.where(is_ctx, 0.0, count(eq & is_ctx))
    sel = gt | (eq & (tie_rank < k_of - n_gt))
    pos = excl_cumsum(jnp.where(sel, 1.0, 0.0))
    slot = jnp.where(is_ctx, pos + float(CAP_LAT), pos - count(sel & is_ctx))
    slot_ref[0] = jnp.where(sel, slot, -1.0).astype(I32)


def route(logits_t, with_ctx):
    idx = jnp.arange(LANE)
    tri = (idx[:, None] <= idx[None, :]).astype(BF16)
    spec = pl.BlockSpec((1, N_EXPERTS, T), lambda b: (b, 0, 0))
    return pl.pallas_call(
        functools.partial(_route_kernel, with_ctx=with_ctx),
        grid=(BATCH,),
        in_specs=[spec, pl.BlockSpec((LANE, LANE), lambda b: (0, 0))],
        out_specs=[spec, spec],
        out_shape=[jax.ShapeDtypeStruct((BATCH, N_EXPERTS, T), I32), jax.ShapeDtypeStruct((BATCH, N_EXPERTS, T), F32)],
        compiler_params=_cp("arbitrary"),
        name="route",
    )(logits_t, tri)


GATHER_WIN = LANE
COMBINE_WIN = LANE


def slot_ranges(slot):
    s = slot.reshape(BATCH, N_EXPERTS, NT, TILE)
    has = s >= 0
    smax = jnp.max(jnp.where(has, s, -1), axis=-1)
    smin = jnp.where(smax >= 0, jnp.min(jnp.where(has, s, CAP_LAT + CAP_CTX), axis=-1), 0)
    return smin.reshape(-1).astype(I32), smax.reshape(-1).astype(I32)


def _gather_kernel(smin_ref, smax_ref, h_ref, slot_ref, xs_ref, *, nslot):
    b = pl.program_id(0)
    t = pl.program_id(1)
    group = 8
    sid = lax.broadcasted_iota(I32, (GATHER_WIN, TILE), 0)
    align = 2 * SUB

    def onehot(e, ws, lo):
        srow = slot_ref[0, e:e + 1, :]
        return jnp.where((sid + ws == srow) & (srow >= lo), 1.0, 0.0).astype(BF16)

    def add_rows(e, ws, rows):
        win = pl.ds(pl.multiple_of(ws, align), GATHER_WIN)
        xs_ref[0, e, win, :] = (xs_ref[0, e, win, :].astype(F32) + rows).astype(BF16)

    @pl.when(t == 0)
    def _():
        for e in range(N_EXPERTS):
            xs_ref[0, e, 0:CAP_LAT, :] = jnp.zeros((CAP_LAT, D_MODEL), BF16)
            if nslot > CAP_LAT:
                cid = lax.broadcasted_iota(I32, (nslot - CAP_LAT, TILE), 0) + CAP_LAT
                pick = jnp.where(cid == slot_ref[0, e:e + 1, :], 1.0, 0.0).astype(BF16)
                xs_ref[0, e, CAP_LAT:nslot, :] = _dot(pick, h_ref[0]).astype(BF16)

    @pl.when(t > 0)
    def _():
        los, wss, extras = [], [], []
        for e in range(N_EXPERTS):
            base = (b * N_EXPERTS + e) * NT + t
            lo = smin_ref[base] & ~(align - 1)
            los.append(lo)
            wss.append(jnp.minimum(lo, CAP_LAT - GATHER_WIN))
            extras.append(lax.shift_right_arithmetic(smax_ref[base] - lo, GATHER_WIN.bit_length() - 1))
        for g0 in range(0, N_EXPERTS, group):
            pick = jnp.concatenate([onehot(e, wss[e], los[e]) for e in range(g0, g0 + group)], axis=0)
            rows = _dot(pick, h_ref[0])
            for i, e in enumerate(range(g0, g0 + group)):
                add_rows(e, wss[e], rows[i * GATHER_WIN:(i + 1) * GATHER_WIN, :])
        most = extras[0]
        for x in extras[1:]:
            most = jnp.maximum(most, x)

        @pl.when(most > 0)
        def _():
            for e in range(N_EXPERTS):
                def more(k, carry, e=e):
                    lo_k = los[e] + k * GATHER_WIN
                    ws = jnp.minimum(lo_k, CAP_LAT - GATHER_WIN)
                    add_rows(e, ws, _dot(onehot(e, ws, lo_k), h_ref[0]))
                    return carry

                lax.fori_loop(1, extras[e] + 1, more, 0)


def moe_gather(h, slot, smin, smax, nslot):
    return pl.pallas_call(
        functools.partial(_gather_kernel, nslot=nslot),
        grid_spec=pltpu.PrefetchScalarGridSpec(
            num_scalar_prefetch=2,
            grid=(BATCH, NT),
            in_specs=[pl.BlockSpec((1, TILE, D_MODEL), lambda b, t, *_: (b, t, 0)),
                      pl.BlockSpec((1, N_EXPERTS, TILE), lambda b, t, *_: (b, 0, t))],
            out_specs=pl.BlockSpec((1, N_EXPERTS, nslot, D_MODEL), lambda b, t, *_: (b, 0, 0, 0))),
        out_shape=jax.ShapeDtypeStruct((BATCH, N_EXPERTS, nslot, D_MODEL), BF16),
        compiler_params=_cp("arbitrary", "arbitrary"),
        name="moe_gather",
    )(smin, smax, h, slot)


def _ffn_kernel(xs_ref, wg_ref, wu_ref, wd_ref, y_ref, w_ref):
    @pl.when(pl.program_id(1) == 0)
    def _():
        w_ref[0] = wg_ref[0, 0].astype(BF16)
        w_ref[1] = wu_ref[0, 0].astype(BF16)
        w_ref[2] = wd_ref[0, 0].astype(BF16)

    xs = xs_ref[0, 0]
    hid = _silu(_dot(xs, w_ref[0])) * _dot(xs, w_ref[1])
    y_ref[0, 0] = _dot(hid.astype(BF16), w_ref[2]).astype(BF16)


def moe_ffn(xs, layer, wg, wu, wd, nslot):
    wspec = lambda: pl.BlockSpec((1, 1, D_MODEL, D_EXPERT), lambda e, b: (layer, e, 0, 0))
    rows = pl.BlockSpec((1, 1, nslot, D_MODEL), lambda e, b: (b, e, 0, 0))
    return pl.pallas_call(
        _ffn_kernel,
        grid=(N_EXPERTS, BATCH),
        in_specs=[rows, wspec(), wspec(), wspec()],
        out_specs=rows,
        out_shape=jax.ShapeDtypeStruct((BATCH, N_EXPERTS, nslot, D_MODEL), BF16),
        scratch_shapes=[pltpu.VMEM((3, D_MODEL, D_EXPERT), BF16)],
        compiler_params=_cp("arbitrary", "arbitrary"),
        name="moe_ffn",
    )(xs, wg, wu, wd)


def _combine_kernel(smin_ref, smax_ref, x_ref, mod_ref, slot_ref, aff_ref, y_ref, fw_ref, o_ref, acc_ref, *,
                    nslot, last):
    b = pl.program_id(0)
    t = pl.program_id(1)
    shift = COMBINE_WIN.bit_length() - 1
    align = 2 * SUB

    def weights(e, lane, lo=None):
        s = slot_ref[0, :, e:e + 1]
        hit = (s == lane) if lo is None else ((s == lane) & (s >= lo))
        return jnp.where(hit, aff_ref[0, :, e:e + 1], 0.0).astype(BF16)

    def finish(acc):
        x2 = x_ref[0] + mod_ref[0, 0, 5:6, :] * acc
        if last:
            x2 = x2 * lax.rsqrt(jnp.mean(x2 * x2, axis=-1, keepdims=True) + EPS) * fw_ref[...]
        o_ref[0] = x2

    if not last:
        @pl.when(t == 0)
        def _():
            lane = lax.broadcasted_iota(I32, (TILE, nslot - CAP_LAT), 1) + CAP_LAT
            acc = jnp.zeros((TILE, D_MODEL), F32)
            for e in range(N_EXPERTS):
                acc = acc + _dot(weights(e, lane), y_ref[0, e, CAP_LAT:nslot, :])
            finish(acc)

    @pl.when(t > 0)
    def _():
        lane = lax.broadcasted_iota(I32, (TILE, COMBINE_WIN), 1)
        acc = jnp.zeros((TILE, D_MODEL), F32)
        los, extras = [], []
        for e in range(N_EXPERTS):
            base = (b * N_EXPERTS + e) * NT + t
            lo = smin_ref[base] & ~(align - 1)
            ws = pl.multiple_of(jnp.minimum(lo, CAP_LAT - COMBINE_WIN), align)
            acc = acc + _dot(weights(e, lane + ws), y_ref[0, e, pl.ds(ws, COMBINE_WIN), :])
            los.append(lo)
            extras.append(lax.shift_right_arithmetic(smax_ref[base] - lo, shift))
        acc_ref[...] = acc
        most = extras[0]
        for x in extras[1:]:
            most = jnp.maximum(most, x)

        @pl.when(most > 0)
        def _():
            for e in range(N_EXPERTS):
                def more(k, carry, e=e):
                    lo_k = los[e] + k * COMBINE_WIN
                    ws = pl.multiple_of(jnp.minimum(lo_k, CAP_LAT - COMBINE_WIN), align)
                    acc_ref[...] += _dot(weights(e, lane + ws, lo_k), y_ref[0, e, pl.ds(ws, COMBINE_WIN), :])
                    return carry

                lax.fori_loop(1, extras[e] + 1, more, 0)

        finish(acc_ref[...])


def moe_combine(x1, mod, slot_tok, aff_tok, y, smin, smax, nslot, final_w):
    last = final_w is not None
    tspec = lambda w: pl.BlockSpec((1, TILE, w), lambda b, t, *_: (b, t, 0))
    if last:
        first_lat = CTX_LEN // TILE
        out_spec = pl.BlockSpec((1, TILE, D_MODEL), lambda b, t, *_: (b, jnp.maximum(t - first_lat, 0), 0))
        out_shape = jax.ShapeDtypeStruct((BATCH, SEQ, D_MODEL), F32)
        fw = final_w.astype(F32).reshape(1, D_MODEL)
    else:
        out_spec = tspec(D_MODEL)
        out_shape = jax.ShapeDtypeStruct((BATCH, T, D_MODEL), F32)
        fw = jnp.ones((1, D_MODEL), F32)
    return pl.pallas_call(
        functools.partial(_combine_kernel, nslot=nslot, last=last),
        grid_spec=pltpu.PrefetchScalarGridSpec(
            num_scalar_prefetch=2,
            grid=(BATCH, NT),
            in_specs=[tspec(D_MODEL),
                      pl.BlockSpec((1, 1, 6, D_MODEL), lambda b, t, *_: (b, _seg(t), 0, 0)),
                      tspec(N_EXPERTS), tspec(N_EXPERTS),
                      pl.BlockSpec((1, N_EXPERTS, nslot, D_MODEL), lambda b, t, *_: (b, 0, 0, 0)),
                      pl.BlockSpec((1, D_MODEL), lambda b, t, *_: (0, 0))],
            out_specs=out_spec,
            scratch_shapes=[pltpu.VMEM((TILE, D_MODEL), F32)]),
        out_shape=out_shape,
        compiler_params=_cp("arbitrary", "arbitrary"),
        name="moe_combine",
    )(smin, smax, x1, mod, slot_tok, aff_tok, y, fw)


def trunk_layer(l, x, mod, cos_t, sin_t, norm1_w, norm2_w, w_in, w_out,
                s5_lam_re, s5_lam_im, s5_log_dt, s5_b_re, s5_b_im, s5_c_re, s5_c_im, s5_d, s5_glu_w, s5_glu_b,
                na_rpb, ssd_conv_w, ssd_conv_b, ssd_dt_bias, ssd_a_log, ssd_d, ssd_norm_w,
                moe_router, wg, wu, wd, final_w):
    with_ctx_out = final_w is None
    w_in_p = jnp.zeros((D_MODEL, IN_COLS_PAD), BF16).at[:, :IN_COLS].set(w_in.astype(BF16))
    u, q, k, v, z, xbc, dt_raw = in_proj(x, mod, norm1_w, w_in_p)

    s5y = s5_mix(u, *s5_params(s5_lam_re, s5_lam_im, s5_log_dt, s5_b_re, s5_b_im, s5_c_re, s5_c_im))
    na = natten(q, k, v, natten_bias(na_rpb))
    xs, bm, cm, dt, a = ssd_prep(xbc, dt_raw, ssd_conv_w, ssd_conv_b, ssd_dt_bias, ssd_a_log, cos_t, sin_t)
    sdf, sdb = ssd_scan(xs, bm, cm, dt, a)

    x1, h, logits = post_mixer(x, mod, u, s5y, na, xs, z, sdf, sdb, s5_d, s5_glu_w, s5_glu_b,
                               ssd_d, ssd_norm_w, w_out, norm2_w, moe_router)
    slot, aff = route(jnp.swapaxes(logits[:, :, :N_EXPERTS], 1, 2), with_ctx_out)
    nslot = CAP_LAT + CAP_CTX if with_ctx_out else CAP_LAT
    smin, smax = slot_ranges(slot)
    y = moe_ffn(moe_gather(h, slot, smin, smax, nslot), l, wg, wu, wd, nslot)
    return moe_combine(x1, mod, jnp.swapaxes(slot, 1, 2), jnp.swapaxes(aff, 1, 2), y, smin, smax, nslot, final_w)


def kernel(x, c, ctx, c_ctx, w_ada, b_ada, norm1_w, norm2_w, w_in, w_out, s5_lam_re, s5_lam_im, s5_log_dt, s5_b_re, s5_b_im, s5_c_re, s5_c_im, s5_d, s5_glu_w, s5_glu_b, na_rpb, ssd_conv_w, ssd_conv_b, ssd_dt_bias, ssd_a_log, ssd_d, ssd_norm_w, moe_router, moe_w_gate, moe_w_up, moe_w_down, final_norm_w):
    xa = jnp.concatenate([ctx, x], axis=1).astype(F32)
    cvec = jnp.zeros((SUB, D_MODEL), F32).at[0].set(c_ctx.astype(F32)).at[1:1 + BATCH].set(c.astype(F32))
    mods = ada_mod(cvec, w_ada.astype(F32), b_ada.astype(F32)).reshape(DEPTH, SUB, 6, D_MODEL)
    cos_t, sin_t = rope_tables()
    wg, wu, wd = moe_w_gate.astype(F32), moe_w_up.astype(F32), moe_w_down.astype(F32)
    for l in range(DEPTH):
        mod = jnp.stack([jnp.broadcast_to(mods[l, 0], (BATCH, 6, D_MODEL)), mods[l, 1:1 + BATCH]], axis=1)
        xa = trunk_layer(
            l, xa, mod, cos_t, sin_t, norm1_w[l], norm2_w[l], w_in[l], w_out[l],
            s5_lam_re[l], s5_lam_im[l], s5_log_dt[l], s5_b_re[l], s5_b_im[l], s5_c_re[l], s5_c_im[l],
            s5_d[l], s5_glu_w[l], s5_glu_b[l],
            na_rpb[l], ssd_conv_w[l], ssd_conv_b[l], ssd_dt_bias[l], ssd_a_log[l], ssd_d[l], ssd_norm_w[l],
            moe_router[l], wg, wu, wd, final_norm_w if l == DEPTH - 1 else None)
    return xa
```

```python
import functools
import math

import jax
import jax.numpy as jnp
import numpy as np
from jax import lax
from jax.experimental import pallas as pl
from jax.experimental.pallas import tpu as pltpu

F32 = jnp.float32
BF16 = jnp.bfloat16
I32 = jnp.int32

D_MODEL = 1024
BATCH = 4
SEQ = 4096
DEPTH = 2
GRID_W = 64
CTX_LEN = 256
EPS = 1e-6

S5_WIDTH = 256
S5_GROUP = 16
S5_NGROUPS = 16
S5_STATE = 64
S5_NSTATE = S5_NGROUPS * S5_STATE

NA_HEADS = 6
NA_HEAD_DIM = 64
NA_WIDTH = 384
NA_KH = 8
NA_KW = 16
NA_BAND = 12
RPB_W = 2 * NA_KW - 1

SSD_HEADS = 6
SSD_HEAD_DIM = 64
SSD_WIDTH = 384
SSD_NGROUPS = 2
SSD_STATE = 128
SSD_CONV = 5
SSD_BC = 256
SSD_XBC = 896

N_EXPERTS = 16
D_EXPERT = 1024
ROPE_BASE = 10000.0

T = CTX_LEN + SEQ
TILE = 256
NT = T // TILE
LANE = 128
SUB = 8
ROWS = SEQ // GRID_W
CAP_LAT = 2 * SEQ // N_EXPERTS
CAP_CTX = 2 * CTX_LEN // N_EXPERTS
NEG = -1e30

C_U = 0
C_Q = 256
C_K = 640
C_V = 1024
C_Z = 1408
C_XBC = 1792
C_DT = 2688
IN_COLS = 2700
IN_COLS_PAD = 2816

VMEM_LIMIT = 56 * 1024 * 1024


def _cp(*sem):
    return pltpu.CompilerParams(dimension_semantics=sem, vmem_limit_bytes=VMEM_LIMIT)


def _dot(a, b):
    return jnp.dot(a, b, preferred_element_type=F32)


def _dot_nt(a, b):
    return lax.dot_general(a, b, (((1,), (1,)), ((), ())), preferred_element_type=F32)


def _split3(x):
    hi = x.astype(BF16)
    r = x - hi.astype(F32)
    mid = r.astype(BF16)
    lo = (r - mid.astype(F32)).astype(BF16)
    return hi, mid, lo


def _dot_exact_rhs(a_bf16, b_f32):
    hi, mid, lo = _split3(b_f32)
    return _dot(a_bf16, hi) + _dot(a_bf16, mid) + _dot(a_bf16, lo)


def _dot_exact_lhs(a_f32, b_bf16):
    hi, mid, lo = _split3(a_f32)
    return _dot(hi, b_bf16) + _dot(mid, b_bf16) + _dot(lo, b_bf16)


def _dot_x3(a, b):
    ah = a.astype(BF16)
    al = (a - ah.astype(F32)).astype(BF16)
    bh = b.astype(BF16)
    bl = (b - bh.astype(F32)).astype(BF16)
    return _dot(ah, bh) + _dot(ah, bl) + _dot(al, bh)


def _silu(x):
    return x * jax.nn.sigmoid(x)


def _seg(t):
    return jnp.where(t >= CTX_LEN // TILE, 1, 0)


def _bwd_tile(i):
    return jnp.where(i == 0, 0, NT - i)


def _ada_kernel(c_ref, w_ref, b_ref, o_ref):
    s = _silu(c_ref[...])
    o_ref[0] = _dot_x3(s, w_ref[0]) + b_ref[0]


def ada_mod(cvec, w_ada, b_ada):
    nb = 1024
    return pl.pallas_call(
        _ada_kernel,
        grid=(DEPTH, 6 * D_MODEL // nb),
        in_specs=[pl.BlockSpec((SUB, D_MODEL), lambda l, j: (0, 0)),
                  pl.BlockSpec((1, D_MODEL, nb), lambda l, j: (l, 0, j)),
                  pl.BlockSpec((1, 1, nb), lambda l, j: (l, 0, j))],
        out_specs=pl.BlockSpec((1, SUB, nb), lambda l, j: (l, 0, j)),
        out_shape=jax.ShapeDtypeStruct((DEPTH, SUB, 6 * D_MODEL), F32),
        compiler_params=_cp("arbitrary", "arbitrary"),
        name="ada_mod",
    )(cvec, w_ada, b_ada.reshape(DEPTH, 1, 6 * D_MODEL))


def _modulated_norm(x, nw, shift, scale):
    y = x * lax.rsqrt(jnp.mean(x * x, axis=-1, keepdims=True) + EPS) * nw
    return y * (1.0 + scale) + shift


def _inproj_kernel(x_ref, mod_ref, nw_ref, w_ref, u_ref, q_ref, k_ref, v_ref, z_ref, xbc_ref, dt_ref):
    h = _modulated_norm(x_ref[0], nw_ref[...], mod_ref[0, 0, 0:1, :], mod_ref[0, 0, 1:2, :]).astype(BF16)

    def proj(lo, hi):
        return _dot(h, w_ref[:, lo:hi])

    for s in range(S5_WIDTH // LANE):
        u_ref[0, s] = proj(C_U + s * LANE, C_U + (s + 1) * LANE)
    q_ref[0] = (proj(C_Q, C_K) * (NA_HEAD_DIM ** -0.5)).astype(BF16)
    k_ref[0] = proj(C_K, C_V).astype(BF16)
    v_ref[0] = proj(C_V, C_Z).astype(BF16)
    z_ref[0] = proj(C_Z, C_XBC).astype(BF16)
    xbc_ref[0] = proj(C_XBC, C_DT).astype(BF16)
    dt_ref[0] = proj(C_DT, IN_COLS_PAD)


def in_proj(x, mod, norm_w, w_in_p):
    tok = lambda w, dt: jax.ShapeDtypeStruct((BATCH, T, w), dt)
    tspec = lambda w: pl.BlockSpec((1, TILE, w), lambda b, t: (b, t, 0))
    return pl.pallas_call(
        _inproj_kernel,
        grid=(BATCH, NT),
        in_specs=[tspec(D_MODEL),
                  pl.BlockSpec((1, 1, 6, D_MODEL), lambda b, t: (b, _seg(t), 0, 0)),
                  pl.BlockSpec((1, D_MODEL), lambda b, t: (0, 0)),
                  pl.BlockSpec((D_MODEL, IN_COLS_PAD), lambda b, t: (0, 0))],
        out_specs=[pl.BlockSpec((1, S5_WIDTH // LANE, TILE, LANE), lambda b, t: (b, 0, t, 0)),
                   tspec(NA_WIDTH), tspec(NA_WIDTH), tspec(NA_WIDTH),
                   tspec(SSD_WIDTH), tspec(SSD_XBC), tspec(LANE)],
        out_shape=[jax.ShapeDtypeStruct((BATCH, S5_WIDTH // LANE, T, LANE), F32),
                   tok(NA_WIDTH, BF16), tok(NA_WIDTH, BF16), tok(NA_WIDTH, BF16),
                   tok(SSD_WIDTH, BF16), tok(SSD_XBC, BF16), tok(LANE, F32)],
        compiler_params=_cp("arbitrary", "arbitrary"),
        name="in_proj",
    )(x, mod, norm_w.reshape(1, D_MODEL), w_in_p)


S5_BLK = SUB
NB = T // S5_BLK
NB_CTX = CTX_LEN // S5_BLK
S5_NPAIR = S5_NGROUPS // 2
S5_PW = 2 * S5_BLK * S5_GROUP


def _s5_kernel(u_ref, pin_ref, pout_ref, w1_ref, w2_ref, w3_ref, mul_ref, y_ref, ub_ref, yb_ref, st_ref):
    n = S5_NSTATE
    half = LANE
    per_half = S5_NPAIR // 2
    toks = [[u_ref[0, h, pl.ds(j, NB, stride=S5_BLK), :].astype(BF16) for j in range(S5_BLK)]
            for h in range(2)]
    for pp in range(S5_NPAIR):
        h, q = divmod(pp, per_half)
        acc = None
        for j in range(S5_BLK):
            term = _dot(toks[h][j], pin_ref[q, j])
            acc = term if acc is None else acc + term
        ub_ref[:, pp * S5_PW:(pp + 1) * S5_PW] = acc.astype(BF16)

    for d in range(2):
        for pp in range(S5_NPAIR):
            s = _dot(ub_ref[:, pp * S5_PW:(pp + 1) * S5_PW], w1_ref[d, pp])
            st_ref[d, :, pp * half:(pp + 1) * half] = s[:, :half]
            st_ref[d, :, n + pp * half:n + (pp + 1) * half] = s[:, half:]

    ngrp = NB // SUB
    nctx = NB_CTX // SUB
    rowid = lax.broadcasted_iota(I32, (SUB, n), 0)
    for d in range(2):
        def body(j, carry, d=d):
            cr, ci = carry
            r = j if d == 0 else jnp.where(j < nctx, nctx - 1 - j, ngrp - 1 + nctx - j)
            row = pl.multiple_of(r * SUB, SUB)
            re = st_ref[d, pl.ds(row, SUB), 0:n]
            im = st_ref[d, pl.ds(row, SUB), n:2 * n]
            for kk, sh in enumerate((1, 2, 4)):
                mr = mul_ref[d, kk * SUB:(kk + 1) * SUB, 0:n]
                mi = mul_ref[d, kk * SUB:(kk + 1) * SUB, n:2 * n]
                s = sh if d == 0 else SUB - sh
                sr = pltpu.roll(re, s, 0)
                si = pltpu.roll(im, s, 0)
                re, im = re + (mr * sr - mi * si), im + (mr * si + mi * sr)
            pr = mul_ref[d, 3 * SUB:4 * SUB, 0:n]
            pi = mul_ref[d, 3 * SUB:4 * SUB, n:2 * n]
            re, im = re + (pr * cr - pi * ci), im + (pr * ci + pi * cr)
            edge, last, sh = (0, SUB - 1, 1) if d == 0 else (SUB - 1, 0, SUB - 1)
            st_ref[d, pl.ds(row, SUB), 0:n] = jnp.where(rowid == edge, cr, pltpu.roll(re, sh, 0))
            st_ref[d, pl.ds(row, SUB), n:2 * n] = jnp.where(rowid == edge, ci, pltpu.roll(im, sh, 0))
            return re[last:last + 1, :], im[last:last + 1, :]

        zero = jnp.zeros((1, n), F32)
        lax.fori_loop(0, ngrp, body, (zero, zero), unroll=2)

    for pp in range(S5_NPAIR):
        up = ub_ref[:, pp * S5_PW:(pp + 1) * S5_PW]
        acc = None
        for d in range(2):
            enter = jnp.concatenate([st_ref[d, :, pp * half:(pp + 1) * half],
                                     st_ref[d, :, n + pp * half:n + (pp + 1) * half]], axis=1).astype(BF16)
            term = _dot(up, w2_ref[d, pp]) + _dot(enter, w3_ref[d, pp])
            acc = term if acc is None else acc + term
        yb_ref[:, pp * S5_PW:(pp + 1) * S5_PW] = acc.astype(BF16)

    for i in range(S5_BLK):
        for h in range(2):
            acc = None
            for q in range(per_half):
                pp = h * per_half + q
                term = _dot(yb_ref[:, pp * S5_PW:(pp + 1) * S5_PW], pout_ref[q, i])
                acc = term if acc is None else acc + term
            y_ref[0, h, pl.ds(i, NB, stride=S5_BLK), :] = acc


def _s5_regroup_matrices():
    per_half = S5_NPAIR // 2
    pin = np.zeros((per_half, S5_BLK, LANE, S5_PW), np.float32)
    for q in range(per_half):
        for j in range(S5_BLK):
            for gg in range(2):
                for c in range(S5_GROUP):
                    pin[q, j, (2 * q + gg) * S5_GROUP + c, gg * S5_BLK * S5_GROUP + j * S5_GROUP + c] = 1.0
    return jnp.asarray(pin, BF16), jnp.asarray(pin.transpose(0, 1, 3, 2), BF16)


def s5_mix(u, w1, w2, w3, mul):
    pin, pout = _s5_regroup_matrices()
    per_half = S5_NPAIR // 2
    wspec = pl.BlockSpec((2, S5_NPAIR, S5_PW, S5_PW), lambda b: (0, 0, 0, 0))
    tok = pl.BlockSpec((1, S5_WIDTH // LANE, T, LANE), lambda b: (b, 0, 0, 0))
    return pl.pallas_call(
        _s5_kernel,
        grid=(BATCH,),
        in_specs=[tok, pl.BlockSpec((per_half, S5_BLK, LANE, S5_PW), lambda b: (0, 0, 0, 0)),
                  pl.BlockSpec((per_half, S5_BLK, S5_PW, LANE), lambda b: (0, 0, 0, 0)), wspec, wspec, wspec,
                  pl.BlockSpec((2, 4 * SUB, 2 * S5_NSTATE), lambda b: (0, 0, 0))],
        out_specs=tok,
        out_shape=jax.ShapeDtypeStruct((BATCH, S5_WIDTH // LANE, T, LANE), F32),
        scratch_shapes=[pltpu.VMEM((NB, S5_NGROUPS * S5_BLK * S5_GROUP), BF16),
                        pltpu.VMEM((NB, S5_NGROUPS * S5_BLK * S5_GROUP), BF16),
                        pltpu.VMEM((2, NB, 2 * S5_NSTATE), F32)],
        compiler_params=_cp("arbitrary"),
        name="s5_mix",
    )(u, pin, pout, w1, w2, w3, mul)


def s5_params(lam_re, lam_im, log_dt, b_re, b_im, c_re, c_im):
    G, P, C = S5_NGROUPS, S5_STATE, S5_GROUP
    lam = lax.complex(lam_re.astype(F32), lam_im.astype(F32))
    step = jnp.exp(log_dt.astype(F32))[..., None]
    log_lb = lam * step
    lam_bar = jnp.exp(log_lb)
    b_bar = ((lam_bar - 1.0) / lam)[..., None] * lax.complex(b_re.astype(F32), b_im.astype(F32))
    c_mat = lax.complex(c_re.astype(F32), c_im.astype(F32))
    J = S5_BLK
    jj = jnp.arange(J)
    hi = lax.Precision.HIGHEST

    def powers(expo):
        e = expo.astype(F32).reshape((2, 1, 1) + expo.shape[1:])
        return jnp.exp(log_lb.reshape((2, G, P) + (1,) * (expo.ndim - 1)) * e)

    def pair_blockdiag(m):
        r, s = m.shape[2:]
        m = m.reshape(2, G // 2, 2, r, s)
        return jnp.einsum('dqgrs,gh->dqgrhs', m, jnp.eye(2, dtype=F32)).reshape(2, G // 2, 2 * r, 2 * s)

    w1c = jnp.einsum('dgpj,dgpc->dgjcp', powers(jnp.stack([J - 1 - jj, jj])), b_bar,
                     precision=hi).reshape(2, G, J * C, P)
    w1 = jnp.concatenate([pair_blockdiag(jnp.real(w1c)), pair_blockdiag(jnp.imag(w1c))], axis=-1)
    lam_l = powers(jnp.stack([jj, jj]))
    c_lam = c_mat[:, :, None, :, :] * jnp.moveaxis(lam_l, 3, 2)[:, :, :, None, :]
    by_lag = (jnp.einsum('dglcp,dgpk->dglck', jnp.real(c_lam), jnp.real(b_bar), precision=hi)
              - jnp.einsum('dglcp,dgpk->dglck', jnp.imag(c_lam), jnp.imag(b_bar), precision=hi))
    lag = np.arange(J)[None, :] - np.arange(J)[:, None]
    w2 = jnp.stack([jnp.where((sgn * lag >= 0)[None, :, :, None, None],
                              by_lag[d][:, np.abs(lag)], 0.0) for d, sgn in enumerate((1, -1))])
    w2 = pair_blockdiag(jnp.transpose(w2, (0, 1, 2, 5, 3, 4)).reshape(2, G, J * C, J * C))
    cl = jnp.einsum('dgcp,dgpi->dgpic', c_mat, powers(jnp.stack([jj + 1, J - jj])), precision=hi)
    cl = cl.reshape(2, G, P, J * C)
    w3 = jnp.concatenate([pair_blockdiag(jnp.real(cl)), pair_blockdiag(-jnp.imag(cl))], axis=2)
    rows = jnp.arange(SUB)
    pieces = []
    for d in range(2):
        log_blk = (log_lb[d] * float(J)).reshape(1, G * P)
        per_d = []
        for sh in (1, 2, 4):
            valid = (rows >= sh) if d == 0 else (rows < SUB - sh)
            per_d.append(jnp.where(valid[:, None], jnp.exp(log_blk * float(sh)), 0.0))
        expo = (rows + 1) if d == 0 else (SUB - rows)
        per_d.append(jnp.exp(log_blk * expo[:, None].astype(F32)))
        m = jnp.concatenate(per_d, axis=0)
        pieces.append(jnp.concatenate([jnp.real(m), jnp.imag(m)], axis=-1))
    mul = jnp.stack(pieces, axis=0).astype(F32)
    return w1.astype(BF16), w2.astype(BF16), w3.astype(BF16), mul


def _softmax_pv(parts):
    m = parts[0][0].max(axis=-1, keepdims=True)
    for s, _ in parts[1:]:
        m = jnp.maximum(m, s.max(axis=-1, keepdims=True))
    den = 0.0
    acc = 0.0
    for s, v in parts:
        p = jnp.exp(s - m)
        den = den + p.sum(axis=-1, keepdims=True)
        acc = acc + _dot(p.astype(BF16), v)
    return acc / den


def _na_kernel(q_ref, k_ref, v_ref, bias_ref, o_ref):
    t = pl.program_id(1)
    first = lax.broadcasted_iota(I32, (1, LANE), 1) < NA_HEAD_DIM

    def pair_attention(pp, start):
        ls = slice(pp * LANE, (pp + 1) * LANE)
        qp = q_ref[0, :, ls]
        kc = k_ref[0, 0:CTX_LEN, ls]
        vc = v_ref[0, 0:CTX_LEN, ls]
        outs = []
        for hh in range(2):
            qm = jnp.where(first if hh == 0 else ~first, qp, jnp.zeros_like(qp))
            parts = [(_dot_nt(qm, kc), vc)]
            if start is not None:
                kb = k_ref[0, pl.ds(start, NA_BAND * GRID_W), ls]
                vb = v_ref[0, pl.ds(start, NA_BAND * GRID_W), ls]
                parts.append((_dot_nt(qm, kb) + bias_ref[2 * pp + hh, 0], vb))
            outs.append(_softmax_pv(parts))
        o_ref[0, :, ls] = jnp.where(first, outs[0], outs[1]).astype(BF16)

    @pl.when(t == 0)
    def _():
        for pp in range(NA_HEADS // 2):
            pair_attention(pp, None)

    @pl.when(t > 0)
    def _():
        first_row = (t - 1) * (TILE // GRID_W)
        u0 = jnp.clip(first_row - NA_KH // 2, 0, ROWS - NA_BAND)
        start = pl.multiple_of(CTX_LEN + u0 * GRID_W, LANE)
        for pp in range(NA_HEADS // 2):
            pair_attention(pp, start)


def _na_cfg(t):
    return jnp.where(t <= 1, 0, jnp.where(t == NT - 1, 2, 1))


def natten(q, k, v, bias):
    whole = pl.BlockSpec((1, T, NA_WIDTH), lambda b, t: (b, 0, 0))
    tile = pl.BlockSpec((1, TILE, NA_WIDTH), lambda b, t: (b, t, 0))
    return pl.pallas_call(
        _na_kernel,
        grid=(BATCH, NT),
        in_specs=[tile, whole, whole,
                  pl.BlockSpec((NA_HEADS, 1, TILE, NA_BAND * GRID_W), lambda b, t: (0, _na_cfg(t), 0, 0))],
        out_specs=tile,
        out_shape=jax.ShapeDtypeStruct((BATCH, T, NA_WIDTH), BF16),
        compiler_params=_cp("arbitrary", "arbitrary"),
        name="natten",
    )(q, k, v, bias)


def natten_bias(rpb):
    col = jnp.arange(GRID_W)
    c0 = jnp.clip(col - NA_KW // 2, 0, GRID_W - NA_KW)
    in_win = (col[None, :] >= c0[:, None]) & (col[None, :] < c0[:, None] + NA_KW)
    rel_c = jnp.clip(col[None, :] - col[:, None] + (NA_KW - 1), 0, RPB_W - 1)
    pick_c = jax.nn.one_hot(rel_c, RPB_W, dtype=F32)
    blocks = jnp.einsum('hax,qkx->haqk', rpb.astype(F32), pick_c, precision=lax.Precision.HIGHEST)
    blocks = jnp.where(in_win[None, None], blocks, NEG)
    blocks = jnp.concatenate([blocks, jnp.full((NA_HEADS, 1, GRID_W, GRID_W), NEG, F32)], axis=1)
    blocks = jnp.concatenate([blocks, blocks], axis=-1)
    return pl.pallas_call(
        _bias_kernel,
        grid=(NA_HEADS, 3),
        in_specs=[pl.BlockSpec((1, 2 * NA_KH, GRID_W, LANE), lambda h, c: (h, 0, 0, 0))],
        out_specs=pl.BlockSpec((1, 1, TILE, NA_BAND * GRID_W), lambda h, c: (h, c, 0, 0)),
        out_shape=jax.ShapeDtypeStruct((NA_HEADS, 3, TILE, NA_BAND * GRID_W), F32),
        compiler_params=_cp("arbitrary", "arbitrary"),
        name="natten_bias",
    )(blocks)


def _bias_kernel(blk_ref, o_ref):
    rows_per_tile = TILE // GRID_W
    masked = 2 * NA_KH - 1
    for cfg, first_row in enumerate((0, 2 * rows_per_tile, ROWS - rows_per_tile)):
        @pl.when(pl.program_id(1) == cfg)
        def _(first_row=first_row):
            u0 = min(max(first_row - NA_KH // 2, 0), ROWS - NA_BAND)
            for rr in range(rows_per_tile):
                r = first_row + rr
                r0 = min(max(r - NA_KH // 2, 0), ROWS - NA_KH)
                for j in range(NA_BAND):
                    kr = u0 + j
                    a = kr - r + NA_KH - 1 if r0 <= kr < r0 + NA_KH else masked
                    half = (j % 2) * GRID_W
                    o_ref[0, 0, rr * GRID_W:(rr + 1) * GRID_W, j * GRID_W:(j + 1) * GRID_W] = (
                        blk_ref[0, a, :, half:half + GRID_W])


def _softplus(x):
    return jnp.maximum(x, 0.0) + jnp.log(1.0 + jnp.exp(-jnp.abs(x)))


def _ssd_prep_kernel(prev_ref, cur_ref, next_ref, dtr_ref, cw_ref, cb_ref, dtb_ref, ar_ref, cos_ref, sin_ref,
                     xs_ref, bm_ref, cm_ref, dt_ref, a_ref):
    t = pl.program_id(1)
    halo = prev_ref.shape[1]
    has_prev = t >= 2
    has_next = (t >= 1) & (t <= NT - 2)
    prev = jnp.where(has_prev, prev_ref[0].astype(F32), 0.0)
    nxt = jnp.where(has_next, next_ref[0].astype(F32), 0.0)
    ext = jnp.concatenate([prev, cur_ref[0].astype(F32), nxt], axis=0)
    n = ext.shape[0]
    acc = cb_ref[...] + cw_ref[SSD_CONV // 2:SSD_CONV // 2 + 1, :] * ext
    for kk in range(SSD_CONV):
        off = kk - SSD_CONV // 2
        if off != 0:
            acc = acc + cw_ref[kk:kk + 1, :] * pltpu.roll(ext, (-off) % n, 0)
    y = _silu(acc[halo:halo + TILE, :])
    xs_ref[0] = y[:, 0:SSD_WIDTH].astype(BF16)

    lane = lax.broadcasted_iota(I32, (1, LANE), 1)
    low = (lane & (NA_HEAD_DIM // 2)) == 0
    cos = cos_ref[...]
    sin = sin_ref[...]
    for g in range(2 * SSD_NGROUPS):
        v = y[:, SSD_WIDTH + g * LANE:SSD_WIDTH + (g + 1) * LANE]
        sw = jnp.where(low, pltpu.roll(v, LANE - 32, 1), pltpu.roll(v, 32, 1))
        rot = (v * cos + sw * sin).astype(BF16)
        if g < SSD_NGROUPS:
            bm_ref[0, :, g * LANE:(g + 1) * LANE] = rot
        else:
            cm_ref[0, :, (g - SSD_NGROUPS) * LANE:(g - SSD_NGROUPS + 1) * LANE] = rot

    dt = _softplus(dtr_ref[0] + dtb_ref[...])
    dt_ref[0] = dt
    a_ref[0] = dt * ar_ref[...]


def ssd_prep(xbc, dt_raw, conv_w, conv_b, dt_bias, a_log, cos_t, sin_t):
    halo = 16
    per = TILE // halo
    nhalo = T // halo
    tok = lambda w, dt: jax.ShapeDtypeStruct((BATCH, T, w), dt)
    tspec = lambda w: pl.BlockSpec((1, TILE, w), lambda b, t: (b, t, 0))
    row = lambda w: pl.BlockSpec((1, w), lambda b, t: (0, 0))
    cw = jnp.zeros((SUB, SSD_XBC), F32).at[:SSD_CONV].set(conv_w.astype(F32))
    pad12 = lambda v: jnp.zeros((1, LANE), F32).at[0, :2 * SSD_HEADS].set(v.astype(F32).reshape(-1))
    return pl.pallas_call(
        _ssd_prep_kernel,
        grid=(BATCH, NT),
        in_specs=[pl.BlockSpec((1, halo, SSD_XBC), lambda b, t: (b, jnp.maximum(t * per - 1, 0), 0)),
                  tspec(SSD_XBC),
                  pl.BlockSpec((1, halo, SSD_XBC), lambda b, t: (b, jnp.minimum((t + 1) * per, nhalo - 1), 0)),
                  tspec(LANE),
                  pl.BlockSpec((SUB, SSD_XBC), lambda b, t: (0, 0)), row(SSD_XBC), row(LANE), row(LANE),
                  pl.BlockSpec((TILE, LANE), lambda b, t: (t, 0)), pl.BlockSpec((TILE, LANE), lambda b, t: (t, 0))],
        out_specs=[tspec(SSD_WIDTH), tspec(SSD_BC), tspec(SSD_BC), tspec(LANE), tspec(LANE)],
        out_shape=[tok(SSD_WIDTH, BF16), tok(SSD_BC, BF16), tok(SSD_BC, BF16), tok(LANE, F32), tok(LANE, F32)],
        compiler_params=_cp("arbitrary", "arbitrary"),
        name="ssd_prep",
    )(xbc, xbc, xbc, dt_raw, cw, conv_b.astype(F32).reshape(1, SSD_XBC), pad12(dt_bias),
      pad12(-jnp.exp(a_log.astype(F32))), cos_t, sin_t)


def rope_tables():
    half = SSD_STATE // 2
    nf = half // 2
    pos = jnp.arange(SEQ)
    inv_freq = ROPE_BASE ** (-jnp.arange(nf, dtype=F32) / nf)
    lane = jnp.arange(LANE)
    p = jnp.where(lane[None, :] < half, (pos // GRID_W)[:, None], (pos % GRID_W)[:, None]).astype(F32)
    ang = p * inv_freq[lane % nf][None, :]
    sign = jnp.where((lane & nf) == 0, -1.0, 1.0)[None, :]
    cos_t = jnp.concatenate([jnp.ones((CTX_LEN, LANE), F32), jnp.cos(ang)], axis=0)
    sin_t = jnp.concatenate([jnp.zeros((CTX_LEN, LANE), F32), jnp.sin(ang) * sign], axis=0)
    return cos_t, sin_t


def _ssd_dir(d, xs_ref, bm_ref, cm_ref, bt_ref, dt_ref, a_ref, at_ref, tri_ref, y_ref, st_ref):
    q = TILE
    lane = lax.broadcasted_iota(I32, (1, LANE), 1)
    first = lane < SSD_HEAD_DIM
    ri = lax.broadcasted_iota(I32, (q, q), 0)
    ci = lax.broadcasted_iota(I32, (q, q), 1)
    keep = (ci <= ri) if d == 0 else (ci >= ri)
    end = q - 1 if d == 0 else 0
    tri_col = tri_ref[d]
    tri_row = tri_ref[1 - d]
    a = a_ref[0]
    dt = dt_ref[0]
    cs_col = _dot_exact_rhs(tri_col, a)
    cs_row = _dot_exact_lhs(at_ref[0], tri_row)
    g_mats = [_dot_nt(cm_ref[0, :, g * SSD_STATE:(g + 1) * SSD_STATE],
                      bm_ref[0, :, g * SSD_STATE:(g + 1) * SSD_STATE]) for g in range(SSD_NGROUPS)]

    def head_col(m, h):
        c = d * SSD_HEADS + h
        return m[:, c:c + 1]

    for pp in range(SSD_HEADS // 2):
        ls = slice(pp * LANE, (pp + 1) * LANE)
        h0, h1 = 2 * pp, 2 * pp + 1
        x = xs_ref[0, :, ls].astype(F32)
        dt_l = jnp.where(first, head_col(dt, h0), head_col(dt, h1))
        cs_l = jnp.where(first, head_col(cs_col, h0), head_col(cs_col, h1))
        cs_end = cs_l[end:end + 1, :]
        xdt = x * dt_l
        xdt_b = xdt.astype(BF16)
        xw = (xdt * jnp.exp(cs_end - cs_l)).astype(BF16)
        st = st_ref[d, pp]
        st_b = st.astype(BF16)
        ys, ups = [], []
        for h in (h0, h1):
            g = h // (SSD_HEADS // SSD_NGROUPS)
            c = d * SSD_HEADS + h
            diff = head_col(cs_col, h) - cs_row[c:c + 1, :]
            decay = jnp.where(keep, jnp.exp(jnp.where(keep, diff, 0.0)), 0.0)
            m = (g_mats[g] * decay).astype(BF16)
            y_h = _dot(m, xdt_b) + _dot(cm_ref[0, :, g * SSD_STATE:(g + 1) * SSD_STATE], st_b) * jnp.exp(cs_l)
            ys.append(y_h)
            ups.append(_dot(bt_ref[0, g * SSD_STATE:(g + 1) * SSD_STATE, :], xw))
        y_ref[0, :, ls] = jnp.where(first, ys[0], ys[1])
        st_ref[d, pp] = jnp.exp(cs_end) * st + jnp.where(first, ups[0], ups[1])


def _ssd_scan_kernel(xs_f, bm_f, cm_f, bt_f, dt_f, a_f, at_f, xs_b, bm_b, cm_b, bt_b, dt_b, a_b, at_b, tri_ref,
                     yf_ref, yb_ref, st_ref):
    @pl.when(pl.program_id(1) == 0)
    def _():
        st_ref[...] = jnp.zeros_like(st_ref)

    _ssd_dir(0, xs_f, bm_f, cm_f, bt_f, dt_f, a_f, at_f, tri_ref, yf_ref, st_ref)
    _ssd_dir(1, xs_b, bm_b, cm_b, bt_b, dt_b, a_b, at_b, tri_ref, yb_ref, st_ref)


def ssd_scan(xs, bm, cm, dt, a):
    bt = jnp.swapaxes(bm, 1, 2)
    at = jnp.swapaxes(a[:, :, :2 * SUB], 1, 2)
    idx = jnp.arange(TILE)
    tri = jnp.stack([idx[None, :] <= idx[:, None], idx[None, :] >= idx[:, None]]).astype(BF16)
    fwd = lambda b, i: (b, i, 0)
    bwd = lambda b, i: (b, _bwd_tile(i), 0)
    fwd_t = lambda b, i: (b, 0, i)
    bwd_t = lambda b, i: (b, 0, _bwd_tile(i))

    def specs(f, ft):
        return [pl.BlockSpec((1, TILE, SSD_WIDTH), f), pl.BlockSpec((1, TILE, SSD_BC), f),
                pl.BlockSpec((1, TILE, SSD_BC), f), pl.BlockSpec((1, SSD_BC, TILE), ft),
                pl.BlockSpec((1, TILE, LANE), f), pl.BlockSpec((1, TILE, LANE), f),
                pl.BlockSpec((1, 2 * SUB, TILE), ft)]

    args = (xs, bm, cm, bt, dt, a, at)
    return pl.pallas_call(
        _ssd_scan_kernel,
        grid=(BATCH, NT),
        in_specs=specs(fwd, fwd_t) + specs(bwd, bwd_t) + [pl.BlockSpec((2, TILE, TILE), lambda b, i: (0, 0, 0))],
        out_specs=[pl.BlockSpec((1, TILE, SSD_WIDTH), fwd), pl.BlockSpec((1, TILE, SSD_WIDTH), bwd)],
        out_shape=[jax.ShapeDtypeStruct((BATCH, T, SSD_WIDTH), F32)] * 2,
        scratch_shapes=[pltpu.VMEM((2, SSD_HEADS // 2, SSD_STATE, LANE), F32)],
        compiler_params=_cp("arbitrary", "arbitrary"),
        name="ssd_scan",
    )(*args, *args, tri)


def _gelu_tanh(x):
    return 0.5 * x * (1.0 + jnp.tanh(math.sqrt(2.0 / math.pi) * (x + 0.044715 * (x * x * x))))


def _post_kernel(x_ref, mod_ref, u_ref, s5y_ref, na_ref, xs_ref, z_ref, sdf_ref, sdb_ref,
                 s5d_ref, gw_ref, gb_ref, sdd_ref, snw_ref, wo_ref, n2w_ref, rt_ref,
                 x1_ref, h_ref, lg_ref):
    slabs = lambda r: jnp.concatenate([r[0, h] for h in range(S5_WIDTH // LANE)], axis=-1)
    ys5 = slabs(u_ref) * s5d_ref[...] + slabs(s5y_ref)
    g = _gelu_tanh(ys5)
    s5o = g * jax.nn.sigmoid(_dot(g.astype(BF16), gw_ref[...]) + gb_ref[...])
    yssd = (xs_ref[0].astype(F32) * sdd_ref[...] + sdf_ref[0] + sdb_ref[0]) * _silu(z_ref[0].astype(F32))
    ssdo = yssd * lax.rsqrt(jnp.mean(yssd * yssd, axis=-1, keepdims=True) + EPS) * snw_ref[...]
    mix = jnp.concatenate([s5o.astype(BF16), na_ref[0], ssdo.astype(BF16)], axis=-1)
    x1 = x_ref[0] + mod_ref[0, 0, 2:3, :] * _dot(mix, wo_ref[...])
    x1_ref[0] = x1
    h = _modulated_norm(x1, n2w_ref[...], mod_ref[0, 0, 3:4, :], mod_ref[0, 0, 4:5, :])
    h_ref[0] = h.astype(BF16)
    lg_ref[0] = _dot_x3(h, rt_ref[...])


def post_mixer(x, mod, u, s5y, na, xs, z, sdf, sdb, s5_d, glu_w, glu_b, ssd_d, ssd_norm_w, w_out, norm2_w,
               router):
    tspec = lambda w: pl.BlockSpec((1, TILE, w), lambda b, t: (b, t, 0))
    whole = lambda *shp: pl.BlockSpec(shp, lambda b, t: (0,) * len(shp))
    rt = jnp.zeros((D_MODEL, LANE), F32).at[:, :N_EXPERTS].set(router.astype(F32))
    slab = pl.BlockSpec((1, S5_WIDTH // LANE, TILE, LANE), lambda b, t: (b, 0, t, 0))
    return pl.pallas_call(
        _post_kernel,
        grid=(BATCH, NT),
        in_specs=[tspec(D_MODEL), pl.BlockSpec((1, 1, 6, D_MODEL), lambda b, t: (b, _seg(t), 0, 0)),
                  slab, slab, tspec(NA_WIDTH),
                  tspec(SSD_WIDTH), tspec(SSD_WIDTH), tspec(SSD_WIDTH), tspec(SSD_WIDTH),
                  whole(1, S5_WIDTH), whole(S5_WIDTH, S5_WIDTH), whole(1, S5_WIDTH),
                  whole(1, SSD_WIDTH), whole(1, SSD_WIDTH), whole(D_MODEL, D_MODEL), whole(1, D_MODEL),
                  whole(D_MODEL, LANE)],
        out_specs=[tspec(D_MODEL), tspec(D_MODEL), tspec(LANE)],
        out_shape=[jax.ShapeDtypeStruct((BATCH, T, D_MODEL), F32), jax.ShapeDtypeStruct((BATCH, T, D_MODEL), BF16),
                   jax.ShapeDtypeStruct((BATCH, T, LANE), F32)],
        compiler_params=_cp("arbitrary", "arbitrary"),
        name="post_mixer",
    )(x, mod, u, s5y, na, xs, z, sdf, sdb,
      s5_d.astype(F32).reshape(1, S5_WIDTH), glu_w.astype(BF16), glu_b.astype(F32).reshape(1, S5_WIDTH),
      jnp.repeat(ssd_d.astype(F32), SSD_HEAD_DIM).reshape(1, SSD_WIDTH), ssd_norm_w.astype(F32).reshape(1, SSD_WIDTH),
      w_out.astype(BF16), norm2_w.astype(F32).reshape(1, D_MODEL), rt)


def _route_kernel(lg_ref, tri_ref, slot_ref, aff_ref, *, with_ctx):
    lg = lg_ref[0]
    m = lg.max(axis=0, keepdims=True)
    e = jnp.exp(lg - m)
    aff = e / e.sum(axis=0, keepdims=True)
    aff_ref[0] = aff
    bits = pltpu.bitcast(aff, I32)
    is_ctx = lax.broadcasted_iota(I32, (N_EXPERTS, T), 1) < CTX_LEN

    def count(mask):
        return jnp.where(mask, 1.0, 0.0).sum(axis=1, keepdims=True)

    def kth_largest(seg, k):
        def body(i, prefix):
            cand = prefix | lax.shift_left(jnp.int32(1), 30 - i)
            return jnp.where(count((bits >= cand) & seg) >= k, cand, prefix)
        return lax.fori_loop(0, 31, body, jnp.zeros((N_EXPERTS, 1), I32))

    def excl_cumsum(x01):
        carry = jnp.zeros((N_EXPERTS, 1), F32)
        pieces = []
        for j in range(T // LANE):
            blk = x01[:, j * LANE:(j + 1) * LANE]
            inc = _dot(blk.astype(BF16), tri_ref[...])
            pieces.append(inc - blk + carry)
            carry = carry + inc[:, LANE - 1:LANE]
        return jnp.concatenate(pieces, axis=1)

    thr = kth_largest(~is_ctx, float(CAP_LAT))
    k_of = jnp.full((N_EXPERTS, T), float(CAP_LAT), F32)
    if with_ctx:
        thr = jnp.where(is_ctx, kth_largest(is_ctx, float(CAP_CTX)), thr)
        k_of = jnp.where(is_ctx, float(CAP_CTX), k_of)
    gt = bits > thr
    eq = bits == thr
    if not with_ctx:
        gt = gt & ~is_ctx
        eq = eq & ~is_ctx
    n_gt = jnp.where(is_ctx, count(gt & is_ctx), count(gt & ~is_ctx))
    tie_rank = excl_cumsum(jnp.where(eq, 1.0, 0.0))
    tie_rank = tie_rank - jnp.where(is_ctx, 0.0, count(eq & is_ctx))
    sel = gt | (eq & (tie_rank < k_of - n_gt))
    pos = excl_cumsum(jnp.where(sel, 1.0, 0.0))
    slot = jnp.where(is_ctx, pos + float(CAP_LAT), pos - count(sel & is_ctx))
    slot_ref[0] = jnp.where(sel, slot, -1.0).astype(I32)


def route(logits_t, with_ctx):
    idx = jnp.arange(LANE)
    tri = (idx[:, None] <= idx[None, :]).astype(BF16)
    spec = pl.BlockSpec((1, N_EXPERTS, T), lambda b: (b, 0, 0))
    return pl.pallas_call(
        functools.partial(_route_kernel, with_ctx=with_ctx),
        grid=(BATCH,),
        in_specs=[spec, pl.BlockSpec((LANE, LANE), lambda b: (0, 0))],
        out_specs=[spec, spec],
        out_shape=[jax.ShapeDtypeStruct((BATCH, N_EXPERTS, T), I32), jax.ShapeDtypeStruct((BATCH, N_EXPERTS, T), F32)],
        compiler_params=_cp("arbitrary"),
        name="route",
    )(logits_t, tri)


GATHER_WIN = LANE // 2
COMBINE_WIN = LANE // 2


def slot_ranges(slot):
    s = slot.reshape(BATCH, N_EXPERTS, NT, TILE)
    has = s >= 0
    smax = jnp.max(jnp.where(has, s, -1), axis=-1)
    smin = jnp.where(smax >= 0, jnp.min(jnp.where(has, s, CAP_LAT + CAP_CTX), axis=-1), 0)
    return smin.reshape(-1).astype(I32), smax.reshape(-1).astype(I32)


def _gather_kernel(smin_ref, smax_ref, h_ref, slot_ref, xs_ref, *, nslot):
    b = pl.program_id(0)
    t = pl.program_id(1)
    group = N_EXPERTS
    sid = lax.broadcasted_iota(I32, (GATHER_WIN, TILE), 0)
    align = 2 * SUB

    def onehot(e, ws, lo):
        srow = slot_ref[0, e:e + 1, :]
        return jnp.where((sid + ws == srow) & (srow >= lo), 1.0, 0.0).astype(BF16)

    def add_rows(e, ws, rows):
        win = pl.ds(pl.multiple_of(ws, align), GATHER_WIN)
        xs_ref[0, e, win, :] = (xs_ref[0, e, win, :].astype(F32) + rows).astype(BF16)

    @pl.when(t == 0)
    def _():
        for e in range(N_EXPERTS):
            xs_ref[0, e, 0:CAP_LAT, :] = jnp.zeros((CAP_LAT, D_MODEL), BF16)
            if nslot > CAP_LAT:
                cid = lax.broadcasted_iota(I32, (nslot - CAP_LAT, TILE), 0) + CAP_LAT
                pick = jnp.where(cid == slot_ref[0, e:e + 1, :], 1.0, 0.0).astype(BF16)
                xs_ref[0, e, CAP_LAT:nslot, :] = _dot(pick, h_ref[0]).astype(BF16)

    @pl.when(t > 0)
    def _():
        los, wss, extras = [], [], []
        for e in range(N_EXPERTS):
            base = (b * N_EXPERTS + e) * NT + t
            lo = smin_ref[base] & ~(align - 1)
            los.append(lo)
            wss.append(jnp.minimum(lo, CAP_LAT - GATHER_WIN))
            extras.append(lax.shift_right_arithmetic(smax_ref[base] - lo, GATHER_WIN.bit_length() - 1))
        for g0 in range(0, N_EXPERTS, group):
            pick = jnp.concatenate([onehot(e, wss[e], los[e]) for e in range(g0, g0 + group)], axis=0)
            rows = _dot(pick, h_ref[0])
            for i, e in enumerate(range(g0, g0 + group)):
                add_rows(e, wss[e], rows[i * GATHER_WIN:(i + 1) * GATHER_WIN, :])
        most = extras[0]
        for x in extras[1:]:
            most = jnp.maximum(most, x)

        @pl.when(most > 0)
        def _():
            for e in range(N_EXPERTS):
                def more(k, carry, e=e):
                    lo_k = los[e] + k * GATHER_WIN
                    ws = jnp.minimum(lo_k, CAP_LAT - GATHER_WIN)
                    add_rows(e, ws, _dot(onehot(e, ws, lo_k), h_ref[0]))
                    return carry

                lax.fori_loop(1, extras[e] + 1, more, 0)


def moe_gather(h, slot, smin, smax, nslot):
    return pl.pallas_call(
        functools.partial(_gather_kernel, nslot=nslot),
        grid_spec=pltpu.PrefetchScalarGridSpec(
            num_scalar_prefetch=2,
            grid=(BATCH, NT),
            in_specs=[pl.BlockSpec((1, TILE, D_MODEL), lambda b, t, *_: (b, t, 0)),
                      pl.BlockSpec((1, N_EXPERTS, TILE), lambda b, t, *_: (b, 0, t))],
            out_specs=pl.BlockSpec((1, N_EXPERTS, nslot, D_MODEL), lambda b, t, *_: (b, 0, 0, 0))),
        out_shape=jax.ShapeDtypeStruct((BATCH, N_EXPERTS, nslot, D_MODEL), BF16),
        compiler_params=_cp("arbitrary", "arbitrary"),
        name="moe_gather",
    )(smin, smax, h, slot)


def _ffn_kernel(xs_ref, wg_ref, wu_ref, wd_ref, y_ref, w_ref):
    @pl.when(pl.program_id(1) == 0)
    def _():
        w_ref[0] = wg_ref[0, 0].astype(BF16)
        w_ref[1] = wu_ref[0, 0].astype(BF16)
        w_ref[2] = wd_ref[0, 0].astype(BF16)

    xs = xs_ref[0, 0]
    hid = _silu(_dot(xs, w_ref[0])) * _dot(xs, w_ref[1])
    y_ref[0, 0] = _dot(hid.astype(BF16), w_ref[2]).astype(BF16)


def moe_ffn(xs, layer, wg, wu, wd, nslot):
    wspec = lambda: pl.BlockSpec((1, 1, D_MODEL, D_EXPERT), lambda e, b: (layer, e, 0, 0))
    rows = pl.BlockSpec((1, 1, nslot, D_MODEL), lambda e, b: (b, e, 0, 0))
    return pl.pallas_call(
        _ffn_kernel,
        grid=(N_EXPERTS, BATCH),
        in_specs=[rows, wspec(), wspec(), wspec()],
        out_specs=rows,
        out_shape=jax.ShapeDtypeStruct((BATCH, N_EXPERTS, nslot, D_MODEL), BF16),
        scratch_shapes=[pltpu.VMEM((3, D_MODEL, D_EXPERT), BF16)],
        compiler_params=_cp("arbitrary", "arbitrary"),
        name="moe_ffn",
    )(xs, wg, wu, wd)


def _combine_kernel(smin_ref, smax_ref, x_ref, mod_ref, slot_ref, aff_ref, y_ref, fw_ref, o_ref, acc_ref, *,
                    nslot, last):
    b = pl.program_id(0)
    t = pl.program_id(1)
    shift = COMBINE_WIN.bit_length() - 1
    align = 2 * SUB

    def weights(e, lane, lo=None):
        s = slot_ref[0, :, e:e + 1]
        hit = (s == lane) if lo is None else ((s == lane) & (s >= lo))
        return jnp.where(hit, aff_ref[0, :, e:e + 1], 0.0).astype(BF16)

    def finish(acc):
        x2 = x_ref[0] + mod_ref[0, 0, 5:6, :] * acc
        if last:
            x2 = x2 * lax.rsqrt(jnp.mean(x2 * x2, axis=-1, keepdims=True) + EPS) * fw_ref[...]
        o_ref[0] = x2

    if not last:
        @pl.when(t == 0)
        def _():
            lane = lax.broadcasted_iota(I32, (TILE, nslot - CAP_LAT), 1) + CAP_LAT
            acc = jnp.zeros((TILE, D_MODEL), F32)
            for e in range(N_EXPERTS):
                acc = acc + _dot(weights(e, lane), y_ref[0, e, CAP_LAT:nslot, :])
            finish(acc)

    @pl.when(t > 0)
    def _():
        lane = lax.broadcasted_iota(I32, (TILE, COMBINE_WIN), 1)
        lane2 = lax.broadcasted_iota(I32, (TILE, 2 * COMBINE_WIN), 1)
        first = lane2 < COMBINE_WIN
        los, wss, extras = [], [], []
        for e in range(N_EXPERTS):
            base = (b * N_EXPERTS + e) * NT + t
            lo = smin_ref[base] & ~(align - 1)
            los.append(lo)
            wss.append(pl.multiple_of(jnp.minimum(lo, CAP_LAT - COMBINE_WIN), align))
            extras.append(lax.shift_right_arithmetic(smax_ref[base] - lo, shift))
        w_parts, y_parts = [], []
        for e in range(0, N_EXPERTS, 2):
            want = jnp.where(first, slot_ref[0, :, e:e + 1] - wss[e],
                             slot_ref[0, :, e + 1:e + 2] - wss[e + 1] + COMBINE_WIN)
            gate = jnp.where(first, aff_ref[0, :, e:e + 1], aff_ref[0, :, e + 1:e + 2])
            w_parts.append(jnp.where(want == lane2, gate, 0.0).astype(BF16))
            y_parts += [y_ref[0, e, pl.ds(wss[e], COMBINE_WIN), :], y_ref[0, e + 1, pl.ds(wss[e + 1], COMBINE_WIN), :]]
        acc_ref[...] = _dot(jnp.concatenate(w_parts, axis=1), jnp.concatenate(y_parts, axis=0))
        most = extras[0]
        for x in extras[1:]:
            most = jnp.maximum(most, x)

        @pl.when(most > 0)
        def _():
            for e in range(N_EXPERTS):
                def more(k, carry, e=e):
                    lo_k = los[e] + k * COMBINE_WIN
                    ws = pl.multiple_of(jnp.minimum(lo_k, CAP_LAT - COMBINE_WIN), align)
                    acc_ref[...] += _dot(weights(e, lane + ws, lo_k), y_ref[0, e, pl.ds(ws, COMBINE_WIN), :])
                    return carry

                lax.fori_loop(1, extras[e] + 1, more, 0)

        finish(acc_ref[...])


def moe_combine(x1, mod, slot_tok, aff_tok, y, smin, smax, nslot, final_w):
    last = final_w is not None
    tspec = lambda w: pl.BlockSpec((1, TILE, w), lambda b, t, *_: (b, t, 0))
    if last:
        first_lat = CTX_LEN // TILE
        out_spec = pl.BlockSpec((1, TILE, D_MODEL), lambda b, t, *_: (b, jnp.maximum(t - first_lat, 0), 0))
        out_shape = jax.ShapeDtypeStruct((BATCH, SEQ, D_MODEL), F32)
        fw = final_w.astype(F32).reshape(1, D_MODEL)
    else:
        out_spec = tspec(D_MODEL)
        out_shape = jax.ShapeDtypeStruct((BATCH, T, D_MODEL), F32)
        fw = jnp.ones((1, D_MODEL), F32)
    return pl.pallas_call(
        functools.partial(_combine_kernel, nslot=nslot, last=last),
        grid_spec=pltpu.PrefetchScalarGridSpec(
            num_scalar_prefetch=2,
            grid=(BATCH, NT),
            in_specs=[tspec(D_MODEL),
                      pl.BlockSpec((1, 1, 6, D_MODEL), lambda b, t, *_: (b, _seg(t), 0, 0)),
                      tspec(N_EXPERTS), tspec(N_EXPERTS),
                      pl.BlockSpec((1, N_EXPERTS, nslot, D_MODEL), lambda b, t, *_: (b, 0, 0, 0)),
                      pl.BlockSpec((1, D_MODEL), lambda b, t, *_: (0, 0))],
            out_specs=out_spec,
            scratch_shapes=[pltpu.VMEM((TILE, D_MODEL), F32)]),
        out_shape=out_shape,
        compiler_params=_cp("arbitrary", "arbitrary"),
        name="moe_combine",
    )(smin, smax, x1, mod, slot_tok, aff_tok, y, fw)


def trunk_layer(l, x, mod, cos_t, sin_t, norm1_w, norm2_w, w_in, w_out,
                s5_lam_re, s5_lam_im, s5_log_dt, s5_b_re, s5_b_im, s5_c_re, s5_c_im, s5_d, s5_glu_w, s5_glu_b,
                na_rpb, ssd_conv_w, ssd_conv_b, ssd_dt_bias, ssd_a_log, ssd_d, ssd_norm_w,
                moe_router, wg, wu, wd, final_w):
    with_ctx_out = final_w is None
    w_in_p = jnp.zeros((D_MODEL, IN_COLS_PAD), BF16).at[:, :IN_COLS].set(w_in.astype(BF16))
    u, q, k, v, z, xbc, dt_raw = in_proj(x, mod, norm1_w, w_in_p)

    s5y = s5_mix(u, *s5_params(s5_lam_re, s5_lam_im, s5_log_dt, s5_b_re, s5_b_im, s5_c_re, s5_c_im))
    na = natten(q, k, v, natten_bias(na_rpb))
    xs, bm, cm, dt, a = ssd_prep(xbc, dt_raw, ssd_conv_w, ssd_conv_b, ssd_dt_bias, ssd_a_log, cos_t, sin_t)
    sdf, sdb = ssd_scan(xs, bm, cm, dt, a)

    x1, h, logits = post_mixer(x, mod, u, s5y, na, xs, z, sdf, sdb, s5_d, s5_glu_w, s5_glu_b,
                               ssd_d, ssd_norm_w, w_out, norm2_w, moe_router)
    slot, aff = route(jnp.swapaxes(logits[:, :, :N_EXPERTS], 1, 2), with_ctx_out)
    nslot = CAP_LAT + CAP_CTX if with_ctx_out else CAP_LAT
    smin, smax = slot_ranges(slot)
    y = moe_ffn(moe_gather(h, slot, smin, smax, nslot), l, wg, wu, wd, nslot)
    return moe_combine(x1, mod, jnp.swapaxes(slot, 1, 2), jnp.swapaxes(aff, 1, 2), y, smin, smax, nslot, final_w)


def kernel(x, c, ctx, c_ctx, w_ada, b_ada, norm1_w, norm2_w, w_in, w_out, s5_lam_re, s5_lam_im, s5_log_dt, s5_b_re, s5_b_im, s5_c_re, s5_c_im, s5_d, s5_glu_w, s5_glu_b, na_rpb, ssd_conv_w, ssd_conv_b, ssd_dt_bias, ssd_a_log, ssd_d, ssd_norm_w, moe_router, moe_w_gate, moe_w_up, moe_w_down, final_norm_w):
    xa = jnp.concatenate([ctx, x], axis=1).astype(F32)
    cvec = jnp.zeros((SUB, D_MODEL), F32).at[0].set(c_ctx.astype(F32)).at[1:1 + BATCH].set(c.astype(F32))
    mods = ada_mod(cvec, w_ada.astype(F32), b_ada.astype(F32)).reshape(DEPTH, SUB, 6, D_MODEL)
    cos_t, sin_t = rope_tables()
    wg, wu, wd = moe_w_gate.astype(F32), moe_w_up.astype(F32), moe_w_down.astype(F32)
    for l in range(DEPTH):
        mod = jnp.stack([jnp.broadcast_to(mods[l, 0], (BATCH, 6, D_MODEL)), mods[l, 1:1 + BATCH]], axis=1)
        xa = trunk_layer(
            l, xa, mod, cos_t, sin_t, norm1_w[l], norm2_w[l], w_in[l], w_out[l],
            s5_lam_re[l], s5_lam_im[l], s5_log_dt[l], s5_b_re[l], s5_b_im[l], s5_c_re[l], s5_c_im[l],
            s5_d[l], s5_glu_w[l], s5_glu_b[l],
            na_rpb[l], ssd_conv_w[l], ssd_conv_b[l], ssd_dt_bias[l], ssd_a_log[l], ssd_d[l], ssd_norm_w[l],
            moe_router[l], wg, wu, wd, final_norm_w if l == DEPTH - 1 else None)
    return xa
```

```python
import functools
import math

import jax
import jax.numpy as jnp
import numpy as np
from jax import lax
from jax.experimental import pallas as pl
from jax.experimental.pallas import tpu as pltpu

F32 = jnp.float32
BF16 = jnp.bfloat16
I32 = jnp.int32

D_MODEL = 1024
BATCH = 4
SEQ = 4096
DEPTH = 2
GRID_W = 64
CTX_LEN = 256
EPS = 1e-6

S5_WIDTH = 256
S5_GROUP = 16
S5_NGROUPS = 16
S5_STATE = 64
S5_NSTATE = S5_NGROUPS * S5_STATE

NA_HEADS = 6
NA_HEAD_DIM = 64
NA_WIDTH = 384
NA_KH = 8
NA_KW = 16
NA_BAND = 12
RPB_W = 2 * NA_KW - 1

SSD_HEADS = 6
SSD_HEAD_DIM = 64
SSD_WIDTH = 384
SSD_NGROUPS = 2
SSD_STATE = 128
SSD_CONV = 5
SSD_BC = 256
SSD_XBC = 896

N_EXPERTS = 16
D_EXPERT = 1024
ROPE_BASE = 10000.0

T = CTX_LEN + SEQ
TILE = 256
NT = T // TILE
LANE = 128
SUB = 8
ROWS = SEQ // GRID_W
CAP_LAT = 2 * SEQ // N_EXPERTS
CAP_CTX = 2 * CTX_LEN // N_EXPERTS
NEG = -1e30

C_U = 0
C_Q = 256
C_K = 640
C_V = 1024
C_Z = 1408
C_XBC = 1792
C_DT = 2688
IN_COLS = 2700
IN_COLS_PAD = 2816

VMEM_LIMIT = 56 * 1024 * 1024


def _cp(*sem):
    return pltpu.CompilerParams(dimension_semantics=sem, vmem_limit_bytes=VMEM_LIMIT)


def _dot(a, b):
    return jnp.dot(a, b, preferred_element_type=F32)


def _dot_nt(a, b):
    return lax.dot_general(a, b, (((1,), (1,)), ((), ())), preferred_element_type=F32)


def _split3(x):
    hi = x.astype(BF16)
    r = x - hi.astype(F32)
    mid = r.astype(BF16)
    lo = (r - mid.astype(F32)).astype(BF16)
    return hi, mid, lo


def _dot_exact_rhs(a_bf16, b_f32):
    hi, mid, lo = _split3(b_f32)
    return _dot(a_bf16, hi) + _dot(a_bf16, mid) + _dot(a_bf16, lo)


def _dot_exact_lhs(a_f32, b_bf16):
    hi, mid, lo = _split3(a_f32)
    return _dot(hi, b_bf16) + _dot(mid, b_bf16) + _dot(lo, b_bf16)


def _dot_x3(a, b):
    ah = a.astype(BF16)
    al = (a - ah.astype(F32)).astype(BF16)
    bh = b.astype(BF16)
    bl = (b - bh.astype(F32)).astype(BF16)
    return _dot(ah, bh) + _dot(ah, bl) + _dot(al, bh)


def _silu(x):
    return x * jax.nn.sigmoid(x)


def _seg(t):
    return jnp.where(t >= CTX_LEN // TILE, 1, 0)


def _bwd_tile(i):
    return jnp.where(i == 0, 0, NT - i)


def _ada_kernel(c_ref, w_ref, b_ref, o_ref):
    s = _silu(c_ref[...])
    o_ref[0] = _dot_x3(s, w_ref[0]) + b_ref[0]


def ada_mod(cvec, w_ada, b_ada):
    nb = 1024
    return pl.pallas_call(
        _ada_kernel,
        grid=(DEPTH, 6 * D_MODEL // nb),
        in_specs=[pl.BlockSpec((SUB, D_MODEL), lambda l, j: (0, 0)),
                  pl.BlockSpec((1, D_MODEL, nb), lambda l, j: (l, 0, j)),
                  pl.BlockSpec((1, 1, nb), lambda l, j: (l, 0, j))],
        out_specs=pl.BlockSpec((1, SUB, nb), lambda l, j: (l, 0, j)),
        out_shape=jax.ShapeDtypeStruct((DEPTH, SUB, 6 * D_MODEL), F32),
        compiler_params=_cp("arbitrary", "arbitrary"),
        name="ada_mod",
    )(cvec, w_ada, b_ada.reshape(DEPTH, 1, 6 * D_MODEL))


def _modulated_norm(x, nw, shift, scale):
    y = x * lax.rsqrt(jnp.mean(x * x, axis=-1, keepdims=True) + EPS) * nw
    return y * (1.0 + scale) + shift


def _stream_specs(stream):
    _, _, off = stream
    return [pl.BlockSpec((1, TILE, D_MODEL), lambda b, t: (b, 0, 0)),
            pl.BlockSpec((1, TILE, D_MODEL), lambda b, t: (b, jnp.maximum(t - off, 0), 0))]


def _stream_tile(head_ref, tail_ref):
    return jnp.where(pl.program_id(1) == 0, head_ref[0], tail_ref[0])


def _inproj_kernel(xh_ref, xt_ref, mod_ref, nw_ref, w_ref, u_ref, q_ref, k_ref, v_ref, z_ref, xbc_ref, dt_ref):
    x = _stream_tile(xh_ref, xt_ref)
    h = _modulated_norm(x, nw_ref[...], mod_ref[0, 0, 0:1, :], mod_ref[0, 0, 1:2, :]).astype(BF16)

    def proj(lo, hi):
        return _dot(h, w_ref[0, :, lo:hi])

    for s in range(S5_WIDTH // LANE):
        u_ref[0, s] = proj(C_U + s * LANE, C_U + (s + 1) * LANE)
    q_ref[0] = (proj(C_Q, C_K) * (NA_HEAD_DIM ** -0.5)).astype(BF16)
    k_ref[0] = proj(C_K, C_V).astype(BF16)
    v_ref[0] = proj(C_V, C_Z).astype(BF16)
    z_ref[0] = proj(C_Z, C_XBC).astype(BF16)
    xbc_ref[0] = proj(C_XBC, C_DT).astype(BF16)
    dt_ref[0] = proj(C_DT, IN_COLS_PAD)


def in_proj(stream, mod, norm_w, w_in_p, layer):
    tok = lambda w, dt: jax.ShapeDtypeStruct((BATCH, T, w), dt)
    tspec = lambda w: pl.BlockSpec((1, TILE, w), lambda b, t: (b, t, 0))
    return pl.pallas_call(
        _inproj_kernel,
        grid=(BATCH, NT),
        in_specs=_stream_specs(stream) + [
                  pl.BlockSpec((1, 1, 6, D_MODEL), lambda b, t: (b, _seg(t), 0, 0)),
                  pl.BlockSpec((1, D_MODEL), lambda b, t: (0, 0)),
                  pl.BlockSpec((1, D_MODEL, IN_COLS_PAD), lambda b, t: (layer, 0, 0))],
        out_specs=[pl.BlockSpec((1, S5_WIDTH // LANE, TILE, LANE), lambda b, t: (b, 0, t, 0)),
                   tspec(NA_WIDTH), tspec(NA_WIDTH), tspec(NA_WIDTH),
                   tspec(SSD_WIDTH), tspec(SSD_XBC), tspec(LANE)],
        out_shape=[jax.ShapeDtypeStruct((BATCH, S5_WIDTH // LANE, T, LANE), F32),
                   tok(NA_WIDTH, BF16), tok(NA_WIDTH, BF16), tok(NA_WIDTH, BF16),
                   tok(SSD_WIDTH, BF16), tok(SSD_XBC, BF16), tok(LANE, F32)],
        compiler_params=_cp("arbitrary", "arbitrary"),
        name="in_proj",
    )(stream[0], stream[1], mod, norm_w.reshape(1, D_MODEL), w_in_p)


S5_BLK = SUB
NB = T // S5_BLK
NB_CTX = CTX_LEN // S5_BLK
S5_NPAIR = S5_NGROUPS // 2
S5_PW = 2 * S5_BLK * S5_GROUP


def _s5_kernel(u_ref, pin_ref, pout_ref, w1_ref, w2_ref, w3_ref, mul_ref, y_ref, ub_ref, yb_ref, st_ref):
    n = S5_NSTATE
    half = LANE
    per_half = S5_NPAIR // 2
    toks = [[u_ref[0, h, pl.ds(j, NB, stride=S5_BLK), :].astype(BF16) for j in range(S5_BLK)]
            for h in range(2)]
    for pp in range(S5_NPAIR):
        h, q = divmod(pp, per_half)
        acc = None
        for j in range(S5_BLK):
            term = _dot(toks[h][j], pin_ref[q, j])
            acc = term if acc is None else acc + term
        ub_ref[:, pp * S5_PW:(pp + 1) * S5_PW] = acc.astype(BF16)

    for d in range(2):
        for pp in range(S5_NPAIR):
            s = _dot(ub_ref[:, pp * S5_PW:(pp + 1) * S5_PW], w1_ref[d, pp])
            st_ref[d, :, pp * half:(pp + 1) * half] = s[:, :half]
            st_ref[d, :, n + pp * half:n + (pp + 1) * half] = s[:, half:]

    ngrp = NB // SUB
    nctx = NB_CTX // SUB
    rowid = lax.broadcasted_iota(I32, (SUB, n), 0)
    for d in range(2):
        def body(j, carry, d=d):
            cr, ci = carry
            r = j if d == 0 else jnp.where(j < nctx, nctx - 1 - j, ngrp - 1 + nctx - j)
            row = pl.multiple_of(r * SUB, SUB)
            re = st_ref[d, pl.ds(row, SUB), 0:n]
            im = st_ref[d, pl.ds(row, SUB), n:2 * n]
            for kk, sh in enumerate((1, 2, 4)):
                mr = mul_ref[d, kk * SUB:(kk + 1) * SUB, 0:n]
                mi = mul_ref[d, kk * SUB:(kk + 1) * SUB, n:2 * n]
                s = sh if d == 0 else SUB - sh
                sr = pltpu.roll(re, s, 0)
                si = pltpu.roll(im, s, 0)
                re, im = re + (mr * sr - mi * si), im + (mr * si + mi * sr)
            pr = mul_ref[d, 3 * SUB:4 * SUB, 0:n]
            pi = mul_ref[d, 3 * SUB:4 * SUB, n:2 * n]
            re, im = re + (pr * cr - pi * ci), im + (pr * ci + pi * cr)
            edge, last, sh = (0, SUB - 1, 1) if d == 0 else (SUB - 1, 0, SUB - 1)
            st_ref[d, pl.ds(row, SUB), 0:n] = jnp.where(rowid == edge, cr, pltpu.roll(re, sh, 0))
            st_ref[d, pl.ds(row, SUB), n:2 * n] = jnp.where(rowid == edge, ci, pltpu.roll(im, sh, 0))
            return re[last:last + 1, :], im[last:last + 1, :]

        zero = jnp.zeros((1, n), F32)
        lax.fori_loop(0, ngrp, body, (zero, zero), unroll=2)

    for pp in range(S5_NPAIR):
        up = ub_ref[:, pp * S5_PW:(pp + 1) * S5_PW]
        acc = None
        for d in range(2):
            enter = jnp.concatenate([st_ref[d, :, pp * half:(pp + 1) * half],
                                     st_ref[d, :, n + pp * half:n + (pp + 1) * half]], axis=1).astype(BF16)
            term = _dot(up, w2_ref[d, pp]) + _dot(enter, w3_ref[d, pp])
            acc = term if acc is None else acc + term
        yb_ref[:, pp * S5_PW:(pp + 1) * S5_PW] = acc.astype(BF16)

    for i in range(S5_BLK):
        for h in range(2):
            acc = None
            for q in range(per_half):
                pp = h * per_half + q
                term = _dot(yb_ref[:, pp * S5_PW:(pp + 1) * S5_PW], pout_ref[q, i])
                acc = term if acc is None else acc + term
            y_ref[0, h, pl.ds(i, NB, stride=S5_BLK), :] = acc


def _s5_regroup_matrices():
    per_half = S5_NPAIR // 2
    pin = np.zeros((per_half, S5_BLK, LANE, S5_PW), np.float32)
    for q in range(per_half):
        for j in range(S5_BLK):
            for gg in range(2):
                for c in range(S5_GROUP):
                    pin[q, j, (2 * q + gg) * S5_GROUP + c, gg * S5_BLK * S5_GROUP + j * S5_GROUP + c] = 1.0
    return jnp.asarray(pin, BF16), jnp.asarray(pin.transpose(0, 1, 3, 2), BF16)


def s5_mix(u, w1, w2, w3, mul):
    pin, pout = _s5_regroup_matrices()
    per_half = S5_NPAIR // 2
    wspec = pl.BlockSpec((2, S5_NPAIR, S5_PW, S5_PW), lambda b: (0, 0, 0, 0))
    tok = pl.BlockSpec((1, S5_WIDTH // LANE, T, LANE), lambda b: (b, 0, 0, 0))
    return pl.pallas_call(
        _s5_kernel,
        grid=(BATCH,),
        in_specs=[tok, pl.BlockSpec((per_half, S5_BLK, LANE, S5_PW), lambda b: (0, 0, 0, 0)),
                  pl.BlockSpec((per_half, S5_BLK, S5_PW, LANE), lambda b: (0, 0, 0, 0)), wspec, wspec, wspec,
                  pl.BlockSpec((2, 4 * SUB, 2 * S5_NSTATE), lambda b: (0, 0, 0))],
        out_specs=tok,
        out_shape=jax.ShapeDtypeStruct((BATCH, S5_WIDTH // LANE, T, LANE), F32),
        scratch_shapes=[pltpu.VMEM((NB, S5_NGROUPS * S5_BLK * S5_GROUP), BF16),
                        pltpu.VMEM((NB, S5_NGROUPS * S5_BLK * S5_GROUP), BF16),
                        pltpu.VMEM((2, NB, 2 * S5_NSTATE), F32)],
        compiler_params=_cp("arbitrary"),
        name="s5_mix",
    )(u, pin, pout, w1, w2, w3, mul)


def s5_params(lam_re, lam_im, log_dt, b_re, b_im, c_re, c_im):
    G, P, C = S5_NGROUPS, S5_STATE, S5_GROUP
    lam = lax.complex(lam_re.astype(F32), lam_im.astype(F32))
    step = jnp.exp(log_dt.astype(F32))[..., None]
    log_lb = lam * step
    lam_bar = jnp.exp(log_lb)
    b_bar = ((lam_bar - 1.0) / lam)[..., None] * lax.complex(b_re.astype(F32), b_im.astype(F32))
    c_mat = lax.complex(c_re.astype(F32), c_im.astype(F32))
    J = S5_BLK
    jj = jnp.arange(J)
    hi = lax.Precision.HIGHEST

    def powers(expo):
        e = expo.astype(F32).reshape((2, 1, 1) + expo.shape[1:])
        return jnp.exp(log_lb.reshape((2, G, P) + (1,) * (expo.ndim - 1)) * e)

    def pair_blockdiag(m):
        r, s = m.shape[2:]
        m = m.reshape(2, G // 2, 2, r, s)
        return jnp.einsum('dqgrs,gh->dqgrhs', m, jnp.eye(2, dtype=F32)).reshape(2, G // 2, 2 * r, 2 * s)

    w1c = jnp.einsum('dgpj,dgpc->dgjcp', powers(jnp.stack([J - 1 - jj, jj])), b_bar,
                     precision=hi).reshape(2, G, J * C, P)
    w1 = jnp.concatenate([pair_blockdiag(jnp.real(w1c)), pair_blockdiag(jnp.imag(w1c))], axis=-1)
    lam_l = powers(jnp.stack([jj, jj]))
    c_lam = c_mat[:, :, None, :, :] * jnp.moveaxis(lam_l, 3, 2)[:, :, :, None, :]
    by_lag = (jnp.einsum('dglcp,dgpk->dglck', jnp.real(c_lam), jnp.real(b_bar), precision=hi)
              - jnp.einsum('dglcp,dgpk->dglck', jnp.imag(c_lam), jnp.imag(b_bar), precision=hi))
    lag = np.arange(J)[None, :] - np.arange(J)[:, None]
    w2 = jnp.stack([jnp.where((sgn * lag >= 0)[None, :, :, None, None],
                              by_lag[d][:, np.abs(lag)], 0.0) for d, sgn in enumerate((1, -1))])
    w2 = pair_blockdiag(jnp.transpose(w2, (0, 1, 2, 5, 3, 4)).reshape(2, G, J * C, J * C))
    cl = jnp.einsum('dgcp,dgpi->dgpic', c_mat, powers(jnp.stack([jj + 1, J - jj])), precision=hi)
    cl = cl.reshape(2, G, P, J * C)
    w3 = jnp.concatenate([pair_blockdiag(jnp.real(cl)), pair_blockdiag(-jnp.imag(cl))], axis=2)
    rows = jnp.arange(SUB)
    pieces = []
    for d in range(2):
        log_blk = (log_lb[d] * float(J)).reshape(1, G * P)
        per_d = []
        for sh in (1, 2, 4):
            valid = (rows >= sh) if d == 0 else (rows < SUB - sh)
            per_d.append(jnp.where(valid[:, None], jnp.exp(log_blk * float(sh)), 0.0))
        expo = (rows + 1) if d == 0 else (SUB - rows)
        per_d.append(jnp.exp(log_blk * expo[:, None].astype(F32)))
        m = jnp.concatenate(per_d, axis=0)
        pieces.append(jnp.concatenate([jnp.real(m), jnp.imag(m)], axis=-1))
    mul = jnp.stack(pieces, axis=0).astype(F32)
    return w1.astype(BF16), w2.astype(BF16), w3.astype(BF16), mul


def _softmax_pv(parts):
    m = parts[0][0].max(axis=-1, keepdims=True)
    for s, _ in parts[1:]:
        m = jnp.maximum(m, s.max(axis=-1, keepdims=True))
    den = 0.0
    acc = 0.0
    for s, v in parts:
        p = jnp.exp(s - m)
        den = den + p.sum(axis=-1, keepdims=True)
        acc = acc + _dot(p.astype(BF16), v)
    return acc / den


def _na_kernel(q_ref, k_ref, v_ref, bias_ref, o_ref):
    t = pl.program_id(1)
    first = lax.broadcasted_iota(I32, (1, LANE), 1) < NA_HEAD_DIM

    def pair_attention(pp, start):
        ls = slice(pp * LANE, (pp + 1) * LANE)
        qp = q_ref[0, :, ls]
        kc = k_ref[0, 0:CTX_LEN, ls]
        vc = v_ref[0, 0:CTX_LEN, ls]
        outs = []
        for hh in range(2):
            qm = jnp.where(first if hh == 0 else ~first, qp, jnp.zeros_like(qp))
            parts = [(_dot_nt(qm, kc), vc)]
            if start is not None:
                kb = k_ref[0, pl.ds(start, NA_BAND * GRID_W), ls]
                vb = v_ref[0, pl.ds(start, NA_BAND * GRID_W), ls]
                parts.append((_dot_nt(qm, kb) + bias_ref[2 * pp + hh, 0], vb))
            outs.append(_softmax_pv(parts))
        o_ref[0, :, ls] = jnp.where(first, outs[0], outs[1]).astype(BF16)

    @pl.when(t == 0)
    def _():
        for pp in range(NA_HEADS // 2):
            pair_attention(pp, None)

    @pl.when(t > 0)
    def _():
        first_row = (t - 1) * (TILE // GRID_W)
        u0 = jnp.clip(first_row - NA_KH // 2, 0, ROWS - NA_BAND)
        start = pl.multiple_of(CTX_LEN + u0 * GRID_W, LANE)
        for pp in range(NA_HEADS // 2):
            pair_attention(pp, start)


def _na_cfg(t):
    return jnp.where(t <= 1, 0, jnp.where(t == NT - 1, 2, 1))


def natten(q, k, v, bias, layer):
    whole = pl.BlockSpec((1, T, NA_WIDTH), lambda b, t: (b, 0, 0))
    tile = pl.BlockSpec((1, TILE, NA_WIDTH), lambda b, t: (b, t, 0))
    return pl.pallas_call(
        _na_kernel,
        grid=(BATCH, NT),
        in_specs=[tile, whole, whole,
                  pl.BlockSpec((NA_HEADS, 1, TILE, NA_BAND * GRID_W), lambda b, t: (layer, _na_cfg(t), 0, 0))],
        out_specs=tile,
        out_shape=jax.ShapeDtypeStruct((BATCH, T, NA_WIDTH), BF16),
        compiler_params=_cp("arbitrary", "arbitrary"),
        name="natten",
    )(q, k, v, bias)


def natten_bias(rpb):
    col = jnp.arange(GRID_W)
    c0 = jnp.clip(col - NA_KW // 2, 0, GRID_W - NA_KW)
    in_win = (col[None, :] >= c0[:, None]) & (col[None, :] < c0[:, None] + NA_KW)
    rel_c = jnp.clip(col[None, :] - col[:, None] + (NA_KW - 1), 0, RPB_W - 1)
    pick_c = jax.nn.one_hot(rel_c, RPB_W, dtype=F32)
    blocks = jnp.einsum('hax,qkx->haqk', rpb.astype(F32), pick_c, precision=lax.Precision.HIGHEST)
    blocks = jnp.where(in_win[None, None], blocks, NEG)
    nh = rpb.shape[0]
    blocks = jnp.concatenate([blocks, jnp.full((nh, 1, GRID_W, GRID_W), NEG, F32)], axis=1)
    blocks = jnp.concatenate([blocks, blocks], axis=-1)
    return pl.pallas_call(
        _bias_kernel,
        grid=(nh, 3),
        in_specs=[pl.BlockSpec((1, 2 * NA_KH, GRID_W, LANE), lambda h, c: (h, 0, 0, 0))],
        out_specs=pl.BlockSpec((1, 1, TILE, NA_BAND * GRID_W), lambda h, c: (h, c, 0, 0)),
        out_shape=jax.ShapeDtypeStruct((nh, 3, TILE, NA_BAND * GRID_W), F32),
        compiler_params=_cp("arbitrary", "arbitrary"),
        name="natten_bias",
    )(blocks)


def _bias_kernel(blk_ref, o_ref):
    rows_per_tile = TILE // GRID_W
    masked = 2 * NA_KH - 1
    for cfg, first_row in enumerate((0, 2 * rows_per_tile, ROWS - rows_per_tile)):
        @pl.when(pl.program_id(1) == cfg)
        def _(first_row=first_row):
            u0 = min(max(first_row - NA_KH // 2, 0), ROWS - NA_BAND)
            for rr in range(rows_per_tile):
                r = first_row + rr
                r0 = min(max(r - NA_KH // 2, 0), ROWS - NA_KH)
                for j in range(NA_BAND):
                    kr = u0 + j
                    a = kr - r + NA_KH - 1 if r0 <= kr < r0 + NA_KH else masked
                    half = (j % 2) * GRID_W
                    o_ref[0, 0, rr * GRID_W:(rr + 1) * GRID_W, j * GRID_W:(j + 1) * GRID_W] = (
                        blk_ref[0, a, :, half:half + GRID_W])


def _softplus(x):
    return jnp.maximum(x, 0.0) + jnp.log(1.0 + jnp.exp(-jnp.abs(x)))


def _ssd_prep_kernel(prev_ref, cur_ref, next_ref, dtr_ref, cw_ref, cb_ref, dtb_ref, ar_ref, cos_ref, sin_ref,
                     xs_ref, bm_ref, cm_ref, dt_ref, a_ref):
    t = pl.program_id(1)
    halo = prev_ref.shape[1]
    has_prev = t >= 2
    has_next = (t >= 1) & (t <= NT - 2)
    prev = jnp.where(has_prev, prev_ref[0].astype(F32), 0.0)
    nxt = jnp.where(has_next, next_ref[0].astype(F32), 0.0)
    ext = jnp.concatenate([prev, cur_ref[0].astype(F32), nxt], axis=0)
    n = ext.shape[0]
    acc = cb_ref[...] + cw_ref[SSD_CONV // 2:SSD_CONV // 2 + 1, :] * ext
    for kk in range(SSD_CONV):
        off = kk - SSD_CONV // 2
        if off != 0:
            acc = acc + cw_ref[kk:kk + 1, :] * pltpu.roll(ext, (-off) % n, 0)
    y = _silu(acc[halo:halo + TILE, :])
    xs_ref[0] = y[:, 0:SSD_WIDTH].astype(BF16)

    lane = lax.broadcasted_iota(I32, (1, LANE), 1)
    low = (lane & (NA_HEAD_DIM // 2)) == 0
    cos = cos_ref[...]
    sin = sin_ref[...]
    for g in range(2 * SSD_NGROUPS):
        v = y[:, SSD_WIDTH + g * LANE:SSD_WIDTH + (g + 1) * LANE]
        sw = jnp.where(low, pltpu.roll(v, LANE - 32, 1), pltpu.roll(v, 32, 1))
        rot = (v * cos + sw * sin).astype(BF16)
        if g < SSD_NGROUPS:
            bm_ref[0, :, g * LANE:(g + 1) * LANE] = rot
        else:
            cm_ref[0, :, (g - SSD_NGROUPS) * LANE:(g - SSD_NGROUPS + 1) * LANE] = rot

    dt = _softplus(dtr_ref[0] + dtb_ref[...])
    dt_ref[0] = dt
    a_ref[0] = dt * ar_ref[...]


def ssd_prep(xbc, dt_raw, conv_w, conv_b, dt_bias, a_log, cos_t, sin_t):
    halo = 16
    per = TILE // halo
    nhalo = T // halo
    tok = lambda w, dt: jax.ShapeDtypeStruct((BATCH, T, w), dt)
    tspec = lambda w: pl.BlockSpec((1, TILE, w), lambda b, t: (b, t, 0))
    row = lambda w: pl.BlockSpec((1, w), lambda b, t: (0, 0))
    cw = jnp.zeros((SUB, SSD_XBC), F32).at[:SSD_CONV].set(conv_w.astype(F32))
    pad12 = lambda v: jnp.zeros((1, LANE), F32).at[0, :2 * SSD_HEADS].set(v.astype(F32).reshape(-1))
    return pl.pallas_call(
        _ssd_prep_kernel,
        grid=(BATCH, NT),
        in_specs=[pl.BlockSpec((1, halo, SSD_XBC), lambda b, t: (b, jnp.maximum(t * per - 1, 0), 0)),
                  tspec(SSD_XBC),
                  pl.BlockSpec((1, halo, SSD_XBC), lambda b, t: (b, jnp.minimum((t + 1) * per, nhalo - 1), 0)),
                  tspec(LANE),
                  pl.BlockSpec((SUB, SSD_XBC), lambda b, t: (0, 0)), row(SSD_XBC), row(LANE), row(LANE),
                  pl.BlockSpec((TILE, LANE), lambda b, t: (t, 0)), pl.BlockSpec((TILE, LANE), lambda b, t: (t, 0))],
        out_specs=[tspec(SSD_WIDTH), tspec(SSD_BC), tspec(SSD_BC), tspec(LANE), tspec(LANE)],
        out_shape=[tok(SSD_WIDTH, BF16), tok(SSD_BC, BF16), tok(SSD_BC, BF16), tok(LANE, F32), tok(LANE, F32)],
        compiler_params=_cp("arbitrary", "arbitrary"),
        name="ssd_prep",
    )(xbc, xbc, xbc, dt_raw, cw, conv_b.astype(F32).reshape(1, SSD_XBC), pad12(dt_bias),
      pad12(-jnp.exp(a_log.astype(F32))), cos_t, sin_t)


def rope_tables():
    half = SSD_STATE // 2
    nf = half // 2
    pos = jnp.arange(SEQ)
    inv_freq = ROPE_BASE ** (-jnp.arange(nf, dtype=F32) / nf)
    lane = jnp.arange(LANE)
    p = jnp.where(lane[None, :] < half, (pos // GRID_W)[:, None], (pos % GRID_W)[:, None]).astype(F32)
    ang = p * inv_freq[lane % nf][None, :]
    sign = jnp.where((lane & nf) == 0, -1.0, 1.0)[None, :]
    cos_t = jnp.concatenate([jnp.ones((CTX_LEN, LANE), F32), jnp.cos(ang)], axis=0)
    sin_t = jnp.concatenate([jnp.zeros((CTX_LEN, LANE), F32), jnp.sin(ang) * sign], axis=0)
    return cos_t, sin_t


def _ssd_dir(d, xs_ref, bm_ref, cm_ref, bt_ref, dt_ref, a_ref, at_ref, tri_ref, y_ref, st_ref):
    q = TILE
    lane = lax.broadcasted_iota(I32, (1, LANE), 1)
    first = lane < SSD_HEAD_DIM
    ri = lax.broadcasted_iota(I32, (q, q), 0)
    ci = lax.broadcasted_iota(I32, (q, q), 1)
    keep = (ci <= ri) if d == 0 else (ci >= ri)
    end = q - 1 if d == 0 else 0
    tri_col = tri_ref[d]
    tri_row = tri_ref[1 - d]
    a = a_ref[0]
    dt = dt_ref[0]
    cs_col = _dot_exact_rhs(tri_col, a)
    cs_row = _dot_exact_lhs(at_ref[0], tri_row)
    g_mats = [_dot_nt(cm_ref[0, :, g * SSD_STATE:(g + 1) * SSD_STATE],
                      bm_ref[0, :, g * SSD_STATE:(g + 1) * SSD_STATE]) for g in range(SSD_NGROUPS)]

    def head_col(m, h):
        c = d * SSD_HEADS + h
        return m[:, c:c + 1]

    for pp in range(SSD_HEADS // 2):
        ls = slice(pp * LANE, (pp + 1) * LANE)
        h0, h1 = 2 * pp, 2 * pp + 1
        x = xs_ref[0, :, ls].astype(F32)
        dt_l = jnp.where(first, head_col(dt, h0), head_col(dt, h1))
        cs_l = jnp.where(first, head_col(cs_col, h0), head_col(cs_col, h1))
        cs_end = cs_l[end:end + 1, :]
        xdt = x * dt_l
        xdt_b = xdt.astype(BF16)
        xw = (xdt * jnp.exp(cs_end - cs_l)).astype(BF16)
        st = st_ref[d, pp]
        st_b = st.astype(BF16)
        ys, ups = [], []
        for h in (h0, h1):
            g = h // (SSD_HEADS // SSD_NGROUPS)
            c = d * SSD_HEADS + h
            diff = head_col(cs_col, h) - cs_row[c:c + 1, :]
            decay = jnp.exp(jnp.where(keep, diff, NEG))
            m = (g_mats[g] * decay).astype(BF16)
            y_h = _dot(m, xdt_b) + _dot(cm_ref[0, :, g * SSD_STATE:(g + 1) * SSD_STATE], st_b) * jnp.exp(cs_l)
            ys.append(y_h)
            ups.append(_dot(bt_ref[0, g * SSD_STATE:(g + 1) * SSD_STATE, :], xw))
        y_ref[0, :, ls] = jnp.where(first, ys[0], ys[1])
        st_ref[d, pp] = jnp.exp(cs_end) * st + jnp.where(first, ups[0], ups[1])


def _ssd_scan_kernel(xs_f, bm_f, cm_f, bt_f, dt_f, a_f, at_f, xs_b, bm_b, cm_b, bt_b, dt_b, a_b, at_b, tri_ref,
                     yf_ref, yb_ref, st_ref):
    @pl.when(pl.program_id(1) == 0)
    def _():
        st_ref[...] = jnp.zeros_like(st_ref)

    _ssd_dir(0, xs_f, bm_f, cm_f, bt_f, dt_f, a_f, at_f, tri_ref, yf_ref, st_ref)
    _ssd_dir(1, xs_b, bm_b, cm_b, bt_b, dt_b, a_b, at_b, tri_ref, yb_ref, st_ref)


def ssd_scan(xs, bm, cm, dt, a):
    bt = jnp.swapaxes(bm, 1, 2)
    at = jnp.swapaxes(a[:, :, :2 * SUB], 1, 2)
    idx = jnp.arange(TILE)
    tri = jnp.stack([idx[None, :] <= idx[:, None], idx[None, :] >= idx[:, None]]).astype(BF16)
    fwd = lambda b, i: (b, i, 0)
    bwd = lambda b, i: (b, _bwd_tile(i), 0)
    fwd_t = lambda b, i: (b, 0, i)
    bwd_t = lambda b, i: (b, 0, _bwd_tile(i))

    def specs(f, ft):
        return [pl.BlockSpec((1, TILE, SSD_WIDTH), f), pl.BlockSpec((1, TILE, SSD_BC), f),
                pl.BlockSpec((1, TILE, SSD_BC), f), pl.BlockSpec((1, SSD_BC, TILE), ft),
                pl.BlockSpec((1, TILE, LANE), f), pl.BlockSpec((1, TILE, LANE), f),
                pl.BlockSpec((1, 2 * SUB, TILE), ft)]

    args = (xs, bm, cm, bt, dt, a, at)
    return pl.pallas_call(
        _ssd_scan_kernel,
        grid=(BATCH, NT),
        in_specs=specs(fwd, fwd_t) + specs(bwd, bwd_t) + [pl.BlockSpec((2, TILE, TILE), lambda b, i: (0, 0, 0))],
        out_specs=[pl.BlockSpec((1, TILE, SSD_WIDTH), fwd), pl.BlockSpec((1, TILE, SSD_WIDTH), bwd)],
        out_shape=[jax.ShapeDtypeStruct((BATCH, T, SSD_WIDTH), F32)] * 2,
        scratch_shapes=[pltpu.VMEM((2, SSD_HEADS // 2, SSD_STATE, LANE), F32)],
        compiler_params=_cp("arbitrary", "arbitrary"),
        name="ssd_scan",
    )(*args, *args, tri)


def _gelu_tanh(x):
    return 0.5 * x * (1.0 + jnp.tanh(math.sqrt(2.0 / math.pi) * (x + 0.044715 * (x * x * x))))


def _post_kernel(xh_ref, xt_ref, mod_ref, u_ref, s5y_ref, na_ref, xs_ref, z_ref, sdf_ref, sdb_ref,
                 s5d_ref, gw_ref, gb_ref, sdd_ref, snw_ref, wo_ref, n2w_ref, rt_ref,
                 x1_ref, h_ref, lg_ref):
    slabs = lambda r: jnp.concatenate([r[0, h] for h in range(S5_WIDTH // LANE)], axis=-1)
    ys5 = slabs(u_ref) * s5d_ref[...] + slabs(s5y_ref)
    g = _gelu_tanh(ys5)
    s5o = g * jax.nn.sigmoid(_dot(g.astype(BF16), gw_ref[...]) + gb_ref[...])
    yssd = (xs_ref[0].astype(F32) * sdd_ref[...] + sdf_ref[0] + sdb_ref[0]) * _silu(z_ref[0].astype(F32))
    ssdo = yssd * lax.rsqrt(jnp.mean(yssd * yssd, axis=-1, keepdims=True) + EPS) * snw_ref[...]
    mix = jnp.concatenate([s5o.astype(BF16), na_ref[0], ssdo.astype(BF16)], axis=-1)
    x1 = _stream_tile(xh_ref, xt_ref) + mod_ref[0, 0, 2:3, :] * _dot(mix, wo_ref[...])
    x1_ref[0] = x1
    h = _modulated_norm(x1, n2w_ref[...], mod_ref[0, 0, 3:4, :], mod_ref[0, 0, 4:5, :])
    h_ref[0] = h.astype(BF16)
    lg_ref[0] = _dot_x3(h, rt_ref[...])


def post_mixer(stream, mod, u, s5y, na, xs, z, sdf, sdb, s5_d, glu_w, glu_b, ssd_d, ssd_norm_w, w_out, norm2_w,
               router):
    tspec = lambda w: pl.BlockSpec((1, TILE, w), lambda b, t: (b, t, 0))
    whole = lambda *shp: pl.BlockSpec(shp, lambda b, t: (0,) * len(shp))
    rt = jnp.zeros((D_MODEL, LANE), F32).at[:, :N_EXPERTS].set(router.astype(F32))
    slab = pl.BlockSpec((1, S5_WIDTH // LANE, TILE, LANE), lambda b, t: (b, 0, t, 0))
    return pl.pallas_call(
        _post_kernel,
        grid=(BATCH, NT),
        in_specs=_stream_specs(stream) + [
                  pl.BlockSpec((1, 1, 6, D_MODEL), lambda b, t: (b, _seg(t), 0, 0)),
                  slab, slab, tspec(NA_WIDTH),
                  tspec(SSD_WIDTH), tspec(SSD_WIDTH), tspec(SSD_WIDTH), tspec(SSD_WIDTH),
                  whole(1, S5_WIDTH), whole(S5_WIDTH, S5_WIDTH), whole(1, S5_WIDTH),
                  whole(1, SSD_WIDTH), whole(1, SSD_WIDTH), whole(D_MODEL, D_MODEL), whole(1, D_MODEL),
                  whole(D_MODEL, LANE)],
        out_specs=[tspec(D_MODEL), tspec(D_MODEL), tspec(LANE)],
        out_shape=[jax.ShapeDtypeStruct((BATCH, T, D_MODEL), F32), jax.ShapeDtypeStruct((BATCH, T, D_MODEL), BF16),
                   jax.ShapeDtypeStruct((BATCH, T, LANE), F32)],
        compiler_params=_cp("arbitrary", "arbitrary"),
        name="post_mixer",
    )(stream[0], stream[1], mod, u, s5y, na, xs, z, sdf, sdb,
      s5_d.astype(F32).reshape(1, S5_WIDTH), glu_w.astype(BF16), glu_b.astype(F32).reshape(1, S5_WIDTH),
      jnp.repeat(ssd_d.astype(F32), SSD_HEAD_DIM).reshape(1, SSD_WIDTH), ssd_norm_w.astype(F32).reshape(1, SSD_WIDTH),
      w_out.astype(BF16), norm2_w.astype(F32).reshape(1, D_MODEL), rt)


def _route_kernel(lg_ref, tri_ref, slot_ref, aff_ref, *, with_ctx):
    lg = lg_ref[0]
    m = lg.max(axis=0, keepdims=True)
    e = jnp.exp(lg - m)
    aff = e / e.sum(axis=0, keepdims=True)
    aff_ref[0] = aff
    bits = pltpu.bitcast(aff, I32)
    is_ctx = lax.broadcasted_iota(I32, (N_EXPERTS, T), 1) < CTX_LEN

    def count(mask):
        return jnp.where(mask, 1.0, 0.0).sum(axis=1, keepdims=True)

    def kth_largest(seg, k):
        def body(i, prefix):
            cand = prefix | lax.shift_left(jnp.int32(1), 30 - i)
            return jnp.where(count((bits >= cand) & seg) >= k, cand, prefix)
        return lax.fori_loop(0, 31, body, jnp.zeros((N_EXPERTS, 1), I32))

    def excl_cumsum(x01):
        carry = jnp.zeros((N_EXPERTS, 1), F32)
        pieces = []
        for j in range(T // LANE):
            blk = x01[:, j * LANE:(j + 1) * LANE]
            inc = _dot(blk.astype(BF16), tri_ref[...])
            pieces.append(inc - blk + carry)
            carry = carry + inc[:, LANE - 1:LANE]
        return jnp.concatenate(pieces, axis=1)

    thr = kth_largest(~is_ctx, float(CAP_LAT))
    k_of = jnp.full((N_EXPERTS, T), float(CAP_LAT), F32)
    if with_ctx:
        thr = jnp.where(is_ctx, kth_largest(is_ctx, float(CAP_CTX)), thr)
        k_of = jnp.where(is_ctx, float(CAP_CTX), k_of)
    gt = bits > thr
    eq = bits == thr
    if not with_ctx:
        gt = gt & ~is_ctx
        eq = eq & ~is_ctx
    n_gt = jnp.where(is_ctx, count(gt & is_ctx), count(gt & ~is_ctx))
    tie_rank = excl_cumsum(jnp.where(eq, 1.0, 0.0))
    tie_rank = tie_rank - jnp.where(is_ctx, 0.0, count(eq & is_ctx))
    sel = gt | (eq & (tie_rank < k_of - n_gt))
    pos = excl_cumsum(jnp.where(sel, 1.0, 0.0))
    slot = jnp.where(is_ctx, pos + float(CAP_LAT), pos - count(sel & is_ctx))
    slot_ref[0] = jnp.where(sel, slot, -1.0).astype(I32)


def route(logits_t, with_ctx):
    idx = jnp.arange(LANE)
    tri = (idx[:, None] <= idx[None, :]).astype(BF16)
    spec = pl.BlockSpec((1, N_EXPERTS, T), lambda b: (b, 0, 0))
    return pl.pallas_call(
        functools.partial(_route_kernel, with_ctx=with_ctx),
        grid=(BATCH,),
        in_specs=[spec, pl.BlockSpec((LANE, LANE), lambda b: (0, 0))],
        out_specs=[spec, spec],
        out_shape=[jax.ShapeDtypeStruct((BATCH, N_EXPERTS, T), I32), jax.ShapeDtypeStruct((BATCH, N_EXPERTS, T), F32)],
        compiler_params=_cp("arbitrary"),
        name="route",
    )(logits_t, tri)


GATHER_WIN = LANE // 2
COMBINE_WIN = LANE // 2


def slot_ranges(slot):
    s = slot.reshape(BATCH, N_EXPERTS, NT, TILE)
    has = s >= 0
    smax = jnp.max(jnp.where(has, s, -1), axis=-1)
    smin = jnp.where(smax >= 0, jnp.min(jnp.where(has, s, CAP_LAT + CAP_CTX), axis=-1), 0)
    return smin.reshape(-1).astype(I32), smax.reshape(-1).astype(I32)


def _gather_kernel(smin_ref, smax_ref, h_ref, slot_ref, xs_ref, *, nslot):
    b = pl.program_id(0)
    t = pl.program_id(1)
    group = N_EXPERTS
    sid = lax.broadcasted_iota(I32, (GATHER_WIN, TILE), 0)
    align = 2 * SUB

    def onehot(e, ws, lo):
        srow = slot_ref[0, e:e + 1, :]
        return jnp.where((sid + ws == srow) & (srow >= lo), 1.0, 0.0).astype(BF16)

    def add_rows(e, ws, rows):
        win = pl.ds(pl.multiple_of(ws, align), GATHER_WIN)
        xs_ref[0, e, win, :] = (xs_ref[0, e, win, :].astype(F32) + rows).astype(BF16)

    @pl.when(t == 0)
    def _():
        for e in range(N_EXPERTS):
            xs_ref[0, e, 0:CAP_LAT, :] = jnp.zeros((CAP_LAT, D_MODEL), BF16)
            if nslot > CAP_LAT:
                cid = lax.broadcasted_iota(I32, (nslot - CAP_LAT, TILE), 0) + CAP_LAT
                pick = jnp.where(cid == slot_ref[0, e:e + 1, :], 1.0, 0.0).astype(BF16)
                xs_ref[0, e, CAP_LAT:nslot, :] = _dot(pick, h_ref[0]).astype(BF16)

    @pl.when(t > 0)
    def _():
        los, wss, extras = [], [], []
        for e in range(N_EXPERTS):
            base = (b * N_EXPERTS + e) * NT + t
            lo = smin_ref[base] & ~(align - 1)
            los.append(lo)
            wss.append(jnp.minimum(lo, CAP_LAT - GATHER_WIN))
            extras.append(lax.shift_right_arithmetic(smax_ref[base] - lo, GATHER_WIN.bit_length() - 1))
        for g0 in range(0, N_EXPERTS, group):
            pick = jnp.concatenate([onehot(e, wss[e], los[e]) for e in range(g0, g0 + group)], axis=0)
            rows = _dot(pick, h_ref[0])
            for i, e in enumerate(range(g0, g0 + group)):
                add_rows(e, wss[e], rows[i * GATHER_WIN:(i + 1) * GATHER_WIN, :])
        most = extras[0]
        for x in extras[1:]:
            most = jnp.maximum(most, x)

        @pl.when(most > 0)
        def _():
            for e in range(N_EXPERTS):
                def more(k, carry, e=e):
                    lo_k = los[e] + k * GATHER_WIN
                    ws = jnp.minimum(lo_k, CAP_LAT - GATHER_WIN)
                    add_rows(e, ws, _dot(onehot(e, ws, lo_k), h_ref[0]))
                    return carry

                lax.fori_loop(1, extras[e] + 1, more, 0)


def moe_gather(h, slot, smin, smax, nslot):
    return pl.pallas_call(
        functools.partial(_gather_kernel, nslot=nslot),
        grid_spec=pltpu.PrefetchScalarGridSpec(
            num_scalar_prefetch=2,
            grid=(BATCH, NT),
            in_specs=[pl.BlockSpec((1, TILE, D_MODEL), lambda b, t, *_: (b, t, 0)),
                      pl.BlockSpec((1, N_EXPERTS, TILE), lambda b, t, *_: (b, 0, t))],
            out_specs=pl.BlockSpec((1, N_EXPERTS, nslot, D_MODEL), lambda b, t, *_: (b, 0, 0, 0))),
        out_shape=jax.ShapeDtypeStruct((BATCH, N_EXPERTS, nslot, D_MODEL), BF16),
        compiler_params=_cp("arbitrary", "arbitrary"),
        name="moe_gather",
    )(smin, smax, h, slot)


def _ffn_kernel(xs_ref, wg_ref, wu_ref, wd_ref, y_ref, w_ref):
    @pl.when(pl.program_id(1) == 0)
    def _():
        w_ref[0] = wg_ref[0, 0].astype(BF16)
        w_ref[1] = wu_ref[0, 0].astype(BF16)
        w_ref[2] = wd_ref[0, 0].astype(BF16)

    xs = xs_ref[0, 0]
    hid = _silu(_dot(xs, w_ref[0])) * _dot(xs, w_ref[1])
    y_ref[0, 0] = _dot(hid.astype(BF16), w_ref[2]).astype(BF16)


def moe_ffn(xs, layer, wg, wu, wd, nslot):
    wspec = lambda: pl.BlockSpec((1, 1, D_MODEL, D_EXPERT), lambda e, b: (layer, e, 0, 0))
    rows = pl.BlockSpec((1, 1, nslot, D_MODEL), lambda e, b: (b, e, 0, 0))
    return pl.pallas_call(
        _ffn_kernel,
        grid=(N_EXPERTS, BATCH),
        in_specs=[rows, wspec(), wspec(), wspec()],
        out_specs=rows,
        out_shape=jax.ShapeDtypeStruct((BATCH, N_EXPERTS, nslot, D_MODEL), BF16),
        scratch_shapes=[pltpu.VMEM((3, D_MODEL, D_EXPERT), BF16)],
        compiler_params=_cp("arbitrary", "arbitrary"),
        name="moe_ffn",
    )(xs, wg, wu, wd)


def _combine_kernel(smin_ref, smax_ref, x_ref, mod_ref, slot_ref, aff_ref, y_ref, fw_ref, o_ref, acc_ref, *,
                    nslot, last):
    b = pl.program_id(0)
    t = pl.program_id(1)
    shift = COMBINE_WIN.bit_length() - 1
    align = 2 * SUB

    def weights(e, lane, lo=None):
        s = slot_ref[0, :, e:e + 1]
        hit = (s == lane) if lo is None else ((s == lane) & (s >= lo))
        return jnp.where(hit, aff_ref[0, :, e:e + 1], 0.0).astype(BF16)

    def finish(acc):
        x2 = x_ref[0] + mod_ref[0, 0, 5:6, :] * acc
        if last:
            x2 = x2 * lax.rsqrt(jnp.mean(x2 * x2, axis=-1, keepdims=True) + EPS) * fw_ref[...]
        o_ref[0] = x2

    if not last:
        @pl.when(t == 0)
        def _():
            lane = lax.broadcasted_iota(I32, (TILE, nslot - CAP_LAT), 1) + CAP_LAT
            acc = jnp.zeros((TILE, D_MODEL), F32)
            for e in range(N_EXPERTS):
                acc = acc + _dot(weights(e, lane), y_ref[0, e, CAP_LAT:nslot, :])
            finish(acc)

    @pl.when(t > 0)
    def _():
        lane = lax.broadcasted_iota(I32, (TILE, COMBINE_WIN), 1)
        lane2 = lax.broadcasted_iota(I32, (TILE, 2 * COMBINE_WIN), 1)
        first = lane2 < COMBINE_WIN
        los, wss, extras = [], [], []
        for e in range(N_EXPERTS):
            base = (b * N_EXPERTS + e) * NT + t
            lo = smin_ref[base] & ~(align - 1)
            los.append(lo)
            wss.append(pl.multiple_of(jnp.minimum(lo, CAP_LAT - COMBINE_WIN), align))
            extras.append(lax.shift_right_arithmetic(smax_ref[base] - lo, shift))
        w_parts, y_parts = [], []
        for e in range(0, N_EXPERTS, 2):
            want = jnp.where(first, slot_ref[0, :, e:e + 1] - wss[e],
                             slot_ref[0, :, e + 1:e + 2] - wss[e + 1] + COMBINE_WIN)
            gate = jnp.where(first, aff_ref[0, :, e:e + 1], aff_ref[0, :, e + 1:e + 2])
            w_parts.append(jnp.where(want == lane2, gate, 0.0).astype(BF16))
            y_parts += [y_ref[0, e, pl.ds(wss[e], COMBINE_WIN), :], y_ref[0, e + 1, pl.ds(wss[e + 1], COMBINE_WIN), :]]
        acc_ref[...] = _dot(jnp.concatenate(w_parts, axis=1), jnp.concatenate(y_parts, axis=0))
        most = extras[0]
        for x in extras[1:]:
            most = jnp.maximum(most, x)

        @pl.when(most > 0)
        def _():
            for e in range(N_EXPERTS):
                def more(k, carry, e=e):
                    lo_k = los[e] + k * COMBINE_WIN
                    ws = pl.multiple_of(jnp.minimum(lo_k, CAP_LAT - COMBINE_WIN), align)
                    acc_ref[...] += _dot(weights(e, lane + ws, lo_k), y_ref[0, e, pl.ds(ws, COMBINE_WIN), :])
                    return carry

                lax.fori_loop(1, extras[e] + 1, more, 0)

        finish(acc_ref[...])


def moe_combine(x1, mod, slot_tok, aff_tok, y, smin, smax, nslot, final_w):
    last = final_w is not None
    tspec = lambda w: pl.BlockSpec((1, TILE, w), lambda b, t, *_: (b, t, 0))
    if last:
        first_lat = CTX_LEN // TILE
        out_spec = pl.BlockSpec((1, TILE, D_MODEL), lambda b, t, *_: (b, jnp.maximum(t - first_lat, 0), 0))
        out_shape = jax.ShapeDtypeStruct((BATCH, SEQ, D_MODEL), F32)
        fw = final_w.astype(F32).reshape(1, D_MODEL)
    else:
        out_spec = tspec(D_MODEL)
        out_shape = jax.ShapeDtypeStruct((BATCH, T, D_MODEL), F32)
        fw = jnp.ones((1, D_MODEL), F32)
    return pl.pallas_call(
        functools.partial(_combine_kernel, nslot=nslot, last=last),
        grid_spec=pltpu.PrefetchScalarGridSpec(
            num_scalar_prefetch=2,
            grid=(BATCH, NT),
            in_specs=[tspec(D_MODEL),
                      pl.BlockSpec((1, 1, 6, D_MODEL), lambda b, t, *_: (b, _seg(t), 0, 0)),
                      tspec(N_EXPERTS), tspec(N_EXPERTS),
                      pl.BlockSpec((1, N_EXPERTS, nslot, D_MODEL), lambda b, t, *_: (b, 0, 0, 0)),
                      pl.BlockSpec((1, D_MODEL), lambda b, t, *_: (0, 0))],
            out_specs=out_spec,
            scratch_shapes=[pltpu.VMEM((TILE, D_MODEL), F32)]),
        out_shape=out_shape,
        compiler_params=_cp("arbitrary", "arbitrary"),
        name="moe_combine",
    )(smin, smax, x1, mod, slot_tok, aff_tok, y, fw)


def trunk_layer(l, stream, mod, cos_t, sin_t, norm1_w, norm2_w, w_in_p, w_out, s5_weights, s5_d, s5_glu_w, s5_glu_b,
                na_bias, ssd_conv_w, ssd_conv_b, ssd_dt_bias, ssd_a_log, ssd_d, ssd_norm_w,
                moe_router, wg, wu, wd, final_w):
    with_ctx_out = final_w is None
    u, q, k, v, z, xbc, dt_raw = in_proj(stream, mod, norm1_w, w_in_p, l)
    s5y = s5_mix(u, *s5_weights)
    na = natten(q, k, v, na_bias, l)
    xs, bm, cm, dt, a = ssd_prep(xbc, dt_raw, ssd_conv_w, ssd_conv_b, ssd_dt_bias, ssd_a_log, cos_t, sin_t)
    sdf, sdb = ssd_scan(xs, bm, cm, dt, a)

    x1, h, logits = post_mixer(stream, mod, u, s5y, na, xs, z, sdf, sdb, s5_d, s5_glu_w, s5_glu_b,
                               ssd_d, ssd_norm_w, w_out, norm2_w, moe_router)
    slot, aff = route(jnp.swapaxes(logits[:, :, :N_EXPERTS], 1, 2), with_ctx_out)
    nslot = CAP_LAT + CAP_CTX if with_ctx_out else CAP_LAT
    smin, smax = slot_ranges(slot)
    y = moe_ffn(moe_gather(h, slot, smin, smax, nslot), l, wg, wu, wd, nslot)
    return moe_combine(x1, mod, jnp.swapaxes(slot, 1, 2), jnp.swapaxes(aff, 1, 2), y, smin, smax, nslot, final_w)


def kernel(x, c, ctx, c_ctx, w_ada, b_ada, norm1_w, norm2_w, w_in, w_out, s5_lam_re, s5_lam_im, s5_log_dt, s5_b_re, s5_b_im, s5_c_re, s5_c_im, s5_d, s5_glu_w, s5_glu_b, na_rpb, ssd_conv_w, ssd_conv_b, ssd_dt_bias, ssd_a_log, ssd_d, ssd_norm_w, moe_router, moe_w_gate, moe_w_up, moe_w_down, final_norm_w):
    cvec = jnp.zeros((SUB, D_MODEL), F32).at[0].set(c_ctx.astype(F32)).at[1:1 + BATCH].set(c.astype(F32))
    mods = ada_mod(cvec, w_ada.astype(F32), b_ada.astype(F32)).reshape(DEPTH, SUB, 6, D_MODEL)
    mods = jnp.stack([jnp.broadcast_to(mods[:, 0:1], (DEPTH, BATCH, 6, D_MODEL)), mods[:, 1:1 + BATCH]], axis=2)
    cos_t, sin_t = rope_tables()
    w_in_p = jnp.zeros((DEPTH, D_MODEL, IN_COLS_PAD), BF16).at[:, :, :IN_COLS].set(w_in.astype(BF16))
    s5_weights = jax.vmap(s5_params)(s5_lam_re, s5_lam_im, s5_log_dt, s5_b_re, s5_b_im, s5_c_re, s5_c_im)
    na_bias = natten_bias(na_rpb.reshape(DEPTH * NA_HEADS, 2 * NA_KH - 1, RPB_W))
    wg, wu, wd = moe_w_gate.astype(F32), moe_w_up.astype(F32), moe_w_down.astype(F32)
    stream = (ctx.astype(F32), x.astype(F32), CTX_LEN // TILE)
    for l in range(DEPTH):
        xa = trunk_layer(
            l, stream, mods[l], cos_t, sin_t, norm1_w[l], norm2_w[l], w_in_p, w_out[l],
            tuple(w[l] for w in s5_weights), s5_d[l], s5_glu_w[l], s5_glu_b[l],
            na_bias, ssd_conv_w[l], ssd_conv_b[l], ssd_dt_bias[l], ssd_a_log[l], ssd_d[l], ssd_norm_w[l],
            moe_router[l], wg, wu, wd, final_norm_w if l == DEPTH - 1 else None)
        stream = (xa, xa, 0)
    return xa
```

```python
import functools
import math

import jax
import jax.numpy as jnp
import numpy as np
from jax import lax
from jax.experimental import pallas as pl
from jax.experimental.pallas import tpu as pltpu

F32 = jnp.float32
BF16 = jnp.bfloat16
I32 = jnp.int32

D_MODEL = 1024
BATCH = 4
SEQ = 4096
DEPTH = 2
GRID_W = 64
CTX_LEN = 256
EPS = 1e-6

S5_WIDTH = 256
S5_GROUP = 16
S5_NGROUPS = 16
S5_STATE = 64
S5_NSTATE = S5_NGROUPS * S5_STATE

NA_HEADS = 6
NA_HEAD_DIM = 64
NA_WIDTH = 384
NA_KH = 8
NA_KW = 16
NA_BAND = 12
RPB_W = 2 * NA_KW - 1

SSD_HEADS = 6
SSD_HEAD_DIM = 64
SSD_WIDTH = 384
SSD_NGROUPS = 2
SSD_STATE = 128
SSD_CONV = 5
SSD_BC = 256
SSD_XBC = 896

N_EXPERTS = 16
D_EXPERT = 1024
ROPE_BASE = 10000.0

T = CTX_LEN + SEQ
TILE = 256
NT = T // TILE
LANE = 128
SUB = 8
ROWS = SEQ // GRID_W
CAP_LAT = 2 * SEQ // N_EXPERTS
CAP_CTX = 2 * CTX_LEN // N_EXPERTS
NEG = -1e30

C_U = 0
C_Q = 256
C_K = 640
C_V = 1024
C_Z = 1408
C_XBC = 1792
C_DT = 2688
IN_COLS = 2700
IN_COLS_PAD = 2816

VMEM_LIMIT = 56 * 1024 * 1024


def _cp(*sem):
    return pltpu.CompilerParams(dimension_semantics=sem, vmem_limit_bytes=VMEM_LIMIT)


def _dot(a, b):
    return jnp.dot(a, b, preferred_element_type=F32)


def _dot_nt(a, b):
    return lax.dot_general(a, b, (((1,), (1,)), ((), ())), preferred_element_type=F32)


def _split3(x):
    hi = x.astype(BF16)
    r = x - hi.astype(F32)
    mid = r.astype(BF16)
    lo = (r - mid.astype(F32)).astype(BF16)
    return hi, mid, lo


def _dot_exact_rhs(a_bf16, b_f32):
    hi, mid, lo = _split3(b_f32)
    return _dot(a_bf16, hi) + _dot(a_bf16, mid) + _dot(a_bf16, lo)


def _dot_exact_lhs(a_f32, b_bf16):
    hi, mid, lo = _split3(a_f32)
    return _dot(hi, b_bf16) + _dot(mid, b_bf16) + _dot(lo, b_bf16)


def _dot_x3(a, b):
    ah = a.astype(BF16)
    al = (a - ah.astype(F32)).astype(BF16)
    bh = b.astype(BF16)
    bl = (b - bh.astype(F32)).astype(BF16)
    return _dot(ah, bh) + _dot(ah, bl) + _dot(al, bh)


def _silu(x):
    return x * jax.nn.sigmoid(x)


def _seg(t):
    return jnp.where(t >= CTX_LEN // TILE, 1, 0)


def _bwd_tile(i):
    return jnp.where(i == 0, 0, NT - i)


def _ada_kernel(c_ref, w_ref, b_ref, o_ref):
    s = _silu(c_ref[...])
    o_ref[0] = _dot_x3(s, w_ref[0]) + b_ref[0]


def ada_mod(cvec, w_ada, b_ada):
    nb = 1024
    return pl.pallas_call(
        _ada_kernel,
        grid=(DEPTH, 6 * D_MODEL // nb),
        in_specs=[pl.BlockSpec((SUB, D_MODEL), lambda l, j: (0, 0)),
                  pl.BlockSpec((1, D_MODEL, nb), lambda l, j: (l, 0, j)),
                  pl.BlockSpec((1, 1, nb), lambda l, j: (l, 0, j))],
        out_specs=pl.BlockSpec((1, SUB, nb), lambda l, j: (l, 0, j)),
        out_shape=jax.ShapeDtypeStruct((DEPTH, SUB, 6 * D_MODEL), F32),
        compiler_params=_cp("arbitrary", "arbitrary"),
        name="ada_mod",
    )(cvec, w_ada, b_ada.reshape(DEPTH, 1, 6 * D_MODEL))


def _modulated_norm(x, nw, shift, scale):
    y = x * lax.rsqrt(jnp.mean(x * x, axis=-1, keepdims=True) + EPS) * nw
    return y * (1.0 + scale) + shift


def _stream_specs(stream):
    _, _, off = stream
    return [pl.BlockSpec((1, TILE, D_MODEL), lambda b, t: (b, 0, 0)),
            pl.BlockSpec((1, TILE, D_MODEL), lambda b, t: (b, jnp.maximum(t - off, 0), 0))]


def _stream_tile(head_ref, tail_ref):
    return jnp.where(pl.program_id(1) == 0, head_ref[0], tail_ref[0])


def _inproj_kernel(xh_ref, xt_ref, mod_ref, nw_ref, w_ref, u_ref, q_ref, k_ref, v_ref, z_ref, xbc_ref, dt_ref):
    x = _stream_tile(xh_ref, xt_ref)
    h = _modulated_norm(x, nw_ref[...], mod_ref[0, 0, 0:1, :], mod_ref[0, 0, 1:2, :]).astype(BF16)

    def proj(lo, hi):
        return _dot(h, w_ref[0, :, lo:hi])

    for s in range(S5_WIDTH // LANE):
        u_ref[0, s] = proj(C_U + s * LANE, C_U + (s + 1) * LANE)
    q_ref[0] = (proj(C_Q, C_K) * (NA_HEAD_DIM ** -0.5)).astype(BF16)
    k_ref[0] = proj(C_K, C_V).astype(BF16)
    v_ref[0] = proj(C_V, C_Z).astype(BF16)
    z_ref[0] = proj(C_Z, C_XBC).astype(BF16)
    xbc_ref[0] = proj(C_XBC, C_DT).astype(BF16)
    dt_ref[0] = proj(C_DT, IN_COLS_PAD)


def in_proj(stream, mod, norm_w, w_in_p, layer):
    tok = lambda w, dt: jax.ShapeDtypeStruct((BATCH, T, w), dt)
    tspec = lambda w: pl.BlockSpec((1, TILE, w), lambda b, t: (b, t, 0))
    return pl.pallas_call(
        _inproj_kernel,
        grid=(BATCH, NT),
        in_specs=_stream_specs(stream) + [
                  pl.BlockSpec((1, 1, 6, D_MODEL), lambda b, t: (b, _seg(t), 0, 0)),
                  pl.BlockSpec((1, D_MODEL), lambda b, t: (0, 0)),
                  pl.BlockSpec((1, D_MODEL, IN_COLS_PAD), lambda b, t: (layer, 0, 0))],
        out_specs=[pl.BlockSpec((1, S5_WIDTH // LANE, TILE, LANE), lambda b, t: (b, 0, t, 0)),
                   tspec(NA_WIDTH), tspec(NA_WIDTH), tspec(NA_WIDTH),
                   tspec(SSD_WIDTH), tspec(SSD_XBC), tspec(LANE)],
        out_shape=[jax.ShapeDtypeStruct((BATCH, S5_WIDTH // LANE, T, LANE), F32),
                   tok(NA_WIDTH, BF16), tok(NA_WIDTH, BF16), tok(NA_WIDTH, BF16),
                   tok(SSD_WIDTH, BF16), tok(SSD_XBC, BF16), tok(LANE, F32)],
        compiler_params=_cp("arbitrary", "arbitrary"),
        name="in_proj",
    )(stream[0], stream[1], mod, norm_w.reshape(1, D_MODEL), w_in_p)


S5_BLK = SUB
NB = T // S5_BLK
NB_CTX = CTX_LEN // S5_BLK
S5_NPAIR = S5_NGROUPS // 2
S5_PW = 2 * S5_BLK * S5_GROUP


def _s5_kernel(u_ref, pin_ref, pout_ref, w1_ref, w2_ref, w3_ref, mul_ref, y_ref, ub_ref, yb_ref, st_ref):
    n = S5_NSTATE
    half = LANE
    per_half = S5_NPAIR // 2
    toks = [[u_ref[0, h, pl.ds(j, NB, stride=S5_BLK), :].astype(BF16) for j in range(S5_BLK)]
            for h in range(2)]
    for pp in range(S5_NPAIR):
        h, q = divmod(pp, per_half)
        acc = None
        for j in range(S5_BLK):
            term = _dot(toks[h][j], pin_ref[q, j])
            acc = term if acc is None else acc + term
        ub_ref[:, pp * S5_PW:(pp + 1) * S5_PW] = acc.astype(BF16)

    for d in range(2):
        for pp in range(S5_NPAIR):
            s = _dot(ub_ref[:, pp * S5_PW:(pp + 1) * S5_PW], w1_ref[d, pp])
            st_ref[d, :, pp * half:(pp + 1) * half] = s[:, :half]
            st_ref[d, :, n + pp * half:n + (pp + 1) * half] = s[:, half:]

    ngrp = NB // SUB
    nctx = NB_CTX // SUB
    rowid = lax.broadcasted_iota(I32, (SUB, n), 0)
    for d in range(2):
        def body(j, carry, d=d):
            cr, ci = carry
            r = j if d == 0 else jnp.where(j < nctx, nctx - 1 - j, ngrp - 1 + nctx - j)
            row = pl.multiple_of(r * SUB, SUB)
            re = st_ref[d, pl.ds(row, SUB), 0:n]
            im = st_ref[d, pl.ds(row, SUB), n:2 * n]
            for kk, sh in enumerate((1, 2, 4)):
                mr = mul_ref[d, kk * SUB:(kk + 1) * SUB, 0:n]
                mi = mul_ref[d, kk * SUB:(kk + 1) * SUB, n:2 * n]
                s = sh if d == 0 else SUB - sh
                sr = pltpu.roll(re, s, 0)
                si = pltpu.roll(im, s, 0)
                re, im = re + (mr * sr - mi * si), im + (mr * si + mi * sr)
            pr = mul_ref[d, 3 * SUB:4 * SUB, 0:n]
            pi = mul_ref[d, 3 * SUB:4 * SUB, n:2 * n]
            re, im = re + (pr * cr - pi * ci), im + (pr * ci + pi * cr)
            edge, last, sh = (0, SUB - 1, 1) if d == 0 else (SUB - 1, 0, SUB - 1)
            st_ref[d, pl.ds(row, SUB), 0:n] = jnp.where(rowid == edge, cr, pltpu.roll(re, sh, 0))
            st_ref[d, pl.ds(row, SUB), n:2 * n] = jnp.where(rowid == edge, ci, pltpu.roll(im, sh, 0))
            return re[last:last + 1, :], im[last:last + 1, :]

        zero = jnp.zeros((1, n), F32)
        lax.fori_loop(0, ngrp, body, (zero, zero), unroll=2)

    for pp in range(S5_NPAIR):
        up = ub_ref[:, pp * S5_PW:(pp + 1) * S5_PW]
        acc = None
        for d in range(2):
            enter = jnp.concatenate([st_ref[d, :, pp * half:(pp + 1) * half],
                                     st_ref[d, :, n + pp * half:n + (pp + 1) * half]], axis=1).astype(BF16)
            term = _dot(up, w2_ref[d, pp]) + _dot(enter, w3_ref[d, pp])
            acc = term if acc is None else acc + term
        yb_ref[:, pp * S5_PW:(pp + 1) * S5_PW] = acc.astype(BF16)

    for i in range(S5_BLK):
        for h in range(2):
            acc = None
            for q in range(per_half):
                pp = h * per_half + q
                term = _dot(yb_ref[:, pp * S5_PW:(pp + 1) * S5_PW], pout_ref[q, i])
                acc = term if acc is None else acc + term
            y_ref[0, h, pl.ds(i, NB, stride=S5_BLK), :] = acc


def _s5_regroup_matrices():
    per_half = S5_NPAIR // 2
    pin = np.zeros((per_half, S5_BLK, LANE, S5_PW), np.float32)
    for q in range(per_half):
        for j in range(S5_BLK):
            for gg in range(2):
                for c in range(S5_GROUP):
                    pin[q, j, (2 * q + gg) * S5_GROUP + c, gg * S5_BLK * S5_GROUP + j * S5_GROUP + c] = 1.0
    return jnp.asarray(pin, BF16), jnp.asarray(pin.transpose(0, 1, 3, 2), BF16)


def s5_mix(u, w1, w2, w3, mul):
    pin, pout = _s5_regroup_matrices()
    per_half = S5_NPAIR // 2
    wspec = pl.BlockSpec((2, S5_NPAIR, S5_PW, S5_PW), lambda b: (0, 0, 0, 0))
    tok = pl.BlockSpec((1, S5_WIDTH // LANE, T, LANE), lambda b: (b, 0, 0, 0))
    return pl.pallas_call(
        _s5_kernel,
        grid=(BATCH,),
        in_specs=[tok, pl.BlockSpec((per_half, S5_BLK, LANE, S5_PW), lambda b: (0, 0, 0, 0)),
                  pl.BlockSpec((per_half, S5_BLK, S5_PW, LANE), lambda b: (0, 0, 0, 0)), wspec, wspec, wspec,
                  pl.BlockSpec((2, 4 * SUB, 2 * S5_NSTATE), lambda b: (0, 0, 0))],
        out_specs=tok,
        out_shape=jax.ShapeDtypeStruct((BATCH, S5_WIDTH // LANE, T, LANE), F32),
        scratch_shapes=[pltpu.VMEM((NB, S5_NGROUPS * S5_BLK * S5_GROUP), BF16),
                        pltpu.VMEM((NB, S5_NGROUPS * S5_BLK * S5_GROUP), BF16),
                        pltpu.VMEM((2, NB, 2 * S5_NSTATE), F32)],
        compiler_params=_cp("arbitrary"),
        name="s5_mix",
    )(u, pin, pout, w1, w2, w3, mul)


def s5_params(lam_re, lam_im, log_dt, b_re, b_im, c_re, c_im):
    G, P, C = S5_NGROUPS, S5_STATE, S5_GROUP
    lam = lax.complex(lam_re.astype(F32), lam_im.astype(F32))
    step = jnp.exp(log_dt.astype(F32))[..., None]
    log_lb = lam * step
    lam_bar = jnp.exp(log_lb)
    b_bar = ((lam_bar - 1.0) / lam)[..., None] * lax.complex(b_re.astype(F32), b_im.astype(F32))
    J = S5_BLK
    row = lambda a: a.reshape(2, G, 1, P)
    col = lambda a: jnp.broadcast_to(a[..., None], (2, G, P, J * C))
    b_t = jnp.swapaxes(b_bar, 2, 3)
    c_t = lambda a: jnp.tile(jnp.swapaxes(a.astype(F32), 2, 3), (1, 1, 1, J))
    wspec = pl.BlockSpec((1, 1, S5_PW, S5_PW), lambda d, q: (d, q, 0, 0))
    pair = lambda r, s: pl.BlockSpec((1, 2, r, s), lambda d, q: (d, q, 0, 0))
    w1, w2, w3 = pl.pallas_call(
        _s5_weight_kernel,
        grid=(2, S5_NPAIR),
        in_specs=[pair(1, P), pair(1, P), pair(P, J * C), pair(P, J * C), pair(C, P), pair(C, P),
                  pair(P, J * C), pair(P, J * C)],
        out_specs=[wspec, wspec, wspec],
        out_shape=[jax.ShapeDtypeStruct((2, S5_NPAIR, S5_PW, S5_PW), BF16)] * 3,
        compiler_params=_cp("arbitrary", "arbitrary"),
        name="s5_weights",
    )(row(jnp.real(log_lb)), row(jnp.imag(log_lb)), col(jnp.real(log_lb)), col(jnp.imag(log_lb)),
      jnp.real(b_t), jnp.imag(b_t), c_t(c_re), c_t(c_im))
    rows = jnp.arange(SUB)
    pieces = []
    for d in range(2):
        log_blk = (log_lb[d] * float(J)).reshape(1, G * P)
        per_d = []
        for sh in (1, 2, 4):
            valid = (rows >= sh) if d == 0 else (rows < SUB - sh)
            per_d.append(jnp.where(valid[:, None], jnp.exp(log_blk * float(sh)), 0.0))
        expo = (rows + 1) if d == 0 else (SUB - rows)
        per_d.append(jnp.exp(log_blk * expo[:, None].astype(F32)))
        m = jnp.concatenate(per_d, axis=0)
        pieces.append(jnp.concatenate([jnp.real(m), jnp.imag(m)], axis=-1))
    mul = jnp.stack(pieces, axis=0).astype(F32)
    return w1, w2, w3, mul


def _s5_weight_kernel(llr_ref, lli_ref, lcr_ref, lci_ref, btr_ref, bti_ref, ctr_ref, cti_ref, w1_ref, w2_ref, w3_ref):
    J, C, P = S5_BLK, S5_GROUP, S5_STATE
    R = J * C
    fwd = pl.program_id(0) == 0
    shift = C.bit_length() - 1
    j_of_row = lax.shift_right_logical(lax.broadcasted_iota(I32, (R, P), 0), shift).astype(F32)
    i_of_col = lax.shift_right_logical(lax.broadcasted_iota(I32, (P, R), 1), shift).astype(F32)
    jr = lax.shift_right_logical(lax.broadcasted_iota(I32, (R, R), 0), shift)
    ic = lax.shift_right_logical(lax.broadcasted_iota(I32, (R, R), 1), shift)
    lag = jnp.where(fwd, ic - jr, jr - ic)

    def cpow(expo, lr, li):
        mag = jnp.exp(lr * expo)
        return mag * jnp.cos(li * expo), mag * jnp.sin(li * expo)

    w1_ref[...] = jnp.zeros_like(w1_ref)
    w2_ref[...] = jnp.zeros_like(w2_ref)
    w3_ref[...] = jnp.zeros_like(w3_ref)
    for gg in range(2):
        llr, lli = llr_ref[0, gg], lli_ref[0, gg]
        br = jnp.concatenate([btr_ref[0, gg]] * J, axis=0)
        bi = jnp.concatenate([bti_ref[0, gg]] * J, axis=0)
        cr, ci = ctr_ref[0, gg], cti_ref[0, gg]
        pr, pi = cpow(jnp.where(fwd, (J - 1) - j_of_row, j_of_row), llr, lli)
        w1_ref[0, 0, gg * R:(gg + 1) * R, gg * P:(gg + 1) * P] = (pr * br - pi * bi).astype(BF16)
        w1_ref[0, 0, gg * R:(gg + 1) * R, 2 * P + gg * P:2 * P + (gg + 1) * P] = (pr * bi + pi * br).astype(BF16)
        qr, qi = cpow(jnp.where(fwd, i_of_col + 1.0, J - i_of_col), lcr_ref[0, gg], lci_ref[0, gg])
        w3_ref[0, 0, gg * P:(gg + 1) * P, gg * R:(gg + 1) * R] = (cr * qr - ci * qi).astype(BF16)
        w3_ref[0, 0, 2 * P + gg * P:2 * P + (gg + 1) * P, gg * R:(gg + 1) * R] = (-(cr * qi + ci * qr)).astype(BF16)
        acc = jnp.zeros((R, R), F32)
        for k in range(J):
            lr, li = cpow(float(k), llr, lli)
            t = _dot_x3(br * lr - bi * li, cr) - _dot_x3(br * li + bi * lr, ci)
            acc = acc + jnp.where(lag == k, t, 0.0)
        w2_ref[0, 0, gg * R:(gg + 1) * R, gg * R:(gg + 1) * R] = acc.astype(BF16)


def _softmax_pv(parts):
    m = parts[0][0].max(axis=-1, keepdims=True)
    for s, _ in parts[1:]:
        m = jnp.maximum(m, s.max(axis=-1, keepdims=True))
    den = 0.0
    acc = 0.0
    for s, v in parts:
        p = jnp.exp(s - m)
        den = den + p.sum(axis=-1, keepdims=True)
        acc = acc + _dot(p.astype(BF16), v)
    return acc / den


def _na_kernel(q_ref, k_ref, v_ref, bias_ref, o_ref):
    t = pl.program_id(1)
    first = lax.broadcasted_iota(I32, (1, LANE), 1) < NA_HEAD_DIM

    def pair_attention(pp, start):
        ls = slice(pp * LANE, (pp + 1) * LANE)
        qp = q_ref[0, :, ls]
        kc = k_ref[0, 0:CTX_LEN, ls]
        vc = v_ref[0, 0:CTX_LEN, ls]
        outs = []
        for hh in range(2):
            qm = jnp.where(first if hh == 0 else ~first, qp, jnp.zeros_like(qp))
            parts = [(_dot_nt(qm, kc), vc)]
            if start is not None:
                kb = k_ref[0, pl.ds(start, NA_BAND * GRID_W), ls]
                vb = v_ref[0, pl.ds(start, NA_BAND * GRID_W), ls]
                parts.append((_dot_nt(qm, kb) + bias_ref[2 * pp + hh, 0], vb))
            outs.append(_softmax_pv(parts))
        o_ref[0, :, ls] = jnp.where(first, outs[0], outs[1]).astype(BF16)

    @pl.when(t == 0)
    def _():
        for pp in range(NA_HEADS // 2):
            pair_attention(pp, None)

    @pl.when(t > 0)
    def _():
        first_row = (t - 1) * (TILE // GRID_W)
        u0 = jnp.clip(first_row - NA_KH // 2, 0, ROWS - NA_BAND)
        start = pl.multiple_of(CTX_LEN + u0 * GRID_W, LANE)
        for pp in range(NA_HEADS // 2):
            pair_attention(pp, start)


def _na_cfg(t):
    return jnp.where(t <= 1, 0, jnp.where(t == NT - 1, 2, 1))


def natten(q, k, v, bias, layer):
    whole = pl.BlockSpec((1, T, NA_WIDTH), lambda b, t: (b, 0, 0))
    tile = pl.BlockSpec((1, TILE, NA_WIDTH), lambda b, t: (b, t, 0))
    return pl.pallas_call(
        _na_kernel,
        grid=(BATCH, NT),
        in_specs=[tile, whole, whole,
                  pl.BlockSpec((NA_HEADS, 1, TILE, NA_BAND * GRID_W), lambda b, t: (layer, _na_cfg(t), 0, 0))],
        out_specs=tile,
        out_shape=jax.ShapeDtypeStruct((BATCH, T, NA_WIDTH), BF16),
        compiler_params=_cp("arbitrary", "arbitrary"),
        name="natten",
    )(q, k, v, bias)


def natten_bias(rpb):
    col = jnp.arange(GRID_W)
    c0 = jnp.clip(col - NA_KW // 2, 0, GRID_W - NA_KW)
    in_win = (col[None, :] >= c0[:, None]) & (col[None, :] < c0[:, None] + NA_KW)
    rel_c = jnp.clip(col[None, :] - col[:, None] + (NA_KW - 1), 0, RPB_W - 1)
    pick_c = jax.nn.one_hot(rel_c, RPB_W, dtype=F32)
    blocks = jnp.einsum('hax,qkx->haqk', rpb.astype(F32), pick_c, precision=lax.Precision.HIGHEST)
    blocks = jnp.where(in_win[None, None], blocks, NEG)
    nh = rpb.shape[0]
    blocks = jnp.concatenate([blocks, jnp.full((nh, 1, GRID_W, GRID_W), NEG, F32)], axis=1)
    blocks = jnp.concatenate([blocks, blocks], axis=-1)
    return pl.pallas_call(
        _bias_kernel,
        grid=(nh, 3),
        in_specs=[pl.BlockSpec((1, 2 * NA_KH, GRID_W, LANE), lambda h, c: (h, 0, 0, 0))],
        out_specs=pl.BlockSpec((1, 1, TILE, NA_BAND * GRID_W), lambda h, c: (h, c, 0, 0)),
        out_shape=jax.ShapeDtypeStruct((nh, 3, TILE, NA_BAND * GRID_W), F32),
        compiler_params=_cp("arbitrary", "arbitrary"),
        name="natten_bias",
    )(blocks)


def _bias_kernel(blk_ref, o_ref):
    rows_per_tile = TILE // GRID_W
    masked = 2 * NA_KH - 1
    for cfg, first_row in enumerate((0, 2 * rows_per_tile, ROWS - rows_per_tile)):
        @pl.when(pl.program_id(1) == cfg)
        def _(first_row=first_row):
            u0 = min(max(first_row - NA_KH // 2, 0), ROWS - NA_BAND)
            for rr in range(rows_per_tile):
                r = first_row + rr
                r0 = min(max(r - NA_KH // 2, 0), ROWS - NA_KH)
                for j in range(NA_BAND):
                    kr = u0 + j
                    a = kr - r + NA_KH - 1 if r0 <= kr < r0 + NA_KH else masked
                    half = (j % 2) * GRID_W
                    o_ref[0, 0, rr * GRID_W:(rr + 1) * GRID_W, j * GRID_W:(j + 1) * GRID_W] = (
                        blk_ref[0, a, :, half:half + GRID_W])


def _softplus(x):
    return jnp.maximum(x, 0.0) + jnp.log(1.0 + jnp.exp(-jnp.abs(x)))


def _ssd_prep_kernel(prev_ref, cur_ref, next_ref, dtr_ref, cw_ref, cb_ref, dtb_ref, ar_ref, cos_ref, sin_ref,
                     xs_ref, bm_ref, cm_ref, dt_ref, a_ref):
    t = pl.program_id(1)
    halo = prev_ref.shape[1]
    has_prev = t >= 2
    has_next = (t >= 1) & (t <= NT - 2)
    prev = jnp.where(has_prev, prev_ref[0].astype(F32), 0.0)
    nxt = jnp.where(has_next, next_ref[0].astype(F32), 0.0)
    ext = jnp.concatenate([prev, cur_ref[0].astype(F32), nxt], axis=0)
    n = ext.shape[0]
    acc = cb_ref[...] + cw_ref[SSD_CONV // 2:SSD_CONV // 2 + 1, :] * ext
    for kk in range(SSD_CONV):
        off = kk - SSD_CONV // 2
        if off != 0:
            acc = acc + cw_ref[kk:kk + 1, :] * pltpu.roll(ext, (-off) % n, 0)
    y = _silu(acc[halo:halo + TILE, :])
    xs_ref[0] = y[:, 0:SSD_WIDTH].astype(BF16)

    lane = lax.broadcasted_iota(I32, (1, LANE), 1)
    low = (lane & (NA_HEAD_DIM // 2)) == 0
    cos = cos_ref[...]
    sin = sin_ref[...]
    for g in range(2 * SSD_NGROUPS):
        v = y[:, SSD_WIDTH + g * LANE:SSD_WIDTH + (g + 1) * LANE]
        sw = jnp.where(low, pltpu.roll(v, LANE - 32, 1), pltpu.roll(v, 32, 1))
        rot = (v * cos + sw * sin).astype(BF16)
        if g < SSD_NGROUPS:
            bm_ref[0, :, g * LANE:(g + 1) * LANE] = rot
        else:
            cm_ref[0, :, (g - SSD_NGROUPS) * LANE:(g - SSD_NGROUPS + 1) * LANE] = rot

    dt = _softplus(dtr_ref[0] + dtb_ref[...])
    dt_ref[0] = dt
    a_ref[0] = dt * ar_ref[...]


def ssd_prep(xbc, dt_raw, conv_w, conv_b, dt_bias, a_log, cos_t, sin_t):
    halo = 16
    per = TILE // halo
    nhalo = T // halo
    tok = lambda w, dt: jax.ShapeDtypeStruct((BATCH, T, w), dt)
    tspec = lambda w: pl.BlockSpec((1, TILE, w), lambda b, t: (b, t, 0))
    row = lambda w: pl.BlockSpec((1, w), lambda b, t: (0, 0))
    cw = jnp.zeros((SUB, SSD_XBC), F32).at[:SSD_CONV].set(conv_w.astype(F32))
    pad12 = lambda v: jnp.zeros((1, LANE), F32).at[0, :2 * SSD_HEADS].set(v.astype(F32).reshape(-1))
    return pl.pallas_call(
        _ssd_prep_kernel,
        grid=(BATCH, NT),
        in_specs=[pl.BlockSpec((1, halo, SSD_XBC), lambda b, t: (b, jnp.maximum(t * per - 1, 0), 0)),
                  tspec(SSD_XBC),
                  pl.BlockSpec((1, halo, SSD_XBC), lambda b, t: (b, jnp.minimum((t + 1) * per, nhalo - 1), 0)),
                  tspec(LANE),
                  pl.BlockSpec((SUB, SSD_XBC), lambda b, t: (0, 0)), row(SSD_XBC), row(LANE), row(LANE),
                  pl.BlockSpec((TILE, LANE), lambda b, t: (t, 0)), pl.BlockSpec((TILE, LANE), lambda b, t: (t, 0))],
        out_specs=[tspec(SSD_WIDTH), tspec(SSD_BC), tspec(SSD_BC), tspec(LANE), tspec(LANE)],
        out_shape=[tok(SSD_WIDTH, BF16), tok(SSD_BC, BF16), tok(SSD_BC, BF16), tok(LANE, F32), tok(LANE, F32)],
        compiler_params=_cp("arbitrary", "arbitrary"),
        name="ssd_prep",
    )(xbc, xbc, xbc, dt_raw, cw, conv_b.astype(F32).reshape(1, SSD_XBC), pad12(dt_bias),
      pad12(-jnp.exp(a_log.astype(F32))), cos_t, sin_t)


def rope_tables():
    half = SSD_STATE // 2
    nf = half // 2
    pos = jnp.arange(SEQ)
    inv_freq = ROPE_BASE ** (-jnp.arange(nf, dtype=F32) / nf)
    lane = jnp.arange(LANE)
    p = jnp.where(lane[None, :] < half, (pos // GRID_W)[:, None], (pos % GRID_W)[:, None]).astype(F32)
    ang = p * inv_freq[lane % nf][None, :]
    sign = jnp.where((lane & nf) == 0, -1.0, 1.0)[None, :]
    cos_t = jnp.concatenate([jnp.ones((CTX_LEN, LANE), F32), jnp.cos(ang)], axis=0)
    sin_t = jnp.concatenate([jnp.zeros((CTX_LEN, LANE), F32), jnp.sin(ang) * sign], axis=0)
    return cos_t, sin_t


def _ssd_dir(d, xs_ref, bm_ref, cm_ref, bt_ref, dt_ref, a_ref, at_ref, tri_ref, y_ref, st_ref):
    q = TILE
    lane = lax.broadcasted_iota(I32, (1, LANE), 1)
    first = lane < SSD_HEAD_DIM
    ri = lax.broadcasted_iota(I32, (q, q), 0)
    ci = lax.broadcasted_iota(I32, (q, q), 1)
    keep = (ci <= ri) if d == 0 else (ci >= ri)
    end = q - 1 if d == 0 else 0
    tri_col = tri_ref[d]
    tri_row = tri_ref[1 - d]
    a = a_ref[0]
    dt = dt_ref[0]
    cs_col = _dot_exact_rhs(tri_col, a)
    cs_row = _dot_exact_lhs(at_ref[0], tri_row)
    g_mats = [_dot_nt(cm_ref[0, :, g * SSD_STATE:(g + 1) * SSD_STATE],
                      bm_ref[0, :, g * SSD_STATE:(g + 1) * SSD_STATE]) for g in range(SSD_NGROUPS)]

    def head_col(m, h):
        c = d * SSD_HEADS + h
        return m[:, c:c + 1]

    for pp in range(SSD_HEADS // 2):
        ls = slice(pp * LANE, (pp + 1) * LANE)
        h0, h1 = 2 * pp, 2 * pp + 1
        x = xs_ref[0, :, ls].astype(F32)
        dt_l = jnp.where(first, head_col(dt, h0), head_col(dt, h1))
        cs_l = jnp.where(first, head_col(cs_col, h0), head_col(cs_col, h1))
        cs_end = cs_l[end:end + 1, :]
        xdt = x * dt_l
        xdt_b = xdt.astype(BF16)
        xw = (xdt * jnp.exp(cs_end - cs_l)).astype(BF16)
        st = st_ref[d, pp]
        st_b = st.astype(BF16)
        ys, ups = [], []
        for h in (h0, h1):
            g = h // (SSD_HEADS // SSD_NGROUPS)
            c = d * SSD_HEADS + h
            diff = head_col(cs_col, h) - cs_row[c:c + 1, :]
            decay = jnp.exp(jnp.where(keep, diff, NEG))
            m = (g_mats[g] * decay).astype(BF16)
            y_h = _dot(m, xdt_b) + _dot(cm_ref[0, :, g * SSD_STATE:(g + 1) * SSD_STATE], st_b) * jnp.exp(cs_l)
            ys.append(y_h)
            ups.append(_dot(bt_ref[0, g * SSD_STATE:(g + 1) * SSD_STATE, :], xw))
        y_ref[0, :, ls] = jnp.where(first, ys[0], ys[1])
        st_ref[d, pp] = jnp.exp(cs_end) * st + jnp.where(first, ups[0], ups[1])


def _ssd_scan_kernel(xs_f, bm_f, cm_f, bt_f, dt_f, a_f, at_f, xs_b, bm_b, cm_b, bt_b, dt_b, a_b, at_b, tri_ref,
                     yf_ref, yb_ref, st_ref):
    @pl.when(pl.program_id(1) == 0)
    def _():
        st_ref[...] = jnp.zeros_like(st_ref)

    _ssd_dir(0, xs_f, bm_f, cm_f, bt_f, dt_f, a_f, at_f, tri_ref, yf_ref, st_ref)
    _ssd_dir(1, xs_b, bm_b, cm_b, bt_b, dt_b, a_b, at_b, tri_ref, yb_ref, st_ref)


def ssd_scan(xs, bm, cm, dt, a):
    bt = jnp.swapaxes(bm, 1, 2)
    at = jnp.swapaxes(a[:, :, :2 * SUB], 1, 2)
    idx = jnp.arange(TILE)
    tri = jnp.stack([idx[None, :] <= idx[:, None], idx[None, :] >= idx[:, None]]).astype(BF16)
    fwd = lambda b, i: (b, i, 0)
    bwd = lambda b, i: (b, _bwd_tile(i), 0)
    fwd_t = lambda b, i: (b, 0, i)
    bwd_t = lambda b, i: (b, 0, _bwd_tile(i))

    def specs(f, ft):
        return [pl.BlockSpec((1, TILE, SSD_WIDTH), f), pl.BlockSpec((1, TILE, SSD_BC), f),
                pl.BlockSpec((1, TILE, SSD_BC), f), pl.BlockSpec((1, SSD_BC, TILE), ft),
                pl.BlockSpec((1, TILE, LANE), f), pl.BlockSpec((1, TILE, LANE), f),
                pl.BlockSpec((1, 2 * SUB, TILE), ft)]

    args = (xs, bm, cm, bt, dt, a, at)
    return pl.pallas_call(
        _ssd_scan_kernel,
        grid=(BATCH, NT),
        in_specs=specs(fwd, fwd_t) + specs(bwd, bwd_t) + [pl.BlockSpec((2, TILE, TILE), lambda b, i: (0, 0, 0))],
        out_specs=[pl.BlockSpec((1, TILE, SSD_WIDTH), fwd), pl.BlockSpec((1, TILE, SSD_WIDTH), bwd)],
        out_shape=[jax.ShapeDtypeStruct((BATCH, T, SSD_WIDTH), F32)] * 2,
        scratch_shapes=[pltpu.VMEM((2, SSD_HEADS // 2, SSD_STATE, LANE), F32)],
        compiler_params=_cp("arbitrary", "arbitrary"),
        name="ssd_scan",
    )(*args, *args, tri)


def _gelu_tanh(x):
    return 0.5 * x * (1.0 + jnp.tanh(math.sqrt(2.0 / math.pi) * (x + 0.044715 * (x * x * x))))


def _post_kernel(xh_ref, xt_ref, mod_ref, u_ref, s5y_ref, na_ref, xs_ref, z_ref, sdf_ref, sdb_ref,
                 s5d_ref, gw_ref, gb_ref, sdd_ref, snw_ref, wo_ref, n2w_ref, rt_ref,
                 x1_ref, h_ref, lg_ref):
    slabs = lambda r: jnp.concatenate([r[0, h] for h in range(S5_WIDTH // LANE)], axis=-1)
    ys5 = slabs(u_ref) * s5d_ref[...] + slabs(s5y_ref)
    g = _gelu_tanh(ys5)
    s5o = g * jax.nn.sigmoid(_dot(g.astype(BF16), gw_ref[...]) + gb_ref[...])
    yssd = (xs_ref[0].astype(F32) * sdd_ref[...] + sdf_ref[0] + sdb_ref[0]) * _silu(z_ref[0].astype(F32))
    ssdo = yssd * lax.rsqrt(jnp.mean(yssd * yssd, axis=-1, keepdims=True) + EPS) * snw_ref[...]
    mix = jnp.concatenate([s5o.astype(BF16), na_ref[0], ssdo.astype(BF16)], axis=-1)
    x1 = _stream_tile(xh_ref, xt_ref) + mod_ref[0, 0, 2:3, :] * _dot(mix, wo_ref[...])
    x1_ref[0] = x1
    h = _modulated_norm(x1, n2w_ref[...], mod_ref[0, 0, 3:4, :], mod_ref[0, 0, 4:5, :])
    h_ref[0] = h.astype(BF16)
    lg_ref[0] = _dot_x3(h, rt_ref[...])


def post_mixer(stream, mod, u, s5y, na, xs, z, sdf, sdb, s5_d, glu_w, glu_b, ssd_d, ssd_norm_w, w_out, norm2_w,
               router):
    tspec = lambda w: pl.BlockSpec((1, TILE, w), lambda b, t: (b, t, 0))
    whole = lambda *shp: pl.BlockSpec(shp, lambda b, t: (0,) * len(shp))
    rt = jnp.zeros((D_MODEL, LANE), F32).at[:, :N_EXPERTS].set(router.astype(F32))
    slab = pl.BlockSpec((1, S5_WIDTH // LANE, TILE, LANE), lambda b, t: (b, 0, t, 0))
    return pl.pallas_call(
        _post_kernel,
        grid=(BATCH, NT),
        in_specs=_stream_specs(stream) + [
                  pl.BlockSpec((1, 1, 6, D_MODEL), lambda b, t: (b, _seg(t), 0, 0)),
                  slab, slab, tspec(NA_WIDTH),
                  tspec(SSD_WIDTH), tspec(SSD_WIDTH), tspec(SSD_WIDTH), tspec(SSD_WIDTH),
                  whole(1, S5_WIDTH), whole(S5_WIDTH, S5_WIDTH), whole(1, S5_WIDTH),
                  whole(1, SSD_WIDTH), whole(1, SSD_WIDTH), whole(D_MODEL, D_MODEL), whole(1, D_MODEL),
                  whole(D_MODEL, LANE)],
        out_specs=[tspec(D_MODEL), tspec(D_MODEL), tspec(LANE)],
        out_shape=[jax.ShapeDtypeStruct((BATCH, T, D_MODEL), F32), jax.ShapeDtypeStruct((BATCH, T, D_MODEL), BF16),
                   jax.ShapeDtypeStruct((BATCH, T, LANE), F32)],
        compiler_params=_cp("arbitrary", "arbitrary"),
        name="post_mixer",
    )(stream[0], stream[1], mod, u, s5y, na, xs, z, sdf, sdb,
      s5_d.astype(F32).reshape(1, S5_WIDTH), glu_w.astype(BF16), glu_b.astype(F32).reshape(1, S5_WIDTH),
      jnp.repeat(ssd_d.astype(F32), SSD_HEAD_DIM).reshape(1, SSD_WIDTH), ssd_norm_w.astype(F32).reshape(1, SSD_WIDTH),
      w_out.astype(BF16), norm2_w.astype(F32).reshape(1, D_MODEL), rt)


def _route_kernel(lg_ref, tri_ref, slot_ref, aff_ref, *, with_ctx):
    lg = lg_ref[0]
    m = lg.max(axis=0, keepdims=True)
    e = jnp.exp(lg - m)
    aff = e / e.sum(axis=0, keepdims=True)
    aff_ref[0] = aff
    bits = pltpu.bitcast(aff, I32)
    is_ctx = lax.broadcasted_iota(I32, (N_EXPERTS, T), 1) < CTX_LEN

    def count(mask):
        return jnp.where(mask, 1.0, 0.0).sum(axis=1, keepdims=True)

    def kth_largest(seg, k):
        def body(i, prefix):
            cand = prefix | lax.shift_left(jnp.int32(1), 30 - i)
            return jnp.where(count((bits >= cand) & seg) >= k, cand, prefix)
        return lax.fori_loop(0, 31, body, jnp.zeros((N_EXPERTS, 1), I32))

    def excl_cumsum(x01):
        carry = jnp.zeros((N_EXPERTS, 1), F32)
        pieces = []
        for j in range(T // LANE):
            blk = x01[:, j * LANE:(j + 1) * LANE]
            inc = _dot(blk.astype(BF16), tri_ref[...])
            pieces.append(inc - blk + carry)
            carry = carry + inc[:, LANE - 1:LANE]
        return jnp.concatenate(pieces, axis=1)

    thr = kth_largest(~is_ctx, float(CAP_LAT))
    k_of = jnp.full((N_EXPERTS, T), float(CAP_LAT), F32)
    if with_ctx:
        thr = jnp.where(is_ctx, kth_largest(is_ctx, float(CAP_CTX)), thr)
        k_of = jnp.where(is_ctx, float(CAP_CTX), k_of)
    gt = bits > thr
    eq = bits == thr
    if not with_ctx:
        gt = gt & ~is_ctx
        eq = eq & ~is_ctx
    n_gt = jnp.where(is_ctx, count(gt & is_ctx), count(gt & ~is_ctx))
    tie_rank = excl_cumsum(jnp.where(eq, 1.0, 0.0))
    tie_rank = tie_rank - jnp.where(is_ctx, 0.0, count(eq & is_ctx))
    sel = gt | (eq & (tie_rank < k_of - n_gt))
    pos = excl_cumsum(jnp.where(sel, 1.0, 0.0))
    slot = jnp.where(is_ctx, pos + float(CAP_LAT), pos - count(sel & is_ctx))
    slot_ref[0] = jnp.where(sel, slot, -1.0).astype(I32)


def route(logits_t, with_ctx):
    idx = jnp.arange(LANE)
    tri = (idx[:, None] <= idx[None, :]).astype(BF16)
    spec = pl.BlockSpec((1, N_EXPERTS, T), lambda b: (b, 0, 0))
    return pl.pallas_call(
        functools.partial(_route_kernel, with_ctx=with_ctx),
        grid=(BATCH,),
        in_specs=[spec, pl.BlockSpec((LANE, LANE), lambda b: (0, 0))],
        out_specs=[spec, spec],
        out_shape=[jax.ShapeDtypeStruct((BATCH, N_EXPERTS, T), I32), jax.ShapeDtypeStruct((BATCH, N_EXPERTS, T), F32)],
        compiler_params=_cp("arbitrary"),
        name="route",
    )(logits_t, tri)


GATHER_WIN = LANE // 2
COMBINE_WIN = LANE // 2


def slot_ranges(slot):
    s = slot.reshape(BATCH, N_EXPERTS, NT, TILE)
    has = s >= 0
    smax = jnp.max(jnp.where(has, s, -1), axis=-1)
    smin = jnp.where(smax >= 0, jnp.min(jnp.where(has, s, CAP_LAT + CAP_CTX), axis=-1), 0)
    return smin.reshape(-1).astype(I32), smax.reshape(-1).astype(I32)


def _gather_kernel(smin_ref, smax_ref, h_ref, slot_ref, xs_ref, *, nslot):
    b = pl.program_id(0)
    t = pl.program_id(1)
    group = N_EXPERTS
    sid = lax.broadcasted_iota(I32, (GATHER_WIN, TILE), 0)
    align = 2 * SUB

    def onehot(e, ws, lo):
        srow = slot_ref[0, e:e + 1, :]
        return jnp.where((sid + ws == srow) & (srow >= lo), 1.0, 0.0).astype(BF16)

    def add_rows(e, ws, rows):
        win = pl.ds(pl.multiple_of(ws, align), GATHER_WIN)
        xs_ref[0, e, win, :] = (xs_ref[0, e, win, :].astype(F32) + rows).astype(BF16)

    @pl.when(t == 0)
    def _():
        for e in range(N_EXPERTS):
            xs_ref[0, e, 0:CAP_LAT, :] = jnp.zeros((CAP_LAT, D_MODEL), BF16)
            if nslot > CAP_LAT:
                cid = lax.broadcasted_iota(I32, (nslot - CAP_LAT, TILE), 0) + CAP_LAT
                pick = jnp.where(cid == slot_ref[0, e:e + 1, :], 1.0, 0.0).astype(BF16)
                xs_ref[0, e, CAP_LAT:nslot, :] = _dot(pick, h_ref[0]).astype(BF16)

    @pl.when(t > 0)
    def _():
        los, wss, extras = [], [], []
        for e in range(N_EXPERTS):
            base = (b * N_EXPERTS + e) * NT + t
            lo = smin_ref[base] & ~(align - 1)
            los.append(lo)
            wss.append(jnp.minimum(lo, CAP_LAT - GATHER_WIN))
            extras.append(lax.shift_right_arithmetic(smax_ref[base] - lo, GATHER_WIN.bit_length() - 1))
        for g0 in range(0, N_EXPERTS, group):
            pick = jnp.concatenate([onehot(e, wss[e], los[e]) for e in range(g0, g0 + group)], axis=0)
            rows = _dot(pick, h_ref[0])
            for i, e in enumerate(range(g0, g0 + group)):
                add_rows(e, wss[e], rows[i * GATHER_WIN:(i + 1) * GATHER_WIN, :])
        most = extras[0]
        for x in extras[1:]:
            most = jnp.maximum(most, x)

        @pl.when(most > 0)
        def _():
            for e in range(N_EXPERTS):
                def more(k, carry, e=e):
                    lo_k = los[e] + k * GATHER_WIN
                    ws = jnp.minimum(lo_k, CAP_LAT - GATHER_WIN)
                    add_rows(e, ws, _dot(onehot(e, ws, lo_k), h_ref[0]))
                    return carry

                lax.fori_loop(1, extras[e] + 1, more, 0)


def moe_gather(h, slot, smin, smax, nslot):
    return pl.pallas_call(
        functools.partial(_gather_kernel, nslot=nslot),
        grid_spec=pltpu.PrefetchScalarGridSpec(
            num_scalar_prefetch=2,
            grid=(BATCH, NT),
            in_specs=[pl.BlockSpec((1, TILE, D_MODEL), lambda b, t, *_: (b, t, 0)),
                      pl.BlockSpec((1, N_EXPERTS, TILE), lambda b, t, *_: (b, 0, t))],
            out_specs=pl.BlockSpec((1, N_EXPERTS, nslot, D_MODEL), lambda b, t, *_: (b, 0, 0, 0))),
        out_shape=jax.ShapeDtypeStruct((BATCH, N_EXPERTS, nslot, D_MODEL), BF16),
        compiler_params=_cp("arbitrary", "arbitrary"),
        name="moe_gather",
    )(smin, smax, h, slot)


def _ffn_kernel(xs_ref, wg_ref, wu_ref, wd_ref, y_ref, w_ref):
    @pl.when(pl.program_id(1) == 0)
    def _():
        w_ref[0] = wg_ref[0, 0].astype(BF16)
        w_ref[1] = wu_ref[0, 0].astype(BF16)
        w_ref[2] = wd_ref[0, 0].astype(BF16)

    xs = xs_ref[0, 0]
    hid = _silu(_dot(xs, w_ref[0])) * _dot(xs, w_ref[1])
    y_ref[0, 0] = _dot(hid.astype(BF16), w_ref[2]).astype(BF16)


def moe_ffn(xs, layer, wg, wu, wd, nslot):
    wspec = lambda: pl.BlockSpec((1, 1, D_MODEL, D_EXPERT), lambda e, b: (layer, e, 0, 0))
    rows = pl.BlockSpec((1, 1, nslot, D_MODEL), lambda e, b: (b, e, 0, 0))
    return pl.pallas_call(
        _ffn_kernel,
        grid=(N_EXPERTS, BATCH),
        in_specs=[rows, wspec(), wspec(), wspec()],
        out_specs=rows,
        out_shape=jax.ShapeDtypeStruct((BATCH, N_EXPERTS, nslot, D_MODEL), BF16),
        scratch_shapes=[pltpu.VMEM((3, D_MODEL, D_EXPERT), BF16)],
        compiler_params=_cp("arbitrary", "arbitrary"),
        name="moe_ffn",
    )(xs, wg, wu, wd)


def _combine_kernel(smin_ref, smax_ref, x_ref, mod_ref, slot_ref, aff_ref, y_ref, fw_ref, o_ref, acc_ref, *,
                    nslot, last):
    b = pl.program_id(0)
    t = pl.program_id(1)
    shift = COMBINE_WIN.bit_length() - 1
    align = 2 * SUB

    def weights(e, lane, lo=None):
        s = slot_ref[0, :, e:e + 1]
        hit = (s == lane) if lo is None else ((s == lane) & (s >= lo))
        return jnp.where(hit, aff_ref[0, :, e:e + 1], 0.0).astype(BF16)

    def finish(acc):
        x2 = x_ref[0] + mod_ref[0, 0, 5:6, :] * acc
        if last:
            x2 = x2 * lax.rsqrt(jnp.mean(x2 * x2, axis=-1, keepdims=True) + EPS) * fw_ref[...]
        o_ref[0] = x2

    if not last:
        @pl.when(t == 0)
        def _():
            lane = lax.broadcasted_iota(I32, (TILE, nslot - CAP_LAT), 1) + CAP_LAT
            acc = jnp.zeros((TILE, D_MODEL), F32)
            for e in range(N_EXPERTS):
                acc = acc + _dot(weights(e, lane), y_ref[0, e, CAP_LAT:nslot, :])
            finish(acc)

    @pl.when(t > 0)
    def _():
        lane = lax.broadcasted_iota(I32, (TILE, COMBINE_WIN), 1)
        lane2 = lax.broadcasted_iota(I32, (TILE, 2 * COMBINE_WIN), 1)
        first = lane2 < COMBINE_WIN
        los, wss, extras = [], [], []
        for e in range(N_EXPERTS):
            base = (b * N_EXPERTS + e) * NT + t
            lo = smin_ref[base] & ~(align - 1)
            los.append(lo)
            wss.append(pl.multiple_of(jnp.minimum(lo, CAP_LAT - COMBINE_WIN), align))
            extras.append(lax.shift_right_arithmetic(smax_ref[base] - lo, shift))
        w_parts, y_parts = [], []
        for e in range(0, N_EXPERTS, 2):
            want = jnp.where(first, slot_ref[0, :, e:e + 1] - wss[e],
                             slot_ref[0, :, e + 1:e + 2] - wss[e + 1] + COMBINE_WIN)
            gate = jnp.where(first, aff_ref[0, :, e:e + 1], aff_ref[0, :, e + 1:e + 2])
            w_parts.append(jnp.where(want == lane2, gate, 0.0).astype(BF16))
            y_parts += [y_ref[0, e, pl.ds(wss[e], COMBINE_WIN), :], y_ref[0, e + 1, pl.ds(wss[e + 1], COMBINE_WIN), :]]
        acc_ref[...] = _dot(jnp.concatenate(w_parts, axis=1), jnp.concatenate(y_parts, axis=0))
        most = extras[0]
        for x in extras[1:]:
            most = jnp.maximum(most, x)

        @pl.when(most > 0)
        def _():
            for e in range(N_EXPERTS):
                def more(k, carry, e=e):
                    lo_k = los[e] + k * COMBINE_WIN
                    ws = pl.multiple_of(jnp.minimum(lo_k, CAP_LAT - COMBINE_WIN), align)
                    acc_ref[...] += _dot(weights(e, lane + ws, lo_k), y_ref[0, e, pl.ds(ws, COMBINE_WIN), :])
                    return carry

                lax.fori_loop(1, extras[e] + 1, more, 0)

        finish(acc_ref[...])


def moe_combine(x1, mod, slot_tok, aff_tok, y, smin, smax, nslot, final_w):
    last = final_w is not None
    tspec = lambda w: pl.BlockSpec((1, TILE, w), lambda b, t, *_: (b, t, 0))
    if last:
        first_lat = CTX_LEN // TILE
        out_spec = pl.BlockSpec((1, TILE, D_MODEL), lambda b, t, *_: (b, jnp.maximum(t - first_lat, 0), 0))
        out_shape = jax.ShapeDtypeStruct((BATCH, SEQ, D_MODEL), F32)
        fw = final_w.astype(F32).reshape(1, D_MODEL)
    else:
        out_spec = tspec(D_MODEL)
        out_shape = jax.ShapeDtypeStruct((BATCH, T, D_MODEL), F32)
        fw = jnp.ones((1, D_MODEL), F32)
    return pl.pallas_call(
        functools.partial(_combine_kernel, nslot=nslot, last=last),
        grid_spec=pltpu.PrefetchScalarGridSpec(
            num_scalar_prefetch=2,
            grid=(BATCH, NT),
            in_specs=[tspec(D_MODEL),
                      pl.BlockSpec((1, 1, 6, D_MODEL), lambda b, t, *_: (b, _seg(t), 0, 0)),
                      tspec(N_EXPERTS), tspec(N_EXPERTS),
                      pl.BlockSpec((1, N_EXPERTS, nslot, D_MODEL), lambda b, t, *_: (b, 0, 0, 0)),
                      pl.BlockSpec((1, D_MODEL), lambda b, t, *_: (0, 0))],
            out_specs=out_spec,
            scratch_shapes=[pltpu.VMEM((TILE, D_MODEL), F32)]),
        out_shape=out_shape,
        compiler_params=_cp("arbitrary", "arbitrary"),
        name="moe_combine",
    )(smin, smax, x1, mod, slot_tok, aff_tok, y, fw)


def trunk_layer(l, stream, mod, cos_t, sin_t, norm1_w, norm2_w, w_in_p, w_out, s5_weights, s5_d, s5_glu_w, s5_glu_b,
                na_bias, ssd_conv_w, ssd_conv_b, ssd_dt_bias, ssd_a_log, ssd_d, ssd_norm_w,
                moe_router, wg, wu, wd, final_w):
    with_ctx_out = final_w is None
    u, q, k, v, z, xbc, dt_raw = in_proj(stream, mod, norm1_w, w_in_p, l)
    s5y = s5_mix(u, *s5_weights)
    na = natten(q, k, v, na_bias, l)
    xs, bm, cm, dt, a = ssd_prep(xbc, dt_raw, ssd_conv_w, ssd_conv_b, ssd_dt_bias, ssd_a_log, cos_t, sin_t)
    sdf, sdb = ssd_scan(xs, bm, cm, dt, a)

    x1, h, logits = post_mixer(stream, mod, u, s5y, na, xs, z, sdf, sdb, s5_d, s5_glu_w, s5_glu_b,
                               ssd_d, ssd_norm_w, w_out, norm2_w, moe_router)
    slot, aff = route(jnp.swapaxes(logits[:, :, :N_EXPERTS], 1, 2), with_ctx_out)
    nslot = CAP_LAT + CAP_CTX if with_ctx_out else CAP_LAT
    smin, smax = slot_ranges(slot)
    y = moe_ffn(moe_gather(h, slot, smin, smax, nslot), l, wg, wu, wd, nslot)
    return moe_combine(x1, mod, jnp.swapaxes(slot, 1, 2), jnp.swapaxes(aff, 1, 2), y, smin, smax, nslot, final_w)


def kernel(x, c, ctx, c_ctx, w_ada, b_ada, norm1_w, norm2_w, w_in, w_out, s5_lam_re, s5_lam_im, s5_log_dt, s5_b_re, s5_b_im, s5_c_re, s5_c_im, s5_d, s5_glu_w, s5_glu_b, na_rpb, ssd_conv_w, ssd_conv_b, ssd_dt_bias, ssd_a_log, ssd_d, ssd_norm_w, moe_router, moe_w_gate, moe_w_up, moe_w_down, final_norm_w):
    cvec = jnp.zeros((SUB, D_MODEL), F32).at[0].set(c_ctx.astype(F32)).at[1:1 + BATCH].set(c.astype(F32))
    mods = ada_mod(cvec, w_ada.astype(F32), b_ada.astype(F32)).reshape(DEPTH, SUB, 6, D_MODEL)
    mods = jnp.stack([jnp.broadcast_to(mods[:, 0:1], (DEPTH, BATCH, 6, D_MODEL)), mods[:, 1:1 + BATCH]], axis=2)
    cos_t, sin_t = rope_tables()
    w_in_p = jnp.zeros((DEPTH, D_MODEL, IN_COLS_PAD), BF16).at[:, :, :IN_COLS].set(w_in.astype(BF16))
    s5_weights = [s5_params(s5_lam_re[l], s5_lam_im[l], s5_log_dt[l], s5_b_re[l], s5_b_im[l], s5_c_re[l], s5_c_im[l])
                  for l in range(DEPTH)]
    na_bias = natten_bias(na_rpb.reshape(DEPTH * NA_HEADS, 2 * NA_KH - 1, RPB_W))
    wg, wu, wd = moe_w_gate.astype(F32), moe_w_up.astype(F32), moe_w_down.astype(F32)
    stream = (ctx.astype(F32), x.astype(F32), CTX_LEN // TILE)
    for l in range(DEPTH):
        xa = trunk_layer(
            l, stream, mods[l], cos_t, sin_t, norm1_w[l], norm2_w[l], w_in_p, w_out[l],
            s5_weights[l], s5_d[l], s5_glu_w[l], s5_glu_b[l],
            na_bias, ssd_conv_w[l], ssd_conv_b[l], ssd_dt_bias[l], ssd_a_log[l], ssd_d[l], ssd_norm_w[l],
            moe_router[l], wg, wu, wd, final_norm_w if l == DEPTH - 1 else None)
        stream = (xa, xa, 0)
    return xa
```

```python
import functools
import math

import jax
import jax.numpy as jnp
import numpy as np
from jax import lax
from jax.experimental import pallas as pl
from jax.experimental.pallas import tpu as pltpu

F32 = jnp.float32
BF16 = jnp.bfloat16
I32 = jnp.int32

D_MODEL = 1024
BATCH = 4
SEQ = 4096
DEPTH = 2
GRID_W = 64
CTX_LEN = 256
EPS = 1e-6

S5_WIDTH = 256
S5_GROUP = 16
S5_NGROUPS = 16
S5_STATE = 64
S5_NSTATE = S5_NGROUPS * S5_STATE

NA_HEADS = 6
NA_HEAD_DIM = 64
NA_WIDTH = 384
NA_KH = 8
NA_KW = 16
NA_BAND = 12
RPB_W = 2 * NA_KW - 1

SSD_HEADS = 6
SSD_HEAD_DIM = 64
SSD_WIDTH = 384
SSD_NGROUPS = 2
SSD_STATE = 128
SSD_CONV = 5
SSD_BC = 256
SSD_XBC = 896

N_EXPERTS = 16
D_EXPERT = 1024
ROPE_BASE = 10000.0

T = CTX_LEN + SEQ
TILE = 256
NT = T // TILE
LANE = 128
SUB = 8
ROWS = SEQ // GRID_W
CAP_LAT = 2 * SEQ // N_EXPERTS
CAP_CTX = 2 * CTX_LEN // N_EXPERTS
NEG = -1e30

C_U = 0
C_Q = 256
C_K = 640
C_V = 1024
C_Z = 1408
C_XBC = 1792
C_DT = 2688
IN_COLS = 2700
IN_COLS_PAD = 2816

VMEM_LIMIT = 56 * 1024 * 1024


def _cp(*sem):
    return pltpu.CompilerParams(dimension_semantics=sem, vmem_limit_bytes=VMEM_LIMIT)


def _dot(a, b):
    return jnp.dot(a, b, preferred_element_type=F32)


def _dot_nt(a, b):
    return lax.dot_general(a, b, (((1,), (1,)), ((), ())), preferred_element_type=F32)


def _split3(x):
    hi = x.astype(BF16)
    r = x - hi.astype(F32)
    mid = r.astype(BF16)
    lo = (r - mid.astype(F32)).astype(BF16)
    return hi, mid, lo


def _dot_exact_rhs(a_bf16, b_f32):
    hi, mid, lo = _split3(b_f32)
    return _dot(a_bf16, hi) + _dot(a_bf16, mid) + _dot(a_bf16, lo)


def _dot_exact_lhs(a_f32, b_bf16):
    hi, mid, lo = _split3(a_f32)
    return _dot(hi, b_bf16) + _dot(mid, b_bf16) + _dot(lo, b_bf16)


def _dot_x3(a, b):
    ah = a.astype(BF16)
    al = (a - ah.astype(F32)).astype(BF16)
    bh = b.astype(BF16)
    bl = (b - bh.astype(F32)).astype(BF16)
    return _dot(ah, bh) + _dot(ah, bl) + _dot(al, bh)


def _silu(x):
    return x * jax.nn.sigmoid(x)


def _seg(t):
    return jnp.where(t >= CTX_LEN // TILE, 1, 0)


def _bwd_tile(i):
    return jnp.where(i == 0, 0, NT - i)


def _ada_kernel(c_ref, w_ref, b_ref, o_ref):
    s = _silu(c_ref[...])
    o_ref[0] = _dot_x3(s, w_ref[0]) + b_ref[0]


def ada_mod(cvec, w_ada, b_ada):
    nb = 1024
    return pl.pallas_call(
        _ada_kernel,
        grid=(DEPTH, 6 * D_MODEL // nb),
        in_specs=[pl.BlockSpec((SUB, D_MODEL), lambda l, j: (0, 0)),
                  pl.BlockSpec((1, D_MODEL, nb), lambda l, j: (l, 0, j)),
                  pl.BlockSpec((1, 1, nb), lambda l, j: (l, 0, j))],
        out_specs=pl.BlockSpec((1, SUB, nb), lambda l, j: (l, 0, j)),
        out_shape=jax.ShapeDtypeStruct((DEPTH, SUB, 6 * D_MODEL), F32),
        compiler_params=_cp("arbitrary", "arbitrary"),
        name="ada_mod",
    )(cvec, w_ada, b_ada.reshape(DEPTH, 1, 6 * D_MODEL))


def _modulated_norm(x, nw, shift, scale):
    y = x * lax.rsqrt(jnp.mean(x * x, axis=-1, keepdims=True) + EPS) * nw
    return y * (1.0 + scale) + shift


def _stream_specs(stream):
    _, _, off = stream
    return [pl.BlockSpec((1, TILE, D_MODEL), lambda b, t: (b, 0, 0)),
            pl.BlockSpec((1, TILE, D_MODEL), lambda b, t: (b, jnp.maximum(t - off, 0), 0))]


def _stream_tile(head_ref, tail_ref):
    return jnp.where(pl.program_id(1) == 0, head_ref[0], tail_ref[0])


def _inproj_kernel(xh_ref, xt_ref, mod_ref, nw_ref, w_ref, u_ref, q_ref, k_ref, v_ref, z_ref, xbc_ref, dt_ref):
    x = _stream_tile(xh_ref, xt_ref)
    h = _modulated_norm(x, nw_ref[...], mod_ref[0, 0, 0:1, :], mod_ref[0, 0, 1:2, :]).astype(BF16)

    def proj(lo, hi):
        return _dot(h, w_ref[0, :, lo:hi])

    for s in range(S5_WIDTH // LANE):
        u_ref[0, s] = proj(C_U + s * LANE, C_U + (s + 1) * LANE)
    q_ref[0] = (proj(C_Q, C_K) * (NA_HEAD_DIM ** -0.5)).astype(BF16)
    k_ref[0] = proj(C_K, C_V).astype(BF16)
    v_ref[0] = proj(C_V, C_Z).astype(BF16)
    z_ref[0] = proj(C_Z, C_XBC).astype(BF16)
    xbc_ref[0] = proj(C_XBC, C_DT).astype(BF16)
    dt_ref[0] = proj(C_DT, IN_COLS_PAD)


def in_proj(stream, mod, norm_w, w_in_p, layer):
    tok = lambda w, dt: jax.ShapeDtypeStruct((BATCH, T, w), dt)
    tspec = lambda w: pl.BlockSpec((1, TILE, w), lambda b, t: (b, t, 0))
    return pl.pallas_call(
        _inproj_kernel,
        grid=(BATCH, NT),
        in_specs=_stream_specs(stream) + [
                  pl.BlockSpec((1, 1, 6, D_MODEL), lambda b, t: (b, _seg(t), 0, 0)),
                  pl.BlockSpec((1, D_MODEL), lambda b, t: (0, 0)),
                  pl.BlockSpec((1, D_MODEL, IN_COLS_PAD), lambda b, t: (layer, 0, 0))],
        out_specs=[pl.BlockSpec((1, S5_WIDTH // LANE, TILE, LANE), lambda b, t: (b, 0, t, 0)),
                   tspec(NA_WIDTH), tspec(NA_WIDTH), tspec(NA_WIDTH),
                   tspec(SSD_WIDTH), tspec(SSD_XBC), tspec(LANE)],
        out_shape=[jax.ShapeDtypeStruct((BATCH, S5_WIDTH // LANE, T, LANE), F32),
                   tok(NA_WIDTH, BF16), tok(NA_WIDTH, BF16), tok(NA_WIDTH, BF16),
                   tok(SSD_WIDTH, BF16), tok(SSD_XBC, BF16), tok(LANE, F32)],
        compiler_params=_cp("arbitrary", "arbitrary"),
        name="in_proj",
    )(stream[0], stream[1], mod, norm_w.reshape(1, D_MODEL), w_in_p)


S5_BLK = SUB
NB = T // S5_BLK
NB_CTX = CTX_LEN // S5_BLK
S5_NPAIR = S5_NGROUPS // 2
S5_PW = 2 * S5_BLK * S5_GROUP


def _s5_kernel(u_ref, pin_ref, pout_ref, w1_ref, w2_ref, w3_ref, mul_ref, y_ref, ub_ref, yb_ref, st_ref):
    n = S5_NSTATE
    half = LANE
    per_half = S5_NPAIR // 2
    toks = [jnp.concatenate([u_ref[0, h, pl.ds(j, NB, stride=S5_BLK), :].astype(BF16) for j in range(S5_BLK)],
                            axis=1) for h in range(2)]
    for pp in range(S5_NPAIR):
        h, q = divmod(pp, per_half)
        ub_ref[:, pp * S5_PW:(pp + 1) * S5_PW] = _dot(toks[h], pin_ref[q]).astype(BF16)

    for d in range(2):
        for pp in range(S5_NPAIR):
            s = _dot(ub_ref[:, pp * S5_PW:(pp + 1) * S5_PW], w1_ref[d, pp])
            st_ref[d, :, pp * half:(pp + 1) * half] = s[:, :half]
            st_ref[d, :, n + pp * half:n + (pp + 1) * half] = s[:, half:]

    ngrp = NB // SUB
    nctx = NB_CTX // SUB
    rowid = lax.broadcasted_iota(I32, (SUB, n), 0)
    for d in range(2):
        def body(j, carry, d=d):
            cr, ci = carry
            r = j if d == 0 else jnp.where(j < nctx, nctx - 1 - j, ngrp - 1 + nctx - j)
            row = pl.multiple_of(r * SUB, SUB)
            re = st_ref[d, pl.ds(row, SUB), 0:n]
            im = st_ref[d, pl.ds(row, SUB), n:2 * n]
            for kk, sh in enumerate((1, 2, 4)):
                mr = mul_ref[d, kk * SUB:(kk + 1) * SUB, 0:n]
                mi = mul_ref[d, kk * SUB:(kk + 1) * SUB, n:2 * n]
                s = sh if d == 0 else SUB - sh
                sr = pltpu.roll(re, s, 0)
                si = pltpu.roll(im, s, 0)
                re, im = re + (mr * sr - mi * si), im + (mr * si + mi * sr)
            pr = mul_ref[d, 3 * SUB:4 * SUB, 0:n]
            pi = mul_ref[d, 3 * SUB:4 * SUB, n:2 * n]
            re, im = re + (pr * cr - pi * ci), im + (pr * ci + pi * cr)
            edge, last, sh = (0, SUB - 1, 1) if d == 0 else (SUB - 1, 0, SUB - 1)
            st_ref[d, pl.ds(row, SUB), 0:n] = jnp.where(rowid == edge, cr, pltpu.roll(re, sh, 0))
            st_ref[d, pl.ds(row, SUB), n:2 * n] = jnp.where(rowid == edge, ci, pltpu.roll(im, sh, 0))
            return re[last:last + 1, :], im[last:last + 1, :]

        zero = jnp.zeros((1, n), F32)
        lax.fori_loop(0, ngrp, body, (zero, zero), unroll=2)

    for pp in range(S5_NPAIR):
        up = ub_ref[:, pp * S5_PW:(pp + 1) * S5_PW]
        acc = None
        for d in range(2):
            enter = jnp.concatenate([st_ref[d, :, pp * half:(pp + 1) * half],
                                     st_ref[d, :, n + pp * half:n + (pp + 1) * half]], axis=1).astype(BF16)
            term = _dot(up, w2_ref[d, pp]) + _dot(enter, w3_ref[d, pp])
            acc = term if acc is None else acc + term
        yb_ref[:, pp * S5_PW:(pp + 1) * S5_PW] = acc.astype(BF16)

    for m in range(S5_BLK // 2):
        for h in range(2):
            acc = None
            for q in range(per_half):
                pp = h * per_half + q
                term = _dot(yb_ref[:, pp * S5_PW:(pp + 1) * S5_PW], pout_ref[q, m])
                acc = term if acc is None else acc + term
            for k in range(2):
                y_ref[0, h, pl.ds(2 * m + k, NB, stride=S5_BLK), :] = acc[:, k * LANE:(k + 1) * LANE]


def _s5_regroup_matrices():
    per_half = S5_NPAIR // 2
    pin = np.zeros((per_half, S5_BLK, LANE, S5_PW), np.float32)
    for q in range(per_half):
        for j in range(S5_BLK):
            for gg in range(2):
                for c in range(S5_GROUP):
                    pin[q, j, (2 * q + gg) * S5_GROUP + c, gg * S5_BLK * S5_GROUP + j * S5_GROUP + c] = 1.0
    pout = pin.transpose(0, 1, 3, 2).reshape(per_half, S5_BLK // 2, 2, S5_PW, LANE)
    pout = pout.transpose(0, 1, 3, 2, 4).reshape(per_half, S5_BLK // 2, S5_PW, 2 * LANE)
    return jnp.asarray(pin.reshape(per_half, S5_BLK * LANE, S5_PW), BF16), jnp.asarray(pout, BF16)


def s5_mix(u, w1, w2, w3, mul):
    pin, pout = _s5_regroup_matrices()
    per_half = S5_NPAIR // 2
    wspec = pl.BlockSpec((2, S5_NPAIR, S5_PW, S5_PW), lambda b: (0, 0, 0, 0))
    tok = pl.BlockSpec((1, S5_WIDTH // LANE, T, LANE), lambda b: (b, 0, 0, 0))
    return pl.pallas_call(
        _s5_kernel,
        grid=(BATCH,),
        in_specs=[tok, pl.BlockSpec((per_half, S5_BLK * LANE, S5_PW), lambda b: (0, 0, 0)),
                  pl.BlockSpec((per_half, S5_BLK // 2, S5_PW, 2 * LANE), lambda b: (0, 0, 0, 0)), wspec, wspec, wspec,
                  pl.BlockSpec((2, 4 * SUB, 2 * S5_NSTATE), lambda b: (0, 0, 0))],
        out_specs=tok,
        out_shape=jax.ShapeDtypeStruct((BATCH, S5_WIDTH // LANE, T, LANE), F32),
        scratch_shapes=[pltpu.VMEM((NB, S5_NGROUPS * S5_BLK * S5_GROUP), BF16),
                        pltpu.VMEM((NB, S5_NGROUPS * S5_BLK * S5_GROUP), BF16),
                        pltpu.VMEM((2, NB, 2 * S5_NSTATE), F32)],
        compiler_params=_cp("arbitrary"),
        name="s5_mix",
    )(u, pin, pout, w1, w2, w3, mul)


def s5_params(lam_re, lam_im, log_dt, b_re, b_im, c_re, c_im):
    G, P, C = S5_NGROUPS, S5_STATE, S5_GROUP
    lam = lax.complex(lam_re.astype(F32), lam_im.astype(F32))
    step = jnp.exp(log_dt.astype(F32))[..., None]
    log_lb = lam * step
    lam_bar = jnp.exp(log_lb)
    b_bar = ((lam_bar - 1.0) / lam)[..., None] * lax.complex(b_re.astype(F32), b_im.astype(F32))
    J = S5_BLK
    row = lambda a: a.reshape(2, G, 1, P)
    col = lambda a: jnp.broadcast_to(a[..., None], (2, G, P, J * C))
    b_t = jnp.swapaxes(b_bar, 2, 3)
    c_t = lambda a: jnp.tile(jnp.swapaxes(a.astype(F32), 2, 3), (1, 1, 1, J))
    wspec = pl.BlockSpec((1, 1, S5_PW, S5_PW), lambda d, q: (d, q, 0, 0))
    pair = lambda r, s: pl.BlockSpec((1, 2, r, s), lambda d, q: (d, q, 0, 0))
    w1, w2, w3 = pl.pallas_call(
        _s5_weight_kernel,
        grid=(2, S5_NPAIR),
        in_specs=[pair(1, P), pair(1, P), pair(P, J * C), pair(P, J * C), pair(C, P), pair(C, P),
                  pair(P, J * C), pair(P, J * C)],
        out_specs=[wspec, wspec, wspec],
        out_shape=[jax.ShapeDtypeStruct((2, S5_NPAIR, S5_PW, S5_PW), BF16)] * 3,
        compiler_params=_cp("arbitrary", "arbitrary"),
        name="s5_weights",
    )(row(jnp.real(log_lb)), row(jnp.imag(log_lb)), col(jnp.real(log_lb)), col(jnp.imag(log_lb)),
      jnp.real(b_t), jnp.imag(b_t), c_t(c_re), c_t(c_im))
    rows = jnp.arange(SUB)
    pieces = []
    for d in range(2):
        log_blk = (log_lb[d] * float(J)).reshape(1, G * P)
        per_d = []
        for sh in (1, 2, 4):
            valid = (rows >= sh) if d == 0 else (rows < SUB - sh)
            per_d.append(jnp.where(valid[:, None], jnp.exp(log_blk * float(sh)), 0.0))
        expo = (rows + 1) if d == 0 else (SUB - rows)
        per_d.append(jnp.exp(log_blk * expo[:, None].astype(F32)))
        m = jnp.concatenate(per_d, axis=0)
        pieces.append(jnp.concatenate([jnp.real(m), jnp.imag(m)], axis=-1))
    mul = jnp.stack(pieces, axis=0).astype(F32)
    return w1, w2, w3, mul


def _s5_weight_kernel(llr_ref, lli_ref, lcr_ref, lci_ref, btr_ref, bti_ref, ctr_ref, cti_ref, w1_ref, w2_ref, w3_ref):
    J, C, P = S5_BLK, S5_GROUP, S5_STATE
    R = J * C
    fwd = pl.program_id(0) == 0
    shift = C.bit_length() - 1
    j_of_row = lax.shift_right_logical(lax.broadcasted_iota(I32, (R, P), 0), shift).astype(F32)
    i_of_col = lax.shift_right_logical(lax.broadcasted_iota(I32, (P, R), 1), shift).astype(F32)
    jr = lax.shift_right_logical(lax.broadcasted_iota(I32, (R, R), 0), shift)
    ic = lax.shift_right_logical(lax.broadcasted_iota(I32, (R, R), 1), shift)
    lag = jnp.where(fwd, ic - jr, jr - ic)

    def cpow(expo, lr, li):
        mag = jnp.exp(lr * expo)
        return mag * jnp.cos(li * expo), mag * jnp.sin(li * expo)

    w1_ref[...] = jnp.zeros_like(w1_ref)
    w2_ref[...] = jnp.zeros_like(w2_ref)
    w3_ref[...] = jnp.zeros_like(w3_ref)
    for gg in range(2):
        llr, lli = llr_ref[0, gg], lli_ref[0, gg]
        br = jnp.concatenate([btr_ref[0, gg]] * J, axis=0)
        bi = jnp.concatenate([bti_ref[0, gg]] * J, axis=0)
        cr, ci = ctr_ref[0, gg], cti_ref[0, gg]
        pr, pi = cpow(jnp.where(fwd, (J - 1) - j_of_row, j_of_row), llr, lli)
        w1_ref[0, 0, gg * R:(gg + 1) * R, gg * P:(gg + 1) * P] = (pr * br - pi * bi).astype(BF16)
        w1_ref[0, 0, gg * R:(gg + 1) * R, 2 * P + gg * P:2 * P + (gg + 1) * P] = (pr * bi + pi * br).astype(BF16)
        qr, qi = cpow(jnp.where(fwd, i_of_col + 1.0, J - i_of_col), lcr_ref[0, gg], lci_ref[0, gg])
        w3_ref[0, 0, gg * P:(gg + 1) * P, gg * R:(gg + 1) * R] = (cr * qr - ci * qi).astype(BF16)
        w3_ref[0, 0, 2 * P + gg * P:2 * P + (gg + 1) * P, gg * R:(gg + 1) * R] = (-(cr * qi + ci * qr)).astype(BF16)
        acc = jnp.zeros((R, R), F32)
        for k in range(J):
            lr, li = cpow(float(k), llr, lli)
            t = _dot_x3(br * lr - bi * li, cr) - _dot_x3(br * li + bi * lr, ci)
            acc = acc + jnp.where(lag == k, t, 0.0)
        w2_ref[0, 0, gg * R:(gg + 1) * R, gg * R:(gg + 1) * R] = acc.astype(BF16)


def _softmax_pv(parts):
    m = parts[0][0].max(axis=-1, keepdims=True)
    for s, _ in parts[1:]:
        m = jnp.maximum(m, s.max(axis=-1, keepdims=True))
    den = 0.0
    acc = 0.0
    for s, v in parts:
        p = jnp.exp(s - m)
        den = den + p.sum(axis=-1, keepdims=True)
        acc = acc + _dot(p.astype(BF16), v)
    return acc / den


def _na_kernel(q_ref, k_ref, v_ref, bias_ref, o_ref):
    t = pl.program_id(1)
    first = lax.broadcasted_iota(I32, (1, LANE), 1) < NA_HEAD_DIM

    def pair_scores(pp, start):
        ls = slice(pp * LANE, (pp + 1) * LANE)
        qp = q_ref[0, :, ls]
        kc = k_ref[0, 0:CTX_LEN, ls]
        vc = v_ref[0, 0:CTX_LEN, ls]
        scores = []
        for hh in range(2):
            qm = jnp.where(first if hh == 0 else ~first, qp, jnp.zeros_like(qp))
            parts = [(_dot_nt(qm, kc), vc)]
            if start is not None:
                kb = k_ref[0, pl.ds(start, NA_BAND * GRID_W), ls]
                vb = v_ref[0, pl.ds(start, NA_BAND * GRID_W), ls]
                parts.append((_dot_nt(qm, kb) + bias_ref[2 * pp + hh, 0], vb))
            scores.append(parts)
        return scores

    def attention(start):
        scores = [pair_scores(pp, start) for pp in range(NA_HEADS // 2)]
        for pp, (s0, s1) in enumerate(scores):
            o_ref[0, :, pp * LANE:(pp + 1) * LANE] = jnp.where(first, _softmax_pv(s0), _softmax_pv(s1)).astype(BF16)

    @pl.when(t == 0)
    def _():
        attention(None)

    @pl.when(t > 0)
    def _():
        first_row = (t - 1) * (TILE // GRID_W)
        u0 = jnp.clip(first_row - NA_KH // 2, 0, ROWS - NA_BAND)
        attention(pl.multiple_of(CTX_LEN + u0 * GRID_W, LANE))


def _na_cfg(t):
    return jnp.where(t <= 1, 0, jnp.where(t == NT - 1, 2, 1))


def natten(q, k, v, bias, layer):
    whole = pl.BlockSpec((1, T, NA_WIDTH), lambda b, t: (b, 0, 0))
    tile = pl.BlockSpec((1, TILE, NA_WIDTH), lambda b, t: (b, t, 0))
    return pl.pallas_call(
        _na_kernel,
        grid=(BATCH, NT),
        in_specs=[tile, whole, whole,
                  pl.BlockSpec((NA_HEADS, 1, TILE, NA_BAND * GRID_W), lambda b, t: (layer, _na_cfg(t), 0, 0))],
        out_specs=tile,
        out_shape=jax.ShapeDtypeStruct((BATCH, T, NA_WIDTH), BF16),
        compiler_params=_cp("arbitrary", "arbitrary"),
        name="natten",
    )(q, k, v, bias)


def natten_bias(rpb):
    col = jnp.arange(GRID_W)
    c0 = jnp.clip(col - NA_KW // 2, 0, GRID_W - NA_KW)
    in_win = (col[None, :] >= c0[:, None]) & (col[None, :] < c0[:, None] + NA_KW)
    rel_c = jnp.clip(col[None, :] - col[:, None] + (NA_KW - 1), 0, RPB_W - 1)
    pick_c = jax.nn.one_hot(rel_c, RPB_W, dtype=F32)
    blocks = jnp.einsum('hax,qkx->haqk', rpb.astype(F32), pick_c, precision=lax.Precision.HIGHEST)
    blocks = jnp.where(in_win[None, None], blocks, NEG)
    nh = rpb.shape[0]
    blocks = jnp.concatenate([blocks, jnp.full((nh, 1, GRID_W, GRID_W), NEG, F32)], axis=1)
    blocks = jnp.concatenate([blocks, blocks], axis=-1)
    return pl.pallas_call(
        _bias_kernel,
        grid=(nh, 3),
        in_specs=[pl.BlockSpec((1, 2 * NA_KH, GRID_W, LANE), lambda h, c: (h, 0, 0, 0))],
        out_specs=pl.BlockSpec((1, 1, TILE, NA_BAND * GRID_W), lambda h, c: (h, c, 0, 0)),
        out_shape=jax.ShapeDtypeStruct((nh, 3, TILE, NA_BAND * GRID_W), F32),
        compiler_params=_cp("arbitrary", "arbitrary"),
        name="natten_bias",
    )(blocks)


def _bias_kernel(blk_ref, o_ref):
    rows_per_tile = TILE // GRID_W
    masked = 2 * NA_KH - 1
    for cfg, first_row in enumerate((0, 2 * rows_per_tile, ROWS - rows_per_tile)):
        @pl.when(pl.program_id(1) == cfg)
        def _(first_row=first_row):
            u0 = min(max(first_row - NA_KH // 2, 0), ROWS - NA_BAND)
            for rr in range(rows_per_tile):
                r = first_row + rr
                r0 = min(max(r - NA_KH // 2, 0), ROWS - NA_KH)
                for j in range(NA_BAND):
                    kr = u0 + j
                    a = kr - r + NA_KH - 1 if r0 <= kr < r0 + NA_KH else masked
                    half = (j % 2) * GRID_W
                    o_ref[0, 0, rr * GRID_W:(rr + 1) * GRID_W, j * GRID_W:(j + 1) * GRID_W] = (
                        blk_ref[0, a, :, half:half + GRID_W])


def _softplus(x):
    return jnp.maximum(x, 0.0) + jnp.log(1.0 + jnp.exp(-jnp.abs(x)))


def _ssd_prep_kernel(prev_ref, cur_ref, next_ref, dtr_ref, cw_ref, cb_ref, dtb_ref, ar_ref, cos_ref, sin_ref,
                     xs_ref, bm_ref, cm_ref, dt_ref, a_ref):
    t = pl.program_id(1)
    halo = prev_ref.shape[1]
    has_prev = t >= 2
    has_next = (t >= 1) & (t <= NT - 2)
    prev = jnp.where(has_prev, prev_ref[0].astype(F32), 0.0)
    nxt = jnp.where(has_next, next_ref[0].astype(F32), 0.0)
    ext = jnp.concatenate([prev, cur_ref[0].astype(F32), nxt], axis=0)
    n = ext.shape[0]
    acc = cb_ref[...] + cw_ref[SSD_CONV // 2:SSD_CONV // 2 + 1, :] * ext
    for kk in range(SSD_CONV):
        off = kk - SSD_CONV // 2
        if off != 0:
            acc = acc + cw_ref[kk:kk + 1, :] * pltpu.roll(ext, (-off) % n, 0)
    y = _silu(acc[halo:halo + TILE, :])
    xs_ref[0] = y[:, 0:SSD_WIDTH].astype(BF16)

    lane = lax.broadcasted_iota(I32, (1, LANE), 1)
    low = (lane & (NA_HEAD_DIM // 2)) == 0
    cos = cos_ref[...]
    sin = sin_ref[...]
    for g in range(2 * SSD_NGROUPS):
        v = y[:, SSD_WIDTH + g * LANE:SSD_WIDTH + (g + 1) * LANE]
        sw = jnp.where(low, pltpu.roll(v, LANE - 32, 1), pltpu.roll(v, 32, 1))
        rot = (v * cos + sw * sin).astype(BF16)
        if g < SSD_NGROUPS:
            bm_ref[0, :, g * LANE:(g + 1) * LANE] = rot
        else:
            cm_ref[0, :, (g - SSD_NGROUPS) * LANE:(g - SSD_NGROUPS + 1) * LANE] = rot

    dt = _softplus(dtr_ref[0] + dtb_ref[...])
    dt_ref[0] = dt
    a_ref[0] = dt * ar_ref[...]


def ssd_prep(xbc, dt_raw, conv_w, conv_b, dt_bias, a_log, cos_t, sin_t):
    halo = 16
    per = TILE // halo
    nhalo = T // halo
    tok = lambda w, dt: jax.ShapeDtypeStruct((BATCH, T, w), dt)
    tspec = lambda w: pl.BlockSpec((1, TILE, w), lambda b, t: (b, t, 0))
    row = lambda w: pl.BlockSpec((1, w), lambda b, t: (0, 0))
    cw = jnp.zeros((SUB, SSD_XBC), F32).at[:SSD_CONV].set(conv_w.astype(F32))
    pad12 = lambda v: jnp.zeros((1, LANE), F32).at[0, :2 * SSD_HEADS].set(v.astype(F32).reshape(-1))
    return pl.pallas_call(
        _ssd_prep_kernel,
        grid=(BATCH, NT),
        in_specs=[pl.BlockSpec((1, halo, SSD_XBC), lambda b, t: (b, jnp.maximum(t * per - 1, 0), 0)),
                  tspec(SSD_XBC),
                  pl.BlockSpec((1, halo, SSD_XBC), lambda b, t: (b, jnp.minimum((t + 1) * per, nhalo - 1), 0)),
                  tspec(LANE),
                  pl.BlockSpec((SUB, SSD_XBC), lambda b, t: (0, 0)), row(SSD_XBC), row(LANE), row(LANE),
                  pl.BlockSpec((TILE, LANE), lambda b, t: (t, 0)), pl.BlockSpec((TILE, LANE), lambda b, t: (t, 0))],
        out_specs=[tspec(SSD_WIDTH), tspec(SSD_BC), tspec(SSD_BC), tspec(LANE), tspec(LANE)],
        out_shape=[tok(SSD_WIDTH, BF16), tok(SSD_BC, BF16), tok(SSD_BC, BF16), tok(LANE, F32), tok(LANE, F32)],
        compiler_params=_cp("arbitrary", "arbitrary"),
        name="ssd_prep",
    )(xbc, xbc, xbc, dt_raw, cw, conv_b.astype(F32).reshape(1, SSD_XBC), pad12(dt_bias),
      pad12(-jnp.exp(a_log.astype(F32))), cos_t, sin_t)


def rope_tables():
    half = SSD_STATE // 2
    nf = half // 2
    pos = jnp.arange(SEQ)
    inv_freq = ROPE_BASE ** (-jnp.arange(nf, dtype=F32) / nf)
    lane = jnp.arange(LANE)
    p = jnp.where(lane[None, :] < half, (pos // GRID_W)[:, None], (pos % GRID_W)[:, None]).astype(F32)
    ang = p * inv_freq[lane % nf][None, :]
    sign = jnp.where((lane & nf) == 0, -1.0, 1.0)[None, :]
    cos_t = jnp.concatenate([jnp.ones((CTX_LEN, LANE), F32), jnp.cos(ang)], axis=0)
    sin_t = jnp.concatenate([jnp.zeros((CTX_LEN, LANE), F32), jnp.sin(ang) * sign], axis=0)
    return cos_t, sin_t


def _ssd_sums(d, bm_ref, cm_ref, a_ref, at_ref, tri_ref):
    tri_col = tri_ref[d]
    tri_row = tri_ref[1 - d]
    cs_col = _dot_exact_rhs(tri_col, a_ref[0])
    cs_row = _dot_exact_lhs(at_ref[0], tri_row)
    g_mats = [_dot_nt(cm_ref[0, :, g * SSD_STATE:(g + 1) * SSD_STATE],
                      bm_ref[0, :, g * SSD_STATE:(g + 1) * SSD_STATE]) for g in range(SSD_NGROUPS)]
    return cs_col, cs_row, g_mats


def _ssd_dir(d, sums, xs_ref, cm_ref, bt_ref, dt_ref, y_ref, st_ref):
    q = TILE
    cs_col, cs_row, g_mats = sums
    lane = lax.broadcasted_iota(I32, (1, LANE), 1)
    first = lane < SSD_HEAD_DIM
    ri = lax.broadcasted_iota(I32, (q, q), 0)
    ci = lax.broadcasted_iota(I32, (q, q), 1)
    keep = (ci <= ri) if d == 0 else (ci >= ri)
    end = q - 1 if d == 0 else 0
    dt = dt_ref[0]

    def head_col(m, h):
        c = d * SSD_HEADS + h
        return m[:, c:c + 1]

    for pp in range(SSD_HEADS // 2):
        ls = slice(pp * LANE, (pp + 1) * LANE)
        h0, h1 = 2 * pp, 2 * pp + 1
        x = xs_ref[0, :, ls].astype(F32)
        dt_l = jnp.where(first, head_col(dt, h0), head_col(dt, h1))
        cs_l = jnp.where(first, head_col(cs_col, h0), head_col(cs_col, h1))
        cs_end = cs_l[end:end + 1, :]
        xdt = x * dt_l
        xdt_b = xdt.astype(BF16)
        xw = (xdt * jnp.exp(cs_end - cs_l)).astype(BF16)
        st = st_ref[d, pp]
        st_b = st.astype(BF16)
        ys, ups = [], []
        for h in (h0, h1):
            g = h // (SSD_HEADS // SSD_NGROUPS)
            c = d * SSD_HEADS + h
            diff = head_col(cs_col, h) - cs_row[c:c + 1, :]
            decay = jnp.exp(jnp.where(keep, diff, NEG))
            m = (g_mats[g] * decay).astype(BF16)
            y_h = _dot(m, xdt_b) + _dot(cm_ref[0, :, g * SSD_STATE:(g + 1) * SSD_STATE], st_b) * jnp.exp(cs_l)
            ys.append(y_h)
            ups.append(_dot(bt_ref[0, g * SSD_STATE:(g + 1) * SSD_STATE, :], xw))
        y_ref[0, :, ls] = jnp.where(first, ys[0], ys[1])
        st_ref[d, pp] = jnp.exp(cs_end) * st + jnp.where(first, ups[0], ups[1])


def _ssd_scan_kernel(xs_f, bm_f, cm_f, bt_f, dt_f, a_f, at_f, xs_b, bm_b, cm_b, bt_b, dt_b, a_b, at_b, tri_ref,
                     yf_ref, yb_ref, st_ref):
    @pl.when(pl.program_id(1) == 0)
    def _():
        st_ref[...] = jnp.zeros_like(st_ref)

    sums_f = _ssd_sums(0, bm_f, cm_f, a_f, at_f, tri_ref)
    sums_b = _ssd_sums(1, bm_b, cm_b, a_b, at_b, tri_ref)
    _ssd_dir(0, sums_f, xs_f, cm_f, bt_f, dt_f, yf_ref, st_ref)
    _ssd_dir(1, sums_b, xs_b, cm_b, bt_b, dt_b, yb_ref, st_ref)


def ssd_scan(xs, bm, cm, dt, a):
    bt = jnp.swapaxes(bm, 1, 2)
    at = jnp.swapaxes(a[:, :, :2 * SUB], 1, 2)
    idx = jnp.arange(TILE)
    tri = jnp.stack([idx[None, :] <= idx[:, None], idx[None, :] >= idx[:, None]]).astype(BF16)
    fwd = lambda b, i: (b, i, 0)
    bwd = lambda b, i: (b, _bwd_tile(i), 0)
    fwd_t = lambda b, i: (b, 0, i)
    bwd_t = lambda b, i: (b, 0, _bwd_tile(i))

    def specs(f, ft):
        return [pl.BlockSpec((1, TILE, SSD_WIDTH), f), pl.BlockSpec((1, TILE, SSD_BC), f),
                pl.BlockSpec((1, TILE, SSD_BC), f), pl.BlockSpec((1, SSD_BC, TILE), ft),
                pl.BlockSpec((1, TILE, LANE), f), pl.BlockSpec((1, TILE, LANE), f),
                pl.BlockSpec((1, 2 * SUB, TILE), ft)]

    args = (xs, bm, cm, bt, dt, a, at)
    return pl.pallas_call(
        _ssd_scan_kernel,
        grid=(BATCH, NT),
        in_specs=specs(fwd, fwd_t) + specs(bwd, bwd_t) + [pl.BlockSpec((2, TILE, TILE), lambda b, i: (0, 0, 0))],
        out_specs=[pl.BlockSpec((1, TILE, SSD_WIDTH), fwd), pl.BlockSpec((1, TILE, SSD_WIDTH), bwd)],
        out_shape=[jax.ShapeDtypeStruct((BATCH, T, SSD_WIDTH), F32)] * 2,
        scratch_shapes=[pltpu.VMEM((2, SSD_HEADS // 2, SSD_STATE, LANE), F32)],
        compiler_params=_cp("arbitrary", "arbitrary"),
        name="ssd_scan",
    )(*args, *args, tri)


def _gelu_tanh(x):
    return 0.5 * x * (1.0 + jnp.tanh(math.sqrt(2.0 / math.pi) * (x + 0.044715 * (x * x * x))))


def _post_kernel(xh_ref, xt_ref, mod_ref, u_ref, s5y_ref, na_ref, xs_ref, z_ref, sdf_ref, sdb_ref,
                 s5d_ref, gw_ref, gb_ref, sdd_ref, snw_ref, wo_ref, n2w_ref, rt_ref,
                 x1_ref, h_ref, lg_ref):
    slabs = lambda r: jnp.concatenate([r[0, h] for h in range(S5_WIDTH // LANE)], axis=-1)
    ys5 = slabs(u_ref) * s5d_ref[...] + slabs(s5y_ref)
    g = _gelu_tanh(ys5)
    s5o = g * jax.nn.sigmoid(_dot(g.astype(BF16), gw_ref[...]) + gb_ref[...])
    yssd = (xs_ref[0].astype(F32) * sdd_ref[...] + sdf_ref[0] + sdb_ref[0]) * _silu(z_ref[0].astype(F32))
    ssdo = yssd * lax.rsqrt(jnp.mean(yssd * yssd, axis=-1, keepdims=True) + EPS) * snw_ref[...]
    mix = jnp.concatenate([s5o.astype(BF16), na_ref[0], ssdo.astype(BF16)], axis=-1)
    x1 = _stream_tile(xh_ref, xt_ref) + mod_ref[0, 0, 2:3, :] * _dot(mix, wo_ref[...])
    x1_ref[0] = x1
    h = _modulated_norm(x1, n2w_ref[...], mod_ref[0, 0, 3:4, :], mod_ref[0, 0, 4:5, :])
    h_ref[0] = h.astype(BF16)
    lg_ref[0] = _dot_x3(h, rt_ref[...])


def post_mixer(stream, mod, u, s5y, na, xs, z, sdf, sdb, s5_d, glu_w, glu_b, ssd_d, ssd_norm_w, w_out, norm2_w,
               router):
    tspec = lambda w: pl.BlockSpec((1, TILE, w), lambda b, t: (b, t, 0))
    whole = lambda *shp: pl.BlockSpec(shp, lambda b, t: (0,) * len(shp))
    rt = jnp.zeros((D_MODEL, LANE), F32).at[:, :N_EXPERTS].set(router.astype(F32))
    slab = pl.BlockSpec((1, S5_WIDTH // LANE, TILE, LANE), lambda b, t: (b, 0, t, 0))
    return pl.pallas_call(
        _post_kernel,
        grid=(BATCH, NT),
        in_specs=_stream_specs(stream) + [
                  pl.BlockSpec((1, 1, 6, D_MODEL), lambda b, t: (b, _seg(t), 0, 0)),
                  slab, slab, tspec(NA_WIDTH),
                  tspec(SSD_WIDTH), tspec(SSD_WIDTH), tspec(SSD_WIDTH), tspec(SSD_WIDTH),
                  whole(1, S5_WIDTH), whole(S5_WIDTH, S5_WIDTH), whole(1, S5_WIDTH),
                  whole(1, SSD_WIDTH), whole(1, SSD_WIDTH), whole(D_MODEL, D_MODEL), whole(1, D_MODEL),
                  whole(D_MODEL, LANE)],
        out_specs=[tspec(D_MODEL), tspec(D_MODEL), tspec(LANE)],
        out_shape=[jax.ShapeDtypeStruct((BATCH, T, D_MODEL), F32), jax.ShapeDtypeStruct((BATCH, T, D_MODEL), BF16),
                   jax.ShapeDtypeStruct((BATCH, T, LANE), F32)],
        compiler_params=_cp("arbitrary", "arbitrary"),
        name="post_mixer",
    )(stream[0], stream[1], mod, u, s5y, na, xs, z, sdf, sdb,
      s5_d.astype(F32).reshape(1, S5_WIDTH), glu_w.astype(BF16), glu_b.astype(F32).reshape(1, S5_WIDTH),
      jnp.repeat(ssd_d.astype(F32), SSD_HEAD_DIM).reshape(1, SSD_WIDTH), ssd_norm_w.astype(F32).reshape(1, SSD_WIDTH),
      w_out.astype(BF16), norm2_w.astype(F32).reshape(1, D_MODEL), rt)


def _route_kernel(lg_ref, tri_ref, slot_ref, aff_ref, *, with_ctx):
    lg = lg_ref[0]
    m = lg.max(axis=0, keepdims=True)
    e = jnp.exp(lg - m)
    aff = e / e.sum(axis=0, keepdims=True)
    aff_ref[0] = aff
    bits = pltpu.bitcast(aff, I32)
    is_ctx = lax.broadcasted_iota(I32, (N_EXPERTS, T), 1) < CTX_LEN

    def count(mask):
        return jnp.where(mask, 1.0, 0.0).sum(axis=1, keepdims=True)

    def kth_largest(seg, k):
        def body(i, prefix):
            cand = prefix | lax.shift_left(jnp.int32(1), 30 - i)
            return jnp.where(count((bits >= cand) & seg) >= k, cand, prefix)
        return lax.fori_loop(0, 31, body, jnp.zeros((N_EXPERTS, 1), I32))

    def excl_cumsum(x01):
        carry = jnp.zeros((N_EXPERTS, 1), F32)
        pieces = []
        for j in range(T // LANE):
            blk = x01[:, j * LANE:(j + 1) * LANE]
            inc = _dot(blk.astype(BF16), tri_ref[...])
            pieces.append(inc - blk + carry)
            carry = carry + inc[:, LANE - 1:LANE]
        return jnp.concatenate(pieces, axis=1)

    thr = kth_largest(~is_ctx, float(CAP_LAT))
    k_of = jnp.full((N_EXPERTS, T), float(CAP_LAT), F32)
    if with_ctx:
        thr = jnp.where(is_ctx, kth_largest(is_ctx, float(CAP_CTX)), thr)
        k_of = jnp.where(is_ctx, float(CAP_CTX), k_of)
    gt = bits > thr
    eq = bits == thr
    if not with_ctx:
        gt = gt & ~is_ctx
        eq = eq & ~is_ctx
    n_gt = jnp.where(is_ctx, count(gt & is_ctx), count(gt & ~is_ctx))
    tie_rank = excl_cumsum(jnp.where(eq, 1.0, 0.0))
    tie_rank = tie_rank - jnp.where(is_ctx, 0.0, count(eq & is_ctx))
    sel = gt | (eq & (tie_rank < k_of - n_gt))
    pos = excl_cumsum(jnp.where(sel, 1.0, 0.0))
    slot = jnp.where(is_ctx, pos + float(CAP_LAT), pos - count(sel & is_ctx))
    slot_ref[0] = jnp.where(sel, slot, -1.0).astype(I32)


def route(logits_t, with_ctx):
    idx = jnp.arange(LANE)
    tri = (idx[:, None] <= idx[None, :]).astype(BF16)
    spec = pl.BlockSpec((1, N_EXPERTS, T), lambda b: (b, 0, 0))
    return pl.pallas_call(
        functools.partial(_route_kernel, with_ctx=with_ctx),
        grid=(BATCH,),
        in_specs=[spec, pl.BlockSpec((LANE, LANE), lambda b: (0, 0))],
        out_specs=[spec, spec],
        out_shape=[jax.ShapeDtypeStruct((BATCH, N_EXPERTS, T), I32), jax.ShapeDtypeStruct((BATCH, N_EXPERTS, T), F32)],
        compiler_params=_cp("arbitrary"),
        name="route",
    )(logits_t, tri)


GATHER_WIN = LANE // 2
COMBINE_WIN = LANE // 2


def slot_ranges(slot):
    s = slot.reshape(BATCH, N_EXPERTS, NT, TILE)
    has = s >= 0
    smax = jnp.max(jnp.where(has, s, -1), axis=-1)
    smin = jnp.where(smax >= 0, jnp.min(jnp.where(has, s, CAP_LAT + CAP_CTX), axis=-1), 0)
    return smin.reshape(-1).astype(I32), smax.reshape(-1).astype(I32)


def _gather_kernel(smin_ref, smax_ref, h_ref, slot_ref, xs_ref, *, nslot):
    b = pl.program_id(0)
    t = pl.program_id(1)
    group = N_EXPERTS
    sid = lax.broadcasted_iota(I32, (GATHER_WIN, TILE), 0)
    align = 2 * SUB

    def onehot(e, ws, lo):
        srow = slot_ref[0, e:e + 1, :]
        return jnp.where((sid + ws == srow) & (srow >= lo), 1.0, 0.0).astype(BF16)

    def add_rows(e, ws, rows):
        win = pl.ds(pl.multiple_of(ws, align), GATHER_WIN)
        xs_ref[0, e, win, :] = (xs_ref[0, e, win, :].astype(F32) + rows).astype(BF16)

    @pl.when(t == 0)
    def _():
        for e in range(N_EXPERTS):
            xs_ref[0, e, 0:CAP_LAT, :] = jnp.zeros((CAP_LAT, D_MODEL), BF16)
            if nslot > CAP_LAT:
                cid = lax.broadcasted_iota(I32, (nslot - CAP_LAT, TILE), 0) + CAP_LAT
                pick = jnp.where(cid == slot_ref[0, e:e + 1, :], 1.0, 0.0).astype(BF16)
                xs_ref[0, e, CAP_LAT:nslot, :] = _dot(pick, h_ref[0]).astype(BF16)

    @pl.when(t > 0)
    def _():
        los, wss, extras = [], [], []
        for e in range(N_EXPERTS):
            base = (b * N_EXPERTS + e) * NT + t
            lo = smin_ref[base] & ~(align - 1)
            los.append(lo)
            wss.append(jnp.minimum(lo, CAP_LAT - GATHER_WIN))
            extras.append(lax.shift_right_arithmetic(smax_ref[base] - lo, GATHER_WIN.bit_length() - 1))
        for g0 in range(0, N_EXPERTS, group):
            pick = jnp.concatenate([onehot(e, wss[e], los[e]) for e in range(g0, g0 + group)], axis=0)
            rows = _dot(pick, h_ref[0])
            for i, e in enumerate(range(g0, g0 + group)):
                add_rows(e, wss[e], rows[i * GATHER_WIN:(i + 1) * GATHER_WIN, :])
        most = extras[0]
        for x in extras[1:]:
            most = jnp.maximum(most, x)

        @pl.when(most > 0)
        def _():
            for e in range(N_EXPERTS):
                def more(k, carry, e=e):
                    lo_k = los[e] + k * GATHER_WIN
                    ws = jnp.minimum(lo_k, CAP_LAT - GATHER_WIN)
                    add_rows(e, ws, _dot(onehot(e, ws, lo_k), h_ref[0]))
                    return carry

                lax.fori_loop(1, extras[e] + 1, more, 0)


def moe_gather(h, slot, smin, smax, nslot):
    return pl.pallas_call(
        functools.partial(_gather_kernel, nslot=nslot),
        grid_spec=pltpu.PrefetchScalarGridSpec(
            num_scalar_prefetch=2,
            grid=(BATCH, NT),
            in_specs=[pl.BlockSpec((1, TILE, D_MODEL), lambda b, t, *_: (b, t, 0)),
                      pl.BlockSpec((1, N_EXPERTS, TILE), lambda b, t, *_: (b, 0, t))],
            out_specs=pl.BlockSpec((1, N_EXPERTS, nslot, D_MODEL), lambda b, t, *_: (b, 0, 0, 0))),
        out_shape=jax.ShapeDtypeStruct((BATCH, N_EXPERTS, nslot, D_MODEL), BF16),
        compiler_params=_cp("arbitrary", "arbitrary"),
        name="moe_gather",
    )(smin, smax, h, slot)


def _ffn_kernel(xs_ref, wg_ref, wu_ref, wd_ref, y_ref, w_ref):
    @pl.when(pl.program_id(1) == 0)
    def _():
        w_ref[0] = wg_ref[0, 0].astype(BF16)
        w_ref[1] = wu_ref[0, 0].astype(BF16)
        w_ref[2] = wd_ref[0, 0].astype(BF16)

    xs = xs_ref[0, 0]
    hid = _silu(_dot(xs, w_ref[0])) * _dot(xs, w_ref[1])
    y_ref[0, 0] = _dot(hid.astype(BF16), w_ref[2]).astype(BF16)


def moe_ffn(xs, layer, wg, wu, wd, nslot):
    wspec = lambda: pl.BlockSpec((1, 1, D_MODEL, D_EXPERT), lambda e, b: (layer, e, 0, 0))
    rows = pl.BlockSpec((1, 1, nslot, D_MODEL), lambda e, b: (b, e, 0, 0))
    return pl.pallas_call(
        _ffn_kernel,
        grid=(N_EXPERTS, BATCH),
        in_specs=[rows, wspec(), wspec(), wspec()],
        out_specs=rows,
        out_shape=jax.ShapeDtypeStruct((BATCH, N_EXPERTS, nslot, D_MODEL), BF16),
        scratch_shapes=[pltpu.VMEM((3, D_MODEL, D_EXPERT), BF16)],
        compiler_params=_cp("arbitrary", "arbitrary"),
        name="moe_ffn",
    )(xs, wg, wu, wd)


def _combine_kernel(smin_ref, smax_ref, x_ref, mod_ref, slot_ref, aff_ref, y_ref, fw_ref, o_ref, acc_ref, *,
                    nslot, last):
    b = pl.program_id(0)
    t = pl.program_id(1)
    shift = COMBINE_WIN.bit_length() - 1
    align = 2 * SUB

    def weights(e, lane, lo=None):
        s = slot_ref[0, :, e:e + 1]
        hit = (s == lane) if lo is None else ((s == lane) & (s >= lo))
        return jnp.where(hit, aff_ref[0, :, e:e + 1], 0.0).astype(BF16)

    def finish(acc):
        x2 = x_ref[0] + mod_ref[0, 0, 5:6, :] * acc
        if last:
            x2 = x2 * lax.rsqrt(jnp.mean(x2 * x2, axis=-1, keepdims=True) + EPS) * fw_ref[...]
        o_ref[0] = x2

    if not last:
        @pl.when(t == 0)
        def _():
            lane = lax.broadcasted_iota(I32, (TILE, nslot - CAP_LAT), 1) + CAP_LAT
            acc = jnp.zeros((TILE, D_MODEL), F32)
            for e in range(N_EXPERTS):
                acc = acc + _dot(weights(e, lane), y_ref[0, e, CAP_LAT:nslot, :])
            finish(acc)

    @pl.when(t > 0)
    def _():
        lane = lax.broadcasted_iota(I32, (TILE, COMBINE_WIN), 1)
        lane2 = lax.broadcasted_iota(I32, (TILE, 2 * COMBINE_WIN), 1)
        first = lane2 < COMBINE_WIN
        los, wss, extras = [], [], []
        for e in range(N_EXPERTS):
            base = (b * N_EXPERTS + e) * NT + t
            lo = smin_ref[base] & ~(align - 1)
            los.append(lo)
            wss.append(pl.multiple_of(jnp.minimum(lo, CAP_LAT - COMBINE_WIN), align))
            extras.append(lax.shift_right_arithmetic(smax_ref[base] - lo, shift))
        w_parts, y_parts = [], []
        for e in range(0, N_EXPERTS, 2):
            want = jnp.where(first, slot_ref[0, :, e:e + 1] - wss[e],
                             slot_ref[0, :, e + 1:e + 2] - wss[e + 1] + COMBINE_WIN)
            gate = jnp.where(first, aff_ref[0, :, e:e + 1], aff_ref[0, :, e + 1:e + 2])
            w_parts.append(jnp.where(want == lane2, gate, 0.0).astype(BF16))
            y_parts += [y_ref[0, e, pl.ds(wss[e], COMBINE_WIN), :], y_ref[0, e + 1, pl.ds(wss[e + 1], COMBINE_WIN), :]]
        acc_ref[...] = _dot(jnp.concatenate(w_parts, axis=1), jnp.concatenate(y_parts, axis=0))
        most = extras[0]
        for x in extras[1:]:
            most = jnp.maximum(most, x)

        @pl.when(most > 0)
        def _():
            for e in range(N_EXPERTS):
                def more(k, carry, e=e):
                    lo_k = los[e] + k * COMBINE_WIN
                    ws = pl.multiple_of(jnp.minimum(lo_k, CAP_LAT - COMBINE_WIN), align)
                    acc_ref[...] += _dot(weights(e, lane + ws, lo_k), y_ref[0, e, pl.ds(ws, COMBINE_WIN), :])
                    return carry

                lax.fori_loop(1, extras[e] + 1, more, 0)

        finish(acc_ref[...])


def moe_combine(x1, mod, slot_tok, aff_tok, y, smin, smax, nslot, final_w):
    last = final_w is not None
    tspec = lambda w: pl.BlockSpec((1, TILE, w), lambda b, t, *_: (b, t, 0))
    if last:
        first_lat = CTX_LEN // TILE
        out_spec = pl.BlockSpec((1, TILE, D_MODEL), lambda b, t, *_: (b, jnp.maximum(t - first_lat, 0), 0))
        out_shape = jax.ShapeDtypeStruct((BATCH, SEQ, D_MODEL), F32)
        fw = final_w.astype(F32).reshape(1, D_MODEL)
    else:
        out_spec = tspec(D_MODEL)
        out_shape = jax.ShapeDtypeStruct((BATCH, T, D_MODEL), F32)
        fw = jnp.ones((1, D_MODEL), F32)
    return pl.pallas_call(
        functools.partial(_combine_kernel, nslot=nslot, last=last),
        grid_spec=pltpu.PrefetchScalarGridSpec(
            num_scalar_prefetch=2,
            grid=(BATCH, NT),
            in_specs=[tspec(D_MODEL),
                      pl.BlockSpec((1, 1, 6, D_MODEL), lambda b, t, *_: (b, _seg(t), 0, 0)),
                      tspec(N_EXPERTS), tspec(N_EXPERTS),
                      pl.BlockSpec((1, N_EXPERTS, nslot, D_MODEL), lambda b, t, *_: (b, 0, 0, 0)),
                      pl.BlockSpec((1, D_MODEL), lambda b, t, *_: (0, 0))],
            out_specs=out_spec,
            scratch_shapes=[pltpu.VMEM((TILE, D_MODEL), F32)]),
        out_shape=out_shape,
        compiler_params=_cp("arbitrary", "arbitrary"),
        name="moe_combine",
    )(smin, smax, x1, mod, slot_tok, aff_tok, y, fw)


def trunk_layer(l, stream, mod, cos_t, sin_t, norm1_w, norm2_w, w_in_p, w_out, s5_weights, s5_d, s5_glu_w, s5_glu_b,
                na_bias, ssd_conv_w, ssd_conv_b, ssd_dt_bias, ssd_a_log, ssd_d, ssd_norm_w,
                moe_router, wg, wu, wd, final_w):
    with_ctx_out = final_w is None
    u, q, k, v, z, xbc, dt_raw = in_proj(stream, mod, norm1_w, w_in_p, l)
    s5y = s5_mix(u, *s5_weights)
    na = natten(q, k, v, na_bias, l)
    xs, bm, cm, dt, a = ssd_prep(xbc, dt_raw, ssd_conv_w, ssd_conv_b, ssd_dt_bias, ssd_a_log, cos_t, sin_t)
    sdf, sdb = ssd_scan(xs, bm, cm, dt, a)

    x1, h, logits = post_mixer(stream, mod, u, s5y, na, xs, z, sdf, sdb, s5_d, s5_glu_w, s5_glu_b,
                               ssd_d, ssd_norm_w, w_out, norm2_w, moe_router)
    slot, aff = route(jnp.swapaxes(logits[:, :, :N_EXPERTS], 1, 2), with_ctx_out)
    nslot = CAP_LAT + CAP_CTX if with_ctx_out else CAP_LAT
    smin, smax = slot_ranges(slot)
    y = moe_ffn(moe_gather(h, slot, smin, smax, nslot), l, wg, wu, wd, nslot)
    return moe_combine(x1, mod, jnp.swapaxes(slot, 1, 2), jnp.swapaxes(aff, 1, 2), y, smin, smax, nslot, final_w)


def kernel(x, c, ctx, c_ctx, w_ada, b_ada, norm1_w, norm2_w, w_in, w_out, s5_lam_re, s5_lam_im, s5_log_dt, s5_b_re, s5_b_im, s5_c_re, s5_c_im, s5_d, s5_glu_w, s5_glu_b, na_rpb, ssd_conv_w, ssd_conv_b, ssd_dt_bias, ssd_a_log, ssd_d, ssd_norm_w, moe_router, moe_w_gate, moe_w_up, moe_w_down, final_norm_w):
    cvec = jnp.zeros((SUB, D_MODEL), F32).at[0].set(c_ctx.astype(F32)).at[1:1 + BATCH].set(c.astype(F32))
    mods = ada_mod(cvec, w_ada.astype(F32), b_ada.astype(F32)).reshape(DEPTH, SUB, 6, D_MODEL)
    mods = jnp.stack([jnp.broadcast_to(mods[:, 0:1], (DEPTH, BATCH, 6, D_MODEL)), mods[:, 1:1 + BATCH]], axis=2)
    cos_t, sin_t = rope_tables()
    w_in_p = jnp.zeros((DEPTH, D_MODEL, IN_COLS_PAD), BF16).at[:, :, :IN_COLS].set(w_in.astype(BF16))
    s5_weights = [s5_params(s5_lam_re[l], s5_lam_im[l], s5_log_dt[l], s5_b_re[l], s5_b_im[l], s5_c_re[l], s5_c_im[l])
                  for l in range(DEPTH)]
    na_bias = natten_bias(na_rpb.reshape(DEPTH * NA_HEADS, 2 * NA_KH - 1, RPB_W))
    wg, wu, wd = moe_w_gate.astype(F32), moe_w_up.astype(F32), moe_w_down.astype(F32)
    stream = (ctx.astype(F32), x.astype(F32), CTX_LEN // TILE)
    for l in range(DEPTH):
        xa = trunk_layer(
            l, stream, mods[l], cos_t, sin_t, norm1_w[l], norm2_w[l], w_in_p, w_out[l],
            s5_weights[l], s5_d[l], s5_glu_w[l], s5_glu_b[l],
            na_bias, ssd_conv_w[l], ssd_conv_b[l], ssd_dt_bias[l], ssd_a_log[l], ssd_d[l], ssd_norm_w[l],
            moe_router[l], wg, wu, wd, final_norm_w if l == DEPTH - 1 else None)
        stream = (xa, xa, 0)
    return xa
```

```python
import functools
import math

import jax
import jax.numpy as jnp
import numpy as np
from jax import lax
from jax.experimental import pallas as pl
from jax.experimental.pallas import tpu as pltpu

F32 = jnp.float32
BF16 = jnp.bfloat16
I32 = jnp.int32

D_MODEL = 1024
BATCH = 4
SEQ = 4096
DEPTH = 2
GRID_W = 64
CTX_LEN = 256
EPS = 1e-6

S5_WIDTH = 256
S5_GROUP = 16
S5_NGROUPS = 16
S5_STATE = 64
S5_NSTATE = S5_NGROUPS * S5_STATE

NA_HEADS = 6
NA_HEAD_DIM = 64
NA_WIDTH = 384
NA_KH = 8
NA_KW = 16
NA_BAND = 12
RPB_W = 2 * NA_KW - 1

SSD_HEADS = 6
SSD_HEAD_DIM = 64
SSD_WIDTH = 384
SSD_NGROUPS = 2
SSD_STATE = 128
SSD_CONV = 5
SSD_BC = 256
SSD_XBC = 896

N_EXPERTS = 16
D_EXPERT = 1024
ROPE_BASE = 10000.0

T = CTX_LEN + SEQ
TILE = 256
NT = T // TILE
BS = 2
LANE = 128
SUB = 8
ROWS = SEQ // GRID_W
CAP_LAT = 2 * SEQ // N_EXPERTS
CAP_CTX = 2 * CTX_LEN // N_EXPERTS
NEG = -1e30

C_U = 0
C_Q = 256
C_K = 640
C_V = 1024
C_Z = 1408
C_XBC = 1792
C_DT = 2688
IN_COLS = 2700
IN_COLS_PAD = 2816

VMEM_LIMIT = 56 * 1024 * 1024


def _cp(*sem):
    return pltpu.CompilerParams(dimension_semantics=sem, vmem_limit_bytes=VMEM_LIMIT)


def _dot(a, b):
    return jnp.dot(a, b, preferred_element_type=F32)


def _dot_nt(a, b):
    return lax.dot_general(a, b, (((1,), (1,)), ((), ())), preferred_element_type=F32)


def _split3(x):
    hi = x.astype(BF16)
    r = x - hi.astype(F32)
    mid = r.astype(BF16)
    lo = (r - mid.astype(F32)).astype(BF16)
    return hi, mid, lo


def _dot_exact_rhs(a_bf16, b_f32):
    hi, mid, lo = _split3(b_f32)
    return _dot(a_bf16, hi) + _dot(a_bf16, mid) + _dot(a_bf16, lo)


def _dot_exact_lhs(a_f32, b_bf16):
    hi, mid, lo = _split3(a_f32)
    return _dot(hi, b_bf16) + _dot(mid, b_bf16) + _dot(lo, b_bf16)


def _dot_x3(a, b):
    ah = a.astype(BF16)
    al = (a - ah.astype(F32)).astype(BF16)
    bh = b.astype(BF16)
    bl = (b - bh.astype(F32)).astype(BF16)
    return _dot(ah, bh) + _dot(ah, bl) + _dot(al, bh)


def _silu(x):
    return x * jax.nn.sigmoid(x)


def _seg(t):
    return jnp.where(t >= CTX_LEN // TILE, 1, 0)


def _bwd_tile(i):
    return jnp.where(i == 0, 0, NT - i)


def _ada_kernel(c_ref, w_ref, b_ref, o_ref):
    s = _silu(c_ref[...])
    o_ref[0] = _dot_x3(s, w_ref[0]) + b_ref[0]


def ada_mod(cvec, w_ada, b_ada):
    nb = 1024
    return pl.pallas_call(
        _ada_kernel,
        grid=(DEPTH, 6 * D_MODEL // nb),
        in_specs=[pl.BlockSpec((SUB, D_MODEL), lambda l, j: (0, 0)),
                  pl.BlockSpec((1, D_MODEL, nb), lambda l, j: (l, 0, j)),
                  pl.BlockSpec((1, 1, nb), lambda l, j: (l, 0, j))],
        out_specs=pl.BlockSpec((1, SUB, nb), lambda l, j: (l, 0, j)),
        out_shape=jax.ShapeDtypeStruct((DEPTH, SUB, 6 * D_MODEL), F32),
        compiler_params=_cp("arbitrary", "arbitrary"),
        name="ada_mod",
    )(cvec, w_ada, b_ada.reshape(DEPTH, 1, 6 * D_MODEL))


def _modulated_norm(x, nw, shift, scale):
    y = x * lax.rsqrt(jnp.mean(x * x, axis=-1, keepdims=True) + EPS) * nw
    return y * (1.0 + scale) + shift


def _stream_specs(stream):
    _, _, off = stream
    return [pl.BlockSpec((BS, TILE, D_MODEL), lambda b, t: (b, 0, 0)),
            pl.BlockSpec((BS, TILE, D_MODEL), lambda b, t: (b, jnp.maximum(t - off, 0), 0))]


def _stream_tile(head_ref, tail_ref, bb):
    return jnp.where(pl.program_id(1) == 0, head_ref[bb], tail_ref[bb])


def _put_rows(ref, val, cast=None):
    for bb in range(BS):
        rows = val[bb * TILE:(bb + 1) * TILE]
        ref[bb] = rows if cast is None else rows.astype(cast)


def _inproj_kernel(xh_ref, xt_ref, mod_ref, nw_ref, w_ref, u_ref, q_ref, k_ref, v_ref, z_ref, xbc_ref, dt_ref):
    h = jnp.concatenate([_modulated_norm(_stream_tile(xh_ref, xt_ref, bb), nw_ref[...], mod_ref[bb, 0, 0:1, :],
                                         mod_ref[bb, 0, 1:2, :]).astype(BF16) for bb in range(BS)], axis=0)

    def proj(lo, hi):
        return _dot(h, w_ref[0, :, lo:hi])

    for s in range(S5_WIDTH // LANE):
        us = proj(C_U + s * LANE, C_U + (s + 1) * LANE)
        for bb in range(BS):
            u_ref[bb, s] = us[bb * TILE:(bb + 1) * TILE]
    _put_rows(q_ref, proj(C_Q, C_K) * (NA_HEAD_DIM ** -0.5), BF16)
    _put_rows(k_ref, proj(C_K, C_V), BF16)
    _put_rows(v_ref, proj(C_V, C_Z), BF16)
    _put_rows(z_ref, proj(C_Z, C_XBC), BF16)
    _put_rows(xbc_ref, proj(C_XBC, C_DT), BF16)
    _put_rows(dt_ref, proj(C_DT, IN_COLS_PAD))


def in_proj(stream, mod, norm_w, w_in_p, layer):
    tok = lambda w, dt: jax.ShapeDtypeStruct((BATCH, T, w), dt)
    tspec = lambda w: pl.BlockSpec((BS, TILE, w), lambda b, t: (b, t, 0))
    return pl.pallas_call(
        _inproj_kernel,
        grid=(BATCH // BS, NT),
        in_specs=_stream_specs(stream) + [
                  pl.BlockSpec((BS, 1, 6, D_MODEL), lambda b, t: (b, _seg(t), 0, 0)),
                  pl.BlockSpec((1, D_MODEL), lambda b, t: (0, 0)),
                  pl.BlockSpec((1, D_MODEL, IN_COLS_PAD), lambda b, t: (layer, 0, 0))],
        out_specs=[pl.BlockSpec((BS, S5_WIDTH // LANE, TILE, LANE), lambda b, t: (b, 0, t, 0)),
                   tspec(NA_WIDTH), tspec(NA_WIDTH), tspec(NA_WIDTH),
                   tspec(SSD_WIDTH), tspec(SSD_XBC), tspec(LANE)],
        out_shape=[jax.ShapeDtypeStruct((BATCH, S5_WIDTH // LANE, T, LANE), F32),
                   tok(NA_WIDTH, BF16), tok(NA_WIDTH, BF16), tok(NA_WIDTH, BF16),
                   tok(SSD_WIDTH, BF16), tok(SSD_XBC, BF16), tok(LANE, F32)],
        compiler_params=_cp("arbitrary", "arbitrary"),
        name="in_proj",
    )(stream[0], stream[1], mod, norm_w.reshape(1, D_MODEL), w_in_p)


S5_BLK = SUB
NB = T // S5_BLK
NB_CTX = CTX_LEN // S5_BLK
S5_NPAIR = S5_NGROUPS // 2
S5_PW = 2 * S5_BLK * S5_GROUP


def _s5_kernel(u_ref, pin_ref, pout_ref, w1_ref, w2_ref, w3_ref, mul_ref, y_ref, ub_ref, yb_ref, st_ref):
    n = S5_NSTATE
    half = LANE
    per_half = S5_NPAIR // 2
    toks = [jnp.concatenate([u_ref[0, h, pl.ds(j, NB, stride=S5_BLK), :].astype(BF16) for j in range(S5_BLK)],
                            axis=1) for h in range(2)]
    for pp in range(S5_NPAIR):
        h, q = divmod(pp, per_half)
        ub_ref[:, pp * S5_PW:(pp + 1) * S5_PW] = _dot(toks[h], pin_ref[q]).astype(BF16)

    for d in range(2):
        for pp in range(S5_NPAIR):
            s = _dot(ub_ref[:, pp * S5_PW:(pp + 1) * S5_PW], w1_ref[d, pp])
            st_ref[d, :, pp * half:(pp + 1) * half] = s[:, :half]
            st_ref[d, :, n + pp * half:n + (pp + 1) * half] = s[:, half:]

    ngrp = NB // SUB
    nctx = NB_CTX // SUB
    rowid = lax.broadcasted_iota(I32, (SUB, n), 0)
    for d in range(2):
        def body(j, carry, d=d):
            cr, ci = carry
            r = j if d == 0 else jnp.where(j < nctx, nctx - 1 - j, ngrp - 1 + nctx - j)
            row = pl.multiple_of(r * SUB, SUB)
            re = st_ref[d, pl.ds(row, SUB), 0:n]
            im = st_ref[d, pl.ds(row, SUB), n:2 * n]
            for kk, sh in enumerate((1, 2, 4)):
                mr = mul_ref[d, kk * SUB:(kk + 1) * SUB, 0:n]
                mi = mul_ref[d, kk * SUB:(kk + 1) * SUB, n:2 * n]
                s = sh if d == 0 else SUB - sh
                sr = pltpu.roll(re, s, 0)
                si = pltpu.roll(im, s, 0)
                re, im = re + (mr * sr - mi * si), im + (mr * si + mi * sr)
            pr = mul_ref[d, 3 * SUB:4 * SUB, 0:n]
            pi = mul_ref[d, 3 * SUB:4 * SUB, n:2 * n]
            re, im = re + (pr * cr - pi * ci), im + (pr * ci + pi * cr)
            edge, last, sh = (0, SUB - 1, 1) if d == 0 else (SUB - 1, 0, SUB - 1)
            st_ref[d, pl.ds(row, SUB), 0:n] = jnp.where(rowid == edge, cr, pltpu.roll(re, sh, 0))
            st_ref[d, pl.ds(row, SUB), n:2 * n] = jnp.where(rowid == edge, ci, pltpu.roll(im, sh, 0))
            return re[last:last + 1, :], im[last:last + 1, :]

        zero = jnp.zeros((1, n), F32)
        lax.fori_loop(0, ngrp, body, (zero, zero), unroll=2)

    for pp in range(S5_NPAIR):
        up = ub_ref[:, pp * S5_PW:(pp + 1) * S5_PW]
        acc = None
        for d in range(2):
            enter = jnp.concatenate([st_ref[d, :, pp * half:(pp + 1) * half],
                                     st_ref[d, :, n + pp * half:n + (pp + 1) * half]], axis=1).astype(BF16)
            term = _dot(up, w2_ref[d, pp]) + _dot(enter, w3_ref[d, pp])
            acc = term if acc is None else acc + term
        yb_ref[:, pp * S5_PW:(pp + 1) * S5_PW] = acc.astype(BF16)

    for m in range(S5_BLK // 2):
        for h in range(2):
            acc = None
            for q in range(per_half):
                pp = h * per_half + q
                term = _dot(yb_ref[:, pp * S5_PW:(pp + 1) * S5_PW], pout_ref[q, m])
                acc = term if acc is None else acc + term
            for k in range(2):
                y_ref[0, h, pl.ds(2 * m + k, NB, stride=S5_BLK), :] = acc[:, k * LANE:(k + 1) * LANE]


def _s5_regroup_matrices():
    per_half = S5_NPAIR // 2
    pin = np.zeros((per_half, S5_BLK, LANE, S5_PW), np.float32)
    for q in range(per_half):
        for j in range(S5_BLK):
            for gg in range(2):
                for c in range(S5_GROUP):
                    pin[q, j, (2 * q + gg) * S5_GROUP + c, gg * S5_BLK * S5_GROUP + j * S5_GROUP + c] = 1.0
    pout = pin.transpose(0, 1, 3, 2).reshape(per_half, S5_BLK // 2, 2, S5_PW, LANE)
    pout = pout.transpose(0, 1, 3, 2, 4).reshape(per_half, S5_BLK // 2, S5_PW, 2 * LANE)
    return jnp.asarray(pin.reshape(per_half, S5_BLK * LANE, S5_PW), BF16), jnp.asarray(pout, BF16)


def s5_mix(u, w1, w2, w3, mul):
    pin, pout = _s5_regroup_matrices()
    per_half = S5_NPAIR // 2
    wspec = pl.BlockSpec((2, S5_NPAIR, S5_PW, S5_PW), lambda b: (0, 0, 0, 0))
    tok = pl.BlockSpec((1, S5_WIDTH // LANE, T, LANE), lambda b: (b, 0, 0, 0))
    return pl.pallas_call(
        _s5_kernel,
        grid=(BATCH,),
        in_specs=[tok, pl.BlockSpec((per_half, S5_BLK * LANE, S5_PW), lambda b: (0, 0, 0)),
                  pl.BlockSpec((per_half, S5_BLK // 2, S5_PW, 2 * LANE), lambda b: (0, 0, 0, 0)), wspec, wspec, wspec,
                  pl.BlockSpec((2, 4 * SUB, 2 * S5_NSTATE), lambda b: (0, 0, 0))],
        out_specs=tok,
        out_shape=jax.ShapeDtypeStruct((BATCH, S5_WIDTH // LANE, T, LANE), F32),
        scratch_shapes=[pltpu.VMEM((NB, S5_NGROUPS * S5_BLK * S5_GROUP), BF16),
                        pltpu.VMEM((NB, S5_NGROUPS * S5_BLK * S5_GROUP), BF16),
                        pltpu.VMEM((2, NB, 2 * S5_NSTATE), F32)],
        compiler_params=_cp("arbitrary"),
        name="s5_mix",
    )(u, pin, pout, w1, w2, w3, mul)


def s5_params(lam_re, lam_im, log_dt, b_re, b_im, c_re, c_im):
    G, P, C = S5_NGROUPS, S5_STATE, S5_GROUP
    lam = lax.complex(lam_re.astype(F32), lam_im.astype(F32))
    step = jnp.exp(log_dt.astype(F32))[..., None]
    log_lb = lam * step
    lam_bar = jnp.exp(log_lb)
    b_bar = ((lam_bar - 1.0) / lam)[..., None] * lax.complex(b_re.astype(F32), b_im.astype(F32))
    J = S5_BLK
    row = lambda a: a.reshape(2, G, 1, P)
    col = lambda a: jnp.broadcast_to(a[..., None], (2, G, P, J * C))
    b_t = jnp.swapaxes(b_bar, 2, 3)
    c_t = lambda a: jnp.tile(jnp.swapaxes(a.astype(F32), 2, 3), (1, 1, 1, J))
    wspec = pl.BlockSpec((1, 1, S5_PW, S5_PW), lambda d, q: (d, q, 0, 0))
    pair = lambda r, s: pl.BlockSpec((1, 2, r, s), lambda d, q: (d, q, 0, 0))
    w1, w2, w3 = pl.pallas_call(
        _s5_weight_kernel,
        grid=(2, S5_NPAIR),
        in_specs=[pair(1, P), pair(1, P), pair(P, J * C), pair(P, J * C), pair(C, P), pair(C, P),
                  pair(P, J * C), pair(P, J * C)],
        out_specs=[wspec, wspec, wspec],
        out_shape=[jax.ShapeDtypeStruct((2, S5_NPAIR, S5_PW, S5_PW), BF16)] * 3,
        compiler_params=_cp("arbitrary", "arbitrary"),
        name="s5_weights",
    )(row(jnp.real(log_lb)), row(jnp.imag(log_lb)), col(jnp.real(log_lb)), col(jnp.imag(log_lb)),
      jnp.real(b_t), jnp.imag(b_t), c_t(c_re), c_t(c_im))
    rows = jnp.arange(SUB)
    pieces = []
    for d in range(2):
        log_blk = (log_lb[d] * float(J)).reshape(1, G * P)
        per_d = []
        for sh in (1, 2, 4):
            valid = (rows >= sh) if d == 0 else (rows < SUB - sh)
            per_d.append(jnp.where(valid[:, None], jnp.exp(log_blk * float(sh)), 0.0))
        expo = (rows + 1) if d == 0 else (SUB - rows)
        per_d.append(jnp.exp(log_blk * expo[:, None].astype(F32)))
        m = jnp.concatenate(per_d, axis=0)
        pieces.append(jnp.concatenate([jnp.real(m), jnp.imag(m)], axis=-1))
    mul = jnp.stack(pieces, axis=0).astype(F32)
    return w1, w2, w3, mul


def _s5_weight_kernel(llr_ref, lli_ref, lcr_ref, lci_ref, btr_ref, bti_ref, ctr_ref, cti_ref, w1_ref, w2_ref, w3_ref):
    J, C, P = S5_BLK, S5_GROUP, S5_STATE
    R = J * C
    fwd = pl.program_id(0) == 0
    shift = C.bit_length() - 1
    j_of_row = lax.shift_right_logical(lax.broadcasted_iota(I32, (R, P), 0), shift).astype(F32)
    i_of_col = lax.shift_right_logical(lax.broadcasted_iota(I32, (P, R), 1), shift).astype(F32)
    jr = lax.shift_right_logical(lax.broadcasted_iota(I32, (R, R), 0), shift)
    ic = lax.shift_right_logical(lax.broadcasted_iota(I32, (R, R), 1), shift)
    lag = jnp.where(fwd, ic - jr, jr - ic)

    def cpow(expo, lr, li):
        mag = jnp.exp(lr * expo)
        return mag * jnp.cos(li * expo), mag * jnp.sin(li * expo)

    w1_ref[...] = jnp.zeros_like(w1_ref)
    w2_ref[...] = jnp.zeros_like(w2_ref)
    w3_ref[...] = jnp.zeros_like(w3_ref)
    for gg in range(2):
        llr, lli = llr_ref[0, gg], lli_ref[0, gg]
        br = jnp.concatenate([btr_ref[0, gg]] * J, axis=0)
        bi = jnp.concatenate([bti_ref[0, gg]] * J, axis=0)
        cr, ci = ctr_ref[0, gg], cti_ref[0, gg]
        pr, pi = cpow(jnp.where(fwd, (J - 1) - j_of_row, j_of_row), llr, lli)
        w1_ref[0, 0, gg * R:(gg + 1) * R, gg * P:(gg + 1) * P] = (pr * br - pi * bi).astype(BF16)
        w1_ref[0, 0, gg * R:(gg + 1) * R, 2 * P + gg * P:2 * P + (gg + 1) * P] = (pr * bi + pi * br).astype(BF16)
        qr, qi = cpow(jnp.where(fwd, i_of_col + 1.0, J - i_of_col), lcr_ref[0, gg], lci_ref[0, gg])
        w3_ref[0, 0, gg * P:(gg + 1) * P, gg * R:(gg + 1) * R] = (cr * qr - ci * qi).astype(BF16)
        w3_ref[0, 0, 2 * P + gg * P:2 * P + (gg + 1) * P, gg * R:(gg + 1) * R] = (-(cr * qi + ci * qr)).astype(BF16)
        acc = jnp.zeros((R, R), F32)
        for k in range(J):
            lr, li = cpow(float(k), llr, lli)
            t = _dot_x3(br * lr - bi * li, cr) - _dot_x3(br * li + bi * lr, ci)
            acc = acc + jnp.where(lag == k, t, 0.0)
        w2_ref[0, 0, gg * R:(gg + 1) * R, gg * R:(gg + 1) * R] = acc.astype(BF16)


def _softmax_pv(parts):
    m = parts[0][0].max(axis=-1, keepdims=True)
    for s, _ in parts[1:]:
        m = jnp.maximum(m, s.max(axis=-1, keepdims=True))
    den = 0.0
    acc = 0.0
    for s, v in parts:
        p = jnp.exp(s - m)
        den = den + p.sum(axis=-1, keepdims=True)
        acc = acc + _dot(p.astype(BF16), v)
    return acc / den


def _na_kernel(q_ref, k_ref, v_ref, bias_ref, o_ref):
    t = pl.program_id(1)
    first = lax.broadcasted_iota(I32, (1, LANE), 1) < NA_HEAD_DIM

    def pair_scores(bb, pp, start):
        ls = slice(pp * LANE, (pp + 1) * LANE)
        qp = q_ref[bb, :, ls]
        kc = k_ref[bb, 0:CTX_LEN, ls]
        vc = v_ref[bb, 0:CTX_LEN, ls]
        scores = []
        for hh in range(2):
            qm = jnp.where(first if hh == 0 else ~first, qp, jnp.zeros_like(qp))
            parts = [(_dot_nt(qm, kc), vc)]
            if start is not None:
                kb = k_ref[bb, pl.ds(start, NA_BAND * GRID_W), ls]
                vb = v_ref[bb, pl.ds(start, NA_BAND * GRID_W), ls]
                parts.append((_dot_nt(qm, kb) + bias_ref[2 * pp + hh, 0], vb))
            scores.append(parts)
        return scores

    def attention(start):
        for bb in range(BS):
            scores = [pair_scores(bb, pp, start) for pp in range(NA_HEADS // 2)]
            for pp, (s0, s1) in enumerate(scores):
                o_ref[bb, :, pp * LANE:(pp + 1) * LANE] = jnp.where(first, _softmax_pv(s0),
                                                                    _softmax_pv(s1)).astype(BF16)

    @pl.when(t == 0)
    def _():
        attention(None)

    @pl.when(t > 0)
    def _():
        first_row = (t - 1) * (TILE // GRID_W)
        u0 = jnp.clip(first_row - NA_KH // 2, 0, ROWS - NA_BAND)
        attention(pl.multiple_of(CTX_LEN + u0 * GRID_W, LANE))


def _na_cfg(t):
    return jnp.where(t <= 1, 0, jnp.where(t == NT - 1, 2, 1))


def natten(q, k, v, bias, layer):
    whole = pl.BlockSpec((BS, T, NA_WIDTH), lambda b, t: (b, 0, 0))
    tile = pl.BlockSpec((BS, TILE, NA_WIDTH), lambda b, t: (b, t, 0))
    return pl.pallas_call(
        _na_kernel,
        grid=(BATCH // BS, NT),
        in_specs=[tile, whole, whole,
                  pl.BlockSpec((NA_HEADS, 1, TILE, NA_BAND * GRID_W), lambda b, t: (layer, _na_cfg(t), 0, 0))],
        out_specs=tile,
        out_shape=jax.ShapeDtypeStruct((BATCH, T, NA_WIDTH), BF16),
        compiler_params=_cp("arbitrary", "arbitrary"),
        name="natten",
    )(q, k, v, bias)


def natten_bias(rpb):
    col = jnp.arange(GRID_W)
    c0 = jnp.clip(col - NA_KW // 2, 0, GRID_W - NA_KW)
    in_win = (col[None, :] >= c0[:, None]) & (col[None, :] < c0[:, None] + NA_KW)
    rel_c = jnp.clip(col[None, :] - col[:, None] + (NA_KW - 1), 0, RPB_W - 1)
    pick_c = jax.nn.one_hot(rel_c, RPB_W, dtype=F32)
    blocks = jnp.einsum('hax,qkx->haqk', rpb.astype(F32), pick_c, precision=lax.Precision.HIGHEST)
    blocks = jnp.where(in_win[None, None], blocks, NEG)
    nh = rpb.shape[0]
    blocks = jnp.concatenate([blocks, jnp.full((nh, 1, GRID_W, GRID_W), NEG, F32)], axis=1)
    blocks = jnp.concatenate([blocks, blocks], axis=-1)
    return pl.pallas_call(
        _bias_kernel,
        grid=(nh, 3),
        in_specs=[pl.BlockSpec((1, 2 * NA_KH, GRID_W, LANE), lambda h, c: (h, 0, 0, 0))],
        out_specs=pl.BlockSpec((1, 1, TILE, NA_BAND * GRID_W), lambda h, c: (h, c, 0, 0)),
        out_shape=jax.ShapeDtypeStruct((nh, 3, TILE, NA_BAND * GRID_W), F32),
        compiler_params=_cp("arbitrary", "arbitrary"),
        name="natten_bias",
    )(blocks)


def _bias_kernel(blk_ref, o_ref):
    rows_per_tile = TILE // GRID_W
    masked = 2 * NA_KH - 1
    for cfg, first_row in enumerate((0, 2 * rows_per_tile, ROWS - rows_per_tile)):
        @pl.when(pl.program_id(1) == cfg)
        def _(first_row=first_row):
            u0 = min(max(first_row - NA_KH // 2, 0), ROWS - NA_BAND)
            for rr in range(rows_per_tile):
                r = first_row + rr
                r0 = min(max(r - NA_KH // 2, 0), ROWS - NA_KH)
                for j in range(NA_BAND):
                    kr = u0 + j
                    a = kr - r + NA_KH - 1 if r0 <= kr < r0 + NA_KH else masked
                    half = (j % 2) * GRID_W
                    o_ref[0, 0, rr * GRID_W:(rr + 1) * GRID_W, j * GRID_W:(j + 1) * GRID_W] = (
                        blk_ref[0, a, :, half:half + GRID_W])


def _softplus(x):
    return jnp.maximum(x, 0.0) + jnp.log(1.0 + jnp.exp(-jnp.abs(x)))


def _ssd_prep_kernel(prev_ref, cur_ref, next_ref, dtr_ref, cw_ref, cb_ref, dtb_ref, ar_ref, cos_ref, sin_ref,
                     xs_ref, bm_ref, cm_ref, dt_ref, a_ref):
    for bb in range(BS):
        _ssd_prep_sample(bb, prev_ref, cur_ref, next_ref, dtr_ref, cw_ref, cb_ref, dtb_ref, ar_ref, cos_ref, sin_ref,
                         xs_ref, bm_ref, cm_ref, dt_ref, a_ref)


def _ssd_prep_sample(bb, prev_ref, cur_ref, next_ref, dtr_ref, cw_ref, cb_ref, dtb_ref, ar_ref, cos_ref, sin_ref,
                     xs_ref, bm_ref, cm_ref, dt_ref, a_ref):
    t = pl.program_id(1)
    halo = prev_ref.shape[1]
    has_prev = t >= 2
    has_next = (t >= 1) & (t <= NT - 2)
    prev = jnp.where(has_prev, prev_ref[bb].astype(F32), 0.0)
    nxt = jnp.where(has_next, next_ref[bb].astype(F32), 0.0)
    ext = jnp.concatenate([prev, cur_ref[bb].astype(F32), nxt], axis=0)
    n = ext.shape[0]
    acc = cb_ref[...] + cw_ref[SSD_CONV // 2:SSD_CONV // 2 + 1, :] * ext
    for kk in range(SSD_CONV):
        off = kk - SSD_CONV // 2
        if off != 0:
            acc = acc + cw_ref[kk:kk + 1, :] * pltpu.roll(ext, (-off) % n, 0)
    y = _silu(acc[halo:halo + TILE, :])
    xs_ref[bb] = y[:, 0:SSD_WIDTH].astype(BF16)

    lane = lax.broadcasted_iota(I32, (1, LANE), 1)
    low = (lane & (NA_HEAD_DIM // 2)) == 0
    cos = cos_ref[...]
    sin = sin_ref[...]
    for g in range(2 * SSD_NGROUPS):
        v = y[:, SSD_WIDTH + g * LANE:SSD_WIDTH + (g + 1) * LANE]
        sw = jnp.where(low, pltpu.roll(v, LANE - 32, 1), pltpu.roll(v, 32, 1))
        rot = (v * cos + sw * sin).astype(BF16)
        if g < SSD_NGROUPS:
            bm_ref[bb, :, g * LANE:(g + 1) * LANE] = rot
        else:
            cm_ref[bb, :, (g - SSD_NGROUPS) * LANE:(g - SSD_NGROUPS + 1) * LANE] = rot

    dt = _softplus(dtr_ref[bb] + dtb_ref[...])
    dt_ref[bb] = dt
    a_ref[bb] = dt * ar_ref[...]


def ssd_prep(xbc, dt_raw, conv_w, conv_b, dt_bias, a_log, cos_t, sin_t):
    halo = 16
    per = TILE // halo
    nhalo = T // halo
    tok = lambda w, dt: jax.ShapeDtypeStruct((BATCH, T, w), dt)
    tspec = lambda w: pl.BlockSpec((BS, TILE, w), lambda b, t: (b, t, 0))
    row = lambda w: pl.BlockSpec((1, w), lambda b, t: (0, 0))
    cw = jnp.zeros((SUB, SSD_XBC), F32).at[:SSD_CONV].set(conv_w.astype(F32))
    pad12 = lambda v: jnp.zeros((1, LANE), F32).at[0, :2 * SSD_HEADS].set(v.astype(F32).reshape(-1))
    return pl.pallas_call(
        _ssd_prep_kernel,
        grid=(BATCH // BS, NT),
        in_specs=[pl.BlockSpec((BS, halo, SSD_XBC), lambda b, t: (b, jnp.maximum(t * per - 1, 0), 0)),
                  tspec(SSD_XBC),
                  pl.BlockSpec((BS, halo, SSD_XBC), lambda b, t: (b, jnp.minimum((t + 1) * per, nhalo - 1), 0)),
                  tspec(LANE),
                  pl.BlockSpec((SUB, SSD_XBC), lambda b, t: (0, 0)), row(SSD_XBC), row(LANE), row(LANE),
                  pl.BlockSpec((TILE, LANE), lambda b, t: (t, 0)), pl.BlockSpec((TILE, LANE), lambda b, t: (t, 0))],
        out_specs=[tspec(SSD_WIDTH), tspec(SSD_BC), tspec(SSD_BC), tspec(LANE), tspec(LANE)],
        out_shape=[tok(SSD_WIDTH, BF16), tok(SSD_BC, BF16), tok(SSD_BC, BF16), tok(LANE, F32), tok(LANE, F32)],
        compiler_params=_cp("arbitrary", "arbitrary"),
        name="ssd_prep",
    )(xbc, xbc, xbc, dt_raw, cw, conv_b.astype(F32).reshape(1, SSD_XBC), pad12(dt_bias),
      pad12(-jnp.exp(a_log.astype(F32))), cos_t, sin_t)


def rope_tables():
    half = SSD_STATE // 2
    nf = half // 2
    pos = jnp.arange(SEQ)
    inv_freq = ROPE_BASE ** (-jnp.arange(nf, dtype=F32) / nf)
    lane = jnp.arange(LANE)
    p = jnp.where(lane[None, :] < half, (pos // GRID_W)[:, None], (pos % GRID_W)[:, None]).astype(F32)
    ang = p * inv_freq[lane % nf][None, :]
    sign = jnp.where((lane & nf) == 0, -1.0, 1.0)[None, :]
    cos_t = jnp.concatenate([jnp.ones((CTX_LEN, LANE), F32), jnp.cos(ang)], axis=0)
    sin_t = jnp.concatenate([jnp.zeros((CTX_LEN, LANE), F32), jnp.sin(ang) * sign], axis=0)
    return cos_t, sin_t


def _ssd_sums(d, bm_ref, cm_ref, a_ref, at_ref, tri_ref):
    tri_col = tri_ref[d]
    tri_row = tri_ref[1 - d]
    cs_col = _dot_exact_rhs(tri_col, a_ref[0])
    cs_row = _dot_exact_lhs(at_ref[0], tri_row)
    g_mats = [_dot_nt(cm_ref[0, :, g * SSD_STATE:(g + 1) * SSD_STATE],
                      bm_ref[0, :, g * SSD_STATE:(g + 1) * SSD_STATE]) for g in range(SSD_NGROUPS)]
    return cs_col, cs_row, g_mats


def _ssd_dir(d, sums, xs_ref, cm_ref, bt_ref, dt_ref, y_ref, st_ref):
    q = TILE
    cs_col, cs_row, g_mats = sums
    lane = lax.broadcasted_iota(I32, (1, LANE), 1)
    first = lane < SSD_HEAD_DIM
    ri = lax.broadcasted_iota(I32, (q, q), 0)
    ci = lax.broadcasted_iota(I32, (q, q), 1)
    keep = (ci <= ri) if d == 0 else (ci >= ri)
    end = q - 1 if d == 0 else 0
    dt = dt_ref[0]

    def head_col(m, h):
        c = d * SSD_HEADS + h
        return m[:, c:c + 1]

    for pp in range(SSD_HEADS // 2):
        ls = slice(pp * LANE, (pp + 1) * LANE)
        h0, h1 = 2 * pp, 2 * pp + 1
        x = xs_ref[0, :, ls].astype(F32)
        dt_l = jnp.where(first, head_col(dt, h0), head_col(dt, h1))
        cs_l = jnp.where(first, head_col(cs_col, h0), head_col(cs_col, h1))
        cs_end = cs_l[end:end + 1, :]
        xdt = x * dt_l
        xdt_b = xdt.astype(BF16)
        xw = (xdt * jnp.exp(cs_end - cs_l)).astype(BF16)
        st = st_ref[d, pp]
        st_b = st.astype(BF16)
        ys, ups = [], []
        for h in (h0, h1):
            g = h // (SSD_HEADS // SSD_NGROUPS)
            c = d * SSD_HEADS + h
            diff = head_col(cs_col, h) - cs_row[c:c + 1, :]
            decay = jnp.exp(jnp.where(keep, diff, NEG))
            m = (g_mats[g] * decay).astype(BF16)
            y_h = _dot(m, xdt_b) + _dot(cm_ref[0, :, g * SSD_STATE:(g + 1) * SSD_STATE], st_b) * jnp.exp(cs_l)
            ys.append(y_h)
            ups.append(_dot(bt_ref[0, g * SSD_STATE:(g + 1) * SSD_STATE, :], xw))
        y_ref[0, :, ls] = jnp.where(first, ys[0], ys[1])
        st_ref[d, pp] = jnp.exp(cs_end) * st + jnp.where(first, ups[0], ups[1])


def _ssd_scan_kernel(xs_f, bm_f, cm_f, bt_f, dt_f, a_f, at_f, xs_b, bm_b, cm_b, bt_b, dt_b, a_b, at_b, tri_ref,
                     yf_ref, yb_ref, st_ref):
    @pl.when(pl.program_id(1) == 0)
    def _():
        st_ref[...] = jnp.zeros_like(st_ref)

    for bb in range(BS):
        one = lambda *refs: [r.at[pl.ds(bb, 1)] for r in refs]
        xsf, bmf, cmf, btf, dtf, af, atf, yf = one(xs_f, bm_f, cm_f, bt_f, dt_f, a_f, at_f, yf_ref)
        xsb, bmb, cmb, btb, dtb, ab, atb, yb = one(xs_b, bm_b, cm_b, bt_b, dt_b, a_b, at_b, yb_ref)
        st = st_ref.at[bb]
        sums_f = _ssd_sums(0, bmf, cmf, af, atf, tri_ref)
        sums_b = _ssd_sums(1, bmb, cmb, ab, atb, tri_ref)
        _ssd_dir(0, sums_f, xsf, cmf, btf, dtf, yf, st)
        _ssd_dir(1, sums_b, xsb, cmb, btb, dtb, yb, st)


def ssd_scan(xs, bm, cm, dt, a):
    bt = jnp.swapaxes(bm, 1, 2)
    at = jnp.swapaxes(a[:, :, :2 * SUB], 1, 2)
    idx = jnp.arange(TILE)
    tri = jnp.stack([idx[None, :] <= idx[:, None], idx[None, :] >= idx[:, None]]).astype(BF16)
    fwd = lambda b, i: (b, i, 0)
    bwd = lambda b, i: (b, _bwd_tile(i), 0)
    fwd_t = lambda b, i: (b, 0, i)
    bwd_t = lambda b, i: (b, 0, _bwd_tile(i))

    def specs(f, ft):
        return [pl.BlockSpec((BS, TILE, SSD_WIDTH), f), pl.BlockSpec((BS, TILE, SSD_BC), f),
                pl.BlockSpec((BS, TILE, SSD_BC), f), pl.BlockSpec((BS, SSD_BC, TILE), ft),
                pl.BlockSpec((BS, TILE, LANE), f), pl.BlockSpec((BS, TILE, LANE), f),
                pl.BlockSpec((BS, 2 * SUB, TILE), ft)]

    args = (xs, bm, cm, bt, dt, a, at)
    return pl.pallas_call(
        _ssd_scan_kernel,
        grid=(BATCH // BS, NT),
        in_specs=specs(fwd, fwd_t) + specs(bwd, bwd_t) + [pl.BlockSpec((2, TILE, TILE), lambda b, i: (0, 0, 0))],
        out_specs=[pl.BlockSpec((BS, TILE, SSD_WIDTH), fwd), pl.BlockSpec((BS, TILE, SSD_WIDTH), bwd)],
        out_shape=[jax.ShapeDtypeStruct((BATCH, T, SSD_WIDTH), F32)] * 2,
        scratch_shapes=[pltpu.VMEM((BS, 2, SSD_HEADS // 2, SSD_STATE, LANE), F32)],
        compiler_params=_cp("arbitrary", "arbitrary"),
        name="ssd_scan",
    )(*args, *args, tri)


def _gelu_tanh(x):
    return 0.5 * x * (1.0 + jnp.tanh(math.sqrt(2.0 / math.pi) * (x + 0.044715 * (x * x * x))))


def _post_kernel(xh_ref, xt_ref, mod_ref, u_ref, s5y_ref, na_ref, xs_ref, z_ref, sdf_ref, sdb_ref,
                 s5d_ref, gw_ref, gb_ref, sdd_ref, snw_ref, wo_ref, n2w_ref, rt_ref,
                 x1_ref, h_ref, lg_ref):
    for bb in range(BS):
        slabs = lambda r: jnp.concatenate([r[bb, s] for s in range(S5_WIDTH // LANE)], axis=-1)
        ys5 = slabs(u_ref) * s5d_ref[...] + slabs(s5y_ref)
        g = _gelu_tanh(ys5)
        s5o = g * jax.nn.sigmoid(_dot(g.astype(BF16), gw_ref[...]) + gb_ref[...])
        yssd = (xs_ref[bb].astype(F32) * sdd_ref[...] + sdf_ref[bb] + sdb_ref[bb]) * _silu(z_ref[bb].astype(F32))
        ssdo = yssd * lax.rsqrt(jnp.mean(yssd * yssd, axis=-1, keepdims=True) + EPS) * snw_ref[...]
        mix = jnp.concatenate([s5o.astype(BF16), na_ref[bb], ssdo.astype(BF16)], axis=-1)
        x1 = _stream_tile(xh_ref, xt_ref, bb) + mod_ref[bb, 0, 2:3, :] * _dot(mix, wo_ref[...])
        x1_ref[bb] = x1
        h = _modulated_norm(x1, n2w_ref[...], mod_ref[bb, 0, 3:4, :], mod_ref[bb, 0, 4:5, :])
        h_ref[bb] = h.astype(BF16)
        lg_ref[bb] = _dot_x3(h, rt_ref[...])


def post_mixer(stream, mod, u, s5y, na, xs, z, sdf, sdb, s5_d, glu_w, glu_b, ssd_d, ssd_norm_w, w_out, norm2_w,
               router):
    tspec = lambda w: pl.BlockSpec((BS, TILE, w), lambda b, t: (b, t, 0))
    whole = lambda *shp: pl.BlockSpec(shp, lambda b, t: (0,) * len(shp))
    rt = jnp.zeros((D_MODEL, LANE), F32).at[:, :N_EXPERTS].set(router.astype(F32))
    slab = pl.BlockSpec((BS, S5_WIDTH // LANE, TILE, LANE), lambda b, t: (b, 0, t, 0))
    return pl.pallas_call(
        _post_kernel,
        grid=(BATCH // BS, NT),
        in_specs=_stream_specs(stream) + [
                  pl.BlockSpec((BS, 1, 6, D_MODEL), lambda b, t: (b, _seg(t), 0, 0)),
                  slab, slab, tspec(NA_WIDTH),
                  tspec(SSD_WIDTH), tspec(SSD_WIDTH), tspec(SSD_WIDTH), tspec(SSD_WIDTH),
                  whole(1, S5_WIDTH), whole(S5_WIDTH, S5_WIDTH), whole(1, S5_WIDTH),
                  whole(1, SSD_WIDTH), whole(1, SSD_WIDTH), whole(D_MODEL, D_MODEL), whole(1, D_MODEL),
                  whole(D_MODEL, LANE)],
        out_specs=[tspec(D_MODEL), tspec(D_MODEL), tspec(LANE)],
        out_shape=[jax.ShapeDtypeStruct((BATCH, T, D_MODEL), F32), jax.ShapeDtypeStruct((BATCH, T, D_MODEL), BF16),
                   jax.ShapeDtypeStruct((BATCH, T, LANE), F32)],
        compiler_params=_cp("arbitrary", "arbitrary"),
        name="post_mixer",
    )(stream[0], stream[1], mod, u, s5y, na, xs, z, sdf, sdb,
      s5_d.astype(F32).reshape(1, S5_WIDTH), glu_w.astype(BF16), glu_b.astype(F32).reshape(1, S5_WIDTH),
      jnp.repeat(ssd_d.astype(F32), SSD_HEAD_DIM).reshape(1, SSD_WIDTH), ssd_norm_w.astype(F32).reshape(1, SSD_WIDTH),
      w_out.astype(BF16), norm2_w.astype(F32).reshape(1, D_MODEL), rt)


def _route_kernel(lg_ref, tri_ref, slot_ref, aff_ref, *, with_ctx):
    lg = lg_ref[0]
    m = lg.max(axis=0, keepdims=True)
    e = jnp.exp(lg - m)
    aff = e / e.sum(axis=0, keepdims=True)
    aff_ref[0] = aff
    bits = pltpu.bitcast(aff, I32)
    is_ctx = lax.broadcasted_iota(I32, (N_EXPERTS, T), 1) < CTX_LEN

    def count(mask):
        return jnp.where(mask, 1.0, 0.0).sum(axis=1, keepdims=True)

    def kth_largest(seg, k):
        def body(i, prefix):
            cand = prefix | lax.shift_left(jnp.int32(1), 30 - i)
            return jnp.where(count((bits >= cand) & seg) >= k, cand, prefix)
        return lax.fori_loop(0, 31, body, jnp.zeros((N_EXPERTS, 1), I32))

    def excl_cumsum(x01):
        carry = jnp.zeros((N_EXPERTS, 1), F32)
        pieces = []
        for j in range(T // LANE):
            blk = x01[:, j * LANE:(j + 1) * LANE]
            inc = _dot(blk.astype(BF16), tri_ref[...])
            pieces.append(inc - blk + carry)
            carry = carry + inc[:, LANE - 1:LANE]
        return jnp.concatenate(pieces, axis=1)

    thr = kth_largest(~is_ctx, float(CAP_LAT))
    k_of = jnp.full((N_EXPERTS, T), float(CAP_LAT), F32)
    if with_ctx:
        thr = jnp.where(is_ctx, kth_largest(is_ctx, float(CAP_CTX)), thr)
        k_of = jnp.where(is_ctx, float(CAP_CTX), k_of)
    gt = bits > thr
    eq = bits == thr
    if not with_ctx:
        gt = gt & ~is_ctx
        eq = eq & ~is_ctx
    n_gt = jnp.where(is_ctx, count(gt & is_ctx), count(gt & ~is_ctx))
    tie_rank = excl_cumsum(jnp.where(eq, 1.0, 0.0))
    tie_rank = tie_rank - jnp.where(is_ctx, 0.0, count(eq & is_ctx))
    sel = gt | (eq & (tie_rank < k_of - n_gt))
    pos = excl_cumsum(jnp.where(sel, 1.0, 0.0))
    slot = jnp.where(is_ctx, pos + float(CAP_LAT), pos - count(sel & is_ctx))
    slot_ref[0] = jnp.where(sel, slot, -1.0).astype(I32)


def route(logits_t, with_ctx):
    idx = jnp.arange(LANE)
    tri = (idx[:, None] <= idx[None, :]).astype(BF16)
    spec = pl.BlockSpec((1, N_EXPERTS, T), lambda b: (b, 0, 0))
    return pl.pallas_call(
        functools.partial(_route_kernel, with_ctx=with_ctx),
        grid=(BATCH,),
        in_specs=[spec, pl.BlockSpec((LANE, LANE), lambda b: (0, 0))],
        out_specs=[spec, spec],
        out_shape=[jax.ShapeDtypeStruct((BATCH, N_EXPERTS, T), I32), jax.ShapeDtypeStruct((BATCH, N_EXPERTS, T), F32)],
        compiler_params=_cp("arbitrary"),
        name="route",
    )(logits_t, tri)


GATHER_WIN = LANE // 2
COMBINE_WIN = LANE // 2


def slot_ranges(slot):
    s = slot.reshape(BATCH, N_EXPERTS, NT, TILE)
    has = s >= 0
    smax = jnp.max(jnp.where(has, s, -1), axis=-1)
    smin = jnp.where(smax >= 0, jnp.min(jnp.where(has, s, CAP_LAT + CAP_CTX), axis=-1), 0)
    return smin.reshape(-1).astype(I32), smax.reshape(-1).astype(I32)


def _gather_kernel(smin_ref, smax_ref, h_ref, slot_ref, xs_ref, *, nslot):
    b = pl.program_id(0)
    t = pl.program_id(1)
    group = N_EXPERTS
    sid = lax.broadcasted_iota(I32, (GATHER_WIN, TILE), 0)
    align = 2 * SUB

    def onehot(e, ws, lo):
        srow = slot_ref[0, e:e + 1, :]
        return jnp.where((sid + ws == srow) & (srow >= lo), 1.0, 0.0).astype(BF16)

    def add_rows(e, ws, rows):
        win = pl.ds(pl.multiple_of(ws, align), GATHER_WIN)
        xs_ref[0, e, win, :] = (xs_ref[0, e, win, :].astype(F32) + rows).astype(BF16)

    @pl.when(t == 0)
    def _():
        for e in range(N_EXPERTS):
            xs_ref[0, e, 0:CAP_LAT, :] = jnp.zeros((CAP_LAT, D_MODEL), BF16)
            if nslot > CAP_LAT:
                cid = lax.broadcasted_iota(I32, (nslot - CAP_LAT, TILE), 0) + CAP_LAT
                pick = jnp.where(cid == slot_ref[0, e:e + 1, :], 1.0, 0.0).astype(BF16)
                xs_ref[0, e, CAP_LAT:nslot, :] = _dot(pick, h_ref[0]).astype(BF16)

    @pl.when(t > 0)
    def _():
        los, wss, extras = [], [], []
        for e in range(N_EXPERTS):
            base = (b * N_EXPERTS + e) * NT + t
            lo = smin_ref[base] & ~(align - 1)
            los.append(lo)
            wss.append(jnp.minimum(lo, CAP_LAT - GATHER_WIN))
            extras.append(lax.shift_right_arithmetic(smax_ref[base] - lo, GATHER_WIN.bit_length() - 1))
        for g0 in range(0, N_EXPERTS, group):
            pick = jnp.concatenate([onehot(e, wss[e], los[e]) for e in range(g0, g0 + group)], axis=0)
            rows = _dot(pick, h_ref[0])
            for i, e in enumerate(range(g0, g0 + group)):
                add_rows(e, wss[e], rows[i * GATHER_WIN:(i + 1) * GATHER_WIN, :])
        most = extras[0]
        for x in extras[1:]:
            most = jnp.maximum(most, x)

        @pl.when(most > 0)
        def _():
            for e in range(N_EXPERTS):
                def more(k, carry, e=e):
                    lo_k = los[e] + k * GATHER_WIN
                    ws = jnp.minimum(lo_k, CAP_LAT - GATHER_WIN)
                    add_rows(e, ws, _dot(onehot(e, ws, lo_k), h_ref[0]))
                    return carry

                lax.fori_loop(1, extras[e] + 1, more, 0)


def moe_gather(h, slot, smin, smax, nslot):
    return pl.pallas_call(
        functools.partial(_gather_kernel, nslot=nslot),
        grid_spec=pltpu.PrefetchScalarGridSpec(
            num_scalar_prefetch=2,
            grid=(BATCH, NT),
            in_specs=[pl.BlockSpec((1, TILE, D_MODEL), lambda b, t, *_: (b, t, 0)),
                      pl.BlockSpec((1, N_EXPERTS, TILE), lambda b, t, *_: (b, 0, t))],
            out_specs=pl.BlockSpec((1, N_EXPERTS, nslot, D_MODEL), lambda b, t, *_: (b, 0, 0, 0))),
        out_shape=jax.ShapeDtypeStruct((BATCH, N_EXPERTS, nslot, D_MODEL), BF16),
        compiler_params=_cp("arbitrary", "arbitrary"),
        name="moe_gather",
    )(smin, smax, h, slot)


def _ffn_kernel(xs_ref, wg_ref, wu_ref, wd_ref, y_ref, w_ref):
    @pl.when(pl.program_id(1) == 0)
    def _():
        w_ref[0] = wg_ref[0, 0].astype(BF16)
        w_ref[1] = wu_ref[0, 0].astype(BF16)
        w_ref[2] = wd_ref[0, 0].astype(BF16)

    xs = xs_ref[0, 0]
    hid = _silu(_dot(xs, w_ref[0])) * _dot(xs, w_ref[1])
    y_ref[0, 0] = _dot(hid.astype(BF16), w_ref[2]).astype(BF16)


def moe_ffn(xs, layer, wg, wu, wd, nslot):
    wspec = lambda: pl.BlockSpec((1, 1, D_MODEL, D_EXPERT), lambda e, b: (layer, e, 0, 0))
    rows = pl.BlockSpec((1, 1, nslot, D_MODEL), lambda e, b: (b, e, 0, 0))
    return pl.pallas_call(
        _ffn_kernel,
        grid=(N_EXPERTS, BATCH),
        in_specs=[rows, wspec(), wspec(), wspec()],
        out_specs=rows,
        out_shape=jax.ShapeDtypeStruct((BATCH, N_EXPERTS, nslot, D_MODEL), BF16),
        scratch_shapes=[pltpu.VMEM((3, D_MODEL, D_EXPERT), BF16)],
        compiler_params=_cp("arbitrary", "arbitrary"),
        name="moe_ffn",
    )(xs, wg, wu, wd)


def _combine_kernel(smin_ref, smax_ref, x_ref, mod_ref, slot_ref, aff_ref, y_ref, fw_ref, o_ref, acc_ref, *,
                    nslot, last):
    b = pl.program_id(0)
    t = pl.program_id(1)
    shift = COMBINE_WIN.bit_length() - 1
    align = 2 * SUB

    def weights(e, lane, lo=None):
        s = slot_ref[0, :, e:e + 1]
        hit = (s == lane) if lo is None else ((s == lane) & (s >= lo))
        return jnp.where(hit, aff_ref[0, :, e:e + 1], 0.0).astype(BF16)

    def finish(acc):
        x2 = x_ref[0] + mod_ref[0, 0, 5:6, :] * acc
        if last:
            x2 = x2 * lax.rsqrt(jnp.mean(x2 * x2, axis=-1, keepdims=True) + EPS) * fw_ref[...]
        o_ref[0] = x2

    if not last:
        @pl.when(t == 0)
        def _():
            lane = lax.broadcasted_iota(I32, (TILE, nslot - CAP_LAT), 1) + CAP_LAT
            acc = jnp.zeros((TILE, D_MODEL), F32)
            for e in range(N_EXPERTS):
                acc = acc + _dot(weights(e, lane), y_ref[0, e, CAP_LAT:nslot, :])
            finish(acc)

    @pl.when(t > 0)
    def _():
        lane = lax.broadcasted_iota(I32, (TILE, COMBINE_WIN), 1)
        lane2 = lax.broadcasted_iota(I32, (TILE, 2 * COMBINE_WIN), 1)
        first = lane2 < COMBINE_WIN
        los, wss, extras = [], [], []
        for e in range(N_EXPERTS):
            base = (b * N_EXPERTS + e) * NT + t
            lo = smin_ref[base] & ~(align - 1)
            los.append(lo)
            wss.append(pl.multiple_of(jnp.minimum(lo, CAP_LAT - COMBINE_WIN), align))
            extras.append(lax.shift_right_arithmetic(smax_ref[base] - lo, shift))
        w_parts, y_parts = [], []
        for e in range(0, N_EXPERTS, 2):
            want = jnp.where(first, slot_ref[0, :, e:e + 1] - wss[e],
                             slot_ref[0, :, e + 1:e + 2] - wss[e + 1] + COMBINE_WIN)
            gate = jnp.where(first, aff_ref[0, :, e:e + 1], aff_ref[0, :, e + 1:e + 2])
            w_parts.append(jnp.where(want == lane2, gate, 0.0).astype(BF16))
            y_parts += [y_ref[0, e, pl.ds(wss[e], COMBINE_WIN), :], y_ref[0, e + 1, pl.ds(wss[e + 1], COMBINE_WIN), :]]
        acc_ref[...] = _dot(jnp.concatenate(w_parts, axis=1), jnp.concatenate(y_parts, axis=0))
        most = extras[0]
        for x in extras[1:]:
            most = jnp.maximum(most, x)

        @pl.when(most > 0)
        def _():
            for e in range(N_EXPERTS):
                def more(k, carry, e=e):
                    lo_k = los[e] + k * COMBINE_WIN
                    ws = pl.multiple_of(jnp.minimum(lo_k, CAP_LAT - COMBINE_WIN), align)
                    acc_ref[...] += _dot(weights(e, lane + ws, lo_k), y_ref[0, e, pl.ds(ws, COMBINE_WIN), :])
                    return carry

                lax.fori_loop(1, extras[e] + 1, more, 0)

        finish(acc_ref[...])


def moe_combine(x1, mod, slot_tok, aff_tok, y, smin, smax, nslot, final_w):
    last = final_w is not None
    tspec = lambda w: pl.BlockSpec((1, TILE, w), lambda b, t, *_: (b, t, 0))
    if last:
        first_lat = CTX_LEN // TILE
        out_spec = pl.BlockSpec((1, TILE, D_MODEL), lambda b, t, *_: (b, jnp.maximum(t - first_lat, 0), 0))
        out_shape = jax.ShapeDtypeStruct((BATCH, SEQ, D_MODEL), F32)
        fw = final_w.astype(F32).reshape(1, D_MODEL)
    else:
        out_spec = tspec(D_MODEL)
        out_shape = jax.ShapeDtypeStruct((BATCH, T, D_MODEL), F32)
        fw = jnp.ones((1, D_MODEL), F32)
    return pl.pallas_call(
        functools.partial(_combine_kernel, nslot=nslot, last=last),
        grid_spec=pltpu.PrefetchScalarGridSpec(
            num_scalar_prefetch=2,
            grid=(BATCH, NT),
            in_specs=[tspec(D_MODEL),
                      pl.BlockSpec((1, 1, 6, D_MODEL), lambda b, t, *_: (b, _seg(t), 0, 0)),
                      tspec(N_EXPERTS), tspec(N_EXPERTS),
                      pl.BlockSpec((1, N_EXPERTS, nslot, D_MODEL), lambda b, t, *_: (b, 0, 0, 0)),
                      pl.BlockSpec((1, D_MODEL), lambda b, t, *_: (0, 0))],
            out_specs=out_spec,
            scratch_shapes=[pltpu.VMEM((TILE, D_MODEL), F32)]),
        out_shape=out_shape,
        compiler_params=_cp("arbitrary", "arbitrary"),
        name="moe_combine",
    )(smin, smax, x1, mod, slot_tok, aff_tok, y, fw)


def trunk_layer(l, stream, mod, cos_t, sin_t, norm1_w, norm2_w, w_in_p, w_out, s5_weights, s5_d, s5_glu_w, s5_glu_b,
                na_bias, ssd_conv_w, ssd_conv_b, ssd_dt_bias, ssd_a_log, ssd_d, ssd_norm_w,
                moe_router, wg, wu, wd, final_w):
    with_ctx_out = final_w is None
    u, q, k, v, z, xbc, dt_raw = in_proj(stream, mod, norm1_w, w_in_p, l)
    s5y = s5_mix(u, *s5_weights)
    na = natten(q, k, v, na_bias, l)
    xs, bm, cm, dt, a = ssd_prep(xbc, dt_raw, ssd_conv_w, ssd_conv_b, ssd_dt_bias, ssd_a_log, cos_t, sin_t)
    sdf, sdb = ssd_scan(xs, bm, cm, dt, a)

    x1, h, logits = post_mixer(stream, mod, u, s5y, na, xs, z, sdf, sdb, s5_d, s5_glu_w, s5_glu_b,
                               ssd_d, ssd_norm_w, w_out, norm2_w, moe_router)
    slot, aff = route(jnp.swapaxes(logits[:, :, :N_EXPERTS], 1, 2), with_ctx_out)
    nslot = CAP_LAT + CAP_CTX if with_ctx_out else CAP_LAT
    smin, smax = slot_ranges(slot)
    y = moe_ffn(moe_gather(h, slot, smin, smax, nslot), l, wg, wu, wd, nslot)
    return moe_combine(x1, mod, jnp.swapaxes(slot, 1, 2), jnp.swapaxes(aff, 1, 2), y, smin, smax, nslot, final_w)


def kernel(x, c, ctx, c_ctx, w_ada, b_ada, norm1_w, norm2_w, w_in, w_out, s5_lam_re, s5_lam_im, s5_log_dt, s5_b_re, s5_b_im, s5_c_re, s5_c_im, s5_d, s5_glu_w, s5_glu_b, na_rpb, ssd_conv_w, ssd_conv_b, ssd_dt_bias, ssd_a_log, ssd_d, ssd_norm_w, moe_router, moe_w_gate, moe_w_up, moe_w_down, final_norm_w):
    cvec = jnp.zeros((SUB, D_MODEL), F32).at[0].set(c_ctx.astype(F32)).at[1:1 + BATCH].set(c.astype(F32))
    mods = ada_mod(cvec, w_ada.astype(F32), b_ada.astype(F32)).reshape(DEPTH, SUB, 6, D_MODEL)
    mods = jnp.stack([jnp.broadcast_to(mods[:, 0:1], (DEPTH, BATCH, 6, D_MODEL)), mods[:, 1:1 + BATCH]], axis=2)
    cos_t, sin_t = rope_tables()
    w_in_p = jnp.zeros((DEPTH, D_MODEL, IN_COLS_PAD), BF16).at[:, :, :IN_COLS].set(w_in.astype(BF16))
    s5_weights = [s5_params(s5_lam_re[l], s5_lam_im[l], s5_log_dt[l], s5_b_re[l], s5_b_im[l], s5_c_re[l], s5_c_im[l])
                  for l in range(DEPTH)]
    na_bias = natten_bias(na_rpb.reshape(DEPTH * NA_HEADS, 2 * NA_KH - 1, RPB_W))
    wg, wu, wd = moe_w_gate.astype(F32), moe_w_up.astype(F32), moe_w_down.astype(F32)
    stream = (ctx.astype(F32), x.astype(F32), CTX_LEN // TILE)
    for l in range(DEPTH):
        xa = trunk_layer(
            l, stream, mods[l], cos_t, sin_t, norm1_w[l], norm2_w[l], w_in_p, w_out[l],
            s5_weights[l], s5_d[l], s5_glu_w[l], s5_glu_b[l],
            na_bias, ssd_conv_w[l], ssd_conv_b[l], ssd_dt_bias[l], ssd_a_log[l], ssd_d[l], ssd_norm_w[l],
            moe_router[l], wg, wu, wd, final_norm_w if l == DEPTH - 1 else None)
        stream = (xa, xa, 0)
    return xa
```

```python
import functools
import math

import jax
import jax.numpy as jnp
import numpy as np
from jax import lax
from jax.experimental import pallas as pl
from jax.experimental.pallas import tpu as pltpu

F32 = jnp.float32
BF16 = jnp.bfloat16
I32 = jnp.int32

D_MODEL = 1024
BATCH = 4
SEQ = 4096
DEPTH = 2
GRID_W = 64
CTX_LEN = 256
EPS = 1e-6

S5_WIDTH = 256
S5_GROUP = 16
S5_NGROUPS = 16
S5_STATE = 64
S5_NSTATE = S5_NGROUPS * S5_STATE

NA_HEADS = 6
NA_HEAD_DIM = 64
NA_WIDTH = 384
NA_KH = 8
NA_KW = 16
NA_BAND = 12
RPB_W = 2 * NA_KW - 1

SSD_HEADS = 6
SSD_HEAD_DIM = 64
SSD_WIDTH = 384
SSD_NGROUPS = 2
SSD_STATE = 128
SSD_CONV = 5
SSD_BC = 256
SSD_XBC = 896

N_EXPERTS = 16
D_EXPERT = 1024
ROPE_BASE = 10000.0

T = CTX_LEN + SEQ
TILE = 256
NT = T // TILE
BS = 2
BS_WIDE = 4
LANE = 128
SUB = 8
ROWS = SEQ // GRID_W
CAP_LAT = 2 * SEQ // N_EXPERTS
CAP_CTX = 2 * CTX_LEN // N_EXPERTS
NEG = -1e30

C_U = 0
C_Q = 256
C_K = 640
C_V = 1024
C_Z = 1408
C_XBC = 1792
C_DT = 2688
IN_COLS = 2700
IN_COLS_PAD = 2816

VMEM_LIMIT = 56 * 1024 * 1024


def _cp(*sem):
    return pltpu.CompilerParams(dimension_semantics=sem, vmem_limit_bytes=VMEM_LIMIT)


def _dot(a, b):
    return jnp.dot(a, b, preferred_element_type=F32)


def _dot_nt(a, b):
    return lax.dot_general(a, b, (((1,), (1,)), ((), ())), preferred_element_type=F32)


def _split3(x):
    hi = x.astype(BF16)
    r = x - hi.astype(F32)
    mid = r.astype(BF16)
    lo = (r - mid.astype(F32)).astype(BF16)
    return hi, mid, lo


def _dot_exact_rhs(a_bf16, b_f32):
    hi, mid, lo = _split3(b_f32)
    return _dot(a_bf16, hi) + _dot(a_bf16, mid) + _dot(a_bf16, lo)


def _dot_exact_lhs(a_f32, b_bf16):
    hi, mid, lo = _split3(a_f32)
    return _dot(hi, b_bf16) + _dot(mid, b_bf16) + _dot(lo, b_bf16)


def _dot_x3(a, b):
    ah = a.astype(BF16)
    al = (a - ah.astype(F32)).astype(BF16)
    bh = b.astype(BF16)
    bl = (b - bh.astype(F32)).astype(BF16)
    return _dot(ah, bh) + _dot(ah, bl) + _dot(al, bh)


def _silu(x):
    return x * jax.nn.sigmoid(x)


def _seg(t):
    return jnp.where(t >= CTX_LEN // TILE, 1, 0)


def _bwd_tile(i):
    return jnp.where(i == 0, 0, NT - i)


def _ada_kernel(c_ref, w_ref, b_ref, o_ref):
    s = _silu(c_ref[...])
    o_ref[0] = _dot_x3(s, w_ref[0]) + b_ref[0]


def ada_mod(cvec, w_ada, b_ada):
    nb = 1024
    return pl.pallas_call(
        _ada_kernel,
        grid=(DEPTH, 6 * D_MODEL // nb),
        in_specs=[pl.BlockSpec((SUB, D_MODEL), lambda l, j: (0, 0)),
                  pl.BlockSpec((1, D_MODEL, nb), lambda l, j: (l, 0, j)),
                  pl.BlockSpec((1, 1, nb), lambda l, j: (l, 0, j))],
        out_specs=pl.BlockSpec((1, SUB, nb), lambda l, j: (l, 0, j)),
        out_shape=jax.ShapeDtypeStruct((DEPTH, SUB, 6 * D_MODEL), F32),
        compiler_params=_cp("arbitrary", "arbitrary"),
        name="ada_mod",
    )(cvec, w_ada, b_ada.reshape(DEPTH, 1, 6 * D_MODEL))


def _modulated_norm(x, nw, shift, scale):
    y = x * lax.rsqrt(jnp.mean(x * x, axis=-1, keepdims=True) + EPS) * nw
    return y * (1.0 + scale) + shift


def _stream_specs(stream):
    _, _, off = stream
    return [pl.BlockSpec((BS, TILE, D_MODEL), lambda b, t: (b, 0, 0)),
            pl.BlockSpec((BS, TILE, D_MODEL), lambda b, t: (b, jnp.maximum(t - off, 0), 0))]


def _stream_tile(head_ref, tail_ref, bb):
    return jnp.where(pl.program_id(1) == 0, head_ref[bb], tail_ref[bb])


def _put_rows(ref, val, cast=None):
    for bb in range(BS):
        rows = val[bb * TILE:(bb + 1) * TILE]
        ref[bb] = rows if cast is None else rows.astype(cast)


def _inproj_kernel(xh_ref, xt_ref, mod_ref, nw_ref, w_ref, u_ref, q_ref, k_ref, v_ref, z_ref, xbc_ref, dt_ref):
    h = jnp.concatenate([_modulated_norm(_stream_tile(xh_ref, xt_ref, bb), nw_ref[...], mod_ref[bb, 0, 0:1, :],
                                         mod_ref[bb, 0, 1:2, :]).astype(BF16) for bb in range(BS)], axis=0)

    def proj(lo, hi):
        return _dot(h, w_ref[0, :, lo:hi])

    for s in range(S5_WIDTH // LANE):
        us = proj(C_U + s * LANE, C_U + (s + 1) * LANE)
        for bb in range(BS):
            u_ref[bb, s] = us[bb * TILE:(bb + 1) * TILE]
    _put_rows(q_ref, proj(C_Q, C_K) * (NA_HEAD_DIM ** -0.5), BF16)
    _put_rows(k_ref, proj(C_K, C_V), BF16)
    _put_rows(v_ref, proj(C_V, C_Z), BF16)
    _put_rows(z_ref, proj(C_Z, C_XBC), BF16)
    _put_rows(xbc_ref, proj(C_XBC, C_DT), BF16)
    _put_rows(dt_ref, proj(C_DT, IN_COLS_PAD))


def in_proj(stream, mod, norm_w, w_in_p, layer):
    tok = lambda w, dt: jax.ShapeDtypeStruct((BATCH, T, w), dt)
    tspec = lambda w: pl.BlockSpec((BS, TILE, w), lambda b, t: (b, t, 0))
    return pl.pallas_call(
        _inproj_kernel,
        grid=(BATCH // BS, NT),
        in_specs=_stream_specs(stream) + [
                  pl.BlockSpec((BS, 1, 6, D_MODEL), lambda b, t: (b, _seg(t), 0, 0)),
                  pl.BlockSpec((1, D_MODEL), lambda b, t: (0, 0)),
                  pl.BlockSpec((1, D_MODEL, IN_COLS_PAD), lambda b, t: (layer, 0, 0))],
        out_specs=[pl.BlockSpec((BS, S5_WIDTH // LANE, TILE, LANE), lambda b, t: (b, 0, t, 0)),
                   tspec(NA_WIDTH), tspec(NA_WIDTH), tspec(NA_WIDTH),
                   tspec(SSD_WIDTH), tspec(SSD_XBC), tspec(LANE)],
        out_shape=[jax.ShapeDtypeStruct((BATCH, S5_WIDTH // LANE, T, LANE), F32),
                   tok(NA_WIDTH, BF16), tok(NA_WIDTH, BF16), tok(NA_WIDTH, BF16),
                   tok(SSD_WIDTH, BF16), tok(SSD_XBC, BF16), tok(LANE, F32)],
        compiler_params=_cp("arbitrary", "arbitrary"),
        name="in_proj",
    )(stream[0], stream[1], mod, norm_w.reshape(1, D_MODEL), w_in_p)


S5_BLK = SUB
NB = T // S5_BLK
NB_CTX = CTX_LEN // S5_BLK
S5_NPAIR = S5_NGROUPS // 2
S5_PW = 2 * S5_BLK * S5_GROUP


def _s5_kernel(u_ref, pin_ref, pout_ref, w1_ref, w2_ref, w3_ref, mul_ref, y_ref, ub_ref, yb_ref, st_ref):
    n = S5_NSTATE
    half = LANE
    per_half = S5_NPAIR // 2
    toks = [jnp.concatenate([u_ref[0, h, pl.ds(j, NB, stride=S5_BLK), :].astype(BF16) for j in range(S5_BLK)],
                            axis=1) for h in range(2)]
    for pp in range(S5_NPAIR):
        h, q = divmod(pp, per_half)
        ub_ref[:, pp * S5_PW:(pp + 1) * S5_PW] = _dot(toks[h], pin_ref[q]).astype(BF16)

    for d in range(2):
        for pp in range(S5_NPAIR):
            s = _dot(ub_ref[:, pp * S5_PW:(pp + 1) * S5_PW], w1_ref[d, pp])
            st_ref[d, :, pp * half:(pp + 1) * half] = s[:, :half]
            st_ref[d, :, n + pp * half:n + (pp + 1) * half] = s[:, half:]

    ngrp = NB // SUB
    nctx = NB_CTX // SUB
    rowid = lax.broadcasted_iota(I32, (SUB, n), 0)
    for d in range(2):
        def body(j, carry, d=d):
            cr, ci = carry
            r = j if d == 0 else jnp.where(j < nctx, nctx - 1 - j, ngrp - 1 + nctx - j)
            row = pl.multiple_of(r * SUB, SUB)
            re = st_ref[d, pl.ds(row, SUB), 0:n]
            im = st_ref[d, pl.ds(row, SUB), n:2 * n]
            for kk, sh in enumerate((1, 2, 4)):
                mr = mul_ref[d, kk * SUB:(kk + 1) * SUB, 0:n]
                mi = mul_ref[d, kk * SUB:(kk + 1) * SUB, n:2 * n]
                s = sh if d == 0 else SUB - sh
                sr = pltpu.roll(re, s, 0)
                si = pltpu.roll(im, s, 0)
                re, im = re + (mr * sr - mi * si), im + (mr * si + mi * sr)
            pr = mul_ref[d, 3 * SUB:4 * SUB, 0:n]
            pi = mul_ref[d, 3 * SUB:4 * SUB, n:2 * n]
            re, im = re + (pr * cr - pi * ci), im + (pr * ci + pi * cr)
            edge, last, sh = (0, SUB - 1, 1) if d == 0 else (SUB - 1, 0, SUB - 1)
            st_ref[d, pl.ds(row, SUB), 0:n] = jnp.where(rowid == edge, cr, pltpu.roll(re, sh, 0))
            st_ref[d, pl.ds(row, SUB), n:2 * n] = jnp.where(rowid == edge, ci, pltpu.roll(im, sh, 0))
            return re[last:last + 1, :], im[last:last + 1, :]

        zero = jnp.zeros((1, n), F32)
        lax.fori_loop(0, ngrp, body, (zero, zero), unroll=2)

    for pp in range(S5_NPAIR):
        up = ub_ref[:, pp * S5_PW:(pp + 1) * S5_PW]
        acc = None
        for d in range(2):
            enter = jnp.concatenate([st_ref[d, :, pp * half:(pp + 1) * half],
                                     st_ref[d, :, n + pp * half:n + (pp + 1) * half]], axis=1).astype(BF16)
            term = _dot(up, w2_ref[d, pp]) + _dot(enter, w3_ref[d, pp])
            acc = term if acc is None else acc + term
        yb_ref[:, pp * S5_PW:(pp + 1) * S5_PW] = acc.astype(BF16)

    for m in range(S5_BLK // 2):
        for h in range(2):
            acc = None
            for q in range(per_half):
                pp = h * per_half + q
                term = _dot(yb_ref[:, pp * S5_PW:(pp + 1) * S5_PW], pout_ref[q, m])
                acc = term if acc is None else acc + term
            for k in range(2):
                y_ref[0, h, pl.ds(2 * m + k, NB, stride=S5_BLK), :] = acc[:, k * LANE:(k + 1) * LANE]


def _s5_regroup_matrices():
    per_half = S5_NPAIR // 2
    pin = np.zeros((per_half, S5_BLK, LANE, S5_PW), np.float32)
    for q in range(per_half):
        for j in range(S5_BLK):
            for gg in range(2):
                for c in range(S5_GROUP):
                    pin[q, j, (2 * q + gg) * S5_GROUP + c, gg * S5_BLK * S5_GROUP + j * S5_GROUP + c] = 1.0
    pout = pin.transpose(0, 1, 3, 2).reshape(per_half, S5_BLK // 2, 2, S5_PW, LANE)
    pout = pout.transpose(0, 1, 3, 2, 4).reshape(per_half, S5_BLK // 2, S5_PW, 2 * LANE)
    return jnp.asarray(pin.reshape(per_half, S5_BLK * LANE, S5_PW), BF16), jnp.asarray(pout, BF16)


def s5_mix(u, w1, w2, w3, mul):
    pin, pout = _s5_regroup_matrices()
    per_half = S5_NPAIR // 2
    wspec = pl.BlockSpec((2, S5_NPAIR, S5_PW, S5_PW), lambda b: (0, 0, 0, 0))
    tok = pl.BlockSpec((1, S5_WIDTH // LANE, T, LANE), lambda b: (b, 0, 0, 0))
    return pl.pallas_call(
        _s5_kernel,
        grid=(BATCH,),
        in_specs=[tok, pl.BlockSpec((per_half, S5_BLK * LANE, S5_PW), lambda b: (0, 0, 0)),
                  pl.BlockSpec((per_half, S5_BLK // 2, S5_PW, 2 * LANE), lambda b: (0, 0, 0, 0)), wspec, wspec, wspec,
                  pl.BlockSpec((2, 4 * SUB, 2 * S5_NSTATE), lambda b: (0, 0, 0))],
        out_specs=tok,
        out_shape=jax.ShapeDtypeStruct((BATCH, S5_WIDTH // LANE, T, LANE), F32),
        scratch_shapes=[pltpu.VMEM((NB, S5_NGROUPS * S5_BLK * S5_GROUP), BF16),
                        pltpu.VMEM((NB, S5_NGROUPS * S5_BLK * S5_GROUP), BF16),
                        pltpu.VMEM((2, NB, 2 * S5_NSTATE), F32)],
        compiler_params=_cp("arbitrary"),
        name="s5_mix",
    )(u, pin, pout, w1, w2, w3, mul)


def s5_params(lam_re, lam_im, log_dt, b_re, b_im, c_re, c_im):
    G, P, C = S5_NGROUPS, S5_STATE, S5_GROUP
    lam = lax.complex(lam_re.astype(F32), lam_im.astype(F32))
    step = jnp.exp(log_dt.astype(F32))[..., None]
    log_lb = lam * step
    lam_bar = jnp.exp(log_lb)
    b_bar = ((lam_bar - 1.0) / lam)[..., None] * lax.complex(b_re.astype(F32), b_im.astype(F32))
    J = S5_BLK
    row = lambda a: a.reshape(2, G, 1, P)
    col = lambda a: jnp.broadcast_to(a[..., None], (2, G, P, J * C))
    b_t = jnp.swapaxes(b_bar, 2, 3)
    c_t = lambda a: jnp.tile(jnp.swapaxes(a.astype(F32), 2, 3), (1, 1, 1, J))
    wspec = pl.BlockSpec((1, 1, S5_PW, S5_PW), lambda d, q: (d, q, 0, 0))
    pair = lambda r, s: pl.BlockSpec((1, 2, r, s), lambda d, q: (d, q, 0, 0))
    w1, w2, w3 = pl.pallas_call(
        _s5_weight_kernel,
        grid=(2, S5_NPAIR),
        in_specs=[pair(1, P), pair(1, P), pair(P, J * C), pair(P, J * C), pair(C, P), pair(C, P),
                  pair(P, J * C), pair(P, J * C)],
        out_specs=[wspec, wspec, wspec],
        out_shape=[jax.ShapeDtypeStruct((2, S5_NPAIR, S5_PW, S5_PW), BF16)] * 3,
        compiler_params=_cp("arbitrary", "arbitrary"),
        name="s5_weights",
    )(row(jnp.real(log_lb)), row(jnp.imag(log_lb)), col(jnp.real(log_lb)), col(jnp.imag(log_lb)),
      jnp.real(b_t), jnp.imag(b_t), c_t(c_re), c_t(c_im))
    rows = jnp.arange(SUB)
    pieces = []
    for d in range(2):
        log_blk = (log_lb[d] * float(J)).reshape(1, G * P)
        per_d = []
        for sh in (1, 2, 4):
            valid = (rows >= sh) if d == 0 else (rows < SUB - sh)
            per_d.append(jnp.where(valid[:, None], jnp.exp(log_blk * float(sh)), 0.0))
        expo = (rows + 1) if d == 0 else (SUB - rows)
        per_d.append(jnp.exp(log_blk * expo[:, None].astype(F32)))
        m = jnp.concatenate(per_d, axis=0)
        pieces.append(jnp.concatenate([jnp.real(m), jnp.imag(m)], axis=-1))
    mul = jnp.stack(pieces, axis=0).astype(F32)
    return w1, w2, w3, mul


def _s5_weight_kernel(llr_ref, lli_ref, lcr_ref, lci_ref, btr_ref, bti_ref, ctr_ref, cti_ref, w1_ref, w2_ref, w3_ref):
    J, C, P = S5_BLK, S5_GROUP, S5_STATE
    R = J * C
    fwd = pl.program_id(0) == 0
    shift = C.bit_length() - 1
    j_of_row = lax.shift_right_logical(lax.broadcasted_iota(I32, (R, P), 0), shift).astype(F32)
    i_of_col = lax.shift_right_logical(lax.broadcasted_iota(I32, (P, R), 1), shift).astype(F32)
    jr = lax.shift_right_logical(lax.broadcasted_iota(I32, (R, R), 0), shift)
    ic = lax.shift_right_logical(lax.broadcasted_iota(I32, (R, R), 1), shift)
    lag = jnp.where(fwd, ic - jr, jr - ic)

    def cpow(expo, lr, li):
        mag = jnp.exp(lr * expo)
        return mag * jnp.cos(li * expo), mag * jnp.sin(li * expo)

    w1_ref[...] = jnp.zeros_like(w1_ref)
    w2_ref[...] = jnp.zeros_like(w2_ref)
    w3_ref[...] = jnp.zeros_like(w3_ref)
    for gg in range(2):
        llr, lli = llr_ref[0, gg], lli_ref[0, gg]
        br = jnp.concatenate([btr_ref[0, gg]] * J, axis=0)
        bi = jnp.concatenate([bti_ref[0, gg]] * J, axis=0)
        cr, ci = ctr_ref[0, gg], cti_ref[0, gg]
        pr, pi = cpow(jnp.where(fwd, (J - 1) - j_of_row, j_of_row), llr, lli)
        w1_ref[0, 0, gg * R:(gg + 1) * R, gg * P:(gg + 1) * P] = (pr * br - pi * bi).astype(BF16)
        w1_ref[0, 0, gg * R:(gg + 1) * R, 2 * P + gg * P:2 * P + (gg + 1) * P] = (pr * bi + pi * br).astype(BF16)
        qr, qi = cpow(jnp.where(fwd, i_of_col + 1.0, J - i_of_col), lcr_ref[0, gg], lci_ref[0, gg])
        w3_ref[0, 0, gg * P:(gg + 1) * P, gg * R:(gg + 1) * R] = (cr * qr - ci * qi).astype(BF16)
        w3_ref[0, 0, 2 * P + gg * P:2 * P + (gg + 1) * P, gg * R:(gg + 1) * R] = (-(cr * qi + ci * qr)).astype(BF16)
        acc = jnp.zeros((R, R), F32)
        for k in range(J):
            lr, li = cpow(float(k), llr, lli)
            t = _dot_x3(br * lr - bi * li, cr) - _dot_x3(br * li + bi * lr, ci)
            acc = acc + jnp.where(lag == k, t, 0.0)
        w2_ref[0, 0, gg * R:(gg + 1) * R, gg * R:(gg + 1) * R] = acc.astype(BF16)


def _softmax_pv(parts):
    m = parts[0][0].max(axis=-1, keepdims=True)
    for s, _ in parts[1:]:
        m = jnp.maximum(m, s.max(axis=-1, keepdims=True))
    den = 0.0
    acc = 0.0
    for s, v in parts:
        p = jnp.exp(s - m)
        den = den + p.sum(axis=-1, keepdims=True)
        acc = acc + _dot(p.astype(BF16), v)
    return acc / den


def _na_kernel(q_ref, k_ref, v_ref, bias_ref, o_ref):
    t = pl.program_id(1)
    first = lax.broadcasted_iota(I32, (1, LANE), 1) < NA_HEAD_DIM

    def pair_scores(bb, pp, start):
        ls = slice(pp * LANE, (pp + 1) * LANE)
        qp = q_ref[bb, :, ls]
        kc = k_ref[bb, 0:CTX_LEN, ls]
        vc = v_ref[bb, 0:CTX_LEN, ls]
        scores = []
        for hh in range(2):
            qm = jnp.where(first if hh == 0 else ~first, qp, jnp.zeros_like(qp))
            parts = [(_dot_nt(qm, kc), vc)]
            if start is not None:
                kb = k_ref[bb, pl.ds(start, NA_BAND * GRID_W), ls]
                vb = v_ref[bb, pl.ds(start, NA_BAND * GRID_W), ls]
                parts.append((_dot_nt(qm, kb) + bias_ref[2 * pp + hh, 0], vb))
            scores.append(parts)
        return scores

    def attention(start):
        for bb in range(BS):
            scores = [pair_scores(bb, pp, start) for pp in range(NA_HEADS // 2)]
            for pp, (s0, s1) in enumerate(scores):
                o_ref[bb, :, pp * LANE:(pp + 1) * LANE] = jnp.where(first, _softmax_pv(s0),
                                                                    _softmax_pv(s1)).astype(BF16)

    @pl.when(t == 0)
    def _():
        attention(None)

    @pl.when(t > 0)
    def _():
        first_row = (t - 1) * (TILE // GRID_W)
        u0 = jnp.clip(first_row - NA_KH // 2, 0, ROWS - NA_BAND)
        attention(pl.multiple_of(CTX_LEN + u0 * GRID_W, LANE))


def _na_cfg(t):
    return jnp.where(t <= 1, 0, jnp.where(t == NT - 1, 2, 1))


def natten(q, k, v, bias, layer):
    whole = pl.BlockSpec((BS, T, NA_WIDTH), lambda b, t: (b, 0, 0))
    tile = pl.BlockSpec((BS, TILE, NA_WIDTH), lambda b, t: (b, t, 0))
    return pl.pallas_call(
        _na_kernel,
        grid=(BATCH // BS, NT),
        in_specs=[tile, whole, whole,
                  pl.BlockSpec((NA_HEADS, 1, TILE, NA_BAND * GRID_W), lambda b, t: (layer, _na_cfg(t), 0, 0))],
        out_specs=tile,
        out_shape=jax.ShapeDtypeStruct((BATCH, T, NA_WIDTH), BF16),
        compiler_params=_cp("arbitrary", "arbitrary"),
        name="natten",
    )(q, k, v, bias)


def natten_bias(rpb):
    col = jnp.arange(GRID_W)
    c0 = jnp.clip(col - NA_KW // 2, 0, GRID_W - NA_KW)
    in_win = (col[None, :] >= c0[:, None]) & (col[None, :] < c0[:, None] + NA_KW)
    rel_c = jnp.clip(col[None, :] - col[:, None] + (NA_KW - 1), 0, RPB_W - 1)
    pick_c = jax.nn.one_hot(rel_c, RPB_W, dtype=F32)
    blocks = jnp.einsum('hax,qkx->haqk', rpb.astype(F32), pick_c, precision=lax.Precision.HIGHEST)
    blocks = jnp.where(in_win[None, None], blocks, NEG)
    nh = rpb.shape[0]
    blocks = jnp.concatenate([blocks, jnp.full((nh, 1, GRID_W, GRID_W), NEG, F32)], axis=1)
    blocks = jnp.concatenate([blocks, blocks], axis=-1)
    return pl.pallas_call(
        _bias_kernel,
        grid=(nh, 3),
        in_specs=[pl.BlockSpec((1, 2 * NA_KH, GRID_W, LANE), lambda h, c: (h, 0, 0, 0))],
        out_specs=pl.BlockSpec((1, 1, TILE, NA_BAND * GRID_W), lambda h, c: (h, c, 0, 0)),
        out_shape=jax.ShapeDtypeStruct((nh, 3, TILE, NA_BAND * GRID_W), F32),
        compiler_params=_cp("arbitrary", "arbitrary"),
        name="natten_bias",
    )(blocks)


def _bias_kernel(blk_ref, o_ref):
    rows_per_tile = TILE // GRID_W
    masked = 2 * NA_KH - 1
    for cfg, first_row in enumerate((0, 2 * rows_per_tile, ROWS - rows_per_tile)):
        @pl.when(pl.program_id(1) == cfg)
        def _(first_row=first_row):
            u0 = min(max(first_row - NA_KH // 2, 0), ROWS - NA_BAND)
            for rr in range(rows_per_tile):
                r = first_row + rr
                r0 = min(max(r - NA_KH // 2, 0), ROWS - NA_KH)
                for j in range(NA_BAND):
                    kr = u0 + j
                    a = kr - r + NA_KH - 1 if r0 <= kr < r0 + NA_KH else masked
                    half = (j % 2) * GRID_W
                    o_ref[0, 0, rr * GRID_W:(rr + 1) * GRID_W, j * GRID_W:(j + 1) * GRID_W] = (
                        blk_ref[0, a, :, half:half + GRID_W])


def _softplus(x):
    return jnp.maximum(x, 0.0) + jnp.log(1.0 + jnp.exp(-jnp.abs(x)))


def _ssd_prep_kernel(prev_ref, cur_ref, next_ref, dtr_ref, cw_ref, cb_ref, dtb_ref, ar_ref, cos_ref, sin_ref,
                     xs_ref, bm_ref, cm_ref, dt_ref, a_ref):
    for bb in range(cur_ref.shape[0]):
        _ssd_prep_sample(bb, prev_ref, cur_ref, next_ref, dtr_ref, cw_ref, cb_ref, dtb_ref, ar_ref, cos_ref, sin_ref,
                         xs_ref, bm_ref, cm_ref, dt_ref, a_ref)


def _ssd_prep_sample(bb, prev_ref, cur_ref, next_ref, dtr_ref, cw_ref, cb_ref, dtb_ref, ar_ref, cos_ref, sin_ref,
                     xs_ref, bm_ref, cm_ref, dt_ref, a_ref):
    t = pl.program_id(1)
    halo = prev_ref.shape[1]
    has_prev = t >= 2
    has_next = (t >= 1) & (t <= NT - 2)
    prev = jnp.where(has_prev, prev_ref[bb].astype(F32), 0.0)
    nxt = jnp.where(has_next, next_ref[bb].astype(F32), 0.0)
    ext = jnp.concatenate([prev, cur_ref[bb].astype(F32), nxt], axis=0)
    n = ext.shape[0]
    acc = cb_ref[...] + cw_ref[SSD_CONV // 2:SSD_CONV // 2 + 1, :] * ext
    for kk in range(SSD_CONV):
        off = kk - SSD_CONV // 2
        if off != 0:
            acc = acc + cw_ref[kk:kk + 1, :] * pltpu.roll(ext, (-off) % n, 0)
    y = _silu(acc[halo:halo + TILE, :])
    xs_ref[bb] = y[:, 0:SSD_WIDTH].astype(BF16)

    lane = lax.broadcasted_iota(I32, (1, LANE), 1)
    low = (lane & (NA_HEAD_DIM // 2)) == 0
    cos = cos_ref[...]
    sin = sin_ref[...]
    for g in range(2 * SSD_NGROUPS):
        v = y[:, SSD_WIDTH + g * LANE:SSD_WIDTH + (g + 1) * LANE]
        sw = jnp.where(low, pltpu.roll(v, LANE - 32, 1), pltpu.roll(v, 32, 1))
        rot = (v * cos + sw * sin).astype(BF16)
        if g < SSD_NGROUPS:
            bm_ref[bb, :, g * LANE:(g + 1) * LANE] = rot
        else:
            cm_ref[bb, :, (g - SSD_NGROUPS) * LANE:(g - SSD_NGROUPS + 1) * LANE] = rot

    dt = _softplus(dtr_ref[bb] + dtb_ref[...])
    dt_ref[bb] = dt
    a_ref[bb] = dt * ar_ref[...]


def ssd_prep(xbc, dt_raw, conv_w, conv_b, dt_bias, a_log, cos_t, sin_t):
    halo = 16
    per = TILE // halo
    nhalo = T // halo
    tok = lambda w, dt: jax.ShapeDtypeStruct((BATCH, T, w), dt)
    bs = BS_WIDE
    tspec = lambda w: pl.BlockSpec((bs, TILE, w), lambda b, t: (b, t, 0))
    row = lambda w: pl.BlockSpec((1, w), lambda b, t: (0, 0))
    cw = jnp.zeros((SUB, SSD_XBC), F32).at[:SSD_CONV].set(conv_w.astype(F32))
    pad12 = lambda v: jnp.zeros((1, LANE), F32).at[0, :2 * SSD_HEADS].set(v.astype(F32).reshape(-1))
    return pl.pallas_call(
        _ssd_prep_kernel,
        grid=(BATCH // bs, NT),
        in_specs=[pl.BlockSpec((bs, halo, SSD_XBC), lambda b, t: (b, jnp.maximum(t * per - 1, 0), 0)),
                  tspec(SSD_XBC),
                  pl.BlockSpec((bs, halo, SSD_XBC), lambda b, t: (b, jnp.minimum((t + 1) * per, nhalo - 1), 0)),
                  tspec(LANE),
                  pl.BlockSpec((SUB, SSD_XBC), lambda b, t: (0, 0)), row(SSD_XBC), row(LANE), row(LANE),
                  pl.BlockSpec((TILE, LANE), lambda b, t: (t, 0)), pl.BlockSpec((TILE, LANE), lambda b, t: (t, 0))],
        out_specs=[tspec(SSD_WIDTH), tspec(SSD_BC), tspec(SSD_BC), tspec(LANE), tspec(LANE)],
        out_shape=[tok(SSD_WIDTH, BF16), tok(SSD_BC, BF16), tok(SSD_BC, BF16), tok(LANE, F32), tok(LANE, F32)],
        compiler_params=_cp("arbitrary", "arbitrary"),
        name="ssd_prep",
    )(xbc, xbc, xbc, dt_raw, cw, conv_b.astype(F32).reshape(1, SSD_XBC), pad12(dt_bias),
      pad12(-jnp.exp(a_log.astype(F32))), cos_t, sin_t)


def rope_tables():
    half = SSD_STATE // 2
    nf = half // 2
    pos = jnp.arange(SEQ)
    inv_freq = ROPE_BASE ** (-jnp.arange(nf, dtype=F32) / nf)
    lane = jnp.arange(LANE)
    p = jnp.where(lane[None, :] < half, (pos // GRID_W)[:, None], (pos % GRID_W)[:, None]).astype(F32)
    ang = p * inv_freq[lane % nf][None, :]
    sign = jnp.where((lane & nf) == 0, -1.0, 1.0)[None, :]
    cos_t = jnp.concatenate([jnp.ones((CTX_LEN, LANE), F32), jnp.cos(ang)], axis=0)
    sin_t = jnp.concatenate([jnp.zeros((CTX_LEN, LANE), F32), jnp.sin(ang) * sign], axis=0)
    return cos_t, sin_t


def _ssd_sums(d, bm_ref, cm_ref, a_ref, at_ref, tri_ref):
    tri_col = tri_ref[d]
    tri_row = tri_ref[1 - d]
    cs_col = _dot_exact_rhs(tri_col, a_ref[0])
    cs_row = _dot_exact_lhs(at_ref[0], tri_row)
    g_mats = [_dot_nt(cm_ref[0, :, g * SSD_STATE:(g + 1) * SSD_STATE],
                      bm_ref[0, :, g * SSD_STATE:(g + 1) * SSD_STATE]) for g in range(SSD_NGROUPS)]
    return cs_col, cs_row, g_mats


def _ssd_dir(d, sums, xs_ref, cm_ref, bt_ref, dt_ref, y_ref, st_ref):
    q = TILE
    cs_col, cs_row, g_mats = sums
    lane = lax.broadcasted_iota(I32, (1, LANE), 1)
    first = lane < SSD_HEAD_DIM
    ri = lax.broadcasted_iota(I32, (q, q), 0)
    ci = lax.broadcasted_iota(I32, (q, q), 1)
    keep = (ci <= ri) if d == 0 else (ci >= ri)
    end = q - 1 if d == 0 else 0
    dt = dt_ref[0]

    def head_col(m, h):
        c = d * SSD_HEADS + h
        return m[:, c:c + 1]

    for pp in range(SSD_HEADS // 2):
        ls = slice(pp * LANE, (pp + 1) * LANE)
        h0, h1 = 2 * pp, 2 * pp + 1
        x = xs_ref[0, :, ls].astype(F32)
        dt_l = jnp.where(first, head_col(dt, h0), head_col(dt, h1))
        cs_l = jnp.where(first, head_col(cs_col, h0), head_col(cs_col, h1))
        cs_end = cs_l[end:end + 1, :]
        xdt = x * dt_l
        xdt_b = xdt.astype(BF16)
        xw = (xdt * jnp.exp(cs_end - cs_l)).astype(BF16)
        st = st_ref[d, pp]
        st_b = st.astype(BF16)
        ys, ups = [], []
        for h in (h0, h1):
            g = h // (SSD_HEADS // SSD_NGROUPS)
            c = d * SSD_HEADS + h
            diff = head_col(cs_col, h) - cs_row[c:c + 1, :]
            decay = jnp.exp(jnp.where(keep, diff, NEG))
            m = (g_mats[g] * decay).astype(BF16)
            y_h = _dot(m, xdt_b) + _dot(cm_ref[0, :, g * SSD_STATE:(g + 1) * SSD_STATE], st_b) * jnp.exp(cs_l)
            ys.append(y_h)
            ups.append(_dot(bt_ref[0, g * SSD_STATE:(g + 1) * SSD_STATE, :], xw))
        y_ref[0, :, ls] = jnp.where(first, ys[0], ys[1])
        st_ref[d, pp] = jnp.exp(cs_end) * st + jnp.where(first, ups[0], ups[1])


def _ssd_scan_kernel(xs_f, bm_f, cm_f, bt_f, dt_f, a_f, at_f, xs_b, bm_b, cm_b, bt_b, dt_b, a_b, at_b, tri_ref,
                     yf_ref, yb_ref, st_ref):
    @pl.when(pl.program_id(1) == 0)
    def _():
        st_ref[...] = jnp.zeros_like(st_ref)

    for bb in range(xs_f.shape[0]):
        one = lambda *refs: [r.at[pl.ds(bb, 1)] for r in refs]
        xsf, bmf, cmf, btf, dtf, af, atf, yf = one(xs_f, bm_f, cm_f, bt_f, dt_f, a_f, at_f, yf_ref)
        xsb, bmb, cmb, btb, dtb, ab, atb, yb = one(xs_b, bm_b, cm_b, bt_b, dt_b, a_b, at_b, yb_ref)
        st = st_ref.at[bb]
        sums_f = _ssd_sums(0, bmf, cmf, af, atf, tri_ref)
        sums_b = _ssd_sums(1, bmb, cmb, ab, atb, tri_ref)
        _ssd_dir(0, sums_f, xsf, cmf, btf, dtf, yf, st)
        _ssd_dir(1, sums_b, xsb, cmb, btb, dtb, yb, st)


def ssd_scan(xs, bm, cm, dt, a):
    bs = BS_WIDE
    bt = jnp.swapaxes(bm, 1, 2)
    at = jnp.swapaxes(a[:, :, :2 * SUB], 1, 2)
    idx = jnp.arange(TILE)
    tri = jnp.stack([idx[None, :] <= idx[:, None], idx[None, :] >= idx[:, None]]).astype(BF16)
    fwd = lambda b, i: (b, i, 0)
    bwd = lambda b, i: (b, _bwd_tile(i), 0)
    fwd_t = lambda b, i: (b, 0, i)
    bwd_t = lambda b, i: (b, 0, _bwd_tile(i))

    def specs(f, ft):
        return [pl.BlockSpec((bs, TILE, SSD_WIDTH), f), pl.BlockSpec((bs, TILE, SSD_BC), f),
                pl.BlockSpec((bs, TILE, SSD_BC), f), pl.BlockSpec((bs, SSD_BC, TILE), ft),
                pl.BlockSpec((bs, TILE, LANE), f), pl.BlockSpec((bs, TILE, LANE), f),
                pl.BlockSpec((bs, 2 * SUB, TILE), ft)]

    args = (xs, bm, cm, bt, dt, a, at)
    return pl.pallas_call(
        _ssd_scan_kernel,
        grid=(BATCH // bs, NT),
        in_specs=specs(fwd, fwd_t) + specs(bwd, bwd_t) + [pl.BlockSpec((2, TILE, TILE), lambda b, i: (0, 0, 0))],
        out_specs=[pl.BlockSpec((bs, TILE, SSD_WIDTH), fwd), pl.BlockSpec((bs, TILE, SSD_WIDTH), bwd)],
        out_shape=[jax.ShapeDtypeStruct((BATCH, T, SSD_WIDTH), F32)] * 2,
        scratch_shapes=[pltpu.VMEM((bs, 2, SSD_HEADS // 2, SSD_STATE, LANE), F32)],
        compiler_params=_cp("arbitrary", "arbitrary"),
        name="ssd_scan",
    )(*args, *args, tri)


def _gelu_tanh(x):
    return 0.5 * x * (1.0 + jnp.tanh(math.sqrt(2.0 / math.pi) * (x + 0.044715 * (x * x * x))))


def _post_kernel(xh_ref, xt_ref, mod_ref, u_ref, s5y_ref, na_ref, xs_ref, z_ref, sdf_ref, sdb_ref,
                 s5d_ref, gw_ref, gb_ref, sdd_ref, snw_ref, wo_ref, n2w_ref, rt_ref,
                 x1_ref, h_ref, lg_ref):
    for bb in range(BS):
        slabs = lambda r: jnp.concatenate([r[bb, s] for s in range(S5_WIDTH // LANE)], axis=-1)
        ys5 = slabs(u_ref) * s5d_ref[...] + slabs(s5y_ref)
        g = _gelu_tanh(ys5)
        s5o = g * jax.nn.sigmoid(_dot(g.astype(BF16), gw_ref[...]) + gb_ref[...])
        yssd = (xs_ref[bb].astype(F32) * sdd_ref[...] + sdf_ref[bb] + sdb_ref[bb]) * _silu(z_ref[bb].astype(F32))
        ssdo = yssd * lax.rsqrt(jnp.mean(yssd * yssd, axis=-1, keepdims=True) + EPS) * snw_ref[...]
        mix = jnp.concatenate([s5o.astype(BF16), na_ref[bb], ssdo.astype(BF16)], axis=-1)
        x1 = _stream_tile(xh_ref, xt_ref, bb) + mod_ref[bb, 0, 2:3, :] * _dot(mix, wo_ref[...])
        x1_ref[bb] = x1
        h = _modulated_norm(x1, n2w_ref[...], mod_ref[bb, 0, 3:4, :], mod_ref[bb, 0, 4:5, :])
        h_ref[bb] = h.astype(BF16)
        lg_ref[bb] = _dot_x3(h, rt_ref[...])


def post_mixer(stream, mod, u, s5y, na, xs, z, sdf, sdb, s5_d, glu_w, glu_b, ssd_d, ssd_norm_w, w_out, norm2_w,
               router):
    tspec = lambda w: pl.BlockSpec((BS, TILE, w), lambda b, t: (b, t, 0))
    whole = lambda *shp: pl.BlockSpec(shp, lambda b, t: (0,) * len(shp))
    rt = jnp.zeros((D_MODEL, LANE), F32).at[:, :N_EXPERTS].set(router.astype(F32))
    slab = pl.BlockSpec((BS, S5_WIDTH // LANE, TILE, LANE), lambda b, t: (b, 0, t, 0))
    return pl.pallas_call(
        _post_kernel,
        grid=(BATCH // BS, NT),
        in_specs=_stream_specs(stream) + [
                  pl.BlockSpec((BS, 1, 6, D_MODEL), lambda b, t: (b, _seg(t), 0, 0)),
                  slab, slab, tspec(NA_WIDTH),
                  tspec(SSD_WIDTH), tspec(SSD_WIDTH), tspec(SSD_WIDTH), tspec(SSD_WIDTH),
                  whole(1, S5_WIDTH), whole(S5_WIDTH, S5_WIDTH), whole(1, S5_WIDTH),
                  whole(1, SSD_WIDTH), whole(1, SSD_WIDTH), whole(D_MODEL, D_MODEL), whole(1, D_MODEL),
                  whole(D_MODEL, LANE)],
        out_specs=[tspec(D_MODEL), tspec(D_MODEL), tspec(LANE)],
        out_shape=[jax.ShapeDtypeStruct((BATCH, T, D_MODEL), F32), jax.ShapeDtypeStruct((BATCH, T, D_MODEL), BF16),
                   jax.ShapeDtypeStruct((BATCH, T, LANE), F32)],
        compiler_params=_cp("arbitrary", "arbitrary"),
        name="post_mixer",
    )(stream[0], stream[1], mod, u, s5y, na, xs, z, sdf, sdb,
      s5_d.astype(F32).reshape(1, S5_WIDTH), glu_w.astype(BF16), glu_b.astype(F32).reshape(1, S5_WIDTH),
      jnp.repeat(ssd_d.astype(F32), SSD_HEAD_DIM).reshape(1, SSD_WIDTH), ssd_norm_w.astype(F32).reshape(1, SSD_WIDTH),
      w_out.astype(BF16), norm2_w.astype(F32).reshape(1, D_MODEL), rt)


def _route_kernel(lg_ref, tri_ref, slot_ref, aff_ref, *, with_ctx):
    lg = lg_ref[0]
    m = lg.max(axis=0, keepdims=True)
    e = jnp.exp(lg - m)
    aff = e / e.sum(axis=0, keepdims=True)
    aff_ref[0] = aff
    bits = pltpu.bitcast(aff, I32)
    is_ctx = lax.broadcasted_iota(I32, (N_EXPERTS, T), 1) < CTX_LEN

    def count(mask):
        return jnp.where(mask, 1.0, 0.0).sum(axis=1, keepdims=True)

    def kth_largest(seg, k):
        def body(i, prefix):
            cand = prefix | lax.shift_left(jnp.int32(1), 30 - i)
            return jnp.where(count((bits >= cand) & seg) >= k, cand, prefix)
        return lax.fori_loop(0, 31, body, jnp.zeros((N_EXPERTS, 1), I32))

    def excl_cumsum(x01):
        carry = jnp.zeros((N_EXPERTS, 1), F32)
        pieces = []
        for j in range(T // LANE):
            blk = x01[:, j * LANE:(j + 1) * LANE]
            inc = _dot(blk.astype(BF16), tri_ref[...])
            pieces.append(inc - blk + carry)
            carry = carry + inc[:, LANE - 1:LANE]
        return jnp.concatenate(pieces, axis=1)

    thr = kth_largest(~is_ctx, float(CAP_LAT))
    k_of = jnp.full((N_EXPERTS, T), float(CAP_LAT), F32)
    if with_ctx:
        thr = jnp.where(is_ctx, kth_largest(is_ctx, float(CAP_CTX)), thr)
        k_of = jnp.where(is_ctx, float(CAP_CTX), k_of)
    gt = bits > thr
    eq = bits == thr
    if not with_ctx:
        gt = gt & ~is_ctx
        eq = eq & ~is_ctx
    n_gt = jnp.where(is_ctx, count(gt & is_ctx), count(gt & ~is_ctx))
    tie_rank = excl_cumsum(jnp.where(eq, 1.0, 0.0))
    tie_rank = tie_rank - jnp.where(is_ctx, 0.0, count(eq & is_ctx))
    sel = gt | (eq & (tie_rank < k_of - n_gt))
    pos = excl_cumsum(jnp.where(sel, 1.0, 0.0))
    slot = jnp.where(is_ctx, pos + float(CAP_LAT), pos - count(sel & is_ctx))
    slot_ref[0] = jnp.where(sel, slot, -1.0).astype(I32)


def route(logits_t, with_ctx):
    idx = jnp.arange(LANE)
    tri = (idx[:, None] <= idx[None, :]).astype(BF16)
    spec = pl.BlockSpec((1, N_EXPERTS, T), lambda b: (b, 0, 0))
    return pl.pallas_call(
        functools.partial(_route_kernel, with_ctx=with_ctx),
        grid=(BATCH,),
        in_specs=[spec, pl.BlockSpec((LANE, LANE), lambda b: (0, 0))],
        out_specs=[spec, spec],
        out_shape=[jax.ShapeDtypeStruct((BATCH, N_EXPERTS, T), I32), jax.ShapeDtypeStruct((BATCH, N_EXPERTS, T), F32)],
        compiler_params=_cp("arbitrary"),
        name="route",
    )(logits_t, tri)


GATHER_WIN = LANE // 2
COMBINE_WIN = LANE // 2


def slot_ranges(slot):
    s = slot.reshape(BATCH, N_EXPERTS, NT, TILE)
    has = s >= 0
    smax = jnp.max(jnp.where(has, s, -1), axis=-1)
    smin = jnp.where(smax >= 0, jnp.min(jnp.where(has, s, CAP_LAT + CAP_CTX), axis=-1), 0)
    return smin.reshape(-1).astype(I32), smax.reshape(-1).astype(I32)


def _gather_kernel(smin_ref, smax_ref, h_ref, slot_ref, xs_ref, *, nslot):
    b = pl.program_id(0)
    t = pl.program_id(1)
    group = N_EXPERTS
    sid = lax.broadcasted_iota(I32, (GATHER_WIN, TILE), 0)
    align = 2 * SUB

    def onehot(e, ws, lo):
        srow = slot_ref[0, e:e + 1, :]
        return jnp.where((sid + ws == srow) & (srow >= lo), 1.0, 0.0).astype(BF16)

    def add_rows(e, ws, rows):
        win = pl.ds(pl.multiple_of(ws, align), GATHER_WIN)
        xs_ref[0, e, win, :] = (xs_ref[0, e, win, :].astype(F32) + rows).astype(BF16)

    @pl.when(t == 0)
    def _():
        for e in range(N_EXPERTS):
            xs_ref[0, e, 0:CAP_LAT, :] = jnp.zeros((CAP_LAT, D_MODEL), BF16)
            if nslot > CAP_LAT:
                cid = lax.broadcasted_iota(I32, (nslot - CAP_LAT, TILE), 0) + CAP_LAT
                pick = jnp.where(cid == slot_ref[0, e:e + 1, :], 1.0, 0.0).astype(BF16)
                xs_ref[0, e, CAP_LAT:nslot, :] = _dot(pick, h_ref[0]).astype(BF16)

    @pl.when(t > 0)
    def _():
        los, wss, extras = [], [], []
        for e in range(N_EXPERTS):
            base = (b * N_EXPERTS + e) * NT + t
            lo = smin_ref[base] & ~(align - 1)
            los.append(lo)
            wss.append(jnp.minimum(lo, CAP_LAT - GATHER_WIN))
            extras.append(lax.shift_right_arithmetic(smax_ref[base] - lo, GATHER_WIN.bit_length() - 1))
        for g0 in range(0, N_EXPERTS, group):
            pick = jnp.concatenate([onehot(e, wss[e], los[e]) for e in range(g0, g0 + group)], axis=0)
            rows = _dot(pick, h_ref[0])
            for i, e in enumerate(range(g0, g0 + group)):
                add_rows(e, wss[e], rows[i * GATHER_WIN:(i + 1) * GATHER_WIN, :])
        most = extras[0]
        for x in extras[1:]:
            most = jnp.maximum(most, x)

        @pl.when(most > 0)
        def _():
            for e in range(N_EXPERTS):
                def more(k, carry, e=e):
                    lo_k = los[e] + k * GATHER_WIN
                    ws = jnp.minimum(lo_k, CAP_LAT - GATHER_WIN)
                    add_rows(e, ws, _dot(onehot(e, ws, lo_k), h_ref[0]))
                    return carry

                lax.fori_loop(1, extras[e] + 1, more, 0)


def moe_gather(h, slot, smin, smax, nslot):
    return pl.pallas_call(
        functools.partial(_gather_kernel, nslot=nslot),
        grid_spec=pltpu.PrefetchScalarGridSpec(
            num_scalar_prefetch=2,
            grid=(BATCH, NT),
            in_specs=[pl.BlockSpec((1, TILE, D_MODEL), lambda b, t, *_: (b, t, 0)),
                      pl.BlockSpec((1, N_EXPERTS, TILE), lambda b, t, *_: (b, 0, t))],
            out_specs=pl.BlockSpec((1, N_EXPERTS, nslot, D_MODEL), lambda b, t, *_: (b, 0, 0, 0))),
        out_shape=jax.ShapeDtypeStruct((BATCH, N_EXPERTS, nslot, D_MODEL), BF16),
        compiler_params=_cp("arbitrary", "arbitrary"),
        name="moe_gather",
    )(smin, smax, h, slot)


def _ffn_kernel(xs_ref, wg_ref, wu_ref, wd_ref, y_ref, w_ref):
    @pl.when(pl.program_id(1) == 0)
    def _():
        w_ref[0] = wg_ref[0, 0].astype(BF16)
        w_ref[1] = wu_ref[0, 0].astype(BF16)
        w_ref[2] = wd_ref[0, 0].astype(BF16)

    xs = xs_ref[0, 0]
    hid = _silu(_dot(xs, w_ref[0])) * _dot(xs, w_ref[1])
    y_ref[0, 0] = _dot(hid.astype(BF16), w_ref[2]).astype(BF16)


def moe_ffn(xs, layer, wg, wu, wd, nslot):
    wspec = lambda: pl.BlockSpec((1, 1, D_MODEL, D_EXPERT), lambda e, b: (layer, e, 0, 0))
    rows = pl.BlockSpec((1, 1, nslot, D_MODEL), lambda e, b: (b, e, 0, 0))
    return pl.pallas_call(
        _ffn_kernel,
        grid=(N_EXPERTS, BATCH),
        in_specs=[rows, wspec(), wspec(), wspec()],
        out_specs=rows,
        out_shape=jax.ShapeDtypeStruct((BATCH, N_EXPERTS, nslot, D_MODEL), BF16),
        scratch_shapes=[pltpu.VMEM((3, D_MODEL, D_EXPERT), BF16)],
        compiler_params=_cp("arbitrary", "arbitrary"),
        name="moe_ffn",
    )(xs, wg, wu, wd)


def _combine_kernel(smin_ref, smax_ref, x_ref, mod_ref, slot_ref, aff_ref, y_ref, fw_ref, o_ref, acc_ref, *,
                    nslot, last):
    b = pl.program_id(0)
    t = pl.program_id(1)
    shift = COMBINE_WIN.bit_length() - 1
    align = 2 * SUB

    def weights(e, lane, lo=None):
        s = slot_ref[0, :, e:e + 1]
        hit = (s == lane) if lo is None else ((s == lane) & (s >= lo))
        return jnp.where(hit, aff_ref[0, :, e:e + 1], 0.0).astype(BF16)

    def finish(acc):
        x2 = x_ref[0] + mod_ref[0, 0, 5:6, :] * acc
        if last:
            x2 = x2 * lax.rsqrt(jnp.mean(x2 * x2, axis=-1, keepdims=True) + EPS) * fw_ref[...]
        o_ref[0] = x2

    if not last:
        @pl.when(t == 0)
        def _():
            lane = lax.broadcasted_iota(I32, (TILE, nslot - CAP_LAT), 1) + CAP_LAT
            acc = jnp.zeros((TILE, D_MODEL), F32)
            for e in range(N_EXPERTS):
                acc = acc + _dot(weights(e, lane), y_ref[0, e, CAP_LAT:nslot, :])
            finish(acc)

    @pl.when(t > 0)
    def _():
        lane = lax.broadcasted_iota(I32, (TILE, COMBINE_WIN), 1)
        lane2 = lax.broadcasted_iota(I32, (TILE, 2 * COMBINE_WIN), 1)
        first = lane2 < COMBINE_WIN
        los, wss, extras = [], [], []
        for e in range(N_EXPERTS):
            base = (b * N_EXPERTS + e) * NT + t
            lo = smin_ref[base] & ~(align - 1)
            los.append(lo)
            wss.append(pl.multiple_of(jnp.minimum(lo, CAP_LAT - COMBINE_WIN), align))
            extras.append(lax.shift_right_arithmetic(smax_ref[base] - lo, shift))
        w_parts, y_parts = [], []
        for e in range(0, N_EXPERTS, 2):
            want = jnp.where(first, slot_ref[0, :, e:e + 1] - wss[e],
                             slot_ref[0, :, e + 1:e + 2] - wss[e + 1] + COMBINE_WIN)
            gate = jnp.where(first, aff_ref[0, :, e:e + 1], aff_ref[0, :, e + 1:e + 2])
            w_parts.append(jnp.where(want == lane2, gate, 0.0).astype(BF16))
            y_parts += [y_ref[0, e, pl.ds(wss[e], COMBINE_WIN), :], y_ref[0, e + 1, pl.ds(wss[e + 1], COMBINE_WIN), :]]
        acc_ref[...] = _dot(jnp.concatenate(w_parts, axis=1), jnp.concatenate(y_parts, axis=0))
        most = extras[0]
        for x in extras[1:]:
            most = jnp.maximum(most, x)

        @pl.when(most > 0)
        def _():
            for e in range(N_EXPERTS):
                def more(k, carry, e=e):
                    lo_k = los[e] + k * COMBINE_WIN
                    ws = pl.multiple_of(jnp.minimum(lo_k, CAP_LAT - COMBINE_WIN), align)
                    acc_ref[...] += _dot(weights(e, lane + ws, lo_k), y_ref[0, e, pl.ds(ws, COMBINE_WIN), :])
                    return carry

                lax.fori_loop(1, extras[e] + 1, more, 0)

        finish(acc_ref[...])


def moe_combine(x1, mod, slot_tok, aff_tok, y, smin, smax, nslot, final_w):
    last = final_w is not None
    tspec = lambda w: pl.BlockSpec((1, TILE, w), lambda b, t, *_: (b, t, 0))
    if last:
        first_lat = CTX_LEN // TILE
        out_spec = pl.BlockSpec((1, TILE, D_MODEL), lambda b, t, *_: (b, jnp.maximum(t - first_lat, 0), 0))
        out_shape = jax.ShapeDtypeStruct((BATCH, SEQ, D_MODEL), F32)
        fw = final_w.astype(F32).reshape(1, D_MODEL)
    else:
        out_spec = tspec(D_MODEL)
        out_shape = jax.ShapeDtypeStruct((BATCH, T, D_MODEL), F32)
        fw = jnp.ones((1, D_MODEL), F32)
    return pl.pallas_call(
        functools.partial(_combine_kernel, nslot=nslot, last=last),
        grid_spec=pltpu.PrefetchScalarGridSpec(
            num_scalar_prefetch=2,
            grid=(BATCH, NT),
            in_specs=[tspec(D_MODEL),
                      pl.BlockSpec((1, 1, 6, D_MODEL), lambda b, t, *_: (b, _seg(t), 0, 0)),
                      tspec(N_EXPERTS), tspec(N_EXPERTS),
                      pl.BlockSpec((1, N_EXPERTS, nslot, D_MODEL), lambda b, t, *_: (b, 0, 0, 0)),
                      pl.BlockSpec((1, D_MODEL), lambda b, t, *_: (0, 0))],
            out_specs=out_spec,
            scratch_shapes=[pltpu.VMEM((TILE, D_MODEL), F32)]),
        out_shape=out_shape,
        compiler_params=_cp("arbitrary", "arbitrary"),
        name="moe_combine",
    )(smin, smax, x1, mod, slot_tok, aff_tok, y, fw)


def trunk_layer(l, stream, mod, cos_t, sin_t, norm1_w, norm2_w, w_in_p, w_out, s5_weights, s5_d, s5_glu_w, s5_glu_b,
                na_bias, ssd_conv_w, ssd_conv_b, ssd_dt_bias, ssd_a_log, ssd_d, ssd_norm_w,
                moe_router, wg, wu, wd, final_w):
    with_ctx_out = final_w is None
    u, q, k, v, z, xbc, dt_raw = in_proj(stream, mod, norm1_w, w_in_p, l)
    s5y = s5_mix(u, *s5_weights)
    na = natten(q, k, v, na_bias, l)
    xs, bm, cm, dt, a = ssd_prep(xbc, dt_raw, ssd_conv_w, ssd_conv_b, ssd_dt_bias, ssd_a_log, cos_t, sin_t)
    sdf, sdb = ssd_scan(xs, bm, cm, dt, a)

    x1, h, logits = post_mixer(stream, mod, u, s5y, na, xs, z, sdf, sdb, s5_d, s5_glu_w, s5_glu_b,
                               ssd_d, ssd_norm_w, w_out, norm2_w, moe_router)
    slot, aff = route(jnp.swapaxes(logits[:, :, :N_EXPERTS], 1, 2), with_ctx_out)
    nslot = CAP_LAT + CAP_CTX if with_ctx_out else CAP_LAT
    smin, smax = slot_ranges(slot)
    y = moe_ffn(moe_gather(h, slot, smin, smax, nslot), l, wg, wu, wd, nslot)
    return moe_combine(x1, mod, jnp.swapaxes(slot, 1, 2), jnp.swapaxes(aff, 1, 2), y, smin, smax, nslot, final_w)


def kernel(x, c, ctx, c_ctx, w_ada, b_ada, norm1_w, norm2_w, w_in, w_out, s5_lam_re, s5_lam_im, s5_log_dt, s5_b_re, s5_b_im, s5_c_re, s5_c_im, s5_d, s5_glu_w, s5_glu_b, na_rpb, ssd_conv_w, ssd_conv_b, ssd_dt_bias, ssd_a_log, ssd_d, ssd_norm_w, moe_router, moe_w_gate, moe_w_up, moe_w_down, final_norm_w):
    cvec = jnp.zeros((SUB, D_MODEL), F32).at[0].set(c_ctx.astype(F32)).at[1:1 + BATCH].set(c.astype(F32))
    mods = ada_mod(cvec, w_ada.astype(F32), b_ada.astype(F32)).reshape(DEPTH, SUB, 6, D_MODEL)
    mods = jnp.stack([jnp.broadcast_to(mods[:, 0:1], (DEPTH, BATCH, 6, D_MODEL)), mods[:, 1:1 + BATCH]], axis=2)
    cos_t, sin_t = rope_tables()
    w_in_p = jnp.zeros((DEPTH, D_MODEL, IN_COLS_PAD), BF16).at[:, :, :IN_COLS].set(w_in.astype(BF16))
    s5_weights = [s5_params(s5_lam_re[l], s5_lam_im[l], s5_log_dt[l], s5_b_re[l], s5_b_im[l], s5_c_re[l], s5_c_im[l])
                  for l in range(DEPTH)]
    na_bias = natten_bias(na_rpb.reshape(DEPTH * NA_HEADS, 2 * NA_KH - 1, RPB_W))
    wg, wu, wd = moe_w_gate.astype(F32), moe_w_up.astype(F32), moe_w_down.astype(F32)
    stream = (ctx.astype(F32), x.astype(F32), CTX_LEN // TILE)
    for l in range(DEPTH):
        xa = trunk_layer(
            l, stream, mods[l], cos_t, sin_t, norm1_w[l], norm2_w[l], w_in_p, w_out[l],
            s5_weights[l], s5_d[l], s5_glu_w[l], s5_glu_b[l],
            na_bias, ssd_conv_w[l], ssd_conv_b[l], ssd_dt_bias[l], ssd_a_log[l], ssd_d[l], ssd_norm_w[l],
            moe_router[l], wg, wu, wd, final_norm_w if l == DEPTH - 1 else None)
        stream = (xa, xa, 0)
    return xa
```

```python
import functools
import math

import jax
import jax.numpy as jnp
import numpy as np
from jax import lax
from jax.experimental import pallas as pl
from jax.experimental.pallas import tpu as pltpu

F32 = jnp.float32
BF16 = jnp.bfloat16
I32 = jnp.int32

D_MODEL = 1024
BATCH = 4
SEQ = 4096
DEPTH = 2
GRID_W = 64
CTX_LEN = 256
EPS = 1e-6

S5_WIDTH = 256
S5_GROUP = 16
S5_NGROUPS = 16
S5_STATE = 64
S5_NSTATE = S5_NGROUPS * S5_STATE

NA_HEADS = 6
NA_HEAD_DIM = 64
NA_WIDTH = 384
NA_KH = 8
NA_KW = 16
NA_BAND = 12
RPB_W = 2 * NA_KW - 1

SSD_HEADS = 6
SSD_HEAD_DIM = 64
SSD_WIDTH = 384
SSD_NGROUPS = 2
SSD_STATE = 128
SSD_CONV = 5
SSD_BC = 256
SSD_XBC = 896

N_EXPERTS = 16
D_EXPERT = 1024
ROPE_BASE = 10000.0

T = CTX_LEN + SEQ
TILE = 256
NT = T // TILE
BS = 2
BS_WIDE = 4
LANE = 128
SUB = 8
ROWS = SEQ // GRID_W
CAP_LAT = 2 * SEQ // N_EXPERTS
CAP_CTX = 2 * CTX_LEN // N_EXPERTS
NEG = -1e30

C_U = 0
C_Q = 256
C_K = 640
C_V = 1024
C_Z = 1408
C_XBC = 1792
C_DT = 2688
IN_COLS = 2700
IN_COLS_PAD = 2816

VMEM_LIMIT = 56 * 1024 * 1024


def _cp(*sem):
    return pltpu.CompilerParams(dimension_semantics=sem, vmem_limit_bytes=VMEM_LIMIT)


def _dot(a, b):
    return jnp.dot(a, b, preferred_element_type=F32)


def _dot_nt(a, b):
    return lax.dot_general(a, b, (((1,), (1,)), ((), ())), preferred_element_type=F32)


def _split3(x):
    hi = x.astype(BF16)
    r = x - hi.astype(F32)
    mid = r.astype(BF16)
    lo = (r - mid.astype(F32)).astype(BF16)
    return hi, mid, lo


def _dot_exact_rhs(a_bf16, b_f32):
    hi, mid, lo = _split3(b_f32)
    return _dot(a_bf16, hi) + _dot(a_bf16, mid) + _dot(a_bf16, lo)


def _dot_exact_lhs(a_f32, b_bf16):
    hi, mid, lo = _split3(a_f32)
    return _dot(hi, b_bf16) + _dot(mid, b_bf16) + _dot(lo, b_bf16)


def _dot_x3(a, b):
    ah = a.astype(BF16)
    al = (a - ah.astype(F32)).astype(BF16)
    bh = b.astype(BF16)
    bl = (b - bh.astype(F32)).astype(BF16)
    return _dot(ah, bh) + _dot(ah, bl) + _dot(al, bh)


def _silu(x):
    return x * jax.nn.sigmoid(x)


def _seg(t):
    return jnp.where(t >= CTX_LEN // TILE, 1, 0)


def _bwd_tile(i):
    return jnp.where(i == 0, 0, NT - i)


def _ada_kernel(c_ref, w_ref, b_ref, o_ref):
    s = _silu(c_ref[...])
    o_ref[0] = _dot_x3(s, w_ref[0]) + b_ref[0]


def ada_mod(cvec, w_ada, b_ada):
    nb = 1024
    return pl.pallas_call(
        _ada_kernel,
        grid=(DEPTH, 6 * D_MODEL // nb),
        in_specs=[pl.BlockSpec((SUB, D_MODEL), lambda l, j: (0, 0)),
                  pl.BlockSpec((1, D_MODEL, nb), lambda l, j: (l, 0, j)),
                  pl.BlockSpec((1, 1, nb), lambda l, j: (l, 0, j))],
        out_specs=pl.BlockSpec((1, SUB, nb), lambda l, j: (l, 0, j)),
        out_shape=jax.ShapeDtypeStruct((DEPTH, SUB, 6 * D_MODEL), F32),
        compiler_params=_cp("arbitrary", "arbitrary"),
        name="ada_mod",
    )(cvec, w_ada, b_ada.reshape(DEPTH, 1, 6 * D_MODEL))


def _modulated_norm(x, nw, shift, scale):
    y = x * lax.rsqrt(jnp.mean(x * x, axis=-1, keepdims=True) + EPS) * nw
    return y * (1.0 + scale) + shift


def _stream_specs(stream, bs=BS):
    _, _, off = stream
    return [pl.BlockSpec((bs, TILE, D_MODEL), lambda b, t: (b, 0, 0)),
            pl.BlockSpec((bs, TILE, D_MODEL), lambda b, t: (b, jnp.maximum(t - off, 0), 0))]


def _stream_tile(head_ref, tail_ref, bb):
    return jnp.where(pl.program_id(1) == 0, head_ref[bb], tail_ref[bb])


def _put_rows(ref, val, cast=None):
    for bb in range(ref.shape[0]):
        rows = val[bb * TILE:(bb + 1) * TILE]
        ref[bb] = rows if cast is None else rows.astype(cast)


def _inproj_kernel(xh_ref, xt_ref, mod_ref, nw_ref, w_ref, u_ref, q_ref, k_ref, v_ref, z_ref, xbc_ref, dt_ref):
    bs = xh_ref.shape[0]
    h = jnp.concatenate([_modulated_norm(_stream_tile(xh_ref, xt_ref, bb), nw_ref[...], mod_ref[bb, 0, 0:1, :],
                                         mod_ref[bb, 0, 1:2, :]).astype(BF16) for bb in range(bs)], axis=0)

    def proj(lo, hi):
        return _dot(h, w_ref[0, :, lo:hi])

    for s in range(S5_WIDTH // LANE):
        us = proj(C_U + s * LANE, C_U + (s + 1) * LANE)
        for bb in range(bs):
            u_ref[bb, s] = us[bb * TILE:(bb + 1) * TILE]
    _put_rows(q_ref, proj(C_Q, C_K) * (NA_HEAD_DIM ** -0.5), BF16)
    _put_rows(k_ref, proj(C_K, C_V), BF16)
    _put_rows(v_ref, proj(C_V, C_Z), BF16)
    _put_rows(z_ref, proj(C_Z, C_XBC), BF16)
    _put_rows(xbc_ref, proj(C_XBC, C_DT), BF16)
    _put_rows(dt_ref, proj(C_DT, IN_COLS_PAD))


def in_proj(stream, mod, norm_w, w_in_p, layer):
    bs = BS_WIDE
    tok = lambda w, dt: jax.ShapeDtypeStruct((BATCH, T, w), dt)
    tspec = lambda w: pl.BlockSpec((bs, TILE, w), lambda b, t: (b, t, 0))
    return pl.pallas_call(
        _inproj_kernel,
        grid=(BATCH // bs, NT),
        in_specs=_stream_specs(stream, bs) + [
                  pl.BlockSpec((bs, 1, 6, D_MODEL), lambda b, t: (b, _seg(t), 0, 0)),
                  pl.BlockSpec((1, D_MODEL), lambda b, t: (0, 0)),
                  pl.BlockSpec((1, D_MODEL, IN_COLS_PAD), lambda b, t: (layer, 0, 0))],
        out_specs=[pl.BlockSpec((bs, S5_WIDTH // LANE, TILE, LANE), lambda b, t: (b, 0, t, 0)),
                   tspec(NA_WIDTH), tspec(NA_WIDTH), tspec(NA_WIDTH),
                   tspec(SSD_WIDTH), tspec(SSD_XBC), tspec(LANE)],
        out_shape=[jax.ShapeDtypeStruct((BATCH, S5_WIDTH // LANE, T, LANE), F32),
                   tok(NA_WIDTH, BF16), tok(NA_WIDTH, BF16), tok(NA_WIDTH, BF16),
                   tok(SSD_WIDTH, BF16), tok(SSD_XBC, BF16), tok(LANE, F32)],
        compiler_params=_cp("arbitrary", "arbitrary"),
        name="in_proj",
    )(stream[0], stream[1], mod, norm_w.reshape(1, D_MODEL), w_in_p)


S5_BLK = SUB
NB = T // S5_BLK
NB_CTX = CTX_LEN // S5_BLK
S5_NPAIR = S5_NGROUPS // 2
S5_PW = 2 * S5_BLK * S5_GROUP


def _s5_kernel(u_ref, pin_ref, pout_ref, w1_ref, w2_ref, w3_ref, mul_ref, y_ref, ub_ref, yb_ref, st_ref):
    n = S5_NSTATE
    half = LANE
    per_half = S5_NPAIR // 2
    toks = [jnp.concatenate([u_ref[0, h, pl.ds(j, NB, stride=S5_BLK), :].astype(BF16) for j in range(S5_BLK)],
                            axis=1) for h in range(2)]
    for pp in range(S5_NPAIR):
        h, q = divmod(pp, per_half)
        ub_ref[:, pp * S5_PW:(pp + 1) * S5_PW] = _dot(toks[h], pin_ref[q]).astype(BF16)

    for d in range(2):
        for pp in range(S5_NPAIR):
            s = _dot(ub_ref[:, pp * S5_PW:(pp + 1) * S5_PW], w1_ref[d, pp])
            st_ref[d, :, pp * half:(pp + 1) * half] = s[:, :half]
            st_ref[d, :, n + pp * half:n + (pp + 1) * half] = s[:, half:]

    ngrp = NB // SUB
    nctx = NB_CTX // SUB
    rowid = lax.broadcasted_iota(I32, (SUB, n), 0)
    for d in range(2):
        def body(j, carry, d=d):
            cr, ci = carry
            r = j if d == 0 else jnp.where(j < nctx, nctx - 1 - j, ngrp - 1 + nctx - j)
            row = pl.multiple_of(r * SUB, SUB)
            re = st_ref[d, pl.ds(row, SUB), 0:n]
            im = st_ref[d, pl.ds(row, SUB), n:2 * n]
            for kk, sh in enumerate((1, 2, 4)):
                mr = mul_ref[d, kk * SUB:(kk + 1) * SUB, 0:n]
                mi = mul_ref[d, kk * SUB:(kk + 1) * SUB, n:2 * n]
                s = sh if d == 0 else SUB - sh
                sr = pltpu.roll(re, s, 0)
                si = pltpu.roll(im, s, 0)
                re, im = re + (mr * sr - mi * si), im + (mr * si + mi * sr)
            pr = mul_ref[d, 3 * SUB:4 * SUB, 0:n]
            pi = mul_ref[d, 3 * SUB:4 * SUB, n:2 * n]
            re, im = re + (pr * cr - pi * ci), im + (pr * ci + pi * cr)
            edge, last, sh = (0, SUB - 1, 1) if d == 0 else (SUB - 1, 0, SUB - 1)
            st_ref[d, pl.ds(row, SUB), 0:n] = jnp.where(rowid == edge, cr, pltpu.roll(re, sh, 0))
            st_ref[d, pl.ds(row, SUB), n:2 * n] = jnp.where(rowid == edge, ci, pltpu.roll(im, sh, 0))
            return re[last:last + 1, :], im[last:last + 1, :]

        zero = jnp.zeros((1, n), F32)
        lax.fori_loop(0, ngrp, body, (zero, zero), unroll=2)

    for pp in range(S5_NPAIR):
        up = ub_ref[:, pp * S5_PW:(pp + 1) * S5_PW]
        acc = None
        for d in range(2):
            enter = jnp.concatenate([st_ref[d, :, pp * half:(pp + 1) * half],
                                     st_ref[d, :, n + pp * half:n + (pp + 1) * half]], axis=1).astype(BF16)
            term = _dot(up, w2_ref[d, pp]) + _dot(enter, w3_ref[d, pp])
            acc = term if acc is None else acc + term
        yb_ref[:, pp * S5_PW:(pp + 1) * S5_PW] = acc.astype(BF16)

    for m in range(S5_BLK // 2):
        for h in range(2):
            acc = None
            for q in range(per_half):
                pp = h * per_half + q
                term = _dot(yb_ref[:, pp * S5_PW:(pp + 1) * S5_PW], pout_ref[q, m])
                acc = term if acc is None else acc + term
            for k in range(2):
                y_ref[0, h, pl.ds(2 * m + k, NB, stride=S5_BLK), :] = acc[:, k * LANE:(k + 1) * LANE]


def _s5_regroup_matrices():
    per_half = S5_NPAIR // 2
    pin = np.zeros((per_half, S5_BLK, LANE, S5_PW), np.float32)
    for q in range(per_half):
        for j in range(S5_BLK):
            for gg in range(2):
                for c in range(S5_GROUP):
                    pin[q, j, (2 * q + gg) * S5_GROUP + c, gg * S5_BLK * S5_GROUP + j * S5_GROUP + c] = 1.0
    pout = pin.transpose(0, 1, 3, 2).reshape(per_half, S5_BLK // 2, 2, S5_PW, LANE)
    pout = pout.transpose(0, 1, 3, 2, 4).reshape(per_half, S5_BLK // 2, S5_PW, 2 * LANE)
    return jnp.asarray(pin.reshape(per_half, S5_BLK * LANE, S5_PW), BF16), jnp.asarray(pout, BF16)


def s5_mix(u, w1, w2, w3, mul):
    pin, pout = _s5_regroup_matrices()
    per_half = S5_NPAIR // 2
    wspec = pl.BlockSpec((2, S5_NPAIR, S5_PW, S5_PW), lambda b: (0, 0, 0, 0))
    tok = pl.BlockSpec((1, S5_WIDTH // LANE, T, LANE), lambda b: (b, 0, 0, 0))
    return pl.pallas_call(
        _s5_kernel,
        grid=(BATCH,),
        in_specs=[tok, pl.BlockSpec((per_half, S5_BLK * LANE, S5_PW), lambda b: (0, 0, 0)),
                  pl.BlockSpec((per_half, S5_BLK // 2, S5_PW, 2 * LANE), lambda b: (0, 0, 0, 0)), wspec, wspec, wspec,
                  pl.BlockSpec((2, 4 * SUB, 2 * S5_NSTATE), lambda b: (0, 0, 0))],
        out_specs=tok,
        out_shape=jax.ShapeDtypeStruct((BATCH, S5_WIDTH // LANE, T, LANE), F32),
        scratch_shapes=[pltpu.VMEM((NB, S5_NGROUPS * S5_BLK * S5_GROUP), BF16),
                        pltpu.VMEM((NB, S5_NGROUPS * S5_BLK * S5_GROUP), BF16),
                        pltpu.VMEM((2, NB, 2 * S5_NSTATE), F32)],
        compiler_params=_cp("arbitrary"),
        name="s5_mix",
    )(u, pin, pout, w1, w2, w3, mul)


def s5_params(lam_re, lam_im, log_dt, b_re, b_im, c_re, c_im):
    G, P, C = S5_NGROUPS, S5_STATE, S5_GROUP
    lam = lax.complex(lam_re.astype(F32), lam_im.astype(F32))
    step = jnp.exp(log_dt.astype(F32))[..., None]
    log_lb = lam * step
    lam_bar = jnp.exp(log_lb)
    b_bar = ((lam_bar - 1.0) / lam)[..., None] * lax.complex(b_re.astype(F32), b_im.astype(F32))
    J = S5_BLK
    row = lambda a: a.reshape(2, G, 1, P)
    col = lambda a: jnp.broadcast_to(a[..., None], (2, G, P, J * C))
    b_t = jnp.swapaxes(b_bar, 2, 3)
    c_t = lambda a: jnp.tile(jnp.swapaxes(a.astype(F32), 2, 3), (1, 1, 1, J))
    wspec = pl.BlockSpec((1, 1, S5_PW, S5_PW), lambda d, q: (d, q, 0, 0))
    pair = lambda r, s: pl.BlockSpec((1, 2, r, s), lambda d, q: (d, q, 0, 0))
    w1, w2, w3 = pl.pallas_call(
        _s5_weight_kernel,
        grid=(2, S5_NPAIR),
        in_specs=[pair(1, P), pair(1, P), pair(P, J * C), pair(P, J * C), pair(C, P), pair(C, P),
                  pair(P, J * C), pair(P, J * C)],
        out_specs=[wspec, wspec, wspec],
        out_shape=[jax.ShapeDtypeStruct((2, S5_NPAIR, S5_PW, S5_PW), BF16)] * 3,
        compiler_params=_cp("arbitrary", "arbitrary"),
        name="s5_weights",
    )(row(jnp.real(log_lb)), row(jnp.imag(log_lb)), col(jnp.real(log_lb)), col(jnp.imag(log_lb)),
      jnp.real(b_t), jnp.imag(b_t), c_t(c_re), c_t(c_im))
    rows = jnp.arange(SUB)
    pieces = []
    for d in range(2):
        log_blk = (log_lb[d] * float(J)).reshape(1, G * P)
        per_d = []
        for sh in (1, 2, 4):
            valid = (rows >= sh) if d == 0 else (rows < SUB - sh)
            per_d.append(jnp.where(valid[:, None], jnp.exp(log_blk * float(sh)), 0.0))
        expo = (rows + 1) if d == 0 else (SUB - rows)
        per_d.append(jnp.exp(log_blk * expo[:, None].astype(F32)))
        m = jnp.concatenate(per_d, axis=0)
        pieces.append(jnp.concatenate([jnp.real(m), jnp.imag(m)], axis=-1))
    mul = jnp.stack(pieces, axis=0).astype(F32)
    return w1, w2, w3, mul


def _s5_weight_kernel(llr_ref, lli_ref, lcr_ref, lci_ref, btr_ref, bti_ref, ctr_ref, cti_ref, w1_ref, w2_ref, w3_ref):
    J, C, P = S5_BLK, S5_GROUP, S5_STATE
    R = J * C
    fwd = pl.program_id(0) == 0
    shift = C.bit_length() - 1
    j_of_row = lax.shift_right_logical(lax.broadcasted_iota(I32, (R, P), 0), shift).astype(F32)
    i_of_col = lax.shift_right_logical(lax.broadcasted_iota(I32, (P, R), 1), shift).astype(F32)
    jr = lax.shift_right_logical(lax.broadcasted_iota(I32, (R, R), 0), shift)
    ic = lax.shift_right_logical(lax.broadcasted_iota(I32, (R, R), 1), shift)
    lag = jnp.where(fwd, ic - jr, jr - ic)

    def cpow(expo, lr, li):
        mag = jnp.exp(lr * expo)
        return mag * jnp.cos(li * expo), mag * jnp.sin(li * expo)

    w1_ref[...] = jnp.zeros_like(w1_ref)
    w2_ref[...] = jnp.zeros_like(w2_ref)
    w3_ref[...] = jnp.zeros_like(w3_ref)
    for gg in range(2):
        llr, lli = llr_ref[0, gg], lli_ref[0, gg]
        br = jnp.concatenate([btr_ref[0, gg]] * J, axis=0)
        bi = jnp.concatenate([bti_ref[0, gg]] * J, axis=0)
        cr, ci = ctr_ref[0, gg], cti_ref[0, gg]
        pr, pi = cpow(jnp.where(fwd, (J - 1) - j_of_row, j_of_row), llr, lli)
        w1_ref[0, 0, gg * R:(gg + 1) * R, gg * P:(gg + 1) * P] = (pr * br - pi * bi).astype(BF16)
        w1_ref[0, 0, gg * R:(gg + 1) * R, 2 * P + gg * P:2 * P + (gg + 1) * P] = (pr * bi + pi * br).astype(BF16)
        qr, qi = cpow(jnp.where(fwd, i_of_col + 1.0, J - i_of_col), lcr_ref[0, gg], lci_ref[0, gg])
        w3_ref[0, 0, gg * P:(gg + 1) * P, gg * R:(gg + 1) * R] = (cr * qr - ci * qi).astype(BF16)
        w3_ref[0, 0, 2 * P + gg * P:2 * P + (gg + 1) * P, gg * R:(gg + 1) * R] = (-(cr * qi + ci * qr)).astype(BF16)
        acc = jnp.zeros((R, R), F32)
        for k in range(J):
            lr, li = cpow(float(k), llr, lli)
            t = _dot_x3(br * lr - bi * li, cr) - _dot_x3(br * li + bi * lr, ci)
            acc = acc + jnp.where(lag == k, t, 0.0)
        w2_ref[0, 0, gg * R:(gg + 1) * R, gg * R:(gg + 1) * R] = acc.astype(BF16)


def _softmax_pv(parts):
    m = parts[0][0].max(axis=-1, keepdims=True)
    for s, _ in parts[1:]:
        m = jnp.maximum(m, s.max(axis=-1, keepdims=True))
    den = 0.0
    acc = 0.0
    for s, v in parts:
        p = jnp.exp(s - m)
        den = den + p.sum(axis=-1, keepdims=True)
        acc = acc + _dot(p.astype(BF16), v)
    return acc / den


def _na_kernel(q_ref, k_ref, v_ref, bias_ref, o_ref):
    t = pl.program_id(1)
    first = lax.broadcasted_iota(I32, (1, LANE), 1) < NA_HEAD_DIM

    def pair_scores(bb, pp, start):
        ls = slice(pp * LANE, (pp + 1) * LANE)
        qp = q_ref[bb, :, ls]
        kc = k_ref[bb, 0:CTX_LEN, ls]
        vc = v_ref[bb, 0:CTX_LEN, ls]
        scores = []
        for hh in range(2):
            qm = jnp.where(first if hh == 0 else ~first, qp, jnp.zeros_like(qp))
            parts = [(_dot_nt(qm, kc), vc)]
            if start is not None:
                kb = k_ref[bb, pl.ds(start, NA_BAND * GRID_W), ls]
                vb = v_ref[bb, pl.ds(start, NA_BAND * GRID_W), ls]
                parts.append((_dot_nt(qm, kb) + bias_ref[2 * pp + hh, 0], vb))
            scores.append(parts)
        return scores

    def attention(start):
        for bb in range(BS):
            scores = [pair_scores(bb, pp, start) for pp in range(NA_HEADS // 2)]
            for pp, (s0, s1) in enumerate(scores):
                o_ref[bb, :, pp * LANE:(pp + 1) * LANE] = jnp.where(first, _softmax_pv(s0),
                                                                    _softmax_pv(s1)).astype(BF16)

    @pl.when(t == 0)
    def _():
        attention(None)

    @pl.when(t > 0)
    def _():
        first_row = (t - 1) * (TILE // GRID_W)
        u0 = jnp.clip(first_row - NA_KH // 2, 0, ROWS - NA_BAND)
        attention(pl.multiple_of(CTX_LEN + u0 * GRID_W, LANE))


def _na_cfg(t):
    return jnp.where(t <= 1, 0, jnp.where(t == NT - 1, 2, 1))


def natten(q, k, v, bias, layer):
    whole = pl.BlockSpec((BS, T, NA_WIDTH), lambda b, t: (b, 0, 0))
    tile = pl.BlockSpec((BS, TILE, NA_WIDTH), lambda b, t: (b, t, 0))
    return pl.pallas_call(
        _na_kernel,
        grid=(BATCH // BS, NT),
        in_specs=[tile, whole, whole,
                  pl.BlockSpec((NA_HEADS, 1, TILE, NA_BAND * GRID_W), lambda b, t: (layer, _na_cfg(t), 0, 0))],
        out_specs=tile,
        out_shape=jax.ShapeDtypeStruct((BATCH, T, NA_WIDTH), BF16),
        compiler_params=_cp("arbitrary", "arbitrary"),
        name="natten",
    )(q, k, v, bias)


def natten_bias(rpb):
    col = jnp.arange(GRID_W)
    c0 = jnp.clip(col - NA_KW // 2, 0, GRID_W - NA_KW)
    in_win = (col[None, :] >= c0[:, None]) & (col[None, :] < c0[:, None] + NA_KW)
    rel_c = jnp.clip(col[None, :] - col[:, None] + (NA_KW - 1), 0, RPB_W - 1)
    pick_c = jax.nn.one_hot(rel_c, RPB_W, dtype=F32)
    blocks = jnp.einsum('hax,qkx->haqk', rpb.astype(F32), pick_c, precision=lax.Precision.HIGHEST)
    blocks = jnp.where(in_win[None, None], blocks, NEG)
    nh = rpb.shape[0]
    blocks = jnp.concatenate([blocks, jnp.full((nh, 1, GRID_W, GRID_W), NEG, F32)], axis=1)
    blocks = jnp.concatenate([blocks, blocks], axis=-1)
    return pl.pallas_call(
        _bias_kernel,
        grid=(nh, 3),
        in_specs=[pl.BlockSpec((1, 2 * NA_KH, GRID_W, LANE), lambda h, c: (h, 0, 0, 0))],
        out_specs=pl.BlockSpec((1, 1, TILE, NA_BAND * GRID_W), lambda h, c: (h, c, 0, 0)),
        out_shape=jax.ShapeDtypeStruct((nh, 3, TILE, NA_BAND * GRID_W), F32),
        compiler_params=_cp("arbitrary", "arbitrary"),
        name="natten_bias",
    )(blocks)


def _bias_kernel(blk_ref, o_ref):
    rows_per_tile = TILE // GRID_W
    masked = 2 * NA_KH - 1
    for cfg, first_row in enumerate((0, 2 * rows_per_tile, ROWS - rows_per_tile)):
        @pl.when(pl.program_id(1) == cfg)
        def _(first_row=first_row):
            u0 = min(max(first_row - NA_KH // 2, 0), ROWS - NA_BAND)
            for rr in range(rows_per_tile):
                r = first_row + rr
                r0 = min(max(r - NA_KH // 2, 0), ROWS - NA_KH)
                for j in range(NA_BAND):
                    kr = u0 + j
                    a = kr - r + NA_KH - 1 if r0 <= kr < r0 + NA_KH else masked
                    half = (j % 2) * GRID_W
                    o_ref[0, 0, rr * GRID_W:(rr + 1) * GRID_W, j * GRID_W:(j + 1) * GRID_W] = (
                        blk_ref[0, a, :, half:half + GRID_W])


def _softplus(x):
    return jnp.maximum(x, 0.0) + jnp.log(1.0 + jnp.exp(-jnp.abs(x)))


def _ssd_prep_kernel(prev_ref, cur_ref, next_ref, dtr_ref, cw_ref, cb_ref, dtb_ref, ar_ref, cos_ref, sin_ref,
                     xs_ref, bm_ref, cm_ref, dt_ref, a_ref):
    for bb in range(cur_ref.shape[0]):
        _ssd_prep_sample(bb, prev_ref, cur_ref, next_ref, dtr_ref, cw_ref, cb_ref, dtb_ref, ar_ref, cos_ref, sin_ref,
                         xs_ref, bm_ref, cm_ref, dt_ref, a_ref)


def _ssd_prep_sample(bb, prev_ref, cur_ref, next_ref, dtr_ref, cw_ref, cb_ref, dtb_ref, ar_ref, cos_ref, sin_ref,
                     xs_ref, bm_ref, cm_ref, dt_ref, a_ref):
    t = pl.program_id(1)
    halo = prev_ref.shape[1]
    has_prev = t >= 2
    has_next = (t >= 1) & (t <= NT - 2)
    prev = jnp.where(has_prev, prev_ref[bb].astype(F32), 0.0)
    nxt = jnp.where(has_next, next_ref[bb].astype(F32), 0.0)
    ext = jnp.concatenate([prev, cur_ref[bb].astype(F32), nxt], axis=0)
    n = ext.shape[0]
    acc = cb_ref[...] + cw_ref[SSD_CONV // 2:SSD_CONV // 2 + 1, :] * ext
    for kk in range(SSD_CONV):
        off = kk - SSD_CONV // 2
        if off != 0:
            acc = acc + cw_ref[kk:kk + 1, :] * pltpu.roll(ext, (-off) % n, 0)
    y = _silu(acc[halo:halo + TILE, :])
    xs_ref[bb] = y[:, 0:SSD_WIDTH].astype(BF16)

    lane = lax.broadcasted_iota(I32, (1, LANE), 1)
    quarter = SSD_STATE // 4
    low = (lane & quarter) == 0
    cos = cos_ref[...]
    sin = sin_ref[...]
    for g in range(2 * SSD_NGROUPS):
        v = y[:, SSD_WIDTH + g * LANE:SSD_WIDTH + (g + 1) * LANE]
        sw = jnp.where(low, pltpu.roll(v, LANE - quarter, 1), pltpu.roll(v, quarter, 1))
        rot = (v * cos + sw * sin).astype(BF16)
        if g < SSD_NGROUPS:
            bm_ref[bb, :, g * LANE:(g + 1) * LANE] = rot
        else:
            cm_ref[bb, :, (g - SSD_NGROUPS) * LANE:(g - SSD_NGROUPS + 1) * LANE] = rot

    dt = _softplus(dtr_ref[bb] + dtb_ref[...])
    dt_ref[bb] = dt
    a_ref[bb] = dt * ar_ref[...]


def ssd_prep(xbc, dt_raw, conv_w, conv_b, dt_bias, a_log, cos_t, sin_t):
    halo = 2 * SUB
    per = TILE // halo
    nhalo = T // halo
    tok = lambda w, dt: jax.ShapeDtypeStruct((BATCH, T, w), dt)
    bs = BS_WIDE
    tspec = lambda w: pl.BlockSpec((bs, TILE, w), lambda b, t: (b, t, 0))
    row = lambda w: pl.BlockSpec((1, w), lambda b, t: (0, 0))
    cw = jnp.zeros((SUB, SSD_XBC), F32).at[:SSD_CONV].set(conv_w.astype(F32))
    pad12 = lambda v: jnp.zeros((1, LANE), F32).at[0, :2 * SSD_HEADS].set(v.astype(F32).reshape(-1))
    return pl.pallas_call(
        _ssd_prep_kernel,
        grid=(BATCH // bs, NT),
        in_specs=[pl.BlockSpec((bs, halo, SSD_XBC), lambda b, t: (b, jnp.maximum(t * per - 1, 0), 0)),
                  tspec(SSD_XBC),
                  pl.BlockSpec((bs, halo, SSD_XBC), lambda b, t: (b, jnp.minimum((t + 1) * per, nhalo - 1), 0)),
                  tspec(LANE),
                  pl.BlockSpec((SUB, SSD_XBC), lambda b, t: (0, 0)), row(SSD_XBC), row(LANE), row(LANE),
                  pl.BlockSpec((TILE, LANE), lambda b, t: (t, 0)), pl.BlockSpec((TILE, LANE), lambda b, t: (t, 0))],
        out_specs=[tspec(SSD_WIDTH), tspec(SSD_BC), tspec(SSD_BC), tspec(LANE), tspec(LANE)],
        out_shape=[tok(SSD_WIDTH, BF16), tok(SSD_BC, BF16), tok(SSD_BC, BF16), tok(LANE, F32), tok(LANE, F32)],
        compiler_params=_cp("arbitrary", "arbitrary"),
        name="ssd_prep",
    )(xbc, xbc, xbc, dt_raw, cw, conv_b.astype(F32).reshape(1, SSD_XBC), pad12(dt_bias),
      pad12(-jnp.exp(a_log.astype(F32))), cos_t, sin_t)


def rope_tables():
    half = SSD_STATE // 2
    nf = half // 2
    pos = jnp.arange(SEQ)
    inv_freq = ROPE_BASE ** (-jnp.arange(nf, dtype=F32) / nf)
    lane = jnp.arange(LANE)
    p = jnp.where(lane[None, :] < half, (pos // GRID_W)[:, None], (pos % GRID_W)[:, None]).astype(F32)
    ang = p * inv_freq[lane % nf][None, :]
    sign = jnp.where((lane & nf) == 0, -1.0, 1.0)[None, :]
    cos_t = jnp.concatenate([jnp.ones((CTX_LEN, LANE), F32), jnp.cos(ang)], axis=0)
    sin_t = jnp.concatenate([jnp.zeros((CTX_LEN, LANE), F32), jnp.sin(ang) * sign], axis=0)
    return cos_t, sin_t


def _ssd_sums(d, bm_ref, cm_ref, a_ref, at_ref, tri_ref):
    tri_col = tri_ref[d]
    tri_row = tri_ref[1 - d]
    cs_col = _dot_exact_rhs(tri_col, a_ref[0])
    cs_row = _dot_exact_lhs(at_ref[0], tri_row)
    g_mats = [_dot_nt(cm_ref[0, :, g * SSD_STATE:(g + 1) * SSD_STATE],
                      bm_ref[0, :, g * SSD_STATE:(g + 1) * SSD_STATE]) for g in range(SSD_NGROUPS)]
    return cs_col, cs_row, g_mats


def _ssd_dir(d, sums, xs_ref, cm_ref, bt_ref, dt_ref, y_ref, st_ref):
    q = TILE
    cs_col, cs_row, g_mats = sums
    lane = lax.broadcasted_iota(I32, (1, LANE), 1)
    first = lane < SSD_HEAD_DIM
    ri = lax.broadcasted_iota(I32, (q, q), 0)
    ci = lax.broadcasted_iota(I32, (q, q), 1)
    keep = (ci <= ri) if d == 0 else (ci >= ri)
    end = q - 1 if d == 0 else 0
    dt = dt_ref[0]

    def head_col(m, h):
        c = d * SSD_HEADS + h
        return m[:, c:c + 1]

    for pp in range(SSD_HEADS // 2):
        ls = slice(pp * LANE, (pp + 1) * LANE)
        h0, h1 = 2 * pp, 2 * pp + 1
        x = xs_ref[0, :, ls].astype(F32)
        dt_l = jnp.where(first, head_col(dt, h0), head_col(dt, h1))
        cs_l = jnp.where(first, head_col(cs_col, h0), head_col(cs_col, h1))
        cs_end = cs_l[end:end + 1, :]
        xdt = x * dt_l
        xdt_b = xdt.astype(BF16)
        xw = (xdt * jnp.exp(cs_end - cs_l)).astype(BF16)
        st = st_ref[d, pp]
        st_b = st.astype(BF16)
        ys, ups = [], []
        for h in (h0, h1):
            g = h // (SSD_HEADS // SSD_NGROUPS)
            c = d * SSD_HEADS + h
            diff = head_col(cs_col, h) - cs_row[c:c + 1, :]
            decay = jnp.exp(jnp.where(keep, diff, NEG))
            m = (g_mats[g] * decay).astype(BF16)
            y_h = _dot(m, xdt_b) + _dot(cm_ref[0, :, g * SSD_STATE:(g + 1) * SSD_STATE], st_b) * jnp.exp(cs_l)
            ys.append(y_h)
            ups.append(_dot(bt_ref[0, g * SSD_STATE:(g + 1) * SSD_STATE, :], xw))
        y_ref[0, :, ls] = jnp.where(first, ys[0], ys[1])
        st_ref[d, pp] = jnp.exp(cs_end) * st + jnp.where(first, ups[0], ups[1])


def _ssd_scan_kernel(xs_f, bm_f, cm_f, bt_f, dt_f, a_f, at_f, xs_b, bm_b, cm_b, bt_b, dt_b, a_b, at_b, tri_ref,
                     yf_ref, yb_ref, st_ref):
    @pl.when(pl.program_id(1) == 0)
    def _():
        st_ref[...] = jnp.zeros_like(st_ref)

    for bb in range(xs_f.shape[0]):
        one = lambda *refs: [r.at[pl.ds(bb, 1)] for r in refs]
        xsf, bmf, cmf, btf, dtf, af, atf, yf = one(xs_f, bm_f, cm_f, bt_f, dt_f, a_f, at_f, yf_ref)
        xsb, bmb, cmb, btb, dtb, ab, atb, yb = one(xs_b, bm_b, cm_b, bt_b, dt_b, a_b, at_b, yb_ref)
        st = st_ref.at[bb]
        sums_f = _ssd_sums(0, bmf, cmf, af, atf, tri_ref)
        sums_b = _ssd_sums(1, bmb, cmb, ab, atb, tri_ref)
        _ssd_dir(0, sums_f, xsf, cmf, btf, dtf, yf, st)
        _ssd_dir(1, sums_b, xsb, cmb, btb, dtb, yb, st)


def ssd_scan(xs, bm, cm, dt, a):
    bs = BS_WIDE
    bt = jnp.swapaxes(bm, 1, 2)
    at = jnp.swapaxes(a[:, :, :2 * SUB], 1, 2)
    idx = jnp.arange(TILE)
    tri = jnp.stack([idx[None, :] <= idx[:, None], idx[None, :] >= idx[:, None]]).astype(BF16)
    fwd = lambda b, i: (b, i, 0)
    bwd = lambda b, i: (b, _bwd_tile(i), 0)
    fwd_t = lambda b, i: (b, 0, i)
    bwd_t = lambda b, i: (b, 0, _bwd_tile(i))

    def specs(f, ft):
        return [pl.BlockSpec((bs, TILE, SSD_WIDTH), f), pl.BlockSpec((bs, TILE, SSD_BC), f),
                pl.BlockSpec((bs, TILE, SSD_BC), f), pl.BlockSpec((bs, SSD_BC, TILE), ft),
                pl.BlockSpec((bs, TILE, LANE), f), pl.BlockSpec((bs, TILE, LANE), f),
                pl.BlockSpec((bs, 2 * SUB, TILE), ft)]

    args = (xs, bm, cm, bt, dt, a, at)
    return pl.pallas_call(
        _ssd_scan_kernel,
        grid=(BATCH // bs, NT),
        in_specs=specs(fwd, fwd_t) + specs(bwd, bwd_t) + [pl.BlockSpec((2, TILE, TILE), lambda b, i: (0, 0, 0))],
        out_specs=[pl.BlockSpec((bs, TILE, SSD_WIDTH), fwd), pl.BlockSpec((bs, TILE, SSD_WIDTH), bwd)],
        out_shape=[jax.ShapeDtypeStruct((BATCH, T, SSD_WIDTH), F32)] * 2,
        scratch_shapes=[pltpu.VMEM((bs, 2, SSD_HEADS // 2, SSD_STATE, LANE), F32)],
        compiler_params=_cp("arbitrary", "arbitrary"),
        name="ssd_scan",
    )(*args, *args, tri)


def _gelu_tanh(x):
    return 0.5 * x * (1.0 + jnp.tanh(math.sqrt(2.0 / math.pi) * (x + 0.044715 * (x * x * x))))


def _post_kernel(xh_ref, xt_ref, mod_ref, u_ref, s5y_ref, na_ref, xs_ref, z_ref, sdf_ref, sdb_ref,
                 s5d_ref, gw_ref, gb_ref, sdd_ref, snw_ref, wo_ref, n2w_ref, rt_ref,
                 x1_ref, h_ref, lg_ref):
    for bb in range(BS):
        slabs = lambda r: jnp.concatenate([r[bb, s] for s in range(S5_WIDTH // LANE)], axis=-1)
        ys5 = slabs(u_ref) * s5d_ref[...] + slabs(s5y_ref)
        g = _gelu_tanh(ys5)
        s5o = g * jax.nn.sigmoid(_dot(g.astype(BF16), gw_ref[...]) + gb_ref[...])
        yssd = (xs_ref[bb].astype(F32) * sdd_ref[...] + sdf_ref[bb] + sdb_ref[bb]) * _silu(z_ref[bb].astype(F32))
        ssdo = yssd * lax.rsqrt(jnp.mean(yssd * yssd, axis=-1, keepdims=True) + EPS) * snw_ref[...]
        mix = jnp.concatenate([s5o.astype(BF16), na_ref[bb], ssdo.astype(BF16)], axis=-1)
        x1 = _stream_tile(xh_ref, xt_ref, bb) + mod_ref[bb, 0, 2:3, :] * _dot(mix, wo_ref[...])
        x1_ref[bb] = x1
        h = _modulated_norm(x1, n2w_ref[...], mod_ref[bb, 0, 3:4, :], mod_ref[bb, 0, 4:5, :])
        h_ref[bb] = h.astype(BF16)
        lg_ref[bb] = _dot_x3(h, rt_ref[...])


def post_mixer(stream, mod, u, s5y, na, xs, z, sdf, sdb, s5_d, glu_w, glu_b, ssd_d, ssd_norm_w, w_out, norm2_w,
               router):
    tspec = lambda w: pl.BlockSpec((BS, TILE, w), lambda b, t: (b, t, 0))
    whole = lambda *shp: pl.BlockSpec(shp, lambda b, t: (0,) * len(shp))
    rt = jnp.zeros((D_MODEL, LANE), F32).at[:, :N_EXPERTS].set(router.astype(F32))
    slab = pl.BlockSpec((BS, S5_WIDTH // LANE, TILE, LANE), lambda b, t: (b, 0, t, 0))
    return pl.pallas_call(
        _post_kernel,
        grid=(BATCH // BS, NT),
        in_specs=_stream_specs(stream) + [
                  pl.BlockSpec((BS, 1, 6, D_MODEL), lambda b, t: (b, _seg(t), 0, 0)),
                  slab, slab, tspec(NA_WIDTH),
                  tspec(SSD_WIDTH), tspec(SSD_WIDTH), tspec(SSD_WIDTH), tspec(SSD_WIDTH),
                  whole(1, S5_WIDTH), whole(S5_WIDTH, S5_WIDTH), whole(1, S5_WIDTH),
                  whole(1, SSD_WIDTH), whole(1, SSD_WIDTH), whole(D_MODEL, D_MODEL), whole(1, D_MODEL),
                  whole(D_MODEL, LANE)],
        out_specs=[tspec(D_MODEL), tspec(D_MODEL), tspec(LANE)],
        out_shape=[jax.ShapeDtypeStruct((BATCH, T, D_MODEL), F32), jax.ShapeDtypeStruct((BATCH, T, D_MODEL), BF16),
                   jax.ShapeDtypeStruct((BATCH, T, LANE), F32)],
        compiler_params=_cp("arbitrary", "arbitrary"),
        name="post_mixer",
    )(stream[0], stream[1], mod, u, s5y, na, xs, z, sdf, sdb,
      s5_d.astype(F32).reshape(1, S5_WIDTH), glu_w.astype(BF16), glu_b.astype(F32).reshape(1, S5_WIDTH),
      jnp.repeat(ssd_d.astype(F32), SSD_HEAD_DIM).reshape(1, SSD_WIDTH), ssd_norm_w.astype(F32).reshape(1, SSD_WIDTH),
      w_out.astype(BF16), norm2_w.astype(F32).reshape(1, D_MODEL), rt)


def _route_kernel(lg_ref, tri_ref, slot_ref, aff_ref, *, with_ctx):
    lg = lg_ref[0]
    m = lg.max(axis=0, keepdims=True)
    e = jnp.exp(lg - m)
    aff = e / e.sum(axis=0, keepdims=True)
    aff_ref[0] = aff
    bits = pltpu.bitcast(aff, I32)
    is_ctx = lax.broadcasted_iota(I32, (N_EXPERTS, T), 1) < CTX_LEN

    def count(mask):
        return jnp.where(mask, 1.0, 0.0).sum(axis=1, keepdims=True)

    def kth_largest(seg, k):
        def body(i, prefix):
            cand = prefix | lax.shift_left(jnp.int32(1), 30 - i)
            return jnp.where(count((bits >= cand) & seg) >= k, cand, prefix)
        return lax.fori_loop(0, 31, body, jnp.zeros((N_EXPERTS, 1), I32))

    def excl_cumsum(x01):
        carry = jnp.zeros((N_EXPERTS, 1), F32)
        pieces = []
        for j in range(T // LANE):
            blk = x01[:, j * LANE:(j + 1) * LANE]
            inc = _dot(blk.astype(BF16), tri_ref[...])
            pieces.append(inc - blk + carry)
            carry = carry + inc[:, LANE - 1:LANE]
        return jnp.concatenate(pieces, axis=1)

    thr = kth_largest(~is_ctx, float(CAP_LAT))
    k_of = jnp.full((N_EXPERTS, T), float(CAP_LAT), F32)
    if with_ctx:
        thr = jnp.where(is_ctx, kth_largest(is_ctx, float(CAP_CTX)), thr)
        k_of = jnp.where(is_ctx, float(CAP_CTX), k_of)
    gt = bits > thr
    eq = bits == thr
    if not with_ctx:
        gt = gt & ~is_ctx
        eq = eq & ~is_ctx
    n_gt = jnp.where(is_ctx, count(gt & is_ctx), count(gt & ~is_ctx))
    tie_rank = excl_cumsum(jnp.where(eq, 1.0, 0.0))
    tie_rank = tie_rank - jnp.where(is_ctx, 0.0, count(eq & is_ctx))
    sel = gt | (eq & (tie_rank < k_of - n_gt))
    pos = excl_cumsum(jnp.where(sel, 1.0, 0.0))
    slot = jnp.where(is_ctx, pos + float(CAP_LAT), pos - count(sel & is_ctx))
    slot_ref[0] = jnp.where(sel, slot, -1.0).astype(I32)


def route(logits_t, with_ctx):
    idx = jnp.arange(LANE)
    tri = (idx[:, None] <= idx[None, :]).astype(BF16)
    spec = pl.BlockSpec((1, N_EXPERTS, T), lambda b: (b, 0, 0))
    return pl.pallas_call(
        functools.partial(_route_kernel, with_ctx=with_ctx),
        grid=(BATCH,),
        in_specs=[spec, pl.BlockSpec((LANE, LANE), lambda b: (0, 0))],
        out_specs=[spec, spec],
        out_shape=[jax.ShapeDtypeStruct((BATCH, N_EXPERTS, T), I32), jax.ShapeDtypeStruct((BATCH, N_EXPERTS, T), F32)],
        compiler_params=_cp("arbitrary"),
        name="route",
    )(logits_t, tri)


GATHER_WIN = LANE // 2
COMBINE_WIN = LANE // 2


def slot_ranges(slot):
    s = slot.reshape(BATCH, N_EXPERTS, NT, TILE)
    has = s >= 0
    smax = jnp.max(jnp.where(has, s, -1), axis=-1)
    smin = jnp.where(smax >= 0, jnp.min(jnp.where(has, s, CAP_LAT + CAP_CTX), axis=-1), 0)
    return smin.reshape(-1).astype(I32), smax.reshape(-1).astype(I32)


def _gather_kernel(smin_ref, smax_ref, h_ref, slot_ref, xs_ref, *, nslot):
    b = pl.program_id(0)
    t = pl.program_id(1)
    group = N_EXPERTS
    sid = lax.broadcasted_iota(I32, (GATHER_WIN, TILE), 0)
    align = 2 * SUB

    def onehot(e, ws, lo):
        srow = slot_ref[0, e:e + 1, :]
        return jnp.where((sid + ws == srow) & (srow >= lo), 1.0, 0.0).astype(BF16)

    def add_rows(e, ws, rows):
        win = pl.ds(pl.multiple_of(ws, align), GATHER_WIN)
        xs_ref[0, e, win, :] = (xs_ref[0, e, win, :].astype(F32) + rows).astype(BF16)

    @pl.when(t == 0)
    def _():
        for e in range(N_EXPERTS):
            xs_ref[0, e, 0:CAP_LAT, :] = jnp.zeros((CAP_LAT, D_MODEL), BF16)
            if nslot > CAP_LAT:
                cid = lax.broadcasted_iota(I32, (nslot - CAP_LAT, TILE), 0) + CAP_LAT
                pick = jnp.where(cid == slot_ref[0, e:e + 1, :], 1.0, 0.0).astype(BF16)
                xs_ref[0, e, CAP_LAT:nslot, :] = _dot(pick, h_ref[0]).astype(BF16)

    @pl.when(t > 0)
    def _():
        los, wss, extras = [], [], []
        for e in range(N_EXPERTS):
            base = (b * N_EXPERTS + e) * NT + t
            lo = smin_ref[base] & ~(align - 1)
            los.append(lo)
            wss.append(jnp.minimum(lo, CAP_LAT - GATHER_WIN))
            extras.append(lax.shift_right_arithmetic(smax_ref[base] - lo, GATHER_WIN.bit_length() - 1))
        for g0 in range(0, N_EXPERTS, group):
            pick = jnp.concatenate([onehot(e, wss[e], los[e]) for e in range(g0, g0 + group)], axis=0)
            rows = _dot(pick, h_ref[0])
            for i, e in enumerate(range(g0, g0 + group)):
                add_rows(e, wss[e], rows[i * GATHER_WIN:(i + 1) * GATHER_WIN, :])
        most = extras[0]
        for x in extras[1:]:
            most = jnp.maximum(most, x)

        @pl.when(most > 0)
        def _():
            for e in range(N_EXPERTS):
                def more(k, carry, e=e):
                    lo_k = los[e] + k * GATHER_WIN
                    ws = jnp.minimum(lo_k, CAP_LAT - GATHER_WIN)
                    add_rows(e, ws, _dot(onehot(e, ws, lo_k), h_ref[0]))
                    return carry

                lax.fori_loop(1, extras[e] + 1, more, 0)


def moe_gather(h, slot, smin, smax, nslot):
    return pl.pallas_call(
        functools.partial(_gather_kernel, nslot=nslot),
        grid_spec=pltpu.PrefetchScalarGridSpec(
            num_scalar_prefetch=2,
            grid=(BATCH, NT),
            in_specs=[pl.BlockSpec((1, TILE, D_MODEL), lambda b, t, *_: (b, t, 0)),
                      pl.BlockSpec((1, N_EXPERTS, TILE), lambda b, t, *_: (b, 0, t))],
            out_specs=pl.BlockSpec((1, N_EXPERTS, nslot, D_MODEL), lambda b, t, *_: (b, 0, 0, 0))),
        out_shape=jax.ShapeDtypeStruct((BATCH, N_EXPERTS, nslot, D_MODEL), BF16),
        compiler_params=_cp("arbitrary", "arbitrary"),
        name="moe_gather",
    )(smin, smax, h, slot)


def _ffn_kernel(xs_ref, wg_ref, wu_ref, wd_ref, y_ref, w_ref):
    @pl.when(pl.program_id(1) == 0)
    def _():
        w_ref[0] = wg_ref[0, 0].astype(BF16)
        w_ref[1] = wu_ref[0, 0].astype(BF16)
        w_ref[2] = wd_ref[0, 0].astype(BF16)

    xs = xs_ref[0, 0]
    hid = _silu(_dot(xs, w_ref[0])) * _dot(xs, w_ref[1])
    y_ref[0, 0] = _dot(hid.astype(BF16), w_ref[2]).astype(BF16)


def moe_ffn(xs, layer, wg, wu, wd, nslot):
    wspec = lambda: pl.BlockSpec((1, 1, D_MODEL, D_EXPERT), lambda e, b: (layer, e, 0, 0))
    rows = pl.BlockSpec((1, 1, nslot, D_MODEL), lambda e, b: (b, e, 0, 0))
    return pl.pallas_call(
        _ffn_kernel,
        grid=(N_EXPERTS, BATCH),
        in_specs=[rows, wspec(), wspec(), wspec()],
        out_specs=rows,
        out_shape=jax.ShapeDtypeStruct((BATCH, N_EXPERTS, nslot, D_MODEL), BF16),
        scratch_shapes=[pltpu.VMEM((3, D_MODEL, D_EXPERT), BF16)],
        compiler_params=_cp("arbitrary", "arbitrary"),
        name="moe_ffn",
    )(xs, wg, wu, wd)


def _combine_kernel(smin_ref, smax_ref, x_ref, mod_ref, slot_ref, aff_ref, y_ref, fw_ref, o_ref, acc_ref, *,
                    nslot, last):
    b = pl.program_id(0)
    t = pl.program_id(1)
    shift = COMBINE_WIN.bit_length() - 1
    align = 2 * SUB

    def weights(e, lane, lo=None):
        s = slot_ref[0, :, e:e + 1]
        hit = (s == lane) if lo is None else ((s == lane) & (s >= lo))
        return jnp.where(hit, aff_ref[0, :, e:e + 1], 0.0).astype(BF16)

    def finish(acc):
        x2 = x_ref[0] + mod_ref[0, 0, 5:6, :] * acc
        if last:
            x2 = x2 * lax.rsqrt(jnp.mean(x2 * x2, axis=-1, keepdims=True) + EPS) * fw_ref[...]
        o_ref[0] = x2

    if not last:
        @pl.when(t == 0)
        def _():
            lane = lax.broadcasted_iota(I32, (TILE, nslot - CAP_LAT), 1) + CAP_LAT
            acc = jnp.zeros((TILE, D_MODEL), F32)
            for e in range(N_EXPERTS):
                acc = acc + _dot(weights(e, lane), y_ref[0, e, CAP_LAT:nslot, :])
            finish(acc)

    @pl.when(t > 0)
    def _():
        lane = lax.broadcasted_iota(I32, (TILE, COMBINE_WIN), 1)
        lane2 = lax.broadcasted_iota(I32, (TILE, 2 * COMBINE_WIN), 1)
        first = lane2 < COMBINE_WIN
        los, wss, extras = [], [], []
        for e in range(N_EXPERTS):
            base = (b * N_EXPERTS + e) * NT + t
            lo = smin_ref[base] & ~(align - 1)
            los.append(lo)
            wss.append(pl.multiple_of(jnp.minimum(lo, CAP_LAT - COMBINE_WIN), align))
            extras.append(lax.shift_right_arithmetic(smax_ref[base] - lo, shift))
        w_parts, y_parts = [], []
        for e in range(0, N_EXPERTS, 2):
            want = jnp.where(first, slot_ref[0, :, e:e + 1] - wss[e],
                             slot_ref[0, :, e + 1:e + 2] - wss[e + 1] + COMBINE_WIN)
            gate = jnp.where(first, aff_ref[0, :, e:e + 1], aff_ref[0, :, e + 1:e + 2])
            w_parts.append(jnp.where(want == lane2, gate, 0.0).astype(BF16))
            y_parts += [y_ref[0, e, pl.ds(wss[e], COMBINE_WIN), :], y_ref[0, e + 1, pl.ds(wss[e + 1], COMBINE_WIN), :]]
        acc_ref[...] = _dot(jnp.concatenate(w_parts, axis=1), jnp.concatenate(y_parts, axis=0))
        most = extras[0]
        for x in extras[1:]:
            most = jnp.maximum(most, x)

        @pl.when(most > 0)
        def _():
            for e in range(N_EXPERTS):
                def more(k, carry, e=e):
                    lo_k = los[e] + k * COMBINE_WIN
                    ws = pl.multiple_of(jnp.minimum(lo_k, CAP_LAT - COMBINE_WIN), align)
                    acc_ref[...] += _dot(weights(e, lane + ws, lo_k), y_ref[0, e, pl.ds(ws, COMBINE_WIN), :])
                    return carry

                lax.fori_loop(1, extras[e] + 1, more, 0)

        finish(acc_ref[...])


def moe_combine(x1, mod, slot_tok, aff_tok, y, smin, smax, nslot, final_w):
    last = final_w is not None
    tspec = lambda w: pl.BlockSpec((1, TILE, w), lambda b, t, *_: (b, t, 0))
    if last:
        first_lat = CTX_LEN // TILE
        out_spec = pl.BlockSpec((1, TILE, D_MODEL), lambda b, t, *_: (b, jnp.maximum(t - first_lat, 0), 0))
        out_shape = jax.ShapeDtypeStruct((BATCH, SEQ, D_MODEL), F32)
        fw = final_w.astype(F32).reshape(1, D_MODEL)
    else:
        out_spec = tspec(D_MODEL)
        out_shape = jax.ShapeDtypeStruct((BATCH, T, D_MODEL), F32)
        fw = jnp.ones((1, D_MODEL), F32)
    return pl.pallas_call(
        functools.partial(_combine_kernel, nslot=nslot, last=last),
        grid_spec=pltpu.PrefetchScalarGridSpec(
            num_scalar_prefetch=2,
            grid=(BATCH, NT),
            in_specs=[tspec(D_MODEL),
                      pl.BlockSpec((1, 1, 6, D_MODEL), lambda b, t, *_: (b, _seg(t), 0, 0)),
                      tspec(N_EXPERTS), tspec(N_EXPERTS),
                      pl.BlockSpec((1, N_EXPERTS, nslot, D_MODEL), lambda b, t, *_: (b, 0, 0, 0)),
                      pl.BlockSpec((1, D_MODEL), lambda b, t, *_: (0, 0))],
            out_specs=out_spec,
            scratch_shapes=[pltpu.VMEM((TILE, D_MODEL), F32)]),
        out_shape=out_shape,
        compiler_params=_cp("arbitrary", "arbitrary"),
        name="moe_combine",
    )(smin, smax, x1, mod, slot_tok, aff_tok, y, fw)


def trunk_layer(l, stream, mod, cos_t, sin_t, norm1_w, norm2_w, w_in_p, w_out, s5_weights, s5_d, s5_glu_w, s5_glu_b,
                na_bias, ssd_conv_w, ssd_conv_b, ssd_dt_bias, ssd_a_log, ssd_d, ssd_norm_w,
                moe_router, wg, wu, wd, final_w):
    with_ctx_out = final_w is None
    u, q, k, v, z, xbc, dt_raw = in_proj(stream, mod, norm1_w, w_in_p, l)
    s5y = s5_mix(u, *s5_weights)
    na = natten(q, k, v, na_bias, l)
    xs, bm, cm, dt, a = ssd_prep(xbc, dt_raw, ssd_conv_w, ssd_conv_b, ssd_dt_bias, ssd_a_log, cos_t, sin_t)
    sdf, sdb = ssd_scan(xs, bm, cm, dt, a)

    x1, h, logits = post_mixer(stream, mod, u, s5y, na, xs, z, sdf, sdb, s5_d, s5_glu_w, s5_glu_b,
                               ssd_d, ssd_norm_w, w_out, norm2_w, moe_router)
    slot, aff = route(jnp.swapaxes(logits[:, :, :N_EXPERTS], 1, 2), with_ctx_out)
    nslot = CAP_LAT + CAP_CTX if with_ctx_out else CAP_LAT
    smin, smax = slot_ranges(slot)
    y = moe_ffn(moe_gather(h, slot, smin, smax, nslot), l, wg, wu, wd, nslot)
    return moe_combine(x1, mod, jnp.swapaxes(slot, 1, 2), jnp.swapaxes(aff, 1, 2), y, smin, smax, nslot, final_w)


def kernel(x, c, ctx, c_ctx, w_ada, b_ada, norm1_w, norm2_w, w_in, w_out, s5_lam_re, s5_lam_im, s5_log_dt, s5_b_re, s5_b_im, s5_c_re, s5_c_im, s5_d, s5_glu_w, s5_glu_b, na_rpb, ssd_conv_w, ssd_conv_b, ssd_dt_bias, ssd_a_log, ssd_d, ssd_norm_w, moe_router, moe_w_gate, moe_w_up, moe_w_down, final_norm_w):
    cvec = jnp.zeros((SUB, D_MODEL), F32).at[0].set(c_ctx.astype(F32)).at[1:1 + BATCH].set(c.astype(F32))
    mods = ada_mod(cvec, w_ada.astype(F32), b_ada.astype(F32)).reshape(DEPTH, SUB, 6, D_MODEL)
    mods = jnp.stack([jnp.broadcast_to(mods[:, 0:1], (DEPTH, BATCH, 6, D_MODEL)), mods[:, 1:1 + BATCH]], axis=2)
    cos_t, sin_t = rope_tables()
    w_in_p = jnp.zeros((DEPTH, D_MODEL, IN_COLS_PAD), BF16).at[:, :, :IN_COLS].set(w_in.astype(BF16))
    s5_weights = [s5_params(s5_lam_re[l], s5_lam_im[l], s5_log_dt[l], s5_b_re[l], s5_b_im[l], s5_c_re[l], s5_c_im[l])
                  for l in range(DEPTH)]
    na_bias = natten_bias(na_rpb.reshape(DEPTH * NA_HEADS, 2 * NA_KH - 1, RPB_W))
    wg, wu, wd = moe_w_gate.astype(F32), moe_w_up.astype(F32), moe_w_down.astype(F32)
    stream = (ctx.astype(F32), x.astype(F32), CTX_LEN // TILE)
    for l in range(DEPTH):
        xa = trunk_layer(
            l, stream, mods[l], cos_t, sin_t, norm1_w[l], norm2_w[l], w_in_p, w_out[l],
            s5_weights[l], s5_d[l], s5_glu_w[l], s5_glu_b[l],
            na_bias, ssd_conv_w[l], ssd_conv_b[l], ssd_dt_bias[l], ssd_a_log[l], ssd_d[l], ssd_norm_w[l],
            moe_router[l], wg, wu, wd, final_norm_w if l == DEPTH - 1 else None)
        stream = (xa, xa, 0)
    return xa
```

```python
import functools
import math

import jax
import jax.numpy as jnp
import numpy as np
from jax import lax
from jax.experimental import pallas as pl
from jax.experimental.pallas import tpu as pltpu

F32 = jnp.float32
BF16 = jnp.bfloat16
I32 = jnp.int32

D_MODEL = 1024
BATCH = 4
SEQ = 4096
DEPTH = 2
GRID_W = 64
CTX_LEN = 256
EPS = 1e-6

S5_WIDTH = 256
S5_GROUP = 16
S5_NGROUPS = 16
S5_STATE = 64
S5_NSTATE = S5_NGROUPS * S5_STATE

NA_HEADS = 6
NA_HEAD_DIM = 64
NA_WIDTH = 384
NA_KH = 8
NA_KW = 16
NA_BAND = 12
RPB_W = 2 * NA_KW - 1

SSD_HEADS = 6
SSD_HEAD_DIM = 64
SSD_WIDTH = 384
SSD_NGROUPS = 2
SSD_STATE = 128
SSD_CONV = 5
SSD_BC = 256
SSD_XBC = 896

N_EXPERTS = 16
D_EXPERT = 1024
ROPE_BASE = 10000.0

T = CTX_LEN + SEQ
TILE = 256
NT = T // TILE
BS = 2
BS_WIDE = 4
LANE = 128
SUB = 8
ROWS = SEQ // GRID_W
CAP_LAT = 2 * SEQ // N_EXPERTS
CAP_CTX = 2 * CTX_LEN // N_EXPERTS
NEG = -1e30

C_U = 0
C_Q = 256
C_K = 640
C_V = 1024
C_Z = 1408
C_XBC = 1792
C_DT = 2688
IN_COLS = 2700
IN_COLS_PAD = 2816

VMEM_LIMIT = 56 * 1024 * 1024


def _cp(*sem):
    return pltpu.CompilerParams(dimension_semantics=sem, vmem_limit_bytes=VMEM_LIMIT)


def _dot(a, b):
    return jnp.dot(a, b, preferred_element_type=F32)


def _dot_nt(a, b):
    return lax.dot_general(a, b, (((1,), (1,)), ((), ())), preferred_element_type=F32)


def _split3(x):
    hi = x.astype(BF16)
    r = x - hi.astype(F32)
    mid = r.astype(BF16)
    lo = (r - mid.astype(F32)).astype(BF16)
    return hi, mid, lo


def _dot_exact_rhs(a_bf16, b_f32):
    hi, mid, lo = _split3(b_f32)
    return _dot(a_bf16, hi) + _dot(a_bf16, mid) + _dot(a_bf16, lo)


def _dot_exact_lhs(a_f32, b_bf16):
    hi, mid, lo = _split3(a_f32)
    return _dot(hi, b_bf16) + _dot(mid, b_bf16) + _dot(lo, b_bf16)


def _dot_x3(a, b):
    ah = a.astype(BF16)
    al = (a - ah.astype(F32)).astype(BF16)
    bh = b.astype(BF16)
    bl = (b - bh.astype(F32)).astype(BF16)
    return _dot(ah, bh) + _dot(ah, bl) + _dot(al, bh)


def _silu(x):
    return x * jax.nn.sigmoid(x)


def _seg(t):
    return jnp.where(t >= CTX_LEN // TILE, 1, 0)


def _bwd_tile(i):
    return jnp.where(i == 0, 0, NT - i)


def _ada_kernel(c_ref, w_ref, b_ref, o_ref):
    s = _silu(c_ref[...])
    o_ref[0] = _dot_x3(s, w_ref[0]) + b_ref[0]


def ada_mod(cvec, w_ada, b_ada):
    nb = 1024
    return pl.pallas_call(
        _ada_kernel,
        grid=(DEPTH, 6 * D_MODEL // nb),
        in_specs=[pl.BlockSpec((SUB, D_MODEL), lambda l, j: (0, 0)),
                  pl.BlockSpec((1, D_MODEL, nb), lambda l, j: (l, 0, j)),
                  pl.BlockSpec((1, 1, nb), lambda l, j: (l, 0, j))],
        out_specs=pl.BlockSpec((1, SUB, nb), lambda l, j: (l, 0, j)),
        out_shape=jax.ShapeDtypeStruct((DEPTH, SUB, 6 * D_MODEL), F32),
        compiler_params=_cp("arbitrary", "arbitrary"),
        name="ada_mod",
    )(cvec, w_ada, b_ada.reshape(DEPTH, 1, 6 * D_MODEL))


def _modulated_norm(x, nw, shift, scale):
    y = x * lax.rsqrt(jnp.mean(x * x, axis=-1, keepdims=True) + EPS) * nw
    return y * (1.0 + scale) + shift


def _stream_specs(stream, bs=BS):
    _, _, off = stream
    return [pl.BlockSpec((bs, TILE, D_MODEL), lambda b, t: (b, 0, 0)),
            pl.BlockSpec((bs, TILE, D_MODEL), lambda b, t: (b, jnp.maximum(t - off, 0), 0))]


def _stream_tile(head_ref, tail_ref, bb):
    return jnp.where(pl.program_id(1) == 0, head_ref[bb], tail_ref[bb])


def _put_rows(ref, val, cast=None):
    for bb in range(ref.shape[0]):
        rows = val[bb * TILE:(bb + 1) * TILE]
        ref[bb] = rows if cast is None else rows.astype(cast)


def _inproj_kernel(xh_ref, xt_ref, mod_ref, nw_ref, w_ref, u_ref, q_ref, k_ref, v_ref, z_ref, xbc_ref, dt_ref):
    bs = xh_ref.shape[0]
    h = jnp.concatenate([_modulated_norm(_stream_tile(xh_ref, xt_ref, bb), nw_ref[...], mod_ref[bb, 0, 0:1, :],
                                         mod_ref[bb, 0, 1:2, :]).astype(BF16) for bb in range(bs)], axis=0)

    def proj(lo, hi):
        return _dot(h, w_ref[0, :, lo:hi])

    for s in range(S5_WIDTH // LANE):
        us = proj(C_U + s * LANE, C_U + (s + 1) * LANE)
        for bb in range(bs):
            u_ref[bb, s] = us[bb * TILE:(bb + 1) * TILE]
    _put_rows(q_ref, proj(C_Q, C_K) * (NA_HEAD_DIM ** -0.5), BF16)
    _put_rows(k_ref, proj(C_K, C_V), BF16)
    _put_rows(v_ref, proj(C_V, C_Z), BF16)
    _put_rows(z_ref, proj(C_Z, C_XBC), BF16)
    _put_rows(xbc_ref, proj(C_XBC, C_DT), BF16)
    _put_rows(dt_ref, proj(C_DT, IN_COLS_PAD))


def in_proj(stream, mod, norm_w, w_in_p, layer):
    bs = BS_WIDE
    tok = lambda w, dt: jax.ShapeDtypeStruct((BATCH, T, w), dt)
    tspec = lambda w: pl.BlockSpec((bs, TILE, w), lambda b, t: (b, t, 0))
    return pl.pallas_call(
        _inproj_kernel,
        grid=(BATCH // bs, NT),
        in_specs=_stream_specs(stream, bs) + [
                  pl.BlockSpec((bs, 1, 6, D_MODEL), lambda b, t: (b, _seg(t), 0, 0)),
                  pl.BlockSpec((1, D_MODEL), lambda b, t: (0, 0)),
                  pl.BlockSpec((1, D_MODEL, IN_COLS_PAD), lambda b, t: (layer, 0, 0))],
        out_specs=[pl.BlockSpec((bs, S5_WIDTH // LANE, TILE, LANE), lambda b, t: (b, 0, t, 0)),
                   tspec(NA_WIDTH), tspec(NA_WIDTH), tspec(NA_WIDTH),
                   tspec(SSD_WIDTH), tspec(SSD_XBC), tspec(LANE)],
        out_shape=[jax.ShapeDtypeStruct((BATCH, S5_WIDTH // LANE, T, LANE), F32),
                   tok(NA_WIDTH, BF16), tok(NA_WIDTH, BF16), tok(NA_WIDTH, BF16),
                   tok(SSD_WIDTH, BF16), tok(SSD_XBC, BF16), tok(LANE, F32)],
        compiler_params=_cp("arbitrary", "arbitrary"),
        name="in_proj",
    )(stream[0], stream[1], mod, norm_w.reshape(1, D_MODEL), w_in_p)


S5_BLK = SUB
NB = T // S5_BLK
NB_CTX = CTX_LEN // S5_BLK
S5_NPAIR = S5_NGROUPS // 2
S5_PW = 2 * S5_BLK * S5_GROUP


def _s5_kernel(u_ref, pin_ref, pout_ref, w1_ref, w2_ref, w3_ref, mul_ref, y_ref, ub_ref, yb_ref, st_ref):
    n = S5_NSTATE
    half = LANE
    per_half = S5_NPAIR // 2
    toks = [jnp.concatenate([u_ref[0, h, pl.ds(j, NB, stride=S5_BLK), :].astype(BF16) for j in range(S5_BLK)],
                            axis=1) for h in range(2)]
    for pp in range(S5_NPAIR):
        h, q = divmod(pp, per_half)
        ub_ref[:, pp * S5_PW:(pp + 1) * S5_PW] = _dot(toks[h], pin_ref[q]).astype(BF16)

    for d in range(2):
        for pp in range(S5_NPAIR):
            s = _dot(ub_ref[:, pp * S5_PW:(pp + 1) * S5_PW], w1_ref[d, pp])
            st_ref[d, :, pp * half:(pp + 1) * half] = s[:, :half]
            st_ref[d, :, n + pp * half:n + (pp + 1) * half] = s[:, half:]

    ngrp = NB // SUB
    nctx = NB_CTX // SUB
    rowid = lax.broadcasted_iota(I32, (SUB, n), 0)
    for d in range(2):
        def body(j, carry, d=d):
            cr, ci = carry
            r = j if d == 0 else jnp.where(j < nctx, nctx - 1 - j, ngrp - 1 + nctx - j)
            row = pl.multiple_of(r * SUB, SUB)
            re = st_ref[d, pl.ds(row, SUB), 0:n]
            im = st_ref[d, pl.ds(row, SUB), n:2 * n]
            for kk, sh in enumerate((1, 2, 4)):
                mr = mul_ref[d, kk * SUB:(kk + 1) * SUB, 0:n]
                mi = mul_ref[d, kk * SUB:(kk + 1) * SUB, n:2 * n]
                s = sh if d == 0 else SUB - sh
                sr = pltpu.roll(re, s, 0)
                si = pltpu.roll(im, s, 0)
                re, im = re + (mr * sr - mi * si), im + (mr * si + mi * sr)
            pr = mul_ref[d, 3 * SUB:4 * SUB, 0:n]
            pi = mul_ref[d, 3 * SUB:4 * SUB, n:2 * n]
            re, im = re + (pr * cr - pi * ci), im + (pr * ci + pi * cr)
            edge, last, sh = (0, SUB - 1, 1) if d == 0 else (SUB - 1, 0, SUB - 1)
            st_ref[d, pl.ds(row, SUB), 0:n] = jnp.where(rowid == edge, cr, pltpu.roll(re, sh, 0))
            st_ref[d, pl.ds(row, SUB), n:2 * n] = jnp.where(rowid == edge, ci, pltpu.roll(im, sh, 0))
            return re[last:last + 1, :], im[last:last + 1, :]

        zero = jnp.zeros((1, n), F32)
        lax.fori_loop(0, ngrp, body, (zero, zero), unroll=2)

    for pp in range(S5_NPAIR):
        up = ub_ref[:, pp * S5_PW:(pp + 1) * S5_PW]
        acc = None
        for d in range(2):
            enter = jnp.concatenate([st_ref[d, :, pp * half:(pp + 1) * half],
                                     st_ref[d, :, n + pp * half:n + (pp + 1) * half]], axis=1).astype(BF16)
            term = _dot(up, w2_ref[d, pp]) + _dot(enter, w3_ref[d, pp])
            acc = term if acc is None else acc + term
        yb_ref[:, pp * S5_PW:(pp + 1) * S5_PW] = acc.astype(BF16)

    for m in range(S5_BLK // 2):
        for h in range(2):
            acc = None
            for q in range(per_half):
                pp = h * per_half + q
                term = _dot(yb_ref[:, pp * S5_PW:(pp + 1) * S5_PW], pout_ref[q, m])
                acc = term if acc is None else acc + term
            for k in range(2):
                y_ref[0, h, pl.ds(2 * m + k, NB, stride=S5_BLK), :] = acc[:, k * LANE:(k + 1) * LANE]


def _s5_regroup_matrices():
    per_half = S5_NPAIR // 2
    pin = np.zeros((per_half, S5_BLK, LANE, S5_PW), np.float32)
    for q in range(per_half):
        for j in range(S5_BLK):
            for gg in range(2):
                for c in range(S5_GROUP):
                    pin[q, j, (2 * q + gg) * S5_GROUP + c, gg * S5_BLK * S5_GROUP + j * S5_GROUP + c] = 1.0
    pout = pin.transpose(0, 1, 3, 2).reshape(per_half, S5_BLK // 2, 2, S5_PW, LANE)
    pout = pout.transpose(0, 1, 3, 2, 4).reshape(per_half, S5_BLK // 2, S5_PW, 2 * LANE)
    return jnp.asarray(pin.reshape(per_half, S5_BLK * LANE, S5_PW), BF16), jnp.asarray(pout, BF16)


def s5_mix(u, w1, w2, w3, mul):
    pin, pout = _s5_regroup_matrices()
    per_half = S5_NPAIR // 2
    wspec = pl.BlockSpec((2, S5_NPAIR, S5_PW, S5_PW), lambda b: (0, 0, 0, 0))
    tok = pl.BlockSpec((1, S5_WIDTH // LANE, T, LANE), lambda b: (b, 0, 0, 0))
    return pl.pallas_call(
        _s5_kernel,
        grid=(BATCH,),
        in_specs=[tok, pl.BlockSpec((per_half, S5_BLK * LANE, S5_PW), lambda b: (0, 0, 0)),
                  pl.BlockSpec((per_half, S5_BLK // 2, S5_PW, 2 * LANE), lambda b: (0, 0, 0, 0)), wspec, wspec, wspec,
                  pl.BlockSpec((2, 4 * SUB, 2 * S5_NSTATE), lambda b: (0, 0, 0))],
        out_specs=tok,
        out_shape=jax.ShapeDtypeStruct((BATCH, S5_WIDTH // LANE, T, LANE), F32),
        scratch_shapes=[pltpu.VMEM((NB, S5_NGROUPS * S5_BLK * S5_GROUP), BF16),
                        pltpu.VMEM((NB, S5_NGROUPS * S5_BLK * S5_GROUP), BF16),
                        pltpu.VMEM((2, NB, 2 * S5_NSTATE), F32)],
        compiler_params=_cp("arbitrary"),
        name="s5_mix",
    )(u, pin, pout, w1, w2, w3, mul)


def s5_params(lam_re, lam_im, log_dt, b_re, b_im, c_re, c_im):
    G, P, C = S5_NGROUPS, S5_STATE, S5_GROUP
    lam = lax.complex(lam_re.astype(F32), lam_im.astype(F32))
    step = jnp.exp(log_dt.astype(F32))[..., None]
    log_lb = lam * step
    lam_bar = jnp.exp(log_lb)
    b_bar = ((lam_bar - 1.0) / lam)[..., None] * lax.complex(b_re.astype(F32), b_im.astype(F32))
    J = S5_BLK
    row = lambda a: a.reshape(2, G, 1, P)
    col = lambda a: jnp.broadcast_to(a[..., None], (2, G, P, J * C))
    b_t = jnp.swapaxes(b_bar, 2, 3)
    c_t = lambda a: jnp.tile(jnp.swapaxes(a.astype(F32), 2, 3), (1, 1, 1, J))
    wspec = pl.BlockSpec((1, 1, S5_PW, S5_PW), lambda d, q: (d, q, 0, 0))
    pair = lambda r, s: pl.BlockSpec((1, 2, r, s), lambda d, q: (d, q, 0, 0))
    w1, w2, w3 = pl.pallas_call(
        _s5_weight_kernel,
        grid=(2, S5_NPAIR),
        in_specs=[pair(1, P), pair(1, P), pair(P, J * C), pair(P, J * C), pair(C, P), pair(C, P),
                  pair(P, J * C), pair(P, J * C)],
        out_specs=[wspec, wspec, wspec],
        out_shape=[jax.ShapeDtypeStruct((2, S5_NPAIR, S5_PW, S5_PW), BF16)] * 3,
        compiler_params=_cp("arbitrary", "arbitrary"),
        name="s5_weights",
    )(row(jnp.real(log_lb)), row(jnp.imag(log_lb)), col(jnp.real(log_lb)), col(jnp.imag(log_lb)),
      jnp.real(b_t), jnp.imag(b_t), c_t(c_re), c_t(c_im))
    rows = jnp.arange(SUB)
    pieces = []
    for d in range(2):
        log_blk = (log_lb[d] * float(J)).reshape(1, G * P)
        per_d = []
        for sh in (1, 2, 4):
            valid = (rows >= sh) if d == 0 else (rows < SUB - sh)
            per_d.append(jnp.where(valid[:, None], jnp.exp(log_blk * float(sh)), 0.0))
        expo = (rows + 1) if d == 0 else (SUB - rows)
        per_d.append(jnp.exp(log_blk * expo[:, None].astype(F32)))
        m = jnp.concatenate(per_d, axis=0)
        pieces.append(jnp.concatenate([jnp.real(m), jnp.imag(m)], axis=-1))
    mul = jnp.stack(pieces, axis=0).astype(F32)
    return w1, w2, w3, mul


def _s5_weight_kernel(llr_ref, lli_ref, lcr_ref, lci_ref, btr_ref, bti_ref, ctr_ref, cti_ref, w1_ref, w2_ref, w3_ref):
    J, C, P = S5_BLK, S5_GROUP, S5_STATE
    R = J * C
    fwd = pl.program_id(0) == 0
    shift = C.bit_length() - 1
    j_of_row = lax.shift_right_logical(lax.broadcasted_iota(I32, (R, P), 0), shift).astype(F32)
    i_of_col = lax.shift_right_logical(lax.broadcasted_iota(I32, (P, R), 1), shift).astype(F32)
    jr = lax.shift_right_logical(lax.broadcasted_iota(I32, (R, R), 0), shift)
    ic = lax.shift_right_logical(lax.broadcasted_iota(I32, (R, R), 1), shift)
    lag = jnp.where(fwd, ic - jr, jr - ic)

    def cpow(expo, lr, li):
        mag = jnp.exp(lr * expo)
        return mag * jnp.cos(li * expo), mag * jnp.sin(li * expo)

    w1_ref[...] = jnp.zeros_like(w1_ref)
    w2_ref[...] = jnp.zeros_like(w2_ref)
    w3_ref[...] = jnp.zeros_like(w3_ref)
    for gg in range(2):
        llr, lli = llr_ref[0, gg], lli_ref[0, gg]
        br = jnp.concatenate([btr_ref[0, gg]] * J, axis=0)
        bi = jnp.concatenate([bti_ref[0, gg]] * J, axis=0)
        cr, ci = ctr_ref[0, gg], cti_ref[0, gg]
        pr, pi = cpow(jnp.where(fwd, (J - 1) - j_of_row, j_of_row), llr, lli)
        w1_ref[0, 0, gg * R:(gg + 1) * R, gg * P:(gg + 1) * P] = (pr * br - pi * bi).astype(BF16)
        w1_ref[0, 0, gg * R:(gg + 1) * R, 2 * P + gg * P:2 * P + (gg + 1) * P] = (pr * bi + pi * br).astype(BF16)
        qr, qi = cpow(jnp.where(fwd, i_of_col + 1.0, J - i_of_col), lcr_ref[0, gg], lci_ref[0, gg])
        w3_ref[0, 0, gg * P:(gg + 1) * P, gg * R:(gg + 1) * R] = (cr * qr - ci * qi).astype(BF16)
        w3_ref[0, 0, 2 * P + gg * P:2 * P + (gg + 1) * P, gg * R:(gg + 1) * R] = (-(cr * qi + ci * qr)).astype(BF16)
        acc = jnp.zeros((R, R), F32)
        for k in range(J):
            lr, li = cpow(float(k), llr, lli)
            t = _dot_x3(br * lr - bi * li, cr) - _dot_x3(br * li + bi * lr, ci)
            acc = acc + jnp.where(lag == k, t, 0.0)
        w2_ref[0, 0, gg * R:(gg + 1) * R, gg * R:(gg + 1) * R] = acc.astype(BF16)


def _softmax_pv(parts):
    m = parts[0][0].max(axis=-1, keepdims=True)
    for s, _ in parts[1:]:
        m = jnp.maximum(m, s.max(axis=-1, keepdims=True))
    den = 0.0
    acc = 0.0
    for s, v in parts:
        p = jnp.exp(s - m)
        den = den + p.sum(axis=-1, keepdims=True)
        acc = acc + _dot(p.astype(BF16), v)
    return acc / den


def _na_kernel(q_ref, k_ref, v_ref, bias_ref, o_ref):
    t = pl.program_id(1)
    first = lax.broadcasted_iota(I32, (1, LANE), 1) < NA_HEAD_DIM

    def pair_scores(bb, pp, start):
        ls = slice(pp * LANE, (pp + 1) * LANE)
        qp = q_ref[bb, :, ls]
        kc = k_ref[bb, 0:CTX_LEN, ls]
        vc = v_ref[bb, 0:CTX_LEN, ls]
        scores = []
        for hh in range(2):
            qm = jnp.where(first if hh == 0 else ~first, qp, jnp.zeros_like(qp))
            parts = [(_dot_nt(qm, kc), vc)]
            if start is not None:
                kb = k_ref[bb, pl.ds(start, NA_BAND * GRID_W), ls]
                vb = v_ref[bb, pl.ds(start, NA_BAND * GRID_W), ls]
                parts.append((_dot_nt(qm, kb) + bias_ref[2 * pp + hh, 0], vb))
            scores.append(parts)
        return scores

    def attention(start):
        for bb in range(BS):
            scores = [pair_scores(bb, pp, start) for pp in range(NA_HEADS // 2)]
            for pp, (s0, s1) in enumerate(scores):
                o_ref[bb, :, pp * LANE:(pp + 1) * LANE] = jnp.where(first, _softmax_pv(s0),
                                                                    _softmax_pv(s1)).astype(BF16)

    @pl.when(t == 0)
    def _():
        attention(None)

    @pl.when(t > 0)
    def _():
        first_row = (t - 1) * (TILE // GRID_W)
        u0 = jnp.clip(first_row - NA_KH // 2, 0, ROWS - NA_BAND)
        attention(pl.multiple_of(CTX_LEN + u0 * GRID_W, LANE))


def _na_cfg(t):
    return jnp.where(t <= 1, 0, jnp.where(t == NT - 1, 2, 1))


def natten(q, k, v, bias, layer):
    whole = pl.BlockSpec((BS, T, NA_WIDTH), lambda b, t: (b, 0, 0))
    tile = pl.BlockSpec((BS, TILE, NA_WIDTH), lambda b, t: (b, t, 0))
    return pl.pallas_call(
        _na_kernel,
        grid=(BATCH // BS, NT),
        in_specs=[tile, whole, whole,
                  pl.BlockSpec((NA_HEADS, 1, TILE, NA_BAND * GRID_W), lambda b, t: (layer, _na_cfg(t), 0, 0))],
        out_specs=tile,
        out_shape=jax.ShapeDtypeStruct((BATCH, T, NA_WIDTH), BF16),
        compiler_params=_cp("arbitrary", "arbitrary"),
        name="natten",
    )(q, k, v, bias)


def natten_bias(rpb):
    col = jnp.arange(GRID_W)
    c0 = jnp.clip(col - NA_KW // 2, 0, GRID_W - NA_KW)
    in_win = (col[None, :] >= c0[:, None]) & (col[None, :] < c0[:, None] + NA_KW)
    rel_c = jnp.clip(col[None, :] - col[:, None] + (NA_KW - 1), 0, RPB_W - 1)
    pick_c = jax.nn.one_hot(rel_c, RPB_W, dtype=F32)
    blocks = jnp.einsum('hax,qkx->haqk', rpb.astype(F32), pick_c, precision=lax.Precision.HIGHEST)
    blocks = jnp.where(in_win[None, None], blocks, NEG)
    nh = rpb.shape[0]
    blocks = jnp.concatenate([blocks, jnp.full((nh, 1, GRID_W, GRID_W), NEG, F32)], axis=1)
    blocks = jnp.concatenate([blocks, blocks], axis=-1)
    return pl.pallas_call(
        _bias_kernel,
        grid=(nh, 3),
        in_specs=[pl.BlockSpec((1, 2 * NA_KH, GRID_W, LANE), lambda h, c: (h, 0, 0, 0))],
        out_specs=pl.BlockSpec((1, 1, TILE, NA_BAND * GRID_W), lambda h, c: (h, c, 0, 0)),
        out_shape=jax.ShapeDtypeStruct((nh, 3, TILE, NA_BAND * GRID_W), F32),
        compiler_params=_cp("arbitrary", "arbitrary"),
        name="natten_bias",
    )(blocks)


def _bias_kernel(blk_ref, o_ref):
    rows_per_tile = TILE // GRID_W
    masked = 2 * NA_KH - 1
    for cfg, first_row in enumerate((0, 2 * rows_per_tile, ROWS - rows_per_tile)):
        @pl.when(pl.program_id(1) == cfg)
        def _(first_row=first_row):
            u0 = min(max(first_row - NA_KH // 2, 0), ROWS - NA_BAND)
            for rr in range(rows_per_tile):
                r = first_row + rr
                r0 = min(max(r - NA_KH // 2, 0), ROWS - NA_KH)
                for j in range(NA_BAND):
                    kr = u0 + j
                    a = kr - r + NA_KH - 1 if r0 <= kr < r0 + NA_KH else masked
                    half = (j % 2) * GRID_W
                    o_ref[0, 0, rr * GRID_W:(rr + 1) * GRID_W, j * GRID_W:(j + 1) * GRID_W] = (
                        blk_ref[0, a, :, half:half + GRID_W])


def _softplus(x):
    return jnp.maximum(x, 0.0) + jnp.log(1.0 + jnp.exp(-jnp.abs(x)))


def _ssd_prep_kernel(prev_ref, cur_ref, next_ref, dtr_ref, cw_ref, cb_ref, dtb_ref, ar_ref, cos_ref, sin_ref,
                     xs_ref, bm_ref, cm_ref, dt_ref, a_ref):
    for bb in range(cur_ref.shape[0]):
        _ssd_prep_sample(bb, prev_ref, cur_ref, next_ref, dtr_ref, cw_ref, cb_ref, dtb_ref, ar_ref, cos_ref, sin_ref,
                         xs_ref, bm_ref, cm_ref, dt_ref, a_ref)


def _ssd_prep_sample(bb, prev_ref, cur_ref, next_ref, dtr_ref, cw_ref, cb_ref, dtb_ref, ar_ref, cos_ref, sin_ref,
                     xs_ref, bm_ref, cm_ref, dt_ref, a_ref):
    t = pl.program_id(1)
    halo = prev_ref.shape[1]
    has_prev = t >= 2
    has_next = (t >= 1) & (t <= NT - 2)
    prev = jnp.where(has_prev, prev_ref[bb].astype(F32), 0.0)
    nxt = jnp.where(has_next, next_ref[bb].astype(F32), 0.0)
    ext = jnp.concatenate([prev, cur_ref[bb].astype(F32), nxt], axis=0)
    n = ext.shape[0]
    acc = cb_ref[...] + cw_ref[SSD_CONV // 2:SSD_CONV // 2 + 1, :] * ext
    for kk in range(SSD_CONV):
        off = kk - SSD_CONV // 2
        if off != 0:
            acc = acc + cw_ref[kk:kk + 1, :] * pltpu.roll(ext, (-off) % n, 0)
    y = _silu(acc[halo:halo + TILE, :])
    xs_ref[bb] = y[:, 0:SSD_WIDTH].astype(BF16)

    lane = lax.broadcasted_iota(I32, (1, LANE), 1)
    quarter = SSD_STATE // 4
    low = (lane & quarter) == 0
    cos = cos_ref[...]
    sin = sin_ref[...]
    for g in range(2 * SSD_NGROUPS):
        v = y[:, SSD_WIDTH + g * LANE:SSD_WIDTH + (g + 1) * LANE]
        sw = jnp.where(low, pltpu.roll(v, LANE - quarter, 1), pltpu.roll(v, quarter, 1))
        rot = (v * cos + sw * sin).astype(BF16)
        if g < SSD_NGROUPS:
            bm_ref[bb, :, g * LANE:(g + 1) * LANE] = rot
        else:
            cm_ref[bb, :, (g - SSD_NGROUPS) * LANE:(g - SSD_NGROUPS + 1) * LANE] = rot

    dt = _softplus(dtr_ref[bb] + dtb_ref[...])
    dt_ref[bb] = dt
    a_ref[bb] = dt * ar_ref[...]


def ssd_prep(xbc, dt_raw, conv_w, conv_b, dt_bias, a_log, cos_t, sin_t):
    halo = 2 * SUB
    per = TILE // halo
    nhalo = T // halo
    tok = lambda w, dt: jax.ShapeDtypeStruct((BATCH, T, w), dt)
    bs = BS_WIDE
    tspec = lambda w: pl.BlockSpec((bs, TILE, w), lambda b, t: (b, t, 0))
    row = lambda w: pl.BlockSpec((1, w), lambda b, t: (0, 0))
    cw = jnp.zeros((SUB, SSD_XBC), F32).at[:SSD_CONV].set(conv_w.astype(F32))
    pad12 = lambda v: jnp.zeros((1, LANE), F32).at[0, :2 * SSD_HEADS].set(v.astype(F32).reshape(-1))
    return pl.pallas_call(
        _ssd_prep_kernel,
        grid=(BATCH // bs, NT),
        in_specs=[pl.BlockSpec((bs, halo, SSD_XBC), lambda b, t: (b, jnp.maximum(t * per - 1, 0), 0)),
                  tspec(SSD_XBC),
                  pl.BlockSpec((bs, halo, SSD_XBC), lambda b, t: (b, jnp.minimum((t + 1) * per, nhalo - 1), 0)),
                  tspec(LANE),
                  pl.BlockSpec((SUB, SSD_XBC), lambda b, t: (0, 0)), row(SSD_XBC), row(LANE), row(LANE),
                  pl.BlockSpec((TILE, LANE), lambda b, t: (t, 0)), pl.BlockSpec((TILE, LANE), lambda b, t: (t, 0))],
        out_specs=[tspec(SSD_WIDTH), tspec(SSD_BC), tspec(SSD_BC), tspec(LANE), tspec(LANE)],
        out_shape=[tok(SSD_WIDTH, BF16), tok(SSD_BC, BF16), tok(SSD_BC, BF16), tok(LANE, F32), tok(LANE, F32)],
        compiler_params=_cp("arbitrary", "arbitrary"),
        name="ssd_prep",
    )(xbc, xbc, xbc, dt_raw, cw, conv_b.astype(F32).reshape(1, SSD_XBC), pad12(dt_bias),
      pad12(-jnp.exp(a_log.astype(F32))), cos_t, sin_t)


def rope_tables():
    half = SSD_STATE // 2
    nf = half // 2
    pos = jnp.arange(SEQ)
    inv_freq = ROPE_BASE ** (-jnp.arange(nf, dtype=F32) / nf)
    lane = jnp.arange(LANE)
    p = jnp.where(lane[None, :] < half, (pos // GRID_W)[:, None], (pos % GRID_W)[:, None]).astype(F32)
    ang = p * inv_freq[lane % nf][None, :]
    sign = jnp.where((lane & nf) == 0, -1.0, 1.0)[None, :]
    cos_t = jnp.concatenate([jnp.ones((CTX_LEN, LANE), F32), jnp.cos(ang)], axis=0)
    sin_t = jnp.concatenate([jnp.zeros((CTX_LEN, LANE), F32), jnp.sin(ang) * sign], axis=0)
    return cos_t, sin_t


def _ssd_sums(d, bm_ref, cm_ref, a_ref, at_ref, tri_ref):
    tri_row = tri_ref[1 - d]
    cs_row = _dot_exact_lhs(at_ref[0], tri_row)
    cs_col = cs_row.T
    g_mats = [_dot_nt(cm_ref[0, :, g * SSD_STATE:(g + 1) * SSD_STATE],
                      bm_ref[0, :, g * SSD_STATE:(g + 1) * SSD_STATE]) for g in range(SSD_NGROUPS)]
    return cs_col, cs_row, g_mats


def _ssd_dir(d, sums, xs_ref, cm_ref, bt_ref, dt_ref, y_ref, st_ref):
    q = TILE
    cs_col, cs_row, g_mats = sums
    lane = lax.broadcasted_iota(I32, (1, LANE), 1)
    first = lane < SSD_HEAD_DIM
    ri = lax.broadcasted_iota(I32, (q, q), 0)
    ci = lax.broadcasted_iota(I32, (q, q), 1)
    keep = (ci <= ri) if d == 0 else (ci >= ri)
    end = q - 1 if d == 0 else 0
    dt = dt_ref[0]

    def head_col(m, h):
        c = d * SSD_HEADS + h
        return m[:, c:c + 1]

    for pp in range(SSD_HEADS // 2):
        ls = slice(pp * LANE, (pp + 1) * LANE)
        h0, h1 = 2 * pp, 2 * pp + 1
        x = xs_ref[0, :, ls].astype(F32)
        dt_l = jnp.where(first, head_col(dt, h0), head_col(dt, h1))
        cs_l = jnp.where(first, head_col(cs_col, h0), head_col(cs_col, h1))
        cs_end = cs_l[end:end + 1, :]
        xdt = x * dt_l
        xdt_b = xdt.astype(BF16)
        xw = (xdt * jnp.exp(cs_end - cs_l)).astype(BF16)
        st = st_ref[d, pp]
        st_b = st.astype(BF16)
        ys, ups = [], []
        for h in (h0, h1):
            g = h // (SSD_HEADS // SSD_NGROUPS)
            c = d * SSD_HEADS + h
            diff = head_col(cs_col, h) - cs_row[c:c + 1, :]
            decay = jnp.exp(jnp.where(keep, diff, NEG))
            m = (g_mats[g] * decay).astype(BF16)
            y_h = _dot(m, xdt_b) + _dot(cm_ref[0, :, g * SSD_STATE:(g + 1) * SSD_STATE], st_b) * jnp.exp(cs_l)
            ys.append(y_h)
            ups.append(_dot(bt_ref[0, g * SSD_STATE:(g + 1) * SSD_STATE, :], xw))
        y_ref[0, :, ls] = jnp.where(first, ys[0], ys[1])
        st_ref[d, pp] = jnp.exp(cs_end) * st + jnp.where(first, ups[0], ups[1])


def _ssd_scan_kernel(xs_f, bm_f, cm_f, bt_f, dt_f, a_f, at_f, xs_b, bm_b, cm_b, bt_b, dt_b, a_b, at_b, tri_ref,
                     yf_ref, yb_ref, st_ref):
    @pl.when(pl.program_id(1) == 0)
    def _():
        st_ref[...] = jnp.zeros_like(st_ref)

    for bb in range(xs_f.shape[0]):
        one = lambda *refs: [r.at[pl.ds(bb, 1)] for r in refs]
        xsf, bmf, cmf, btf, dtf, af, atf, yf = one(xs_f, bm_f, cm_f, bt_f, dt_f, a_f, at_f, yf_ref)
        xsb, bmb, cmb, btb, dtb, ab, atb, yb = one(xs_b, bm_b, cm_b, bt_b, dt_b, a_b, at_b, yb_ref)
        st = st_ref.at[bb]
        sums_f = _ssd_sums(0, bmf, cmf, af, atf, tri_ref)
        sums_b = _ssd_sums(1, bmb, cmb, ab, atb, tri_ref)
        _ssd_dir(0, sums_f, xsf, cmf, btf, dtf, yf, st)
        _ssd_dir(1, sums_b, xsb, cmb, btb, dtb, yb, st)


def ssd_scan(xs, bm, cm, dt, a):
    bs = BS_WIDE
    bt = jnp.swapaxes(bm, 1, 2)
    at = jnp.swapaxes(a[:, :, :2 * SUB], 1, 2)
    idx = jnp.arange(TILE)
    tri = jnp.stack([idx[None, :] <= idx[:, None], idx[None, :] >= idx[:, None]]).astype(BF16)
    fwd = lambda b, i: (b, i, 0)
    bwd = lambda b, i: (b, _bwd_tile(i), 0)
    fwd_t = lambda b, i: (b, 0, i)
    bwd_t = lambda b, i: (b, 0, _bwd_tile(i))

    def specs(f, ft):
        return [pl.BlockSpec((bs, TILE, SSD_WIDTH), f), pl.BlockSpec((bs, TILE, SSD_BC), f),
                pl.BlockSpec((bs, TILE, SSD_BC), f), pl.BlockSpec((bs, SSD_BC, TILE), ft),
                pl.BlockSpec((bs, TILE, LANE), f), pl.BlockSpec((bs, TILE, LANE), f),
                pl.BlockSpec((bs, 2 * SUB, TILE), ft)]

    args = (xs, bm, cm, bt, dt, a, at)
    return pl.pallas_call(
        _ssd_scan_kernel,
        grid=(BATCH // bs, NT),
        in_specs=specs(fwd, fwd_t) + specs(bwd, bwd_t) + [pl.BlockSpec((2, TILE, TILE), lambda b, i: (0, 0, 0))],
        out_specs=[pl.BlockSpec((bs, TILE, SSD_WIDTH), fwd), pl.BlockSpec((bs, TILE, SSD_WIDTH), bwd)],
        out_shape=[jax.ShapeDtypeStruct((BATCH, T, SSD_WIDTH), F32)] * 2,
        scratch_shapes=[pltpu.VMEM((bs, 2, SSD_HEADS // 2, SSD_STATE, LANE), F32)],
        compiler_params=_cp("arbitrary", "arbitrary"),
        name="ssd_scan",
    )(*args, *args, tri)


def _gelu_tanh(x):
    return 0.5 * x * (1.0 + jnp.tanh(math.sqrt(2.0 / math.pi) * (x + 0.044715 * (x * x * x))))


def _post_kernel(xh_ref, xt_ref, mod_ref, u_ref, s5y_ref, na_ref, xs_ref, z_ref, sdf_ref, sdb_ref,
                 s5d_ref, gw_ref, gb_ref, sdd_ref, snw_ref, wo_ref, n2w_ref, rt_ref,
                 x1_ref, h_ref, lg_ref):
    for bb in range(BS):
        slabs = lambda r: jnp.concatenate([r[bb, s] for s in range(S5_WIDTH // LANE)], axis=-1)
        ys5 = slabs(u_ref) * s5d_ref[...] + slabs(s5y_ref)
        g = _gelu_tanh(ys5)
        s5o = g * jax.nn.sigmoid(_dot(g.astype(BF16), gw_ref[...]) + gb_ref[...])
        yssd = (xs_ref[bb].astype(F32) * sdd_ref[...] + sdf_ref[bb] + sdb_ref[bb]) * _silu(z_ref[bb].astype(F32))
        ssdo = yssd * lax.rsqrt(jnp.mean(yssd * yssd, axis=-1, keepdims=True) + EPS) * snw_ref[...]
        mix = jnp.concatenate([s5o.astype(BF16), na_ref[bb], ssdo.astype(BF16)], axis=-1)
        x1 = _stream_tile(xh_ref, xt_ref, bb) + mod_ref[bb, 0, 2:3, :] * _dot(mix, wo_ref[...])
        x1_ref[bb] = x1
        h = _modulated_norm(x1, n2w_ref[...], mod_ref[bb, 0, 3:4, :], mod_ref[bb, 0, 4:5, :])
        h_ref[bb] = h.astype(BF16)
        lg_ref[bb] = _dot_x3(h, rt_ref[...])


def post_mixer(stream, mod, u, s5y, na, xs, z, sdf, sdb, s5_d, glu_w, glu_b, ssd_d, ssd_norm_w, w_out, norm2_w,
               router):
    tspec = lambda w: pl.BlockSpec((BS, TILE, w), lambda b, t: (b, t, 0))
    whole = lambda *shp: pl.BlockSpec(shp, lambda b, t: (0,) * len(shp))
    rt = jnp.zeros((D_MODEL, LANE), F32).at[:, :N_EXPERTS].set(router.astype(F32))
    slab = pl.BlockSpec((BS, S5_WIDTH // LANE, TILE, LANE), lambda b, t: (b, 0, t, 0))
    return pl.pallas_call(
        _post_kernel,
        grid=(BATCH // BS, NT),
        in_specs=_stream_specs(stream) + [
                  pl.BlockSpec((BS, 1, 6, D_MODEL), lambda b, t: (b, _seg(t), 0, 0)),
                  slab, slab, tspec(NA_WIDTH),
                  tspec(SSD_WIDTH), tspec(SSD_WIDTH), tspec(SSD_WIDTH), tspec(SSD_WIDTH),
                  whole(1, S5_WIDTH), whole(S5_WIDTH, S5_WIDTH), whole(1, S5_WIDTH),
                  whole(1, SSD_WIDTH), whole(1, SSD_WIDTH), whole(D_MODEL, D_MODEL), whole(1, D_MODEL),
                  whole(D_MODEL, LANE)],
        out_specs=[tspec(D_MODEL), tspec(D_MODEL), tspec(LANE)],
        out_shape=[jax.ShapeDtypeStruct((BATCH, T, D_MODEL), F32), jax.ShapeDtypeStruct((BATCH, T, D_MODEL), BF16),
                   jax.ShapeDtypeStruct((BATCH, T, LANE), F32)],
        compiler_params=_cp("arbitrary", "arbitrary"),
        name="post_mixer",
    )(stream[0], stream[1], mod, u, s5y, na, xs, z, sdf, sdb,
      s5_d.astype(F32).reshape(1, S5_WIDTH), glu_w.astype(BF16), glu_b.astype(F32).reshape(1, S5_WIDTH),
      jnp.repeat(ssd_d.astype(F32), SSD_HEAD_DIM).reshape(1, SSD_WIDTH), ssd_norm_w.astype(F32).reshape(1, SSD_WIDTH),
      w_out.astype(BF16), norm2_w.astype(F32).reshape(1, D_MODEL), rt)


def _route_kernel(lg_ref, tri_ref, slot_ref, aff_ref, *, with_ctx):
    lg = lg_ref[0]
    m = lg.max(axis=0, keepdims=True)
    e = jnp.exp(lg - m)
    aff = e / e.sum(axis=0, keepdims=True)
    aff_ref[0] = aff
    bits = pltpu.bitcast(aff, I32)
    is_ctx = lax.broadcasted_iota(I32, (N_EXPERTS, T), 1) < CTX_LEN

    def count(mask):
        return jnp.where(mask, 1.0, 0.0).sum(axis=1, keepdims=True)

    def kth_largest(seg, k):
        def body(i, prefix):
            cand = prefix | lax.shift_left(jnp.int32(1), 30 - i)
            return jnp.where(count((bits >= cand) & seg) >= k, cand, prefix)
        return lax.fori_loop(0, 31, body, jnp.zeros((N_EXPERTS, 1), I32))

    def excl_cumsum(x01):
        carry = jnp.zeros((N_EXPERTS, 1), F32)
        pieces = []
        for j in range(T // LANE):
            blk = x01[:, j * LANE:(j + 1) * LANE]
            inc = _dot(blk.astype(BF16), tri_ref[...])
            pieces.append(inc - blk + carry)
            carry = carry + inc[:, LANE - 1:LANE]
        return jnp.concatenate(pieces, axis=1)

    thr = kth_largest(~is_ctx, float(CAP_LAT))
    k_of = jnp.full((N_EXPERTS, T), float(CAP_LAT), F32)
    if with_ctx:
        thr = jnp.where(is_ctx, kth_largest(is_ctx, float(CAP_CTX)), thr)
        k_of = jnp.where(is_ctx, float(CAP_CTX), k_of)
    gt = bits > thr
    eq = bits == thr
    if not with_ctx:
        gt = gt & ~is_ctx
        eq = eq & ~is_ctx
    n_gt = jnp.where(is_ctx, count(gt & is_ctx), count(gt & ~is_ctx))
    tie_rank = excl_cumsum(jnp.where(eq, 1.0, 0.0))
    tie_rank = tie_rank - jnp.where(is_ctx, 0.0, count(eq & is_ctx))
    sel = gt | (eq & (tie_rank < k_of - n_gt))
    pos = excl_cumsum(jnp.where(sel, 1.0, 0.0))
    slot = jnp.where(is_ctx, pos + float(CAP_LAT), pos - count(sel & is_ctx))
    slot_ref[0] = jnp.where(sel, slot, -1.0).astype(I32)


def route(logits_t, with_ctx):
    idx = jnp.arange(LANE)
    tri = (idx[:, None] <= idx[None, :]).astype(BF16)
    spec = pl.BlockSpec((1, N_EXPERTS, T), lambda b: (b, 0, 0))
    return pl.pallas_call(
        functools.partial(_route_kernel, with_ctx=with_ctx),
        grid=(BATCH,),
        in_specs=[spec, pl.BlockSpec((LANE, LANE), lambda b: (0, 0))],
        out_specs=[spec, spec],
        out_shape=[jax.ShapeDtypeStruct((BATCH, N_EXPERTS, T), I32), jax.ShapeDtypeStruct((BATCH, N_EXPERTS, T), F32)],
        compiler_params=_cp("arbitrary"),
        name="route",
    )(logits_t, tri)


GATHER_WIN = LANE // 2
COMBINE_WIN = LANE // 2


def slot_ranges(slot):
    s = slot.reshape(BATCH, N_EXPERTS, NT, TILE)
    has = s >= 0
    smax = jnp.max(jnp.where(has, s, -1), axis=-1)
    smin = jnp.where(smax >= 0, jnp.min(jnp.where(has, s, CAP_LAT + CAP_CTX), axis=-1), 0)
    return smin.reshape(-1).astype(I32), smax.reshape(-1).astype(I32)


def _gather_kernel(smin_ref, smax_ref, h_ref, slot_ref, xs_ref, *, nslot):
    b = pl.program_id(0)
    t = pl.program_id(1)
    group = N_EXPERTS
    sid = lax.broadcasted_iota(I32, (GATHER_WIN, TILE), 0)
    align = 2 * SUB

    def onehot(e, ws, lo):
        srow = slot_ref[0, e:e + 1, :]
        return jnp.where((sid + ws == srow) & (srow >= lo), 1.0, 0.0).astype(BF16)

    def add_rows(e, ws, rows):
        win = pl.ds(pl.multiple_of(ws, align), GATHER_WIN)
        xs_ref[0, e, win, :] = (xs_ref[0, e, win, :].astype(F32) + rows).astype(BF16)

    @pl.when(t == 0)
    def _():
        for e in range(N_EXPERTS):
            xs_ref[0, e, 0:CAP_LAT, :] = jnp.zeros((CAP_LAT, D_MODEL), BF16)
            if nslot > CAP_LAT:
                cid = lax.broadcasted_iota(I32, (nslot - CAP_LAT, TILE), 0) + CAP_LAT
                pick = jnp.where(cid == slot_ref[0, e:e + 1, :], 1.0, 0.0).astype(BF16)
                xs_ref[0, e, CAP_LAT:nslot, :] = _dot(pick, h_ref[0]).astype(BF16)

    @pl.when(t > 0)
    def _():
        los, wss, extras = [], [], []
        for e in range(N_EXPERTS):
            base = (b * N_EXPERTS + e) * NT + t
            lo = smin_ref[base] & ~(align - 1)
            los.append(lo)
            wss.append(jnp.minimum(lo, CAP_LAT - GATHER_WIN))
            extras.append(lax.shift_right_arithmetic(smax_ref[base] - lo, GATHER_WIN.bit_length() - 1))
        for g0 in range(0, N_EXPERTS, group):
            pick = jnp.concatenate([onehot(e, wss[e], los[e]) for e in range(g0, g0 + group)], axis=0)
            rows = _dot(pick, h_ref[0])
            for i, e in enumerate(range(g0, g0 + group)):
                add_rows(e, wss[e], rows[i * GATHER_WIN:(i + 1) * GATHER_WIN, :])
        most = extras[0]
        for x in extras[1:]:
            most = jnp.maximum(most, x)

        @pl.when(most > 0)
        def _():
            for e in range(N_EXPERTS):
                def more(k, carry, e=e):
                    lo_k = los[e] + k * GATHER_WIN
                    ws = jnp.minimum(lo_k, CAP_LAT - GATHER_WIN)
                    add_rows(e, ws, _dot(onehot(e, ws, lo_k), h_ref[0]))
                    return carry

                lax.fori_loop(1, extras[e] + 1, more, 0)


def moe_gather(h, slot, smin, smax, nslot):
    return pl.pallas_call(
        functools.partial(_gather_kernel, nslot=nslot),
        grid_spec=pltpu.PrefetchScalarGridSpec(
            num_scalar_prefetch=2,
            grid=(BATCH, NT),
            in_specs=[pl.BlockSpec((1, TILE, D_MODEL), lambda b, t, *_: (b, t, 0)),
                      pl.BlockSpec((1, N_EXPERTS, TILE), lambda b, t, *_: (b, 0, t))],
            out_specs=pl.BlockSpec((1, N_EXPERTS, nslot, D_MODEL), lambda b, t, *_: (b, 0, 0, 0))),
        out_shape=jax.ShapeDtypeStruct((BATCH, N_EXPERTS, nslot, D_MODEL), BF16),
        compiler_params=_cp("arbitrary", "arbitrary"),
        name="moe_gather",
    )(smin, smax, h, slot)


def _ffn_kernel(xs_ref, wg_ref, wu_ref, wd_ref, y_ref, w_ref):
    @pl.when(pl.program_id(1) == 0)
    def _():
        w_ref[0] = wg_ref[0, 0].astype(BF16)
        w_ref[1] = wu_ref[0, 0].astype(BF16)
        w_ref[2] = wd_ref[0, 0].astype(BF16)

    xs = xs_ref[0, 0]
    hid = _silu(_dot(xs, w_ref[0])) * _dot(xs, w_ref[1])
    y_ref[0, 0] = _dot(hid.astype(BF16), w_ref[2]).astype(BF16)


def moe_ffn(xs, layer, wg, wu, wd, nslot):
    wspec = lambda: pl.BlockSpec((1, 1, D_MODEL, D_EXPERT), lambda e, b: (layer, e, 0, 0))
    rows = pl.BlockSpec((1, 1, nslot, D_MODEL), lambda e, b: (b, e, 0, 0))
    return pl.pallas_call(
        _ffn_kernel,
        grid=(N_EXPERTS, BATCH),
        in_specs=[rows, wspec(), wspec(), wspec()],
        out_specs=rows,
        out_shape=jax.ShapeDtypeStruct((BATCH, N_EXPERTS, nslot, D_MODEL), BF16),
        scratch_shapes=[pltpu.VMEM((3, D_MODEL, D_EXPERT), BF16)],
        compiler_params=_cp("arbitrary", "arbitrary"),
        name="moe_ffn",
    )(xs, wg, wu, wd)


def _combine_kernel(smin_ref, smax_ref, x_ref, mod_ref, slot_ref, aff_ref, y_ref, fw_ref, o_ref, acc_ref, *,
                    nslot, last):
    b = pl.program_id(0)
    t = pl.program_id(1)
    shift = COMBINE_WIN.bit_length() - 1
    align = 2 * SUB

    def weights(e, lane, lo=None):
        s = slot_ref[0, :, e:e + 1]
        hit = (s == lane) if lo is None else ((s == lane) & (s >= lo))
        return jnp.where(hit, aff_ref[0, :, e:e + 1], 0.0).astype(BF16)

    def finish(acc):
        x2 = x_ref[0] + mod_ref[0, 0, 5:6, :] * acc
        if last:
            x2 = x2 * lax.rsqrt(jnp.mean(x2 * x2, axis=-1, keepdims=True) + EPS) * fw_ref[...]
        o_ref[0] = x2

    if not last:
        @pl.when(t == 0)
        def _():
            lane = lax.broadcasted_iota(I32, (TILE, nslot - CAP_LAT), 1) + CAP_LAT
            acc = jnp.zeros((TILE, D_MODEL), F32)
            for e in range(N_EXPERTS):
                acc = acc + _dot(weights(e, lane), y_ref[0, e, CAP_LAT:nslot, :])
            finish(acc)

    @pl.when(t > 0)
    def _():
        lane = lax.broadcasted_iota(I32, (TILE, COMBINE_WIN), 1)
        lane2 = lax.broadcasted_iota(I32, (TILE, 2 * COMBINE_WIN), 1)
        first = lane2 < COMBINE_WIN
        los, wss, extras = [], [], []
        for e in range(N_EXPERTS):
            base = (b * N_EXPERTS + e) * NT + t
            lo = smin_ref[base] & ~(align - 1)
            los.append(lo)
            wss.append(pl.multiple_of(jnp.minimum(lo, CAP_LAT - COMBINE_WIN), align))
            extras.append(lax.shift_right_arithmetic(smax_ref[base] - lo, shift))
        w_parts, y_parts = [], []
        for e in range(0, N_EXPERTS, 2):
            want = jnp.where(first, slot_ref[0, :, e:e + 1] - wss[e],
                             slot_ref[0, :, e + 1:e + 2] - wss[e + 1] + COMBINE_WIN)
            gate = jnp.where(first, aff_ref[0, :, e:e + 1], aff_ref[0, :, e + 1:e + 2])
            w_parts.append(jnp.where(want == lane2, gate, 0.0).astype(BF16))
            y_parts += [y_ref[0, e, pl.ds(wss[e], COMBINE_WIN), :], y_ref[0, e + 1, pl.ds(wss[e + 1], COMBINE_WIN), :]]
        acc_ref[...] = _dot(jnp.concatenate(w_parts, axis=1), jnp.concatenate(y_parts, axis=0))
        most = extras[0]
        for x in extras[1:]:
            most = jnp.maximum(most, x)

        @pl.when(most > 0)
        def _():
            for e in range(N_EXPERTS):
                def more(k, carry, e=e):
                    lo_k = los[e] + k * COMBINE_WIN
                    ws = pl.multiple_of(jnp.minimum(lo_k, CAP_LAT - COMBINE_WIN), align)
                    acc_ref[...] += _dot(weights(e, lane + ws, lo_k), y_ref[0, e, pl.ds(ws, COMBINE_WIN), :])
                    return carry

                lax.fori_loop(1, extras[e] + 1, more, 0)

        finish(acc_ref[...])


def moe_combine(x1, mod, slot_tok, aff_tok, y, smin, smax, nslot, final_w):
    last = final_w is not None
    tspec = lambda w: pl.BlockSpec((1, TILE, w), lambda b, t, *_: (b, t, 0))
    if last:
        first_lat = CTX_LEN // TILE
        out_spec = pl.BlockSpec((1, TILE, D_MODEL), lambda b, t, *_: (b, jnp.maximum(t - first_lat, 0), 0))
        out_shape = jax.ShapeDtypeStruct((BATCH, SEQ, D_MODEL), F32)
        fw = final_w.astype(F32).reshape(1, D_MODEL)
    else:
        out_spec = tspec(D_MODEL)
        out_shape = jax.ShapeDtypeStruct((BATCH, T, D_MODEL), F32)
        fw = jnp.ones((1, D_MODEL), F32)
    return pl.pallas_call(
        functools.partial(_combine_kernel, nslot=nslot, last=last),
        grid_spec=pltpu.PrefetchScalarGridSpec(
            num_scalar_prefetch=2,
            grid=(BATCH, NT),
            in_specs=[tspec(D_MODEL),
                      pl.BlockSpec((1, 1, 6, D_MODEL), lambda b, t, *_: (b, _seg(t), 0, 0)),
                      tspec(N_EXPERTS), tspec(N_EXPERTS),
                      pl.BlockSpec((1, N_EXPERTS, nslot, D_MODEL), lambda b, t, *_: (b, 0, 0, 0)),
                      pl.BlockSpec((1, D_MODEL), lambda b, t, *_: (0, 0))],
            out_specs=out_spec,
            scratch_shapes=[pltpu.VMEM((TILE, D_MODEL), F32)]),
        out_shape=out_shape,
        compiler_params=_cp("arbitrary", "arbitrary"),
        name="moe_combine",
    )(smin, smax, x1, mod, slot_tok, aff_tok, y, fw)


def trunk_layer(l, stream, mod, cos_t, sin_t, norm1_w, norm2_w, w_in_p, w_out, s5_weights, s5_d, s5_glu_w, s5_glu_b,
                na_bias, ssd_conv_w, ssd_conv_b, ssd_dt_bias, ssd_a_log, ssd_d, ssd_norm_w,
                moe_router, wg, wu, wd, final_w):
    with_ctx_out = final_w is None
    u, q, k, v, z, xbc, dt_raw = in_proj(stream, mod, norm1_w, w_in_p, l)
    s5y = s5_mix(u, *s5_weights)
    na = natten(q, k, v, na_bias, l)
    xs, bm, cm, dt, a = ssd_prep(xbc, dt_raw, ssd_conv_w, ssd_conv_b, ssd_dt_bias, ssd_a_log, cos_t, sin_t)
    sdf, sdb = ssd_scan(xs, bm, cm, dt, a)

    x1, h, logits = post_mixer(stream, mod, u, s5y, na, xs, z, sdf, sdb, s5_d, s5_glu_w, s5_glu_b,
                               ssd_d, ssd_norm_w, w_out, norm2_w, moe_router)
    slot, aff = route(jnp.swapaxes(logits[:, :, :N_EXPERTS], 1, 2), with_ctx_out)
    nslot = CAP_LAT + CAP_CTX if with_ctx_out else CAP_LAT
    smin, smax = slot_ranges(slot)
    y = moe_ffn(moe_gather(h, slot, smin, smax, nslot), l, wg, wu, wd, nslot)
    return moe_combine(x1, mod, jnp.swapaxes(slot, 1, 2), jnp.swapaxes(aff, 1, 2), y, smin, smax, nslot, final_w)


def kernel(x, c, ctx, c_ctx, w_ada, b_ada, norm1_w, norm2_w, w_in, w_out, s5_lam_re, s5_lam_im, s5_log_dt, s5_b_re, s5_b_im, s5_c_re, s5_c_im, s5_d, s5_glu_w, s5_glu_b, na_rpb, ssd_conv_w, ssd_conv_b, ssd_dt_bias, ssd_a_log, ssd_d, ssd_norm_w, moe_router, moe_w_gate, moe_w_up, moe_w_down, final_norm_w):
    cvec = jnp.zeros((SUB, D_MODEL), F32).at[0].set(c_ctx.astype(F32)).at[1:1 + BATCH].set(c.astype(F32))
    mods = ada_mod(cvec, w_ada.astype(F32), b_ada.astype(F32)).reshape(DEPTH, SUB, 6, D_MODEL)
    mods = jnp.stack([jnp.broadcast_to(mods[:, 0:1], (DEPTH, BATCH, 6, D_MODEL)), mods[:, 1:1 + BATCH]], axis=2)
    cos_t, sin_t = rope_tables()
    w_in_p = jnp.zeros((DEPTH, D_MODEL, IN_COLS_PAD), BF16).at[:, :, :IN_COLS].set(w_in.astype(BF16))
    s5_weights = [s5_params(s5_lam_re[l], s5_lam_im[l], s5_log_dt[l], s5_b_re[l], s5_b_im[l], s5_c_re[l], s5_c_im[l])
                  for l in range(DEPTH)]
    na_bias = natten_bias(na_rpb.reshape(DEPTH * NA_HEADS, 2 * NA_KH - 1, RPB_W))
    wg, wu, wd = moe_w_gate.astype(F32), moe_w_up.astype(F32), moe_w_down.astype(F32)
    stream = (ctx.astype(F32), x.astype(F32), CTX_LEN // TILE)
    for l in range(DEPTH):
        xa = trunk_layer(
            l, stream, mods[l], cos_t, sin_t, norm1_w[l], norm2_w[l], w_in_p, w_out[l],
            s5_weights[l], s5_d[l], s5_glu_w[l], s5_glu_b[l],
            na_bias, ssd_conv_w[l], ssd_conv_b[l], ssd_dt_bias[l], ssd_a_log[l], ssd_d[l], ssd_norm_w[l],
            moe_router[l], wg, wu, wd, final_norm_w if l == DEPTH - 1 else None)
        stream = (xa, xa, 0)
    return xa
```

```python
import functools
import math

import jax
import jax.numpy as jnp
import numpy as np
from jax import lax
from jax.experimental import pallas as pl
from jax.experimental.pallas import tpu as pltpu

F32 = jnp.float32
BF16 = jnp.bfloat16
I32 = jnp.int32

D_MODEL = 1024
BATCH = 4
SEQ = 4096
DEPTH = 2
GRID_W = 64
CTX_LEN = 256
EPS = 1e-6

S5_WIDTH = 256
S5_GROUP = 16
S5_NGROUPS = 16
S5_STATE = 64
S5_NSTATE = S5_NGROUPS * S5_STATE

NA_HEADS = 6
NA_HEAD_DIM = 64
NA_WIDTH = 384
NA_KH = 8
NA_KW = 16
NA_BAND = 12
RPB_W = 2 * NA_KW - 1

SSD_HEADS = 6
SSD_HEAD_DIM = 64
SSD_WIDTH = 384
SSD_NGROUPS = 2
SSD_STATE = 128
SSD_CONV = 5
SSD_BC = 256
SSD_XBC = 896

N_EXPERTS = 16
D_EXPERT = 1024
ROPE_BASE = 10000.0

T = CTX_LEN + SEQ
TILE = 256
NT = T // TILE
BS = 2
BS_WIDE = 4
LANE = 128
SUB = 8
ROWS = SEQ // GRID_W
CAP_LAT = 2 * SEQ // N_EXPERTS
CAP_CTX = 2 * CTX_LEN // N_EXPERTS
NEG = -1e30

C_U = 0
C_Q = 256
C_K = 640
C_V = 1024
C_Z = 1408
C_XBC = 1792
C_DT = 2688
IN_COLS = 2700
IN_COLS_PAD = 2816

VMEM_LIMIT = 56 * 1024 * 1024


def _cp(*sem):
    return pltpu.CompilerParams(dimension_semantics=sem, vmem_limit_bytes=VMEM_LIMIT)


def _dot(a, b):
    return jnp.dot(a, b, preferred_element_type=F32)


def _dot_nt(a, b):
    return lax.dot_general(a, b, (((1,), (1,)), ((), ())), preferred_element_type=F32)


def _split3(x):
    hi = x.astype(BF16)
    r = x - hi.astype(F32)
    mid = r.astype(BF16)
    lo = (r - mid.astype(F32)).astype(BF16)
    return hi, mid, lo


def _dot_exact_rhs(a_bf16, b_f32):
    hi, mid, lo = _split3(b_f32)
    return _dot(a_bf16, hi) + _dot(a_bf16, mid) + _dot(a_bf16, lo)


def _dot_exact_lhs(a_f32, b_bf16):
    hi, mid, lo = _split3(a_f32)
    return _dot(hi, b_bf16) + _dot(mid, b_bf16) + _dot(lo, b_bf16)


def _dot_x3(a, b):
    ah = a.astype(BF16)
    al = (a - ah.astype(F32)).astype(BF16)
    bh = b.astype(BF16)
    bl = (b - bh.astype(F32)).astype(BF16)
    return _dot(ah, bh) + _dot(ah, bl) + _dot(al, bh)


def _silu(x):
    return x * jax.nn.sigmoid(x)


def _seg(t):
    return jnp.where(t >= CTX_LEN // TILE, 1, 0)


def _bwd_tile(i):
    return jnp.where(i == 0, 0, NT - i)


def _ada_kernel(c_ref, w_ref, b_ref, o_ref):
    s = _silu(c_ref[...])
    o_ref[0] = _dot_x3(s, w_ref[0]) + b_ref[0]


def ada_mod(cvec, w_ada, b_ada):
    nb = 1024
    return pl.pallas_call(
        _ada_kernel,
        grid=(DEPTH, 6 * D_MODEL // nb),
        in_specs=[pl.BlockSpec((SUB, D_MODEL), lambda l, j: (0, 0)),
                  pl.BlockSpec((1, D_MODEL, nb), lambda l, j: (l, 0, j)),
                  pl.BlockSpec((1, 1, nb), lambda l, j: (l, 0, j))],
        out_specs=pl.BlockSpec((1, SUB, nb), lambda l, j: (l, 0, j)),
        out_shape=jax.ShapeDtypeStruct((DEPTH, SUB, 6 * D_MODEL), F32),
        compiler_params=_cp("arbitrary", "arbitrary"),
        name="ada_mod",
    )(cvec, w_ada, b_ada.reshape(DEPTH, 1, 6 * D_MODEL))


def _modulated_norm(x, nw, shift, scale):
    y = x * lax.rsqrt(jnp.mean(x * x, axis=-1, keepdims=True) + EPS) * nw
    return y * (1.0 + scale) + shift


def _stream_specs(stream, bs=BS):
    _, _, off = stream
    return [pl.BlockSpec((bs, TILE, D_MODEL), lambda b, t: (b, 0, 0)),
            pl.BlockSpec((bs, TILE, D_MODEL), lambda b, t: (b, jnp.maximum(t - off, 0), 0))]


def _stream_tile(head_ref, tail_ref, bb):
    return jnp.where(pl.program_id(1) == 0, head_ref[bb], tail_ref[bb])


def _put_rows(ref, val, cast=None):
    for bb in range(ref.shape[0]):
        rows = val[bb * TILE:(bb + 1) * TILE]
        ref[bb] = rows if cast is None else rows.astype(cast)


def _inproj_kernel(xh_ref, xt_ref, mod_ref, nw_ref, w_ref, u_ref, q_ref, k_ref, v_ref, z_ref, xbc_ref, dt_ref):
    bs = xh_ref.shape[0]
    h = jnp.concatenate([_modulated_norm(_stream_tile(xh_ref, xt_ref, bb), nw_ref[...], mod_ref[bb, 0, 0:1, :],
                                         mod_ref[bb, 0, 1:2, :]).astype(BF16) for bb in range(bs)], axis=0)

    def proj(lo, hi):
        return _dot(h, w_ref[0, :, lo:hi])

    for s in range(S5_WIDTH // LANE):
        us = proj(C_U + s * LANE, C_U + (s + 1) * LANE)
        for bb in range(bs):
            u_ref[bb, s] = us[bb * TILE:(bb + 1) * TILE]
    _put_rows(q_ref, proj(C_Q, C_K) * (NA_HEAD_DIM ** -0.5), BF16)
    _put_rows(k_ref, proj(C_K, C_V), BF16)
    _put_rows(v_ref, proj(C_V, C_Z), BF16)
    _put_rows(z_ref, proj(C_Z, C_XBC), BF16)
    _put_rows(xbc_ref, proj(C_XBC, C_DT), BF16)
    _put_rows(dt_ref, proj(C_DT, IN_COLS_PAD))


def in_proj(stream, mod, norm_w, w_in_p, layer):
    bs = BS_WIDE
    tok = lambda w, dt: jax.ShapeDtypeStruct((BATCH, T, w), dt)
    tspec = lambda w: pl.BlockSpec((bs, TILE, w), lambda b, t: (b, t, 0))
    return pl.pallas_call(
        _inproj_kernel,
        grid=(BATCH // bs, NT),
        in_specs=_stream_specs(stream, bs) + [
                  pl.BlockSpec((bs, 1, 6, D_MODEL), lambda b, t: (b, _seg(t), 0, 0)),
                  pl.BlockSpec((1, D_MODEL), lambda b, t: (0, 0)),
                  pl.BlockSpec((1, D_MODEL, IN_COLS_PAD), lambda b, t: (layer, 0, 0))],
        out_specs=[pl.BlockSpec((bs, S5_WIDTH // LANE, TILE, LANE), lambda b, t: (b, 0, t, 0)),
                   tspec(NA_WIDTH), tspec(NA_WIDTH), tspec(NA_WIDTH),
                   tspec(SSD_WIDTH), tspec(SSD_XBC), tspec(LANE)],
        out_shape=[jax.ShapeDtypeStruct((BATCH, S5_WIDTH // LANE, T, LANE), F32),
                   tok(NA_WIDTH, BF16), tok(NA_WIDTH, BF16), tok(NA_WIDTH, BF16),
                   tok(SSD_WIDTH, BF16), tok(SSD_XBC, BF16), tok(LANE, F32)],
        compiler_params=_cp("arbitrary", "arbitrary"),
        name="in_proj",
    )(stream[0], stream[1], mod, norm_w.reshape(1, D_MODEL), w_in_p)


S5_BLK = SUB
NB = T // S5_BLK
NB_CTX = CTX_LEN // S5_BLK
S5_NPAIR = S5_NGROUPS // 2
S5_PW = 2 * S5_BLK * S5_GROUP


def _s5_kernel(u_ref, pin_ref, pout_ref, w1_ref, w2_ref, w3_ref, mul_ref, y_ref, ub_ref, yb_ref, st_ref):
    n = S5_NSTATE
    half = LANE
    per_half = S5_NPAIR // 2
    toks = [jnp.concatenate([u_ref[0, h, pl.ds(j, NB, stride=S5_BLK), :].astype(BF16) for j in range(S5_BLK)],
                            axis=1) for h in range(2)]
    for pp in range(S5_NPAIR):
        h, q = divmod(pp, per_half)
        ub_ref[:, pp * S5_PW:(pp + 1) * S5_PW] = _dot(toks[h], pin_ref[q]).astype(BF16)

    for d in range(2):
        for pp in range(S5_NPAIR):
            s = _dot(ub_ref[:, pp * S5_PW:(pp + 1) * S5_PW], w1_ref[d, pp])
            st_ref[d, :, pp * half:(pp + 1) * half] = s[:, :half]
            st_ref[d, :, n + pp * half:n + (pp + 1) * half] = s[:, half:]

    ngrp = NB // SUB
    nctx = NB_CTX // SUB
    rowid = lax.broadcasted_iota(I32, (SUB, n), 0)
    for d in range(2):
        def body(j, carry, d=d):
            cr, ci = carry
            r = j if d == 0 else jnp.where(j < nctx, nctx - 1 - j, ngrp - 1 + nctx - j)
            row = pl.multiple_of(r * SUB, SUB)
            re = st_ref[d, pl.ds(row, SUB), 0:n]
            im = st_ref[d, pl.ds(row, SUB), n:2 * n]
            for kk, sh in enumerate((1, 2, 4)):
                mr = mul_ref[d, kk * SUB:(kk + 1) * SUB, 0:n]
                mi = mul_ref[d, kk * SUB:(kk + 1) * SUB, n:2 * n]
                s = sh if d == 0 else SUB - sh
                sr = pltpu.roll(re, s, 0)
                si = pltpu.roll(im, s, 0)
                re, im = re + (mr * sr - mi * si), im + (mr * si + mi * sr)
            pr = mul_ref[d, 3 * SUB:4 * SUB, 0:n]
            pi = mul_ref[d, 3 * SUB:4 * SUB, n:2 * n]
            re, im = re + (pr * cr - pi * ci), im + (pr * ci + pi * cr)
            edge, last, sh = (0, SUB - 1, 1) if d == 0 else (SUB - 1, 0, SUB - 1)
            st_ref[d, pl.ds(row, SUB), 0:n] = jnp.where(rowid == edge, cr, pltpu.roll(re, sh, 0))
            st_ref[d, pl.ds(row, SUB), n:2 * n] = jnp.where(rowid == edge, ci, pltpu.roll(im, sh, 0))
            return re[last:last + 1, :], im[last:last + 1, :]

        zero = jnp.zeros((1, n), F32)
        lax.fori_loop(0, ngrp, body, (zero, zero), unroll=2)

    for pp in range(S5_NPAIR):
        up = ub_ref[:, pp * S5_PW:(pp + 1) * S5_PW]
        acc = None
        for d in range(2):
            enter = jnp.concatenate([st_ref[d, :, pp * half:(pp + 1) * half],
                                     st_ref[d, :, n + pp * half:n + (pp + 1) * half]], axis=1).astype(BF16)
            term = _dot(up, w2_ref[d, pp]) + _dot(enter, w3_ref[d, pp])
            acc = term if acc is None else acc + term
        yb_ref[:, pp * S5_PW:(pp + 1) * S5_PW] = acc.astype(BF16)

    for m in range(S5_BLK // 2):
        for h in range(2):
            acc = None
            for q in range(per_half):
                pp = h * per_half + q
                term = _dot(yb_ref[:, pp * S5_PW:(pp + 1) * S5_PW], pout_ref[q, m])
                acc = term if acc is None else acc + term
            for k in range(2):
                y_ref[0, h, pl.ds(2 * m + k, NB, stride=S5_BLK), :] = acc[:, k * LANE:(k + 1) * LANE]


def _s5_regroup_matrices():
    per_half = S5_NPAIR // 2
    pin = np.zeros((per_half, S5_BLK, LANE, S5_PW), np.float32)
    for q in range(per_half):
        for j in range(S5_BLK):
            for gg in range(2):
                for c in range(S5_GROUP):
                    pin[q, j, (2 * q + gg) * S5_GROUP + c, gg * S5_BLK * S5_GROUP + j * S5_GROUP + c] = 1.0
    pout = pin.transpose(0, 1, 3, 2).reshape(per_half, S5_BLK // 2, 2, S5_PW, LANE)
    pout = pout.transpose(0, 1, 3, 2, 4).reshape(per_half, S5_BLK // 2, S5_PW, 2 * LANE)
    return jnp.asarray(pin.reshape(per_half, S5_BLK * LANE, S5_PW), BF16), jnp.asarray(pout, BF16)


def s5_mix(u, w1, w2, w3, mul):
    pin, pout = _s5_regroup_matrices()
    per_half = S5_NPAIR // 2
    wspec = pl.BlockSpec((2, S5_NPAIR, S5_PW, S5_PW), lambda b: (0, 0, 0, 0))
    tok = pl.BlockSpec((1, S5_WIDTH // LANE, T, LANE), lambda b: (b, 0, 0, 0))
    return pl.pallas_call(
        _s5_kernel,
        grid=(BATCH,),
        in_specs=[tok, pl.BlockSpec((per_half, S5_BLK * LANE, S5_PW), lambda b: (0, 0, 0)),
                  pl.BlockSpec((per_half, S5_BLK // 2, S5_PW, 2 * LANE), lambda b: (0, 0, 0, 0)), wspec, wspec, wspec,
                  pl.BlockSpec((2, 4 * SUB, 2 * S5_NSTATE), lambda b: (0, 0, 0))],
        out_specs=tok,
        out_shape=jax.ShapeDtypeStruct((BATCH, S5_WIDTH // LANE, T, LANE), F32),
        scratch_shapes=[pltpu.VMEM((NB, S5_NGROUPS * S5_BLK * S5_GROUP), BF16),
                        pltpu.VMEM((NB, S5_NGROUPS * S5_BLK * S5_GROUP), BF16),
                        pltpu.VMEM((2, NB, 2 * S5_NSTATE), F32)],
        compiler_params=_cp("arbitrary"),
        name="s5_mix",
    )(u, pin, pout, w1, w2, w3, mul)


def s5_params(lam_re, lam_im, log_dt, b_re, b_im, c_re, c_im):
    G, P, C = S5_NGROUPS, S5_STATE, S5_GROUP
    lam = lax.complex(lam_re.astype(F32), lam_im.astype(F32))
    step = jnp.exp(log_dt.astype(F32))[..., None]
    log_lb = lam * step
    lam_bar = jnp.exp(log_lb)
    b_bar = ((lam_bar - 1.0) / lam)[..., None] * lax.complex(b_re.astype(F32), b_im.astype(F32))
    J = S5_BLK
    row = lambda a: a.reshape(2, G, 1, P)
    col = lambda a: jnp.broadcast_to(a[..., None], (2, G, P, J * C))
    b_t = jnp.swapaxes(b_bar, 2, 3)
    c_t = lambda a: jnp.tile(jnp.swapaxes(a.astype(F32), 2, 3), (1, 1, 1, J))
    wspec = pl.BlockSpec((1, 1, S5_PW, S5_PW), lambda d, q: (d, q, 0, 0))
    pair = lambda r, s: pl.BlockSpec((1, 2, r, s), lambda d, q: (d, q, 0, 0))
    w1, w2, w3 = pl.pallas_call(
        _s5_weight_kernel,
        grid=(2, S5_NPAIR),
        in_specs=[pair(1, P), pair(1, P), pair(P, J * C), pair(P, J * C), pair(C, P), pair(C, P),
                  pair(P, J * C), pair(P, J * C)],
        out_specs=[wspec, wspec, wspec],
        out_shape=[jax.ShapeDtypeStruct((2, S5_NPAIR, S5_PW, S5_PW), BF16)] * 3,
        compiler_params=_cp("arbitrary", "arbitrary"),
        name="s5_weights",
    )(row(jnp.real(log_lb)), row(jnp.imag(log_lb)), col(jnp.real(log_lb)), col(jnp.imag(log_lb)),
      jnp.real(b_t), jnp.imag(b_t), c_t(c_re), c_t(c_im))
    rows = jnp.arange(SUB)
    pieces = []
    for d in range(2):
        log_blk = (log_lb[d] * float(J)).reshape(1, G * P)
        per_d = []
        for sh in (1, 2, 4):
            valid = (rows >= sh) if d == 0 else (rows < SUB - sh)
            per_d.append(jnp.where(valid[:, None], jnp.exp(log_blk * float(sh)), 0.0))
        expo = (rows + 1) if d == 0 else (SUB - rows)
        per_d.append(jnp.exp(log_blk * expo[:, None].astype(F32)))
        m = jnp.concatenate(per_d, axis=0)
        pieces.append(jnp.concatenate([jnp.real(m), jnp.imag(m)], axis=-1))
    mul = jnp.stack(pieces, axis=0).astype(F32)
    return w1, w2, w3, mul


def _s5_weight_kernel(llr_ref, lli_ref, lcr_ref, lci_ref, btr_ref, bti_ref, ctr_ref, cti_ref, w1_ref, w2_ref, w3_ref):
    J, C, P = S5_BLK, S5_GROUP, S5_STATE
    R = J * C
    fwd = pl.program_id(0) == 0
    shift = C.bit_length() - 1
    j_of_row = lax.shift_right_logical(lax.broadcasted_iota(I32, (R, P), 0), shift).astype(F32)
    i_of_col = lax.shift_right_logical(lax.broadcasted_iota(I32, (P, R), 1), shift).astype(F32)
    jr = lax.shift_right_logical(lax.broadcasted_iota(I32, (R, R), 0), shift)
    ic = lax.shift_right_logical(lax.broadcasted_iota(I32, (R, R), 1), shift)
    lag = jnp.where(fwd, ic - jr, jr - ic)

    def cpow(expo, lr, li):
        mag = jnp.exp(lr * expo)
        return mag * jnp.cos(li * expo), mag * jnp.sin(li * expo)

    w1_ref[...] = jnp.zeros_like(w1_ref)
    w2_ref[...] = jnp.zeros_like(w2_ref)
    w3_ref[...] = jnp.zeros_like(w3_ref)
    for gg in range(2):
        llr, lli = llr_ref[0, gg], lli_ref[0, gg]
        br = jnp.concatenate([btr_ref[0, gg]] * J, axis=0)
        bi = jnp.concatenate([bti_ref[0, gg]] * J, axis=0)
        cr, ci = ctr_ref[0, gg], cti_ref[0, gg]
        pr, pi = cpow(jnp.where(fwd, (J - 1) - j_of_row, j_of_row), llr, lli)
        w1_ref[0, 0, gg * R:(gg + 1) * R, gg * P:(gg + 1) * P] = (pr * br - pi * bi).astype(BF16)
        w1_ref[0, 0, gg * R:(gg + 1) * R, 2 * P + gg * P:2 * P + (gg + 1) * P] = (pr * bi + pi * br).astype(BF16)
        qr, qi = cpow(jnp.where(fwd, i_of_col + 1.0, J - i_of_col), lcr_ref[0, gg], lci_ref[0, gg])
        w3_ref[0, 0, gg * P:(gg + 1) * P, gg * R:(gg + 1) * R] = (cr * qr - ci * qi).astype(BF16)
        w3_ref[0, 0, 2 * P + gg * P:2 * P + (gg + 1) * P, gg * R:(gg + 1) * R] = (-(cr * qi + ci * qr)).astype(BF16)
        acc = jnp.zeros((R, R), F32)
        for k in range(J):
            lr, li = cpow(float(k), llr, lli)
            t = _dot_x3(br * lr - bi * li, cr) - _dot_x3(br * li + bi * lr, ci)
            acc = acc + jnp.where(lag == k, t, 0.0)
        w2_ref[0, 0, gg * R:(gg + 1) * R, gg * R:(gg + 1) * R] = acc.astype(BF16)


def _softmax_pv(parts):
    m = parts[0][0].max(axis=-1, keepdims=True)
    for s, _ in parts[1:]:
        m = jnp.maximum(m, s.max(axis=-1, keepdims=True))
    den = 0.0
    acc = 0.0
    for s, v in parts:
        p = jnp.exp(s - m)
        den = den + p.sum(axis=-1, keepdims=True)
        acc = acc + _dot(p.astype(BF16), v)
    return acc / den


def _na_kernel(q_ref, k_ref, v_ref, bias_ref, o_ref):
    t = pl.program_id(1)
    first = lax.broadcasted_iota(I32, (1, LANE), 1) < NA_HEAD_DIM

    def pair_scores(bb, pp, start):
        ls = slice(pp * LANE, (pp + 1) * LANE)
        qp = q_ref[bb, :, ls]
        kc = k_ref[bb, 0:CTX_LEN, ls]
        vc = v_ref[bb, 0:CTX_LEN, ls]
        scores = []
        for hh in range(2):
            qm = jnp.where(first if hh == 0 else ~first, qp, jnp.zeros_like(qp))
            parts = [(_dot_nt(qm, kc), vc)]
            if start is not None:
                kb = k_ref[bb, pl.ds(start, NA_BAND * GRID_W), ls]
                vb = v_ref[bb, pl.ds(start, NA_BAND * GRID_W), ls]
                parts.append((_dot_nt(qm, kb) + bias_ref[2 * pp + hh, 0], vb))
            scores.append(parts)
        return scores

    def attention(start):
        for bb in range(q_ref.shape[0]):
            scores = [pair_scores(bb, pp, start) for pp in range(NA_HEADS // 2)]
            for pp, (s0, s1) in enumerate(scores):
                o_ref[bb, :, pp * LANE:(pp + 1) * LANE] = jnp.where(first, _softmax_pv(s0),
                                                                    _softmax_pv(s1)).astype(BF16)

    @pl.when(t == 0)
    def _():
        attention(None)

    @pl.when(t > 0)
    def _():
        first_row = (t - 1) * (TILE // GRID_W)
        u0 = jnp.clip(first_row - NA_KH // 2, 0, ROWS - NA_BAND)
        attention(pl.multiple_of(CTX_LEN + u0 * GRID_W, LANE))


def _na_cfg(t):
    return jnp.where(t <= 1, 0, jnp.where(t == NT - 1, 2, 1))


def natten(q, k, v, bias, layer):
    bs = BS_WIDE
    whole = pl.BlockSpec((bs, T, NA_WIDTH), lambda b, t: (b, 0, 0), pipeline_mode=pl.Buffered(1))
    tile = pl.BlockSpec((bs, TILE, NA_WIDTH), lambda b, t: (b, t, 0))
    return pl.pallas_call(
        _na_kernel,
        grid=(BATCH // bs, NT),
        in_specs=[tile, whole, whole,
                  pl.BlockSpec((NA_HEADS, 1, TILE, NA_BAND * GRID_W), lambda b, t: (layer, _na_cfg(t), 0, 0))],
        out_specs=tile,
        out_shape=jax.ShapeDtypeStruct((BATCH, T, NA_WIDTH), BF16),
        compiler_params=_cp("arbitrary", "arbitrary"),
        name="natten",
    )(q, k, v, bias)


def natten_bias(rpb):
    col = jnp.arange(GRID_W)
    c0 = jnp.clip(col - NA_KW // 2, 0, GRID_W - NA_KW)
    in_win = (col[None, :] >= c0[:, None]) & (col[None, :] < c0[:, None] + NA_KW)
    rel_c = jnp.clip(col[None, :] - col[:, None] + (NA_KW - 1), 0, RPB_W - 1)
    pick_c = jax.nn.one_hot(rel_c, RPB_W, dtype=F32)
    blocks = jnp.einsum('hax,qkx->haqk', rpb.astype(F32), pick_c, precision=lax.Precision.HIGHEST)
    blocks = jnp.where(in_win[None, None], blocks, NEG)
    nh = rpb.shape[0]
    blocks = jnp.concatenate([blocks, jnp.full((nh, 1, GRID_W, GRID_W), NEG, F32)], axis=1)
    blocks = jnp.concatenate([blocks, blocks], axis=-1)
    return pl.pallas_call(
        _bias_kernel,
        grid=(nh, 3),
        in_specs=[pl.BlockSpec((1, 2 * NA_KH, GRID_W, LANE), lambda h, c: (h, 0, 0, 0))],
        out_specs=pl.BlockSpec((1, 1, TILE, NA_BAND * GRID_W), lambda h, c: (h, c, 0, 0)),
        out_shape=jax.ShapeDtypeStruct((nh, 3, TILE, NA_BAND * GRID_W), F32),
        compiler_params=_cp("arbitrary", "arbitrary"),
        name="natten_bias",
    )(blocks)


def _bias_kernel(blk_ref, o_ref):
    rows_per_tile = TILE // GRID_W
    masked = 2 * NA_KH - 1
    for cfg, first_row in enumerate((0, 2 * rows_per_tile, ROWS - rows_per_tile)):
        @pl.when(pl.program_id(1) == cfg)
        def _(first_row=first_row):
            u0 = min(max(first_row - NA_KH // 2, 0), ROWS - NA_BAND)
            for rr in range(rows_per_tile):
                r = first_row + rr
                r0 = min(max(r - NA_KH // 2, 0), ROWS - NA_KH)
                for j in range(NA_BAND):
                    kr = u0 + j
                    a = kr - r + NA_KH - 1 if r0 <= kr < r0 + NA_KH else masked
                    half = (j % 2) * GRID_W
                    o_ref[0, 0, rr * GRID_W:(rr + 1) * GRID_W, j * GRID_W:(j + 1) * GRID_W] = (
                        blk_ref[0, a, :, half:half + GRID_W])


def _softplus(x):
    return jnp.maximum(x, 0.0) + jnp.log(1.0 + jnp.exp(-jnp.abs(x)))


def _ssd_prep_kernel(prev_ref, cur_ref, next_ref, dtr_ref, cw_ref, cb_ref, dtb_ref, ar_ref, cos_ref, sin_ref,
                     xs_ref, bm_ref, cm_ref, dt_ref, a_ref):
    for bb in range(cur_ref.shape[0]):
        _ssd_prep_sample(bb, prev_ref, cur_ref, next_ref, dtr_ref, cw_ref, cb_ref, dtb_ref, ar_ref, cos_ref, sin_ref,
                         xs_ref, bm_ref, cm_ref, dt_ref, a_ref)


def _ssd_prep_sample(bb, prev_ref, cur_ref, next_ref, dtr_ref, cw_ref, cb_ref, dtb_ref, ar_ref, cos_ref, sin_ref,
                     xs_ref, bm_ref, cm_ref, dt_ref, a_ref):
    t = pl.program_id(1)
    halo = prev_ref.shape[1]
    has_prev = t >= 2
    has_next = (t >= 1) & (t <= NT - 2)
    prev = jnp.where(has_prev, prev_ref[bb].astype(F32), 0.0)
    nxt = jnp.where(has_next, next_ref[bb].astype(F32), 0.0)
    ext = jnp.concatenate([prev, cur_ref[bb].astype(F32), nxt], axis=0)
    n = ext.shape[0]
    acc = cb_ref[...] + cw_ref[SSD_CONV // 2:SSD_CONV // 2 + 1, :] * ext
    for kk in range(SSD_CONV):
        off = kk - SSD_CONV // 2
        if off != 0:
            acc = acc + cw_ref[kk:kk + 1, :] * pltpu.roll(ext, (-off) % n, 0)
    y = _silu(acc[halo:halo + TILE, :])
    xs_ref[bb] = y[:, 0:SSD_WIDTH].astype(BF16)

    lane = lax.broadcasted_iota(I32, (1, LANE), 1)
    quarter = SSD_STATE // 4
    low = (lane & quarter) == 0
    cos = cos_ref[...]
    sin = sin_ref[...]
    for g in range(2 * SSD_NGROUPS):
        v = y[:, SSD_WIDTH + g * LANE:SSD_WIDTH + (g + 1) * LANE]
        sw = jnp.where(low, pltpu.roll(v, LANE - quarter, 1), pltpu.roll(v, quarter, 1))
        rot = (v * cos + sw * sin).astype(BF16)
        if g < SSD_NGROUPS:
            bm_ref[bb, :, g * LANE:(g + 1) * LANE] = rot
        else:
            cm_ref[bb, :, (g - SSD_NGROUPS) * LANE:(g - SSD_NGROUPS + 1) * LANE] = rot

    dt = _softplus(dtr_ref[bb] + dtb_ref[...])
    dt_ref[bb] = dt
    a_ref[bb] = dt * ar_ref[...]


def ssd_prep(xbc, dt_raw, conv_w, conv_b, dt_bias, a_log, cos_t, sin_t):
    halo = 2 * SUB
    per = TILE // halo
    nhalo = T // halo
    tok = lambda w, dt: jax.ShapeDtypeStruct((BATCH, T, w), dt)
    bs = BS_WIDE
    tspec = lambda w: pl.BlockSpec((bs, TILE, w), lambda b, t: (b, t, 0))
    row = lambda w: pl.BlockSpec((1, w), lambda b, t: (0, 0))
    cw = jnp.zeros((SUB, SSD_XBC), F32).at[:SSD_CONV].set(conv_w.astype(F32))
    pad12 = lambda v: jnp.zeros((1, LANE), F32).at[0, :2 * SSD_HEADS].set(v.astype(F32).reshape(-1))
    return pl.pallas_call(
        _ssd_prep_kernel,
        grid=(BATCH // bs, NT),
        in_specs=[pl.BlockSpec((bs, halo, SSD_XBC), lambda b, t: (b, jnp.maximum(t * per - 1, 0), 0)),
                  tspec(SSD_XBC),
                  pl.BlockSpec((bs, halo, SSD_XBC), lambda b, t: (b, jnp.minimum((t + 1) * per, nhalo - 1), 0)),
                  tspec(LANE),
                  pl.BlockSpec((SUB, SSD_XBC), lambda b, t: (0, 0)), row(SSD_XBC), row(LANE), row(LANE),
                  pl.BlockSpec((TILE, LANE), lambda b, t: (t, 0)), pl.BlockSpec((TILE, LANE), lambda b, t: (t, 0))],
        out_specs=[tspec(SSD_WIDTH), tspec(SSD_BC), tspec(SSD_BC), tspec(LANE), tspec(LANE)],
        out_shape=[tok(SSD_WIDTH, BF16), tok(SSD_BC, BF16), tok(SSD_BC, BF16), tok(LANE, F32), tok(LANE, F32)],
        compiler_params=_cp("arbitrary", "arbitrary"),
        name="ssd_prep",
    )(xbc, xbc, xbc, dt_raw, cw, conv_b.astype(F32).reshape(1, SSD_XBC), pad12(dt_bias),
      pad12(-jnp.exp(a_log.astype(F32))), cos_t, sin_t)


def rope_tables():
    half = SSD_STATE // 2
    nf = half // 2
    pos = jnp.arange(SEQ)
    inv_freq = ROPE_BASE ** (-jnp.arange(nf, dtype=F32) / nf)
    lane = jnp.arange(LANE)
    p = jnp.where(lane[None, :] < half, (pos // GRID_W)[:, None], (pos % GRID_W)[:, None]).astype(F32)
    ang = p * inv_freq[lane % nf][None, :]
    sign = jnp.where((lane & nf) == 0, -1.0, 1.0)[None, :]
    cos_t = jnp.concatenate([jnp.ones((CTX_LEN, LANE), F32), jnp.cos(ang)], axis=0)
    sin_t = jnp.concatenate([jnp.zeros((CTX_LEN, LANE), F32), jnp.sin(ang) * sign], axis=0)
    return cos_t, sin_t


def _ssd_sums(d, bm_ref, cm_ref, a_ref, at_ref, tri_ref):
    tri_col = tri_ref[d]
    tri_row = tri_ref[1 - d]
    cs_col = _dot_exact_rhs(tri_col, a_ref[0])
    cs_row = _dot_exact_lhs(at_ref[0], tri_row)
    g_mats = [_dot_nt(cm_ref[0, :, g * SSD_STATE:(g + 1) * SSD_STATE],
                      bm_ref[0, :, g * SSD_STATE:(g + 1) * SSD_STATE]) for g in range(SSD_NGROUPS)]
    return cs_col, cs_row, g_mats


def _ssd_dir(d, sums, xs_ref, cm_ref, bt_ref, dt_ref, y_ref, st_ref):
    q = TILE
    cs_col, cs_row, g_mats = sums
    lane = lax.broadcasted_iota(I32, (1, LANE), 1)
    first = lane < SSD_HEAD_DIM
    ri = lax.broadcasted_iota(I32, (q, q), 0)
    ci = lax.broadcasted_iota(I32, (q, q), 1)
    keep = (ci <= ri) if d == 0 else (ci >= ri)
    end = q - 1 if d == 0 else 0
    dt = dt_ref[0]

    def head_col(m, h):
        c = d * SSD_HEADS + h
        return m[:, c:c + 1]

    for pp in range(SSD_HEADS // 2):
        ls = slice(pp * LANE, (pp + 1) * LANE)
        h0, h1 = 2 * pp, 2 * pp + 1
        x = xs_ref[0, :, ls].astype(F32)
        dt_l = jnp.where(first, head_col(dt, h0), head_col(dt, h1))
        cs_l = jnp.where(first, head_col(cs_col, h0), head_col(cs_col, h1))
        cs_end = cs_l[end:end + 1, :]
        xdt = x * dt_l
        xdt_b = xdt.astype(BF16)
        xw = (xdt * jnp.exp(cs_end - cs_l)).astype(BF16)
        st = st_ref[d, pp]
        st_b = st.astype(BF16)
        ys, ups = [], []
        for h in (h0, h1):
            g = h // (SSD_HEADS // SSD_NGROUPS)
            c = d * SSD_HEADS + h
            diff = head_col(cs_col, h) - cs_row[c:c + 1, :]
            decay = jnp.exp(jnp.where(keep, diff, NEG))
            m = (g_mats[g] * decay).astype(BF16)
            y_h = _dot(m, xdt_b) + _dot(cm_ref[0, :, g * SSD_STATE:(g + 1) * SSD_STATE], st_b) * jnp.exp(cs_l)
            ys.append(y_h)
            ups.append(_dot(bt_ref[0, g * SSD_STATE:(g + 1) * SSD_STATE, :], xw))
        y_ref[0, :, ls] = jnp.where(first, ys[0], ys[1])
        st_ref[d, pp] = jnp.exp(cs_end) * st + jnp.where(first, ups[0], ups[1])


def _ssd_scan_kernel(xs_f, bm_f, cm_f, bt_f, dt_f, a_f, at_f, xs_b, bm_b, cm_b, bt_b, dt_b, a_b, at_b, tri_ref,
                     yf_ref, yb_ref, st_ref):
    @pl.when(pl.program_id(1) == 0)
    def _():
        st_ref[...] = jnp.zeros_like(st_ref)

    for bb in range(xs_f.shape[0]):
        one = lambda *refs: [r.at[pl.ds(bb, 1)] for r in refs]
        xsf, bmf, cmf, btf, dtf, af, atf, yf = one(xs_f, bm_f, cm_f, bt_f, dt_f, a_f, at_f, yf_ref)
        xsb, bmb, cmb, btb, dtb, ab, atb, yb = one(xs_b, bm_b, cm_b, bt_b, dt_b, a_b, at_b, yb_ref)
        st = st_ref.at[bb]
        sums_f = _ssd_sums(0, bmf, cmf, af, atf, tri_ref)
        sums_b = _ssd_sums(1, bmb, cmb, ab, atb, tri_ref)
        _ssd_dir(0, sums_f, xsf, cmf, btf, dtf, yf, st)
        _ssd_dir(1, sums_b, xsb, cmb, btb, dtb, yb, st)


def ssd_scan(xs, bm, cm, dt, a):
    bs = BS_WIDE
    bt = jnp.swapaxes(bm, 1, 2)
    at = jnp.swapaxes(a[:, :, :2 * SUB], 1, 2)
    idx = jnp.arange(TILE)
    tri = jnp.stack([idx[None, :] <= idx[:, None], idx[None, :] >= idx[:, None]]).astype(BF16)
    fwd = lambda b, i: (b, i, 0)
    bwd = lambda b, i: (b, _bwd_tile(i), 0)
    fwd_t = lambda b, i: (b, 0, i)
    bwd_t = lambda b, i: (b, 0, _bwd_tile(i))

    def specs(f, ft):
        return [pl.BlockSpec((bs, TILE, SSD_WIDTH), f), pl.BlockSpec((bs, TILE, SSD_BC), f),
                pl.BlockSpec((bs, TILE, SSD_BC), f), pl.BlockSpec((bs, SSD_BC, TILE), ft),
                pl.BlockSpec((bs, TILE, LANE), f), pl.BlockSpec((bs, TILE, LANE), f),
                pl.BlockSpec((bs, 2 * SUB, TILE), ft)]

    args = (xs, bm, cm, bt, dt, a, at)
    return pl.pallas_call(
        _ssd_scan_kernel,
        grid=(BATCH // bs, NT),
        in_specs=specs(fwd, fwd_t) + specs(bwd, bwd_t) + [pl.BlockSpec((2, TILE, TILE), lambda b, i: (0, 0, 0))],
        out_specs=[pl.BlockSpec((bs, TILE, SSD_WIDTH), fwd), pl.BlockSpec((bs, TILE, SSD_WIDTH), bwd)],
        out_shape=[jax.ShapeDtypeStruct((BATCH, T, SSD_WIDTH), F32)] * 2,
        scratch_shapes=[pltpu.VMEM((bs, 2, SSD_HEADS // 2, SSD_STATE, LANE), F32)],
        compiler_params=_cp("arbitrary", "arbitrary"),
        name="ssd_scan",
    )(*args, *args, tri)


def _gelu_tanh(x):
    return 0.5 * x * (1.0 + jnp.tanh(math.sqrt(2.0 / math.pi) * (x + 0.044715 * (x * x * x))))


def _post_kernel(xh_ref, xt_ref, mod_ref, u_ref, s5y_ref, na_ref, xs_ref, z_ref, sdf_ref, sdb_ref,
                 s5d_ref, gw_ref, gb_ref, sdd_ref, snw_ref, wo_ref, n2w_ref, rt_ref,
                 x1_ref, h_ref, lg_ref):
    for bb in range(BS):
        slabs = lambda r: jnp.concatenate([r[bb, s] for s in range(S5_WIDTH // LANE)], axis=-1)
        ys5 = slabs(u_ref) * s5d_ref[...] + slabs(s5y_ref)
        g = _gelu_tanh(ys5)
        s5o = g * jax.nn.sigmoid(_dot(g.astype(BF16), gw_ref[...]) + gb_ref[...])
        yssd = (xs_ref[bb].astype(F32) * sdd_ref[...] + sdf_ref[bb] + sdb_ref[bb]) * _silu(z_ref[bb].astype(F32))
        ssdo = yssd * lax.rsqrt(jnp.mean(yssd * yssd, axis=-1, keepdims=True) + EPS) * snw_ref[...]
        mix = jnp.concatenate([s5o.astype(BF16), na_ref[bb], ssdo.astype(BF16)], axis=-1)
        x1 = _stream_tile(xh_ref, xt_ref, bb) + mod_ref[bb, 0, 2:3, :] * _dot(mix, wo_ref[...])
        x1_ref[bb] = x1
        h = _modulated_norm(x1, n2w_ref[...], mod_ref[bb, 0, 3:4, :], mod_ref[bb, 0, 4:5, :])
        h_ref[bb] = h.astype(BF16)
        lg_ref[bb] = _dot_x3(h, rt_ref[...])


def post_mixer(stream, mod, u, s5y, na, xs, z, sdf, sdb, s5_d, glu_w, glu_b, ssd_d, ssd_norm_w, w_out, norm2_w,
               router):
    tspec = lambda w: pl.BlockSpec((BS, TILE, w), lambda b, t: (b, t, 0))
    whole = lambda *shp: pl.BlockSpec(shp, lambda b, t: (0,) * len(shp))
    rt = jnp.zeros((D_MODEL, LANE), F32).at[:, :N_EXPERTS].set(router.astype(F32))
    slab = pl.BlockSpec((BS, S5_WIDTH // LANE, TILE, LANE), lambda b, t: (b, 0, t, 0))
    return pl.pallas_call(
        _post_kernel,
        grid=(BATCH // BS, NT),
        in_specs=_stream_specs(stream) + [
                  pl.BlockSpec((BS, 1, 6, D_MODEL), lambda b, t: (b, _seg(t), 0, 0)),
                  slab, slab, tspec(NA_WIDTH),
                  tspec(SSD_WIDTH), tspec(SSD_WIDTH), tspec(SSD_WIDTH), tspec(SSD_WIDTH),
                  whole(1, S5_WIDTH), whole(S5_WIDTH, S5_WIDTH), whole(1, S5_WIDTH),
                  whole(1, SSD_WIDTH), whole(1, SSD_WIDTH), whole(D_MODEL, D_MODEL), whole(1, D_MODEL),
                  whole(D_MODEL, LANE)],
        out_specs=[tspec(D_MODEL), tspec(D_MODEL), tspec(LANE)],
        out_shape=[jax.ShapeDtypeStruct((BATCH, T, D_MODEL), F32), jax.ShapeDtypeStruct((BATCH, T, D_MODEL), BF16),
                   jax.ShapeDtypeStruct((BATCH, T, LANE), F32)],
        compiler_params=_cp("arbitrary", "arbitrary"),
        name="post_mixer",
    )(stream[0], stream[1], mod, u, s5y, na, xs, z, sdf, sdb,
      s5_d.astype(F32).reshape(1, S5_WIDTH), glu_w.astype(BF16), glu_b.astype(F32).reshape(1, S5_WIDTH),
      jnp.repeat(ssd_d.astype(F32), SSD_HEAD_DIM).reshape(1, SSD_WIDTH), ssd_norm_w.astype(F32).reshape(1, SSD_WIDTH),
      w_out.astype(BF16), norm2_w.astype(F32).reshape(1, D_MODEL), rt)


def _route_kernel(lg_ref, tri_ref, slot_ref, aff_ref, *, with_ctx):
    lg = lg_ref[0]
    m = lg.max(axis=0, keepdims=True)
    e = jnp.exp(lg - m)
    aff = e / e.sum(axis=0, keepdims=True)
    aff_ref[0] = aff
    bits = pltpu.bitcast(aff, I32)
    is_ctx = lax.broadcasted_iota(I32, (N_EXPERTS, T), 1) < CTX_LEN

    def count(mask):
        return jnp.where(mask, 1.0, 0.0).sum(axis=1, keepdims=True)

    def kth_largest(seg, k):
        def body(i, prefix):
            cand = prefix | lax.shift_left(jnp.int32(1), 30 - i)
            return jnp.where(count((bits >= cand) & seg) >= k, cand, prefix)
        return lax.fori_loop(0, 31, body, jnp.zeros((N_EXPERTS, 1), I32))

    def excl_cumsum(x01):
        carry = jnp.zeros((N_EXPERTS, 1), F32)
        pieces = []
        for j in range(T // LANE):
            blk = x01[:, j * LANE:(j + 1) * LANE]
            inc = _dot(blk.astype(BF16), tri_ref[...])
            pieces.append(inc - blk + carry)
            carry = carry + inc[:, LANE - 1:LANE]
        return jnp.concatenate(pieces, axis=1)

    thr = kth_largest(~is_ctx, float(CAP_LAT))
    k_of = jnp.full((N_EXPERTS, T), float(CAP_LAT), F32)
    if with_ctx:
        thr = jnp.where(is_ctx, kth_largest(is_ctx, float(CAP_CTX)), thr)
        k_of = jnp.where(is_ctx, float(CAP_CTX), k_of)
    gt = bits > thr
    eq = bits == thr
    if not with_ctx:
        gt = gt & ~is_ctx
        eq = eq & ~is_ctx
    n_gt = jnp.where(is_ctx, count(gt & is_ctx), count(gt & ~is_ctx))
    tie_rank = excl_cumsum(jnp.where(eq, 1.0, 0.0))
    tie_rank = tie_rank - jnp.where(is_ctx, 0.0, count(eq & is_ctx))
    sel = gt | (eq & (tie_rank < k_of - n_gt))
    pos = excl_cumsum(jnp.where(sel, 1.0, 0.0))
    slot = jnp.where(is_ctx, pos + float(CAP_LAT), pos - count(sel & is_ctx))
    slot_ref[0] = jnp.where(sel, slot, -1.0).astype(I32)


def route(logits_t, with_ctx):
    idx = jnp.arange(LANE)
    tri = (idx[:, None] <= idx[None, :]).astype(BF16)
    spec = pl.BlockSpec((1, N_EXPERTS, T), lambda b: (b, 0, 0))
    return pl.pallas_call(
        functools.partial(_route_kernel, with_ctx=with_ctx),
        grid=(BATCH,),
        in_specs=[spec, pl.BlockSpec((LANE, LANE), lambda b: (0, 0))],
        out_specs=[spec, spec],
        out_shape=[jax.ShapeDtypeStruct((BATCH, N_EXPERTS, T), I32), jax.ShapeDtypeStruct((BATCH, N_EXPERTS, T), F32)],
        compiler_params=_cp("arbitrary"),
        name="route",
    )(logits_t, tri)


GATHER_WIN = LANE // 2
COMBINE_WIN = LANE // 2


def slot_ranges(slot):
    s = slot.reshape(BATCH, N_EXPERTS, NT, TILE)
    has = s >= 0
    smax = jnp.max(jnp.where(has, s, -1), axis=-1)
    smin = jnp.where(smax >= 0, jnp.min(jnp.where(has, s, CAP_LAT + CAP_CTX), axis=-1), 0)
    return smin.reshape(-1).astype(I32), smax.reshape(-1).astype(I32)


def _gather_kernel(smin_ref, smax_ref, h_ref, slot_ref, xs_ref, *, nslot):
    b = pl.program_id(0)
    t = pl.program_id(1)
    group = N_EXPERTS
    sid = lax.broadcasted_iota(I32, (GATHER_WIN, TILE), 0)
    align = 2 * SUB

    def onehot(e, ws, lo):
        srow = slot_ref[0, e:e + 1, :]
        return jnp.where((sid + ws == srow) & (srow >= lo), 1.0, 0.0).astype(BF16)

    def add_rows(e, ws, rows):
        win = pl.ds(pl.multiple_of(ws, align), GATHER_WIN)
        xs_ref[0, e, win, :] = (xs_ref[0, e, win, :].astype(F32) + rows).astype(BF16)

    @pl.when(t == 0)
    def _():
        for e in range(N_EXPERTS):
            xs_ref[0, e, 0:CAP_LAT, :] = jnp.zeros((CAP_LAT, D_MODEL), BF16)
            if nslot > CAP_LAT:
                cid = lax.broadcasted_iota(I32, (nslot - CAP_LAT, TILE), 0) + CAP_LAT
                pick = jnp.where(cid == slot_ref[0, e:e + 1, :], 1.0, 0.0).astype(BF16)
                xs_ref[0, e, CAP_LAT:nslot, :] = _dot(pick, h_ref[0]).astype(BF16)

    @pl.when(t > 0)
    def _():
        los, wss, extras = [], [], []
        for e in range(N_EXPERTS):
            base = (b * N_EXPERTS + e) * NT + t
            lo = smin_ref[base] & ~(align - 1)
            los.append(lo)
            wss.append(jnp.minimum(lo, CAP_LAT - GATHER_WIN))
            extras.append(lax.shift_right_arithmetic(smax_ref[base] - lo, GATHER_WIN.bit_length() - 1))
        for g0 in range(0, N_EXPERTS, group):
            pick = jnp.concatenate([onehot(e, wss[e], los[e]) for e in range(g0, g0 + group)], axis=0)
            rows = _dot(pick, h_ref[0])
            for i, e in enumerate(range(g0, g0 + group)):
                add_rows(e, wss[e], rows[i * GATHER_WIN:(i + 1) * GATHER_WIN, :])
        most = extras[0]
        for x in extras[1:]:
            most = jnp.maximum(most, x)

        @pl.when(most > 0)
        def _():
            for e in range(N_EXPERTS):
                def more(k, carry, e=e):
                    lo_k = los[e] + k * GATHER_WIN
                    ws = jnp.minimum(lo_k, CAP_LAT - GATHER_WIN)
                    add_rows(e, ws, _dot(onehot(e, ws, lo_k), h_ref[0]))
                    return carry

                lax.fori_loop(1, extras[e] + 1, more, 0)


def moe_gather(h, slot, smin, smax, nslot):
    return pl.pallas_call(
        functools.partial(_gather_kernel, nslot=nslot),
        grid_spec=pltpu.PrefetchScalarGridSpec(
            num_scalar_prefetch=2,
            grid=(BATCH, NT),
            in_specs=[pl.BlockSpec((1, TILE, D_MODEL), lambda b, t, *_: (b, t, 0)),
                      pl.BlockSpec((1, N_EXPERTS, TILE), lambda b, t, *_: (b, 0, t))],
            out_specs=pl.BlockSpec((1, N_EXPERTS, nslot, D_MODEL), lambda b, t, *_: (b, 0, 0, 0))),
        out_shape=jax.ShapeDtypeStruct((BATCH, N_EXPERTS, nslot, D_MODEL), BF16),
        compiler_params=_cp("arbitrary", "arbitrary"),
        name="moe_gather",
    )(smin, smax, h, slot)


def _ffn_kernel(xs_ref, wg_ref, wu_ref, wd_ref, y_ref, w_ref):
    @pl.when(pl.program_id(1) == 0)
    def _():
        w_ref[0] = wg_ref[0, 0].astype(BF16)
        w_ref[1] = wu_ref[0, 0].astype(BF16)
        w_ref[2] = wd_ref[0, 0].astype(BF16)

    xs = xs_ref[0, 0]
    hid = _silu(_dot(xs, w_ref[0])) * _dot(xs, w_ref[1])
    y_ref[0, 0] = _dot(hid.astype(BF16), w_ref[2]).astype(BF16)


def moe_ffn(xs, layer, wg, wu, wd, nslot):
    wspec = lambda: pl.BlockSpec((1, 1, D_MODEL, D_EXPERT), lambda e, b: (layer, e, 0, 0))
    rows = pl.BlockSpec((1, 1, nslot, D_MODEL), lambda e, b: (b, e, 0, 0))
    return pl.pallas_call(
        _ffn_kernel,
        grid=(N_EXPERTS, BATCH),
        in_specs=[rows, wspec(), wspec(), wspec()],
        out_specs=rows,
        out_shape=jax.ShapeDtypeStruct((BATCH, N_EXPERTS, nslot, D_MODEL), BF16),
        scratch_shapes=[pltpu.VMEM((3, D_MODEL, D_EXPERT), BF16)],
        compiler_params=_cp("arbitrary", "arbitrary"),
        name="moe_ffn",
    )(xs, wg, wu, wd)


def _combine_kernel(smin_ref, smax_ref, x_ref, mod_ref, slot_ref, aff_ref, y_ref, fw_ref, o_ref, acc_ref, *,
                    nslot, last):
    b = pl.program_id(0)
    t = pl.program_id(1)
    shift = COMBINE_WIN.bit_length() - 1
    align = 2 * SUB

    def weights(e, lane, lo=None):
        s = slot_ref[0, :, e:e + 1]
        hit = (s == lane) if lo is None else ((s == lane) & (s >= lo))
        return jnp.where(hit, aff_ref[0, :, e:e + 1], 0.0).astype(BF16)

    def finish(acc):
        x2 = x_ref[0] + mod_ref[0, 0, 5:6, :] * acc
        if last:
            x2 = x2 * lax.rsqrt(jnp.mean(x2 * x2, axis=-1, keepdims=True) + EPS) * fw_ref[...]
        o_ref[0] = x2

    if not last:
        @pl.when(t == 0)
        def _():
            lane = lax.broadcasted_iota(I32, (TILE, nslot - CAP_LAT), 1) + CAP_LAT
            acc = jnp.zeros((TILE, D_MODEL), F32)
            for e in range(N_EXPERTS):
                acc = acc + _dot(weights(e, lane), y_ref[0, e, CAP_LAT:nslot, :])
            finish(acc)

    @pl.when(t > 0)
    def _():
        lane = lax.broadcasted_iota(I32, (TILE, COMBINE_WIN), 1)
        lane2 = lax.broadcasted_iota(I32, (TILE, 2 * COMBINE_WIN), 1)
        first = lane2 < COMBINE_WIN
        los, wss, extras = [], [], []
        for e in range(N_EXPERTS):
            base = (b * N_EXPERTS + e) * NT + t
            lo = smin_ref[base] & ~(align - 1)
            los.append(lo)
            wss.append(pl.multiple_of(jnp.minimum(lo, CAP_LAT - COMBINE_WIN), align))
            extras.append(lax.shift_right_arithmetic(smax_ref[base] - lo, shift))
        w_parts, y_parts = [], []
        for e in range(0, N_EXPERTS, 2):
            want = jnp.where(first, slot_ref[0, :, e:e + 1] - wss[e],
                             slot_ref[0, :, e + 1:e + 2] - wss[e + 1] + COMBINE_WIN)
            gate = jnp.where(first, aff_ref[0, :, e:e + 1], aff_ref[0, :, e + 1:e + 2])
            w_parts.append(jnp.where(want == lane2, gate, 0.0).astype(BF16))
            y_parts += [y_ref[0, e, pl.ds(wss[e], COMBINE_WIN), :], y_ref[0, e + 1, pl.ds(wss[e + 1], COMBINE_WIN), :]]
        acc_ref[...] = _dot(jnp.concatenate(w_parts, axis=1), jnp.concatenate(y_parts, axis=0))
        most = extras[0]
        for x in extras[1:]:
            most = jnp.maximum(most, x)

        @pl.when(most > 0)
        def _():
            for e in range(N_EXPERTS):
                def more(k, carry, e=e):
                    lo_k = los[e] + k * COMBINE_WIN
                    ws = pl.multiple_of(jnp.minimum(lo_k, CAP_LAT - COMBINE_WIN), align)
                    acc_ref[...] += _dot(weights(e, lane + ws, lo_k), y_ref[0, e, pl.ds(ws, COMBINE_WIN), :])
                    return carry

                lax.fori_loop(1, extras[e] + 1, more, 0)

        finish(acc_ref[...])


def moe_combine(x1, mod, slot_tok, aff_tok, y, smin, smax, nslot, final_w):
    last = final_w is not None
    tspec = lambda w: pl.BlockSpec((1, TILE, w), lambda b, t, *_: (b, t, 0))
    if last:
        first_lat = CTX_LEN // TILE
        out_spec = pl.BlockSpec((1, TILE, D_MODEL), lambda b, t, *_: (b, jnp.maximum(t - first_lat, 0), 0))
        out_shape = jax.ShapeDtypeStruct((BATCH, SEQ, D_MODEL), F32)
        fw = final_w.astype(F32).reshape(1, D_MODEL)
    else:
        out_spec = tspec(D_MODEL)
        out_shape = jax.ShapeDtypeStruct((BATCH, T, D_MODEL), F32)
        fw = jnp.ones((1, D_MODEL), F32)
    return pl.pallas_call(
        functools.partial(_combine_kernel, nslot=nslot, last=last),
        grid_spec=pltpu.PrefetchScalarGridSpec(
            num_scalar_prefetch=2,
            grid=(BATCH, NT),
            in_specs=[tspec(D_MODEL),
                      pl.BlockSpec((1, 1, 6, D_MODEL), lambda b, t, *_: (b, _seg(t), 0, 0)),
                      tspec(N_EXPERTS), tspec(N_EXPERTS),
                      pl.BlockSpec((1, N_EXPERTS, nslot, D_MODEL), lambda b, t, *_: (b, 0, 0, 0)),
                      pl.BlockSpec((1, D_MODEL), lambda b, t, *_: (0, 0))],
            out_specs=out_spec,
            scratch_shapes=[pltpu.VMEM((TILE, D_MODEL), F32)]),
        out_shape=out_shape,
        compiler_params=_cp("arbitrary", "arbitrary"),
        name="moe_combine",
    )(smin, smax, x1, mod, slot_tok, aff_tok, y, fw)


def trunk_layer(l, stream, mod, cos_t, sin_t, norm1_w, norm2_w, w_in_p, w_out, s5_weights, s5_d, s5_glu_w, s5_glu_b,
                na_bias, ssd_conv_w, ssd_conv_b, ssd_dt_bias, ssd_a_log, ssd_d, ssd_norm_w,
                moe_router, wg, wu, wd, final_w):
    with_ctx_out = final_w is None
    u, q, k, v, z, xbc, dt_raw = in_proj(stream, mod, norm1_w, w_in_p, l)
    s5y = s5_mix(u, *s5_weights)
    na = natten(q, k, v, na_bias, l)
    xs, bm, cm, dt, a = ssd_prep(xbc, dt_raw, ssd_conv_w, ssd_conv_b, ssd_dt_bias, ssd_a_log, cos_t, sin_t)
    sdf, sdb = ssd_scan(xs, bm, cm, dt, a)

    x1, h, logits = post_mixer(stream, mod, u, s5y, na, xs, z, sdf, sdb, s5_d, s5_glu_w, s5_glu_b,
                               ssd_d, ssd_norm_w, w_out, norm2_w, moe_router)
    slot, aff = route(jnp.swapaxes(logits[:, :, :N_EXPERTS], 1, 2), with_ctx_out)
    nslot = CAP_LAT + CAP_CTX if with_ctx_out else CAP_LAT
    smin, smax = slot_ranges(slot)
    y = moe_ffn(moe_gather(h, slot, smin, smax, nslot), l, wg, wu, wd, nslot)
    return moe_combine(x1, mod, jnp.swapaxes(slot, 1, 2), jnp.swapaxes(aff, 1, 2), y, smin, smax, nslot, final_w)


def kernel(x, c, ctx, c_ctx, w_ada, b_ada, norm1_w, norm2_w, w_in, w_out, s5_lam_re, s5_lam_im, s5_log_dt, s5_b_re, s5_b_im, s5_c_re, s5_c_im, s5_d, s5_glu_w, s5_glu_b, na_rpb, ssd_conv_w, ssd_conv_b, ssd_dt_bias, ssd_a_log, ssd_d, ssd_norm_w, moe_router, moe_w_gate, moe_w_up, moe_w_down, final_norm_w):
    cvec = jnp.zeros((SUB, D_MODEL), F32).at[0].set(c_ctx.astype(F32)).at[1:1 + BATCH].set(c.astype(F32))
    mods = ada_mod(cvec, w_ada.astype(F32), b_ada.astype(F32)).reshape(DEPTH, SUB, 6, D_MODEL)
    mods = jnp.stack([jnp.broadcast_to(mods[:, 0:1], (DEPTH, BATCH, 6, D_MODEL)), mods[:, 1:1 + BATCH]], axis=2)
    cos_t, sin_t = rope_tables()
    w_in_p = jnp.zeros((DEPTH, D_MODEL, IN_COLS_PAD), BF16).at[:, :, :IN_COLS].set(w_in.astype(BF16))
    s5_weights = [s5_params(s5_lam_re[l], s5_lam_im[l], s5_log_dt[l], s5_b_re[l], s5_b_im[l], s5_c_re[l], s5_c_im[l])
                  for l in range(DEPTH)]
    na_bias = natten_bias(na_rpb.reshape(DEPTH * NA_HEADS, 2 * NA_KH - 1, RPB_W))
    wg, wu, wd = moe_w_gate.astype(F32), moe_w_up.astype(F32), moe_w_down.astype(F32)
    stream = (ctx.astype(F32), x.astype(F32), CTX_LEN // TILE)
    for l in range(DEPTH):
        xa = trunk_layer(
            l, stream, mods[l], cos_t, sin_t, norm1_w[l], norm2_w[l], w_in_p, w_out[l],
            s5_weights[l], s5_d[l], s5_glu_w[l], s5_glu_b[l],
            na_bias, ssd_conv_w[l], ssd_conv_b[l], ssd_dt_bias[l], ssd_a_log[l], ssd_d[l], ssd_norm_w[l],
            moe_router[l], wg, wu, wd, final_norm_w if l == DEPTH - 1 else None)
        stream = (xa, xa, 0)
    return xa
```
